```python
import jax
import jax.numpy as jnp
from jax import lax
import numpy as np

D_MODEL = 1024
BATCH = 16
SEQ = 2048
DEPTH = 2
DEC_BATCH = 32
DEC_SEQ = 4
PAST_LEN = 16384
PAGE_SIZE = 128

POOL_WINDOWS = (2, 4, 8, 16)
N_POOL_GROUPS = len(POOL_WINDOWS)
POOL_GROUP = D_MODEL // 8
POOL_W = N_POOL_GROUPS * POOL_GROUP
POOL_STATE = max(POOL_WINDOWS) - 1
ATT_GROUPS = ((128, 1), (512, 4), (2048, 16))
N_ATT_GROUPS = len(ATT_GROUPS)
HEAD_DIM = 64
HEADS_PER_GROUP = D_MODEL // 128
N_ATT_HEADS = N_ATT_GROUPS * HEADS_PER_GROUP
GROUP_W = HEADS_PER_GROUP * HEAD_DIM
QKV_W = N_ATT_GROUPS * GROUP_W
ATT_OUT_W = GROUP_W
N_BRANCH = 2
IN_COLS = POOL_W + 3 * QKV_W + N_BRANCH * D_MODEL
Q_BLOCK = 128
ALIBI_MAX = 8.0
D_FF = ((8 * D_MODEL // 3 + 255) // 256) * 256
N_EXPERTS = 8
TOP_K = 2
D_FF_EXPERT = 7 * D_MODEL // 2
MOE_BLOCK_ROWS = 512
N_DENSE = (DEPTH + 1) // 2
N_MOE = DEPTH // 2
RMS_EPS = 1e-6
NEG_INF = -1e30

kernel_name = 'hybrid_pool_dilated_attn_decoder_step'


def rmsnorm(x, g):
    xf = x.astype(jnp.float32)
    y = xf * lax.rsqrt(jnp.mean(xf * xf, axis=-1, keepdims=True) + RMS_EPS)
    return (y * g.astype(jnp.float32)).astype(x.dtype)


def alibi_slopes():
    i = np.arange(1, N_ATT_HEADS + 1, dtype=np.float32)
    s = np.exp2(-ALIBI_MAX * i / N_ATT_HEADS).astype(np.float32)
    return jnp.asarray(s.reshape(N_ATT_GROUPS, HEADS_PER_GROUP))


def project(xn, w_in, q_gain, k_gain):
    b, l, _ = xn.shape
    z = xn @ w_in
    hs = (b, l, N_ATT_GROUPS, HEADS_PER_GROUP, HEAD_DIM)
    o = POOL_W
    u = z[..., :o]
    q = rmsnorm(z[..., o:o + QKV_W].reshape(hs), q_gain)
    k = rmsnorm(z[..., o + QKV_W:o + 2 * QKV_W].reshape(hs), k_gain)
    v = z[..., o + 2 * QKV_W:o + 3 * QKV_W].reshape(hs)
    gates = z[..., o + 3 * QKV_W:].reshape(b, l, N_BRANCH, D_MODEL)
    return u, q, k, v, gates


def causal_pool(u, pos):
    b, l, _ = u.shape
    uf = u.astype(jnp.float32)
    cp = jnp.concatenate([jnp.zeros((b, 1, POOL_W), jnp.float32), jnp.cumsum(uf, axis=1)], axis=1)
    means = []
    for gi, win in enumerate(POOL_WINDOWS):
        c = cp[..., gi * POOL_GROUP:(gi + 1) * POOL_GROUP]
        lagged = jnp.concatenate([jnp.zeros((b, win - 1, POOL_GROUP), jnp.float32), c[:, :l - win + 1]], axis=1)
        cnt = jnp.minimum(pos + 1, win).astype(jnp.float32)
        means.append((c[:, 1:] - lagged) / cnt[None, :, None])
    return jnp.concatenate(means, axis=-1) - uf


def pool_branch(u, pos, w_pool, pool_scale):
    b, l, _ = u.shape
    m = causal_pool(u, pos).reshape(b, l, N_POOL_GROUPS, POOL_GROUP)
    y = jnp.einsum('blgc,gcd->blgd', m, w_pool.astype(jnp.float32)).reshape(b, l, POOL_W)
    return (y * pool_scale.astype(jnp.float32)).astype(u.dtype)


def band_attention_prompt(q, k, v, dil, band, slopes):
    b, s, h, dh = q.shape
    l = s // dil
    nb = -(-l // Q_BLOCK)
    lp = nb * Q_BLOCK
    bb = b * dil

    def strided(t):
        return t.reshape(b, l, dil, h, dh).transpose(0, 2, 1, 3, 4).reshape(bb, l, h, dh).astype(jnp.float32)

    qs = jnp.pad(strided(q), ((0, 0), (0, lp - l), (0, 0), (0, 0))).reshape(bb, nb, Q_BLOCK, h, dh)

    def key_blocks(t):
        tp = jnp.pad(strided(t), ((0, 0), (Q_BLOCK, lp - l), (0, 0), (0, 0))).reshape(bb, nb + 1, Q_BLOCK, h, dh)
        return jnp.concatenate([tp[:, :-1], tp[:, 1:]], axis=2)

    kb, vb = key_blocks(k), key_blocks(v)
    qi = jnp.arange(Q_BLOCK)[:, None]
    kj = jnp.arange(2 * Q_BLOCK)[None, :]
    dist = qi - kj + Q_BLOCK
    key_idx = jnp.arange(nb)[:, None, None] * Q_BLOCK - Q_BLOCK + kj[None]
    valid = (dist >= 0) & (dist <= band) & (key_idx >= 0)
    sc = jnp.einsum('bnqhd,bnkhd->bnhqk', qs, kb) * (HEAD_DIM ** -0.5)
    sc = sc - slopes.astype(jnp.float32)[:, None, None] * (dist * dil).astype(jnp.float32)
    sc = jnp.where(valid[None, :, None], sc, NEG_INF)
    lse = jax.nn.logsumexp(sc, axis=-1)
    p = jnp.exp(sc - lse[..., None])
    o = jnp.einsum('bnhqk,bnkhd->bnqhd', p, vb).reshape(bb, lp, h, dh)[:, :l]
    o = o.reshape(b, dil, l, h, dh).transpose(0, 2, 1, 3, 4).reshape(b, s, h, dh)
    lse = lse.transpose(0, 1, 3, 2).reshape(bb, lp, h)[:, :l]
    lse = lse.reshape(b, dil, l, h).transpose(0, 2, 1, 3).reshape(b, s, h)
    return o, lse


def band_attention_sample(q, k_all, v_all, n_past, dil, band, slopes):
    t = q.shape[1]
    j = jnp.arange(band + 1)
    idx = n_past + jnp.arange(t)[:, None] - j[None, :] * dil
    valid = idx >= 0
    idx = jnp.maximum(idx, 0)
    kg = k_all[:, idx].astype(jnp.float32)
    vg = v_all[:, idx].astype(jnp.float32)
    sc = jnp.einsum('bthd,btjhd->bhtj', q.astype(jnp.float32), kg) * (HEAD_DIM ** -0.5)
    sc = sc - slopes.astype(jnp.float32)[:, None, None] * (j * dil).astype(jnp.float32)
    sc = jnp.where(valid, sc, NEG_INF)
    lse = jax.nn.logsumexp(sc, axis=-1)
    p = jnp.exp(sc - lse[..., None])
    o = jnp.einsum('bhtj,btjhd->bthd', p, vg)
    return o, lse.transpose(0, 2, 1)


def combine_groups(outs, lses):
    o = jnp.stack(outs, axis=0)
    w = jax.nn.softmax(jnp.stack(lses, axis=0), axis=0)
    y = jnp.sum(o * w[..., None], axis=0)
    return y.reshape(y.shape[0], y.shape[1], ATT_OUT_W)


def merge_branches(pool_o, att_o, gates, w_bp, w_ba, w_o):
    g = jax.nn.sigmoid(gates.astype(jnp.float32)).astype(pool_o.dtype)
    h = g[..., 0, :] * (pool_o @ w_bp) + g[..., 1, :] * (att_o @ w_ba)
    return h @ w_o


def swiglu_dense(x, w1, w3, w2):
    return (jax.nn.silu(x @ w1) * (x @ w3)) @ w2


def moe_rows_per_block(n_assign):
    r = 8
    while r < n_assign // N_EXPERTS and r < MOE_BLOCK_ROWS:
        r *= 2
    return r


def moe_swiglu(x, router_w, router_b, wg, wu, wd):
    shp = x.shape
    xf = x.reshape(-1, D_MODEL)
    n = xf.shape[0]
    logits = xf.astype(jnp.float32) @ router_w.astype(jnp.float32) + router_b.astype(jnp.float32)
    top_v, top_e = lax.top_k(logits, TOP_K)
    gate = jax.nn.softmax(top_v, axis=-1)
    a = n * TOP_K
    rows = moe_rows_per_block(a)
    n_blocks = -(-(a + N_EXPERTS * (rows - 1)) // rows)
    e_flat = top_e.reshape(-1)
    order = jnp.argsort(e_flat)
    e_sorted = e_flat[order]
    tok_sorted = (order // TOP_K).astype(jnp.int32)
    g_sorted = gate.reshape(-1)[order]
    counts = jnp.bincount(e_flat, length=N_EXPERTS)
    padded = (counts + rows - 1) // rows * rows
    start = jnp.cumsum(counts) - counts
    pend = jnp.cumsum(padded)
    pstart = pend - padded
    dest = pstart[e_sorted] + jnp.arange(a) - start[e_sorted]
    row_tok = jnp.full((n_blocks * rows,), n, dtype=jnp.int32).at[dest].set(tok_sorted)
    xb = jnp.concatenate([xf, jnp.zeros((1, D_MODEL), xf.dtype)], axis=0)[row_tok]
    xb = xb.reshape(n_blocks, rows, D_MODEL)
    blk_e = jnp.minimum(jnp.sum(jnp.arange(n_blocks)[:, None] * rows >= pend[None, :], axis=1), N_EXPERTS - 1)

    def expert_block(args):
        xblk, e = args
        return swiglu_dense(xblk, wg[e], wu[e], wd[e])

    yb = lax.map(expert_block, (xb, blk_e)).reshape(n_blocks * rows, D_MODEL)
    y = yb[dest] * g_sorted[:, None].astype(yb.dtype)
    out = jnp.zeros((n, D_MODEL), yb.dtype).at[tok_sorted].add(y)
    return out.reshape(shp)


def setup_inputs(seed: int = 0) -> dict:
    key = jax.random.key(seed)
    ks = jax.random.split(key, 32)
    f32 = jnp.float32

    def nrm(k, shape, scale):
        return jax.random.normal(k, shape, f32) * scale

    lw = [min(win, PAST_LEN) for win, _ in ATT_GROUPS]
    kvs = (HEADS_PER_GROUP, HEAD_DIM)
    return {
        'x_prompt': nrm(ks[0], (BATCH, SEQ, D_MODEL), 1.0),
        'x_sample': nrm(ks[1], (DEC_BATCH, DEC_SEQ, D_MODEL), 1.0),
        'state_pool': nrm(ks[2], (DEPTH, DEC_BATCH, POOL_STATE, POOL_W), 1.0),
        'cache_kv_g0': nrm(ks[3], (DEPTH, DEC_BATCH, lw[0], 2) + kvs, 1.0),
        'cache_kv_g1': nrm(ks[4], (DEPTH, DEC_BATCH, lw[1], 2) + kvs, 1.0),
        'cache_kv_g2': nrm(ks[5], (DEPTH, DEC_BATCH, lw[2], 2) + kvs, 1.0),
        'norm_attn': 1.0 + nrm(ks[6], (DEPTH, D_MODEL), 0.02),
        'w_in': nrm(ks[7], (DEPTH, D_MODEL, IN_COLS), D_MODEL ** -0.5),
        'q_norm': 1.0 + nrm(ks[8], (DEPTH, HEAD_DIM), 0.02),
        'k_norm': 1.0 + nrm(ks[9], (DEPTH, HEAD_DIM), 0.02),
        'w_pool': nrm(ks[10], (DEPTH, N_POOL_GROUPS, POOL_GROUP, POOL_GROUP), POOL_GROUP ** -0.5),
        'pool_scale': 1.0 + nrm(ks[11], (DEPTH, POOL_W), 0.02),
        'w_branch_pool': nrm(ks[12], (DEPTH, POOL_W, D_MODEL), POOL_W ** -0.5),
        'w_branch_attn': nrm(ks[13], (DEPTH, ATT_OUT_W, D_MODEL), ATT_OUT_W ** -0.5),
        'w_out': nrm(ks[14], (DEPTH, D_MODEL, D_MODEL), D_MODEL ** -0.5),
        'norm_ffn': 1.0 + nrm(ks[15], (DEPTH, D_MODEL), 0.02),
        'w1_dense': nrm(ks[16], (N_DENSE, D_MODEL, D_FF), D_MODEL ** -0.5),
        'w3_dense': nrm(ks[17], (N_DENSE, D_MODEL, D_FF), D_MODEL ** -0.5),
        'w2_dense': nrm(ks[18], (N_DENSE, D_FF, D_MODEL), D_FF ** -0.5),
        'router_w': nrm(ks[19], (N_MOE, D_MODEL, N_EXPERTS), D_MODEL ** -0.5),
        'router_b': nrm(ks[20], (N_MOE, N_EXPERTS), 0.01),
        'we_gate': nrm(ks[21], (N_MOE, N_EXPERTS, D_MODEL, D_FF_EXPERT), D_MODEL ** -0.5),
        'we_up': nrm(ks[22], (N_MOE, N_EXPERTS, D_MODEL, D_FF_EXPERT), D_MODEL ** -0.5),
        'we_down': nrm(ks[23], (N_MOE, N_EXPERTS, D_FF_EXPERT, D_MODEL), D_FF_EXPERT ** -0.5),
    }


def reference(x_prompt, x_sample, state_pool, cache_kv_g0, cache_kv_g1, cache_kv_g2,
              norm_attn, w_in, q_norm, k_norm, w_pool, pool_scale, w_branch_pool, w_branch_attn,
              w_out, norm_ffn, w1_dense, w3_dense, w2_dense, router_w, router_b,
              we_gate, we_up, we_down):
    slopes = alibi_slopes()
    caches = (cache_kv_g0, cache_kv_g1, cache_kv_g2)
    seq = x_prompt.shape[1]
    t_new = x_sample.shape[1]
    pos_p = jnp.arange(seq)
    pos_s = PAST_LEN - POOL_STATE + jnp.arange(POOL_STATE + t_new)
    xp, xs = x_prompt, x_sample
    pool_p, pool_s = [], []
    kv_p = [[] for _ in ATT_GROUPS]
    kv_s = [[] for _ in ATT_GROUPS]
    for layer in range(DEPTH):
        u, q, k, v, gates = project(rmsnorm(xp, norm_attn[layer]), w_in[layer], q_norm[layer], k_norm[layer])
        pool_o = pool_branch(u, pos_p, w_pool[layer], pool_scale[layer])
        outs, lses = [], []
        for gi, (win, dil) in enumerate(ATT_GROUPS):
            o, l = band_attention_prompt(q[:, :, gi], k[:, :, gi], v[:, :, gi], dil, win // dil, slopes[gi])
            outs.append(o)
            lses.append(l)
            keep = min(win, seq)
            kv_p[gi].append(jnp.stack([k[:, -keep:, gi], v[:, -keep:, gi]], axis=2))
        pool_p.append(u[:, -POOL_STATE:])
        att_o = combine_groups(outs, lses).astype(xp.dtype)
        xp = xp + merge_branches(pool_o, att_o, gates, w_branch_pool[layer], w_branch_attn[layer], w_out[layer])

        u, q, k, v, gates = project(rmsnorm(xs, norm_attn[layer]), w_in[layer], q_norm[layer], k_norm[layer])
        ext = jnp.concatenate([state_pool[layer].astype(u.dtype), u], axis=1)
        pool_o = pool_branch(ext, pos_s, w_pool[layer], pool_scale[layer])[:, POOL_STATE:]
        outs, lses = [], []
        for gi, (win, dil) in enumerate(ATT_GROUPS):
            c = caches[gi][layer].astype(k.dtype)
            n_past = c.shape[1]
            k_all = jnp.concatenate([c[:, :, 0], k[:, :, gi]], axis=1)
            v_all = jnp.concatenate([c[:, :, 1], v[:, :, gi]], axis=1)
            o, l = band_attention_sample(q[:, :, gi], k_all, v_all, n_past, dil, win // dil, slopes[gi])
            outs.append(o)
            lses.append(l)
            kv_s[gi].append(jnp.stack([k[:, :, gi], v[:, :, gi]], axis=2))
        pool_s.append(ext[:, -POOL_STATE:])
        att_o = combine_groups(outs, lses).astype(xs.dtype)
        xs = xs + merge_branches(pool_o, att_o, gates, w_branch_pool[layer], w_branch_attn[layer], w_out[layer])

        i = layer // 2
        if layer % 2 == 0:
            xp = xp + swiglu_dense(rmsnorm(xp, norm_ffn[layer]), w1_dense[i], w3_dense[i], w2_dense[i])
            xs = xs + swiglu_dense(rmsnorm(xs, norm_ffn[layer]), w1_dense[i], w3_dense[i], w2_dense[i])
        else:
            xp = xp + moe_swiglu(rmsnorm(xp, norm_ffn[layer]), router_w[i], router_b[i], we_gate[i], we_up[i], we_down[i])
            xs = xs + moe_swiglu(rmsnorm(xs, norm_ffn[layer]), router_w[i], router_b[i], we_gate[i], we_up[i], we_down[i])
    return (xp, xs,
            jnp.stack(pool_p), jnp.stack(pool_s),
            jnp.stack(kv_p[0]), jnp.stack(kv_s[0]),
            jnp.stack(kv_p[1]), jnp.stack(kv_s[1]),
            jnp.stack(kv_p[2]), jnp.stack(kv_s[2]))
```

```python
import functools

import numpy as np
import jax
import jax.numpy as jnp
from jax import lax
from jax.experimental import pallas as pl
from jax.experimental.pallas import tpu as pltpu

F32 = jnp.float32
BF16 = jnp.bfloat16

D_MODEL = 1024
PAST_LEN = 16384
POOL_WINDOWS = (2, 4, 8, 16)
POOL_GROUP = 128
POOL_W = 512
POOL_STATE = 15
ATT_GROUPS = ((128, 1), (512, 4), (2048, 16))
HEAD_DIM = 64
HEADS = 8
GROUP_W = 512
QKV_W = 1536
Q_BLOCK = 128
ALIBI_MAX = 8.0
N_EXPERTS = 8
TOP_K = 2
MOE_BLOCK_ROWS = 512
RMS_EPS = 1e-6
NEG_INF = -1e30
COL_BLOCK = 512
N_COL_BLOCKS = (POOL_W + 3 * QKV_W + 2 * D_MODEL) // COL_BLOCK
VMEM_LIMIT = 48 * 1024 * 1024


def _slopes():
    i = np.arange(1, 3 * HEADS + 1, dtype=np.float32)
    return np.exp2(-ALIBI_MAX * i / (3 * HEADS)).astype(np.float32).reshape(3, HEADS)


def _params(sem):
    return pltpu.CompilerParams(dimension_semantics=sem, vmem_limit_bytes=VMEM_LIMIT)


def _rms(x, gain):
    return x * lax.rsqrt(jnp.mean(x * x, axis=-1, keepdims=True) + RMS_EPS) * gain


def _dot(a, b):
    return jnp.dot(a, b, preferred_element_type=F32)


def _proj_kernel(x_ref, g_ref, w_ref, qg_ref, kg_ref, seg_ref,
                 u_ref, q_ref, kv0_ref, kv1_ref, kv2_ref, gate_ref, xn_ref):
    j = pl.program_id(1)

    @pl.when(j == 0)
    def _():
        xn_ref[...] = _rms(x_ref[...], g_ref[...]).astype(BF16)

    z = _dot(xn_ref[...], w_ref[...])

    def headnorm(gain):
        ms = _dot((z * z).astype(BF16), seg_ref[...])
        return z * lax.rsqrt(ms + RMS_EPS) * gain

    @pl.when(j == 0)
    def _():
        u_ref[...] = z

    @pl.when((j >= 1) & (j <= 3))
    def _():
        q_ref[...] = headnorm(qg_ref[...]).astype(BF16)

    def kv_out(g, ref):
        @pl.when(j == 4 + g)
        def _():
            ref[...] = headnorm(kg_ref[...])

        @pl.when(j == 7 + g)
        def _():
            ref[...] = z

    kv_out(0, kv0_ref)
    kv_out(1, kv1_ref)
    kv_out(2, kv2_ref)

    @pl.when(j >= 10)
    def _():
        gate_ref[...] = jax.nn.sigmoid(z).astype(BF16)


def _proj(x, layer, norm_attn, w_in, qgain, kgain, seg, tm):
    n = x.shape[0]
    kv_spec = lambda g: pl.BlockSpec((tm, COL_BLOCK), lambda i, j: (i, jnp.where(j >= 7 + g, 1, 0)))
    kv_shape = jax.ShapeDtypeStruct((n, 2 * GROUP_W), F32)
    return pl.pallas_call(
        _proj_kernel,
        grid=(n // tm, N_COL_BLOCKS),
        in_specs=[
            pl.BlockSpec((tm, D_MODEL), lambda i, j: (i, 0)),
            pl.BlockSpec((None, 1, D_MODEL), lambda i, j: (layer, 0, 0)),
            pl.BlockSpec((None, D_MODEL, COL_BLOCK), lambda i, j: (layer, 0, j)),
            pl.BlockSpec((None, 1, GROUP_W), lambda i, j: (layer, 0, 0)),
            pl.BlockSpec((None, 1, GROUP_W), lambda i, j: (layer, 0, 0)),
            pl.BlockSpec((GROUP_W, GROUP_W), lambda i, j: (0, 0)),
        ],
        out_specs=[
            pl.BlockSpec((tm, COL_BLOCK), lambda i, j: (i, 0)),
            pl.BlockSpec((tm, COL_BLOCK), lambda i, j: (i, jnp.clip(j - 1, 0, 2))),
            kv_spec(0), kv_spec(1), kv_spec(2),
            pl.BlockSpec((tm, COL_BLOCK), lambda i, j: (i, jnp.clip(j - 10, 0, 3))),
        ],
        out_shape=[
            jax.ShapeDtypeStruct((n, POOL_W), F32),
            jax.ShapeDtypeStruct((n, QKV_W), BF16),
            kv_shape, kv_shape, kv_shape,
            jax.ShapeDtypeStruct((n, 2 * D_MODEL), BF16),
        ],
        scratch_shapes=[pltpu.VMEM((tm, D_MODEL), BF16)],
        compiler_params=_params(("parallel", "arbitrary")),
        name="proj",
    )(x, norm_attn, w_in, qgain, kgain, seg)


def _attn_kernel(q_ref, kvc_ref, *rest, slope_dil, tc, has_halo):
    if has_halo:
        kvh_ref, o_ref, lse_ref, kbuf, vbuf = rest
    else:
        o_ref, lse_ref, kbuf, vbuf = rest
    off = Q_BLOCK if has_halo else 0
    if has_halo:
        kbuf[0:Q_BLOCK, :] = kvh_ref[:, 0:GROUP_W].astype(BF16)
        vbuf[0:Q_BLOCK, :] = kvh_ref[:, GROUP_W:2 * GROUP_W].astype(BF16)
    kbuf[off:off + tc, :] = kvc_ref[:, 0:GROUP_W].astype(BF16)
    vbuf[off:off + tc, :] = kvc_ref[:, GROUP_W:2 * GROUP_W].astype(BF16)

    nk = Q_BLOCK + off
    qi = lax.broadcasted_iota(jnp.int32, (Q_BLOCK, nk), 0)
    cj = lax.broadcasted_iota(jnp.int32, (Q_BLOCK, nk), 1)
    dist = qi - cj + off
    distf = dist.astype(F32)
    maskneg = jnp.where((dist >= 0) & (dist <= Q_BLOCK), 0.0, NEG_INF).astype(F32)
    if has_halo:
        first = jnp.where(pl.program_id(2) == 0, NEG_INF, 0.0).astype(F32)
        mask_first = maskneg + jnp.where(cj < Q_BLOCK, first, 0.0)

    for i in range(tc // Q_BLOCK):
        mask = mask_first if (has_halo and i == 0) else maskneg
        rows = slice(i * Q_BLOCK, (i + 1) * Q_BLOCK)
        krows = slice(i * Q_BLOCK, i * Q_BLOCK + nk)
        for hp in range(HEADS // 2):
            outs, lses = [], []
            for h in (2 * hp, 2 * hp + 1):
                cols = slice(h * HEAD_DIM, (h + 1) * HEAD_DIM)
                s = lax.dot_general(q_ref[rows, cols], kbuf[krows, cols],
                                    (((1,), (1,)), ((), ())), preferred_element_type=F32)
                s = s - slope_dil[h] * distf + mask
                m = jnp.max(s, axis=1, keepdims=True)
                p = jnp.exp(s - m)
                l = jnp.sum(p, axis=1, keepdims=True)
                o = _dot(p.astype(BF16), vbuf[krows, cols]) / l
                outs.append(o)
                lses.append(jnp.broadcast_to(m + jnp.log(l), (Q_BLOCK, HEAD_DIM)))
            pc = slice(hp * 2 * HEAD_DIM, (hp + 1) * 2 * HEAD_DIM)
            o_ref[rows, pc] = jnp.concatenate(outs, axis=1).astype(BF16)
            lse_ref[rows, pc] = jnp.concatenate(lses, axis=1)


def _attention(q, kv, gi, batch, seq):
    win, dil = ATT_GROUPS[gi]
    assert win // dil == Q_BLOCK
    l = seq // dil
    tc = min(l, 512)
    has_halo = l > Q_BLOCK
    qv = q.reshape(batch, l, dil * QKV_W)
    kvv = kv.reshape(batch, l, dil * 2 * GROUP_W)
    slope_dil = tuple(float(s) * dil for s in _slopes()[gi])
    in_specs = [
        pl.BlockSpec((None, tc, GROUP_W), lambda b, r, c: (b, c, r * 3 + gi)),
        pl.BlockSpec((None, tc, 2 * GROUP_W), lambda b, r, c: (b, c, r)),
    ]
    args = [qv, kvv]
    if has_halo:
        in_specs.append(pl.BlockSpec((None, Q_BLOCK, 2 * GROUP_W),
                                     lambda b, r, c: (b, jnp.maximum(c * (tc // Q_BLOCK) - 1, 0), r)))
        args.append(kvv)
    nk = tc + (Q_BLOCK if has_halo else 0)
    o, lse = pl.pallas_call(
        functools.partial(_attn_kernel, slope_dil=slope_dil, tc=tc, has_halo=has_halo),
        grid=(batch, dil, l // tc),
        in_specs=in_specs,
        out_specs=[pl.BlockSpec((None, tc, GROUP_W), lambda b, r, c: (b, c, r))] * 2,
        out_shape=[jax.ShapeDtypeStruct((batch, l, dil * GROUP_W), BF16),
                   jax.ShapeDtypeStruct((batch, l, dil * GROUP_W), F32)],
        scratch_shapes=[pltpu.VMEM((nk, GROUP_W), BF16), pltpu.VMEM((nk, GROUP_W), BF16)],
        compiler_params=_params(("parallel", "parallel", "arbitrary")),
        name=f"attn_g{gi}",
    )(*args)
    return o.reshape(batch * seq, GROUP_W), lse.reshape(batch * seq, GROUP_W)


def _mix_tail(x, pool_o, att_o, gate_ref, wbp_ref, wba_ref, wo_ref, nffn_ref, xo_ref, xn_ref):
    hp = _dot(pool_o.astype(BF16), wbp_ref[...])
    ha = _dot(att_o.astype(BF16), wba_ref[...])
    t = gate_ref[:, 0:D_MODEL].astype(F32) * hp + gate_ref[:, D_MODEL:2 * D_MODEL].astype(F32) * ha
    xo = x + _dot(t.astype(BF16), wo_ref[...])
    xo_ref[...] = xo
    xn_ref[...] = _rms(xo, nffn_ref[...]).astype(BF16)


def _group_linear(m, wpool_ref, pscale_ref):
    parts = [_dot(m[:, gi * POOL_GROUP:(gi + 1) * POOL_GROUP].astype(BF16), wpool_ref[gi])
             for gi in range(len(POOL_WINDOWS))]
    return jnp.concatenate(parts, axis=1) * pscale_ref[...]


def _merge_kernel(x_ref, u_ref, uh_ref, o0_ref, o1_ref, o2_ref, l0_ref, l1_ref, l2_ref, gate_ref,
                  wpool_ref, pscale_ref, wbp_ref, wba_ref, wo_ref, nffn_ref,
                  xo_ref, xn_ref, ext_ref, *, tm, tiles_per_seq):
    si = pl.program_id(0) % tiles_per_seq
    halo = POOL_STATE + 1
    ext_ref[0:halo, :] = jnp.where(si == 0, 0.0, uh_ref[...])
    ext_ref[halo:halo + tm, :] = u_ref[...]
    pos = si * tm + lax.broadcasted_iota(jnp.int32, (tm, 1), 0)
    parts = []
    for gi, win in enumerate(POOL_WINDOWS):
        cols = slice(gi * POOL_GROUP, (gi + 1) * POOL_GROUP)
        own = ext_ref[halo:halo + tm, cols]
        acc = own
        for back in range(1, win):
            acc = acc + ext_ref[halo - back:halo - back + tm, cols]
        inv = 1.0 / jnp.minimum(pos + 1, win).astype(F32)
        parts.append(acc * inv - own)
    pool_o = _group_linear(jnp.concatenate(parts, axis=1), wpool_ref, pscale_ref)

    l0, l1, l2 = l0_ref[...], l1_ref[...], l2_ref[...]
    mx = jnp.maximum(l0, jnp.maximum(l1, l2))
    e0, e1, e2 = jnp.exp(l0 - mx), jnp.exp(l1 - mx), jnp.exp(l2 - mx)
    att_o = (e0 * o0_ref[...].astype(F32) + e1 * o1_ref[...].astype(F32)
             + e2 * o2_ref[...].astype(F32)) / (e0 + e1 + e2)
    _mix_tail(x_ref[...], pool_o, att_o, gate_ref, wbp_ref, wba_ref, wo_ref, nffn_ref, xo_ref, xn_ref)


def _wspec(shape, layer):
    nd = len(shape)
    return pl.BlockSpec((None,) + tuple(shape), lambda i: (layer,) + (0,) * nd)


def _tail_weight_specs(layer):
    return [_wspec((POOL_W, D_MODEL), layer), _wspec((GROUP_W, D_MODEL), layer),
            _wspec((D_MODEL, D_MODEL), layer), _wspec((1, D_MODEL), layer)]


def _merge(x, u, outs, lses, gates, layer, w, seq, tm):
    n = x.shape[0]
    halo = POOL_STATE + 1
    row = lambda width: pl.BlockSpec((tm, width), lambda i: (i, 0))
    return pl.pallas_call(
        functools.partial(_merge_kernel, tm=tm, tiles_per_seq=seq // tm),
        grid=(n // tm,),
        in_specs=[row(D_MODEL), row(POOL_W),
                  pl.BlockSpec((halo, POOL_W), lambda i: (jnp.maximum(i * (tm // halo) - 1, 0), 0))]
                 + [row(GROUP_W)] * 6 + [row(2 * D_MODEL),
                  _wspec((len(POOL_WINDOWS), POOL_GROUP, POOL_GROUP), layer), _wspec((1, POOL_W), layer)]
                 + _tail_weight_specs(layer),
        out_specs=[row(D_MODEL), row(D_MODEL)],
        out_shape=[jax.ShapeDtypeStruct((n, D_MODEL), F32), jax.ShapeDtypeStruct((n, D_MODEL), BF16)],
        scratch_shapes=[pltpu.VMEM((halo + tm, POOL_W), F32)],
        compiler_params=_params(("parallel",)),
        name="merge",
    )(x, u, u, *outs, *lses, gates, w["w_pool"], w["pool_scale"], w["w_bp"], w["w_ba"], w["w_o"], w["norm_ffn"])


def _sample_bias(dec_seq, cache_rows):
    slopes = _slopes()
    nk = Q_BLOCK + 8
    bias = np.full((len(ATT_GROUPS), dec_seq, nk, GROUP_W), NEG_INF, np.float32)
    classes = np.zeros((len(ATT_GROUPS), dec_seq), np.int64)
    for gi, (win, dil) in enumerate(ATT_GROUPS):
        band = win // dil
        lc = cache_rows[gi]
        assert lc % dil == 0 and lc // dil == Q_BLOCK
        lane_slope = np.repeat(slopes[gi], HEAD_DIM)
        for t in range(dec_seq):
            cls = (lc + t) % dil
            classes[gi, t] = cls
            r0 = (lc + t - cls) // dil
            for r in range(Q_BLOCK):
                jj = r0 - r
                if 1 <= jj <= band:
                    bias[gi, t, r] = -lane_slope * float(jj * dil)
            for t2 in range(dec_seq):
                if t2 <= t and (t - t2) % dil == 0 and (t - t2) // dil <= band:
                    bias[gi, t, Q_BLOCK + t2] = -lane_slope * float(t - t2)
    return bias, classes


def _sample_mix_kernel(state_ref, u_ref, q_ref, kn0_ref, kn1_ref, kn2_ref, c0_ref, c1_ref, c2_ref,
                       bias_ref, seg_ref, wpool_ref, pscale_ref,
                       pool_ref, att_ref, newpool_ref, ext_ref, m_ref, *, dec_seq, classes):
    kv_width = 2 * GROUP_W
    ext_ref[...] = jnp.zeros_like(ext_ref)
    ext_ref[0:POOL_STATE, :] = state_ref[...]
    ext_ref[POOL_STATE:POOL_STATE + dec_seq, :] = u_ref[...]
    newpool_ref[...] = ext_ref[dec_seq:dec_seq + POOL_STATE, :]

    m_ref[...] = jnp.zeros_like(m_ref)
    for t in range(dec_seq):
        row = POOL_STATE + t
        for gi, win in enumerate(POOL_WINDOWS):
            cols = slice(gi * POOL_GROUP, (gi + 1) * POOL_GROUP)
            tot = jnp.sum(ext_ref[row - win + 1:row + 1, cols], axis=0, keepdims=True)
            cnt = float(min(PAST_LEN + t + 1, win))
            m_ref[t:t + 1, cols] = tot / cnt - ext_ref[row:row + 1, cols]
    pool_ref[...] = _group_linear(m_ref[...], wpool_ref, pscale_ref)[0:dec_seq]

    pad = jnp.zeros((8 - dec_seq, GROUP_W), F32)
    q_all = q_ref[...].astype(F32)
    for t in range(dec_seq):
        outs, lses = [], []
        for gi, (kn_ref, c_ref) in enumerate(((kn0_ref, c0_ref), (kn1_ref, c1_ref), (kn2_ref, c2_ref))):
            base = int(classes[gi][t]) * kv_width
            keys = jnp.concatenate([c_ref[:, base:base + GROUP_W], kn_ref[:, 0:GROUP_W], pad], axis=0)
            vals = jnp.concatenate([c_ref[:, base + GROUP_W:base + kv_width], kn_ref[:, GROUP_W:kv_width], pad], axis=0)
            prod = keys * q_all[t:t + 1, gi * GROUP_W:(gi + 1) * GROUP_W]
            hi = prod.astype(BF16)
            lo = (prod - hi.astype(F32)).astype(BF16)
            s = _dot(hi, seg_ref[...]) + _dot(lo, seg_ref[...]) + bias_ref[gi, t]
            mx = jnp.max(s, axis=0, keepdims=True)
            p = jnp.exp(s - mx)
            l = jnp.sum(p, axis=0, keepdims=True)
            outs.append(jnp.sum(p * vals, axis=0, keepdims=True) / l)
            lses.append(mx + jnp.log(l))
        mx = jnp.maximum(lses[0], jnp.maximum(lses[1], lses[2]))
        es = [jnp.exp(v - mx) for v in lses]
        att_ref[t:t + 1, :] = (es[0] * outs[0] + es[1] * outs[1] + es[2] * outs[2]) / (es[0] + es[1] + es[2])


def _sample_mix(state_pool, u, q, kvn, caches, layer, w, seg_ones, dec_batch, dec_seq):
    cache_rows = [c.shape[2] for c in caches]
    bias, classes = _sample_bias(dec_seq, cache_rows)
    cviews, cspecs = [], []
    for gi, (win, dil) in enumerate(ATT_GROUPS):
        lc = cache_rows[gi]
        cviews.append(caches[gi].reshape(-1, lc // dil, dil * 2 * GROUP_W))
        ncls = int(classes[gi].max()) + 1
        width = 2 * GROUP_W * min(dil, ncls)
        cspecs.append(pl.BlockSpec((None, lc // dil, width), lambda b: (layer * dec_batch + b, 0, 0)))
    per_tok = lambda width: pl.BlockSpec((None, dec_seq, width), lambda b: (b, 0, 0))
    const = lambda shape: pl.BlockSpec(shape, lambda b: (0,) * len(shape))
    return pl.pallas_call(
        functools.partial(_sample_mix_kernel, dec_seq=dec_seq, classes=tuple(map(tuple, classes))),
        grid=(dec_batch,),
        in_specs=[pl.BlockSpec((None, POOL_STATE, POOL_W), lambda b: (layer * dec_batch + b, 0, 0)),
                  per_tok(POOL_W), per_tok(QKV_W)] + [per_tok(2 * GROUP_W)] * 3 + cspecs
                 + [const(bias.shape), const((GROUP_W, GROUP_W)),
                    _wspec((len(POOL_WINDOWS), POOL_GROUP, POOL_GROUP), layer), _wspec((1, POOL_W), layer)],
        out_specs=[per_tok(POOL_W), per_tok(GROUP_W),
                   pl.BlockSpec((None, POOL_STATE, POOL_W), lambda b: (b, 0, 0))],
        out_shape=[jax.ShapeDtypeStruct((dec_batch, dec_seq, POOL_W), F32),
                   jax.ShapeDtypeStruct((dec_batch, dec_seq, GROUP_W), F32),
                   jax.ShapeDtypeStruct((dec_batch, POOL_STATE, POOL_W), F32)],
        scratch_shapes=[pltpu.VMEM((POOL_STATE + dec_seq + 5, POOL_W), F32), pltpu.VMEM((8, POOL_W), F32)],
        compiler_params=_params(("parallel",)),
        name="sample_mix",
    )(state_pool.reshape(-1, POOL_STATE, POOL_W), u.reshape(dec_batch, dec_seq, POOL_W),
      q.reshape(dec_batch, dec_seq, QKV_W), *[k.reshape(dec_batch, dec_seq, 2 * GROUP_W) for k in kvn],
      *cviews, jnp.asarray(bias), seg_ones, w["w_pool"], w["pool_scale"])


def _tail_kernel(x_ref, pool_ref, att_ref, gate_ref, wbp_ref, wba_ref, wo_ref, nffn_ref, xo_ref, xn_ref):
    _mix_tail(x_ref[...], pool_ref[...], att_ref[...], gate_ref, wbp_ref, wba_ref, wo_ref, nffn_ref, xo_ref, xn_ref)


def _tail(x, pool_o, att_o, gates, layer, w):
    n = x.shape[0]
    row = lambda width: pl.BlockSpec((n, width), lambda i: (0, 0))
    return pl.pallas_call(
        _tail_kernel,
        grid=(1,),
        in_specs=[row(D_MODEL), row(POOL_W), row(GROUP_W), row(2 * D_MODEL)] + _tail_weight_specs(layer),
        out_specs=[row(D_MODEL), row(D_MODEL)],
        out_shape=[jax.ShapeDtypeStruct((n, D_MODEL), F32), jax.ShapeDtypeStruct((n, D_MODEL), BF16)],
        compiler_params=_params(("arbitrary",)),
        name="tail",
    )(x, pool_o, att_o, gates, w["w_bp"], w["w_ba"], w["w_o"], w["norm_ffn"])


def _swiglu_step(x, wg_ref, wu_ref, wd_ref, acc_ref):
    h = jax.nn.silu(_dot(x, wg_ref[...])) * _dot(x, wu_ref[...])
    acc_ref[...] += _dot(h.astype(BF16), wd_ref[...])


def _ffn_kernel(x_ref, xn_ref, w1_ref, w3_ref, w2_ref, y_ref, acc_ref):
    f = pl.program_id(1)

    @pl.when(f == 0)
    def _():
        acc_ref[...] = x_ref[...]

    _swiglu_step(xn_ref[...], w1_ref, w3_ref, w2_ref, acc_ref)

    @pl.when(f == pl.num_programs(1) - 1)
    def _():
        y_ref[...] = acc_ref[...]


def _ffn(x, xn, w1, w3, w2, idx, tm, tf):
    n = x.shape[0]
    d_ff = w1.shape[-1]
    return pl.pallas_call(
        _ffn_kernel,
        grid=(n // tm, d_ff // tf),
        in_specs=[pl.BlockSpec((tm, D_MODEL), lambda i, f: (i, 0)),
                  pl.BlockSpec((tm, D_MODEL), lambda i, f: (i, 0)),
                  pl.BlockSpec((None, D_MODEL, tf), lambda i, f: (idx, 0, f)),
                  pl.BlockSpec((None, D_MODEL, tf), lambda i, f: (idx, 0, f)),
                  pl.BlockSpec((None, tf, D_MODEL), lambda i, f: (idx, f, 0))],
        out_specs=pl.BlockSpec((tm, D_MODEL), lambda i, f: (i, 0)),
        out_shape=jax.ShapeDtypeStruct((n, D_MODEL), F32),
        scratch_shapes=[pltpu.VMEM((tm, D_MODEL), F32)],
        compiler_params=_params(("parallel", "arbitrary")),
        name="ffn",
    )(x, xn, w1, w3, w2)


def _router_kernel(x_ref, g_ref, rw_ref, rb_ref, route_ref, xn_ref):
    xn = _rms(x_ref[...], g_ref[...])
    xn_ref[...] = xn.astype(BF16)
    logits = jnp.dot(xn, rw_ref[...], preferred_element_type=F32, precision=lax.Precision.HIGHEST) + rb_ref[...]
    lane = lax.broadcasted_iota(jnp.int32, logits.shape, 1)
    neg = jnp.float32(-jnp.inf)
    logits = jnp.where(lane < N_EXPERTS, logits, neg)
    m1 = jnp.max(logits, axis=1, keepdims=True)
    i1 = jnp.min(jnp.where(logits == m1, lane, 128), axis=1, keepdims=True)
    rest = jnp.where(lane == i1, neg, logits)
    m2 = jnp.max(rest, axis=1, keepdims=True)
    i2 = jnp.min(jnp.where(rest == m2, lane, 128), axis=1, keepdims=True)
    e = jnp.exp(m2 - m1)
    g1 = 1.0 / (1.0 + e)
    g2 = e / (1.0 + e)
    route_ref[...] = jnp.where(lane == 0, i1.astype(F32),
                               jnp.where(lane == 1, i2.astype(F32),
                                         jnp.where(lane == 2, g1, jnp.where(lane == 3, g2, 0.0))))


def _router(x, norm_ffn, rw, rb, layer, idx, tm):
    n = x.shape[0]
    return pl.pallas_call(
        _router_kernel,
        grid=(n // tm,),
        in_specs=[pl.BlockSpec((tm, D_MODEL), lambda i: (i, 0)),
                  pl.BlockSpec((None, 1, D_MODEL), lambda i: (layer, 0, 0)),
                  pl.BlockSpec((None, D_MODEL, 128), lambda i: (idx, 0, 0)),
                  pl.BlockSpec((None, 1, 128), lambda i: (idx, 0, 0))],
        out_specs=[pl.BlockSpec((tm, 128), lambda i: (i, 0)), pl.BlockSpec((tm, D_MODEL), lambda i: (i, 0))],
        out_shape=[jax.ShapeDtypeStruct((n, 128), F32), jax.ShapeDtypeStruct((n, D_MODEL), BF16)],
        compiler_params=_params(("parallel",)),
        name="router",
    )(x, norm_ffn, rw, rb)


def _moe_kernel(blk_ref, xb_ref, wg_ref, wu_ref, wd_ref, yb_ref, acc_ref):
    del blk_ref
    f = pl.program_id(1)

    @pl.when(f == 0)
    def _():
        acc_ref[...] = jnp.zeros_like(acc_ref)

    _swiglu_step(xb_ref[...], wg_ref, wu_ref, wd_ref, acc_ref)

    @pl.when(f == pl.num_programs(1) - 1)
    def _():
        yb_ref[...] = acc_ref[...]


def _moe_rows_per_block(n_assign):
    r = 8
    while r < n_assign // N_EXPERTS and r < MOE_BLOCK_ROWS:
        r *= 2
    return r


def _moe(x, norm_ffn, rw, rb, wg, wu, wd, layer, idx, tm, tf):
    n = x.shape[0]
    route, xn = _router(x, norm_ffn, rw, rb, layer, idx, tm)
    top_e = route[:, 0:TOP_K].astype(jnp.int32)
    gate = route[:, TOP_K:2 * TOP_K]
    a = n * TOP_K
    rows = max(_moe_rows_per_block(a), 16)
    n_blocks = -(-(a + N_EXPERTS * (rows - 1)) // rows)
    e_flat = top_e.reshape(-1)
    order = jnp.argsort(e_flat)
    e_sorted = e_flat[order]
    tok_sorted = (order // TOP_K).astype(jnp.int32)
    g_sorted = gate.reshape(-1)[order]
    counts = jnp.bincount(e_flat, length=N_EXPERTS)
    padded = (counts + rows - 1) // rows * rows
    start = jnp.cumsum(counts) - counts
    pend = jnp.cumsum(padded)
    pstart = pend - padded
    dest = pstart[e_sorted] + jnp.arange(a) - start[e_sorted]
    row_tok = jnp.full((n_blocks * rows,), n, dtype=jnp.int32).at[dest].set(tok_sorted)
    xb = jnp.concatenate([xn, jnp.zeros((1, D_MODEL), xn.dtype)], axis=0)[row_tok]
    blk_e = jnp.minimum(jnp.sum(jnp.arange(n_blocks)[:, None] * rows >= pend[None, :], axis=1),
                        N_EXPERTS - 1).astype(jnp.int32)
    d_ff = wg.shape[-1]
    yb = pl.pallas_call(
        _moe_kernel,
        grid_spec=pltpu.PrefetchScalarGridSpec(
            num_scalar_prefetch=1,
            grid=(n_blocks, d_ff // tf),
            in_specs=[pl.BlockSpec((rows, D_MODEL), lambda i, f, blk: (i, 0)),
                      pl.BlockSpec((None, None, D_MODEL, tf), lambda i, f, blk: (idx, blk[i], 0, f)),
                      pl.BlockSpec((None, None, D_MODEL, tf), lambda i, f, blk: (idx, blk[i], 0, f)),
                      pl.BlockSpec((None, None, tf, D_MODEL), lambda i, f, blk: (idx, blk[i], f, 0))],
            out_specs=pl.BlockSpec((rows, D_MODEL), lambda i, f, blk: (i, 0)),
            scratch_shapes=[pltpu.VMEM((rows, D_MODEL), F32)]),
        out_shape=jax.ShapeDtypeStruct((n_blocks * rows, D_MODEL), F32),
        compiler_params=_params(("parallel", "arbitrary")),
        name="moe",
    )(blk_e, xb, wg, wu, wd)
    y = yb[dest] * g_sorted[:, None]
    return x + jnp.zeros((n, D_MODEL), F32).at[tok_sorted].add(y)


def kernel(x_prompt, x_sample, state_pool, cache_kv_g0, cache_kv_g1, cache_kv_g2, norm_attn, w_in, q_norm, k_norm, w_pool, pool_scale, w_branch_pool, w_branch_attn, w_out, norm_ffn, w1_dense, w3_dense, w2_dense, router_w, router_b, we_gate, we_up, we_down):
    batch, seq, _ = x_prompt.shape
    dec_batch, dec_seq, _ = x_sample.shape
    depth = w_in.shape[0]
    caches = (cache_kv_g0, cache_kv_g1, cache_kv_g2)
    n_p, n_s = batch * seq, dec_batch * dec_seq
    tm_p = 512

    head_of = np.arange(GROUP_W) // HEAD_DIM
    same_head = (head_of[:, None] == head_of[None, :]).astype(np.float32)
    seg_mean = jnp.asarray(same_head / HEAD_DIM, BF16)
    seg_ones = jnp.asarray(same_head, BF16)
    w_in_b = w_in.astype(BF16)
    qgain = (jnp.tile(q_norm, (1, HEADS)) * (HEAD_DIM ** -0.5)).reshape(depth, 1, GROUP_W)
    kgain = jnp.tile(k_norm, (1, HEADS)).reshape(depth, 1, GROUP_W)
    norm_attn3 = norm_attn.reshape(depth, 1, D_MODEL)
    w = {"w_pool": w_pool.astype(BF16), "pool_scale": pool_scale.reshape(depth, 1, POOL_W),
         "w_bp": w_branch_pool.astype(BF16), "w_ba": w_branch_attn.astype(BF16), "w_o": w_out.astype(BF16),
         "norm_ffn": norm_ffn.reshape(depth, 1, D_MODEL)}
    w1_b, w3_b, w2_b = w1_dense.astype(BF16), w3_dense.astype(BF16), w2_dense.astype(BF16)
    wg_b, wu_b, wd_b = we_gate.astype(BF16), we_up.astype(BF16), we_down.astype(BF16)
    rw = jnp.pad(router_w, ((0, 0), (0, 0), (0, 128 - N_EXPERTS)))
    rb = jnp.pad(router_b, ((0, 0), (0, 128 - N_EXPERTS))).reshape(-1, 1, 128)

    xp = x_prompt.reshape(n_p, D_MODEL)
    xs = x_sample.reshape(n_s, D_MODEL)
    pool_p, pool_s = [], []
    kv_p = [[] for _ in ATT_GROUPS]
    kv_s = [[] for _ in ATT_GROUPS]
    for layer in range(depth):
        u, q, kv0, kv1, kv2, gates = _proj(xp, layer, norm_attn3, w_in_b, qgain, kgain, seg_mean, tm_p)
        kvs = (kv0, kv1, kv2)
        outs, lses = [], []
        for gi, (win, _) in enumerate(ATT_GROUPS):
            o, lse = _attention(q, kvs[gi], gi, batch, seq)
            outs.append(o)
            lses.append(lse)
            keep = min(win, seq)
            kv_p[gi].append(kvs[gi].reshape(batch, seq, 2, HEADS, HEAD_DIM)[:, seq - keep:])
        pool_p.append(u.reshape(batch, seq, POOL_W)[:, seq - POOL_STATE:])
        xp, xpn = _merge(xp, u, outs, lses, gates, layer, w, seq, tm_p)

        u, q, kv0, kv1, kv2, gates = _proj(xs, layer, norm_attn3, w_in_b, qgain, kgain, seg_mean, n_s)
        kvn = (kv0, kv1, kv2)
        pool_o, att_o, new_pool = _sample_mix(state_pool, u, q, kvn, caches, layer, w, seg_ones, dec_batch, dec_seq)
        for gi in range(len(ATT_GROUPS)):
            kv_s[gi].append(kvn[gi].reshape(dec_batch, dec_seq, 2, HEADS, HEAD_DIM))
        pool_s.append(new_pool)
        xs, xsn = _tail(xs, pool_o.reshape(n_s, POOL_W), att_o.reshape(n_s, GROUP_W), gates, layer, w)

        i = layer // 2
        if layer % 2 == 0:
            xp = _ffn(xp, xpn, w1_b, w3_b, w2_b, i, 1024, 256)
            xs = _ffn(xs, xsn, w1_b, w3_b, w2_b, i, n_s, 256)
        else:
            xp = _moe(xp, w["norm_ffn"], rw, rb, wg_b, wu_b, wd_b, layer, i, 512, 512)
            xs = _moe(xs, w["norm_ffn"], rw, rb, wg_b, wu_b, wd_b, layer, i, n_s, 512)
    return (xp.reshape(batch, seq, D_MODEL), xs.reshape(dec_batch, dec_seq, D_MODEL),
            jnp.stack(pool_p), jnp.stack(pool_s),
            jnp.stack(kv_p[0]), jnp.stack(kv_s[0]),
            jnp.stack(kv_p[1]), jnp.stack(kv_s[1]),
            jnp.stack(kv_p[2]), jnp.stack(kv_s[2]))
```

```python
import functools

import numpy as np
import jax
import jax.numpy as jnp
from jax import lax
from jax.experimental import pallas as pl
from jax.experimental.pallas import tpu as pltpu

F32 = jnp.float32
BF16 = jnp.bfloat16

D_MODEL = 1024
PAST_LEN = 16384
POOL_WINDOWS = (2, 4, 8, 16)
POOL_GROUP = 128
POOL_W = 512
POOL_STATE = 15
ATT_GROUPS = ((128, 1), (512, 4), (2048, 16))
N_GROUPS = len(ATT_GROUPS)
HEAD_DIM = 64
HEADS = 8
GROUP_W = 512
QKV_W = 1536
Q_BLOCK = 128
ALIBI_MAX = 8.0
N_EXPERTS = 8
TOP_K = 2
MOE_BLOCK_ROWS = 512
RMS_EPS = 1e-6
NEG_INF = -1e30
LANES = 128
LANE_CHUNKS = GROUP_W // LANES
PROJ_W = POOL_W + 3 * QKV_W
VMEM_LIMIT = 56 * 1024 * 1024


def _slopes():
    i = np.arange(1, N_GROUPS * HEADS + 1, dtype=np.float32)
    return np.exp2(-ALIBI_MAX * i / (N_GROUPS * HEADS)).astype(np.float32).reshape(N_GROUPS, HEADS)


def _params(sem):
    return pltpu.CompilerParams(dimension_semantics=sem, vmem_limit_bytes=VMEM_LIMIT)


def _rms(x, gain):
    return x * lax.rsqrt(jnp.mean(x * x, axis=-1, keepdims=True) + RMS_EPS) * gain


def _dot(a, b):
    return jnp.dot(a, b, preferred_element_type=F32)


def _resident(shape, index_map):
    return pl.BlockSpec(shape, index_map, pipeline_mode=pl.Buffered(1))


def _write_classes(dst_ref, col0, val, dil, tmp_ref, slot):
    rows = val.shape[0] // dil
    cols = slice(col0, col0 + GROUP_W)
    if dil == 1:
        dst_ref[0, :, cols] = val.astype(dst_ref.dtype)
        return
    for c in range(LANE_CHUNKS):
        tmp_ref[slot, c] = val[:, c * LANES:(c + 1) * LANES]
    for r in range(dil):
        picked = [tmp_ref[slot, c, pl.ds(r, rows, stride=dil), :] for c in range(LANE_CHUNKS)]
        dst_ref[r, :, cols] = jnp.concatenate(picked, axis=1).astype(dst_ref.dtype)


def _proj_kernel(x_ref, g_ref, w_ref, qg_ref, kg_ref, seg_ref, *rest, tm, dils, keeps, n_alias):
    u_ref, q0_ref, q1_ref, q2_ref, kc0_ref, kc1_ref, kc2_ref, kvo0_ref, kvo1_ref, kvo2_ref, tmp_ref = rest[n_alias:]
    s = pl.program_id(1)
    n_tiles = pl.num_programs(1)
    xn = _rms(x_ref[...], g_ref[...]).astype(BF16)

    def zblk(j):
        return _dot(xn, w_ref[:, j * GROUP_W:(j + 1) * GROUP_W])

    def headnorm(z, gain):
        ms = _dot((z * z).astype(BF16), seg_ref[...])
        return z * lax.rsqrt(ms + RMS_EPS) * gain

    u_ref[...] = zblk(0)
    q_refs = (q0_ref, q1_ref, q2_ref)
    kc_refs = (kc0_ref, kc1_ref, kc2_ref)
    kvo_refs = (kvo0_ref, kvo1_ref, kvo2_ref)
    slot = 0
    for g in range(N_GROUPS):
        q = headnorm(zblk(1 + g), qg_ref[...])
        k = headnorm(zblk(1 + N_GROUPS + g), kg_ref[...])
        v = zblk(1 + 2 * N_GROUPS + g)
        for dst, col0, val in ((q_refs[g], 0, q), (kc_refs[g], 0, k), (kc_refs[g], GROUP_W, v)):
            _write_classes(dst, col0, val, dils[g], tmp_ref, slot % tmp_ref.shape[0])
            slot += dils[g] > 1
        keep = keeps[g]
        if keep >= tm:
            first = n_tiles - keep // tm

            @pl.when(s >= first)
            def _(k=k, v=v, ref=kvo_refs[g]):
                ref[:, 0:GROUP_W] = k
                ref[:, GROUP_W:2 * GROUP_W] = v
        else:
            @pl.when(s == n_tiles - 1)
            def _(k=k, v=v, ref=kvo_refs[g], keep=keep):
                ref[:, 0:GROUP_W] = k[tm - keep:, :]
                ref[:, GROUP_W:2 * GROUP_W] = v[tm - keep:, :]


def _proj(x, layer, norm_attn, w_qkv, qgain, kgain, seg, batch, seq, tm, dils, keeps, out_layer, out_depth, prev_kvo):
    n_tiles = seq // tm
    kvo_specs, kvo_shapes = [], []
    for g in range(N_GROUPS):
        keep = keeps[g]
        assert keep % tm == 0 or (keep < tm and keep % 8 == 0)
        if keep >= tm:
            first = n_tiles - keep // tm
            kvo_specs.append(pl.BlockSpec((None, None, tm, 2 * GROUP_W),
                                          lambda b, s, first=first: (out_layer, b, jnp.maximum(s - first, 0), 0)))
        else:
            kvo_specs.append(pl.BlockSpec((None, None, keep, 2 * GROUP_W), lambda b, s: (out_layer, b, 0, 0)))
        kvo_shapes.append(jax.ShapeDtypeStruct((out_depth, batch, keep, 2 * GROUP_W), F32))
    cls_spec = lambda g, width: pl.BlockSpec((None, dils[g], tm // dils[g], width), lambda b, s: (b, 0, s, 0))
    cls_shape = lambda g, width: jax.ShapeDtypeStruct((batch, dils[g], seq // dils[g], width), BF16)
    n_alias = len(prev_kvo)
    n_in = 6
    return pl.pallas_call(
        functools.partial(_proj_kernel, tm=tm, dils=dils, keeps=keeps, n_alias=n_alias),
        grid=(batch, n_tiles),
        in_specs=[
            pl.BlockSpec((tm, D_MODEL), lambda b, s: (b * n_tiles + s, 0)),
            _resident((None, 1, D_MODEL), lambda b, s: (layer, 0, 0)),
            _resident((None, D_MODEL, PROJ_W), lambda b, s: (layer, 0, 0)),
            _resident((None, 1, GROUP_W), lambda b, s: (layer, 0, 0)),
            _resident((None, 1, GROUP_W), lambda b, s: (layer, 0, 0)),
            _resident((GROUP_W, GROUP_W), lambda b, s: (0, 0)),
        ] + [pl.BlockSpec(memory_space=pl.ANY)] * n_alias,
        out_specs=[pl.BlockSpec((tm, POOL_W), lambda b, s: (b * n_tiles + s, 0))]
                  + [cls_spec(g, GROUP_W) for g in range(N_GROUPS)]
                  + [cls_spec(g, 2 * GROUP_W) for g in range(N_GROUPS)] + kvo_specs,
        out_shape=[jax.ShapeDtypeStruct((batch * seq, POOL_W), F32)]
                  + [cls_shape(g, GROUP_W) for g in range(N_GROUPS)]
                  + [cls_shape(g, 2 * GROUP_W) for g in range(N_GROUPS)] + kvo_shapes,
        scratch_shapes=[pltpu.VMEM((3, LANE_CHUNKS, tm, LANES), F32)],
        input_output_aliases={n_in + g: 1 + 2 * N_GROUPS + g for g in range(n_alias)},
        compiler_params=_params(("parallel", "arbitrary")),
        name="proj",
    )(x, norm_attn, w_qkv, qgain, kgain, seg, *prev_kvo)


def _attn_kernel(q_ref, kvc_ref, *rest, slope_dil, tc, has_halo):
    if has_halo:
        kvh_ref, o_ref, lse_ref, kbuf, vbuf = rest
    else:
        o_ref, lse_ref, kbuf, vbuf = rest
    off = Q_BLOCK if has_halo else 0
    if has_halo:
        kbuf[0:Q_BLOCK, :] = kvh_ref[:, 0:GROUP_W]
        vbuf[0:Q_BLOCK, :] = kvh_ref[:, GROUP_W:2 * GROUP_W]
    kbuf[off:off + tc, :] = kvc_ref[:, 0:GROUP_W]
    vbuf[off:off + tc, :] = kvc_ref[:, GROUP_W:2 * GROUP_W]

    nk = Q_BLOCK + off
    qi = lax.broadcasted_iota(jnp.int32, (Q_BLOCK, nk), 0)
    cj = lax.broadcasted_iota(jnp.int32, (Q_BLOCK, nk), 1)
    dist = qi - cj + off
    distf = dist.astype(F32)
    maskneg = jnp.where((dist >= 0) & (dist <= Q_BLOCK), 0.0, NEG_INF).astype(F32)
    if has_halo:
        first = jnp.where(pl.program_id(2) == 0, NEG_INF, 0.0).astype(F32)
        mask_first = maskneg + jnp.where(cj < Q_BLOCK, first, 0.0)

    for i in range(tc // Q_BLOCK):
        mask = mask_first if (has_halo and i == 0) else maskneg
        rows = slice(i * Q_BLOCK, (i + 1) * Q_BLOCK)
        krows = slice(i * Q_BLOCK, i * Q_BLOCK + nk)
        for hp in range(HEADS // 2):
            outs, lses = [], []
            for h in (2 * hp, 2 * hp + 1):
                cols = slice(h * HEAD_DIM, (h + 1) * HEAD_DIM)
                s = lax.dot_general(q_ref[rows, cols], kbuf[krows, cols],
                                    (((1,), (1,)), ((), ())), preferred_element_type=F32)
                s = s - slope_dil[h] * distf + mask
                m = jnp.max(s, axis=1, keepdims=True)
                p = jnp.exp(s - m)
                l = jnp.sum(p, axis=1, keepdims=True)
                o = _dot(p.astype(BF16), vbuf[krows, cols]) / l
                outs.append(o)
                lses.append(jnp.broadcast_to(m + jnp.log(l), (Q_BLOCK, HEAD_DIM)))
            pc = slice(hp * 2 * HEAD_DIM, (hp + 1) * 2 * HEAD_DIM)
            o_ref[rows, pc] = jnp.concatenate(outs, axis=1).astype(BF16)
            lse_ref[rows, pc] = jnp.concatenate(lses, axis=1)


def _attention(q, kc, gi):
    win, dil = ATT_GROUPS[gi]
    assert win // dil == Q_BLOCK
    batch, _, l, _ = q.shape
    tc = min(l, 512)
    has_halo = l > Q_BLOCK
    slope_dil = tuple(float(s) * dil for s in _slopes()[gi])
    in_specs = [
        pl.BlockSpec((None, None, tc, GROUP_W), lambda b, r, c: (b, r, c, 0)),
        pl.BlockSpec((None, None, tc, 2 * GROUP_W), lambda b, r, c: (b, r, c, 0)),
    ]
    args = [q, kc]
    if has_halo:
        in_specs.append(pl.BlockSpec((None, None, Q_BLOCK, 2 * GROUP_W),
                                     lambda b, r, c: (b, r, jnp.maximum(c * (tc // Q_BLOCK) - 1, 0), 0)))
        args.append(kc)
    nk = tc + (Q_BLOCK if has_halo else 0)
    return pl.pallas_call(
        functools.partial(_attn_kernel, slope_dil=slope_dil, tc=tc, has_halo=has_halo),
        grid=(batch, dil, l // tc),
        in_specs=in_specs,
        out_specs=[pl.BlockSpec((None, None, tc, GROUP_W), lambda b, r, c: (b, r, c, 0))] * 2,
        out_shape=[jax.ShapeDtypeStruct((batch, dil, l, GROUP_W), BF16),
                   jax.ShapeDtypeStruct((batch, dil, l, GROUP_W), F32)],
        scratch_shapes=[pltpu.VMEM((nk, GROUP_W), BF16), pltpu.VMEM((nk, GROUP_W), BF16)],
        compiler_params=_params(("parallel", "parallel", "arbitrary")),
        name=f"attn_g{gi}",
    )(*args)


def _mix_tail(x, pool_o, att_o, nattn_ref, wgate_ref, wbp_ref, wba_ref, wo_ref, nffn_ref, xo_ref, xn_ref):
    gates = jax.nn.sigmoid(_dot(_rms(x, nattn_ref[...]).astype(BF16), wgate_ref[...]))
    hp = _dot(pool_o.astype(BF16), wbp_ref[...])
    ha = _dot(att_o.astype(BF16), wba_ref[...])
    t = gates[:, 0:D_MODEL] * hp + gates[:, D_MODEL:2 * D_MODEL] * ha
    xo = x + _dot(t.astype(BF16), wo_ref[...])
    xo_ref[...] = xo
    xn_ref[...] = _rms(xo, nffn_ref[...]).astype(BF16)


def _group_linear(m, wpool_ref, pscale_ref):
    parts = [_dot(m[:, gi * POOL_GROUP:(gi + 1) * POOL_GROUP].astype(BF16), wpool_ref[gi])
             for gi in range(len(POOL_WINDOWS))]
    return jnp.concatenate(parts, axis=1) * pscale_ref[...]


def _read_classes(src_ref, dil, il_ref, slot):
    if dil == 1:
        return src_ref[0].astype(F32)
    rows = src_ref.shape[1]
    for r in range(dil):
        v = src_ref[r].astype(F32)
        for c in range(LANE_CHUNKS):
            il_ref[slot, c, pl.ds(r, rows, stride=dil), :] = v[:, c * LANES:(c + 1) * LANES]
    return jnp.concatenate([il_ref[slot, c] for c in range(LANE_CHUNKS)], axis=1)


def _merge_kernel(x_ref, u_ref, uh_ref, o0_ref, o1_ref, o2_ref, l0_ref, l1_ref, l2_ref,
                  wpool_ref, pscale_ref, nattn_ref, wgate_ref, wbp_ref, wba_ref, wo_ref, nffn_ref,
                  xo_ref, xn_ref, ext_ref, il_ref, *, tm, dils):
    si = pl.program_id(1)
    halo = POOL_STATE + 1
    ext_ref[0:halo, :] = jnp.where(si == 0, 0.0, uh_ref[...])
    ext_ref[halo:halo + tm, :] = u_ref[...]
    pos = si * tm + lax.broadcasted_iota(jnp.int32, (tm, 1), 0)
    parts = []
    for gi, win in enumerate(POOL_WINDOWS):
        cols = slice(gi * POOL_GROUP, (gi + 1) * POOL_GROUP)
        own = ext_ref[halo:halo + tm, cols]
        acc = own
        for back in range(1, win):
            acc = acc + ext_ref[halo - back:halo - back + tm, cols]
        inv = 1.0 / jnp.minimum(pos + 1, win).astype(F32)
        parts.append(acc * inv - own)
    pool_o = _group_linear(jnp.concatenate(parts, axis=1), wpool_ref, pscale_ref)

    slot = 0
    os_, ls_ = [], []
    for g, (o_ref, l_ref) in enumerate(((o0_ref, l0_ref), (o1_ref, l1_ref), (o2_ref, l2_ref))):
        os_.append(_read_classes(o_ref, dils[g], il_ref, slot))
        slot += dils[g] > 1
        ls_.append(_read_classes(l_ref, dils[g], il_ref, slot))
        slot += dils[g] > 1
    mx = jnp.maximum(ls_[0], jnp.maximum(ls_[1], ls_[2]))
    es = [jnp.exp(v - mx) for v in ls_]
    att_o = (es[0] * os_[0] + es[1] * os_[1] + es[2] * os_[2]) / (es[0] + es[1] + es[2])
    _mix_tail(x_ref[...], pool_o, att_o, nattn_ref, wgate_ref, wbp_ref, wba_ref, wo_ref, nffn_ref, xo_ref, xn_ref)


def _wspec(shape, layer, grid_rank):
    nd = len(shape)
    return _resident((None,) + tuple(shape), lambda *_: (layer,) + (0,) * nd)


def _tail_weight_specs(layer, grid_rank):
    return [_wspec((1, D_MODEL), layer, grid_rank), _wspec((D_MODEL, 2 * D_MODEL), layer, grid_rank),
            _wspec((POOL_W, D_MODEL), layer, grid_rank), _wspec((GROUP_W, D_MODEL), layer, grid_rank),
            _wspec((D_MODEL, D_MODEL), layer, grid_rank), _wspec((1, D_MODEL), layer, grid_rank)]


def _tail_weights(w):
    return (w["norm_attn"], w["w_gate"], w["w_bp"], w["w_ba"], w["w_o"], w["norm_ffn"])


def _merge(x, u, outs, lses, layer, w, batch, seq, tm, dils):
    n = x.shape[0]
    n_tiles = seq // tm
    halo = POOL_STATE + 1
    row = lambda width: pl.BlockSpec((tm, width), lambda b, s: (b * n_tiles + s, 0))
    cls = lambda g: pl.BlockSpec((None, dils[g], tm // dils[g], GROUP_W), lambda b, s: (b, 0, s, 0))
    n_il = 2 * sum(d > 1 for d in dils)
    return pl.pallas_call(
        functools.partial(_merge_kernel, tm=tm, dils=dils),
        grid=(batch, n_tiles),
        in_specs=[row(D_MODEL), row(POOL_W),
                  pl.BlockSpec((halo, POOL_W),
                               lambda b, s: (jnp.maximum((b * n_tiles + s) * (tm // halo) - 1, 0), 0))]
                 + [cls(g) for g in range(N_GROUPS)] * 2
                 + [_wspec((len(POOL_WINDOWS), POOL_GROUP, POOL_GROUP), layer, 2), _wspec((1, POOL_W), layer, 2)]
                 + _tail_weight_specs(layer, 2),
        out_specs=[row(D_MODEL), row(D_MODEL)],
        out_shape=[jax.ShapeDtypeStruct((n, D_MODEL), F32), jax.ShapeDtypeStruct((n, D_MODEL), BF16)],
        scratch_shapes=[pltpu.VMEM((halo + tm, POOL_W), F32), pltpu.VMEM((n_il, LANE_CHUNKS, tm, LANES), F32)],
        compiler_params=_params(("parallel", "parallel")),
        name="merge",
    )(x, u, u, *outs, *lses, w["w_pool"], w["pool_scale"], *_tail_weights(w))


def _sample_bias(dec_seq, cache_rows):
    slopes = _slopes()
    nk = Q_BLOCK + 8
    bias = np.full((N_GROUPS, dec_seq, nk, GROUP_W), NEG_INF, np.float32)
    classes = np.zeros((N_GROUPS, dec_seq), np.int64)
    for gi, (win, dil) in enumerate(ATT_GROUPS):
        band = win // dil
        lc = cache_rows[gi]
        assert lc % dil == 0 and lc // dil == Q_BLOCK
        lane_slope = np.repeat(slopes[gi], HEAD_DIM)
        for t in range(dec_seq):
            cls = (lc + t) % dil
            classes[gi, t] = cls
            r0 = (lc + t - cls) // dil
            for r in range(Q_BLOCK):
                jj = r0 - r
                if 1 <= jj <= band:
                    bias[gi, t, r] = -lane_slope * float(jj * dil)
            for t2 in range(dec_seq):
                if t2 <= t and (t - t2) % dil == 0 and (t - t2) // dil <= band:
                    bias[gi, t, Q_BLOCK + t2] = -lane_slope * float(t - t2)
    return bias, classes


def _sample_mix_kernel(state_ref, u_ref, q0_ref, q1_ref, q2_ref, kn0_ref, kn1_ref, kn2_ref, c0_ref, c1_ref, c2_ref,
                       bias_ref, seg_ref, wpool_ref, pscale_ref,
                       pool_ref, att_ref, newpool_ref, ext_ref, m_ref, *, dec_seq, classes):
    kv_width = 2 * GROUP_W
    ext_ref[...] = jnp.zeros_like(ext_ref)
    ext_ref[0:POOL_STATE, :] = state_ref[...]
    ext_ref[POOL_STATE:POOL_STATE + dec_seq, :] = u_ref[...]
    newpool_ref[...] = ext_ref[dec_seq:dec_seq + POOL_STATE, :]

    m_ref[...] = jnp.zeros_like(m_ref)
    for t in range(dec_seq):
        row = POOL_STATE + t
        for gi, win in enumerate(POOL_WINDOWS):
            cols = slice(gi * POOL_GROUP, (gi + 1) * POOL_GROUP)
            tot = jnp.sum(ext_ref[row - win + 1:row + 1, cols], axis=0, keepdims=True)
            cnt = float(min(PAST_LEN + t + 1, win))
            m_ref[t:t + 1, cols] = tot / cnt - ext_ref[row:row + 1, cols]
    pool_ref[...] = _group_linear(m_ref[...], wpool_ref, pscale_ref)[0:dec_seq]

    pad = jnp.zeros((8 - dec_seq, GROUP_W), F32)
    groups = ((q0_ref, kn0_ref, c0_ref), (q1_ref, kn1_ref, c1_ref), (q2_ref, kn2_ref, c2_ref))
    for t in range(dec_seq):
        outs, lses = [], []
        for gi, (q_ref, kn_ref, c_ref) in enumerate(groups):
            base = int(classes[gi][t]) * kv_width
            keys = jnp.concatenate([c_ref[:, base:base + GROUP_W], kn_ref[:, 0:GROUP_W], pad], axis=0)
            vals = jnp.concatenate([c_ref[:, base + GROUP_W:base + kv_width], kn_ref[:, GROUP_W:kv_width], pad], axis=0)
            prod = keys.astype(BF16).astype(F32) * q_ref[t:t + 1, :].astype(F32)
            hi = prod.astype(BF16)
            lo = (prod - hi.astype(F32)).astype(BF16)
            s = _dot(hi, seg_ref[...]) + _dot(lo, seg_ref[...]) + bias_ref[gi, t]
            mx = jnp.max(s, axis=0, keepdims=True)
            p = jnp.exp(s - mx)
            l = jnp.sum(p, axis=0, keepdims=True)
            pn = (p / l).astype(BF16).astype(F32)
            outs.append(jnp.sum(pn * vals.astype(BF16).astype(F32), axis=0, keepdims=True))
            lses.append(mx + jnp.log(l))
        mx = jnp.maximum(lses[0], jnp.maximum(lses[1], lses[2]))
        es = [jnp.exp(v - mx) for v in lses]
        att_ref[t:t + 1, :] = (es[0] * outs[0] + es[1] * outs[1] + es[2] * outs[2]) / (es[0] + es[1] + es[2])


def _sample_mix(state_pool, u, qs, kvn, caches, layer, w, seg_ones, dec_batch, dec_seq):
    cache_rows = [c.shape[2] for c in caches]
    bias, classes = _sample_bias(dec_seq, cache_rows)
    cviews, cspecs = [], []
    for gi, (win, dil) in enumerate(ATT_GROUPS):
        lc = cache_rows[gi]
        cviews.append(caches[gi].reshape(-1, lc // dil, dil * 2 * GROUP_W))
        ncls = int(classes[gi].max()) + 1
        width = 2 * GROUP_W * min(dil, ncls)
        cspecs.append(pl.BlockSpec((None, lc // dil, width), lambda b: (layer * dec_batch + b, 0, 0)))
    per_tok = lambda width: pl.BlockSpec((None, dec_seq, width), lambda b: (b, 0, 0))
    const = lambda shape: _resident(shape, lambda b: (0,) * len(shape))
    return pl.pallas_call(
        functools.partial(_sample_mix_kernel, dec_seq=dec_seq, classes=tuple(map(tuple, classes))),
        grid=(dec_batch,),
        in_specs=[pl.BlockSpec((None, POOL_STATE, POOL_W), lambda b: (layer * dec_batch + b, 0, 0)),
                  per_tok(POOL_W)] + [per_tok(GROUP_W)] * 3 + [per_tok(2 * GROUP_W)] * 3 + cspecs
                 + [const(bias.shape), const((GROUP_W, GROUP_W)),
                    _wspec((len(POOL_WINDOWS), POOL_GROUP, POOL_GROUP), layer, 1), _wspec((1, POOL_W), layer, 1)],
        out_specs=[per_tok(POOL_W), per_tok(GROUP_W),
                   pl.BlockSpec((None, POOL_STATE, POOL_W), lambda b: (b, 0, 0))],
        out_shape=[jax.ShapeDtypeStruct((dec_batch, dec_seq, POOL_W), F32),
                   jax.ShapeDtypeStruct((dec_batch, dec_seq, GROUP_W), F32),
                   jax.ShapeDtypeStruct((dec_batch, POOL_STATE, POOL_W), F32)],
        scratch_shapes=[pltpu.VMEM((POOL_STATE + dec_seq + 5, POOL_W), F32), pltpu.VMEM((8, POOL_W), F32)],
        compiler_params=_params(("parallel",)),
        name="sample_mix",
    )(state_pool.reshape(-1, POOL_STATE, POOL_W), u.reshape(dec_batch, dec_seq, POOL_W),
      *[q.reshape(dec_batch, dec_seq, GROUP_W) for q in qs],
      *[k.reshape(dec_batch, dec_seq, 2 * GROUP_W) for k in kvn],
      *cviews, jnp.asarray(bias), seg_ones, w["w_pool"], w["pool_scale"])


def _tail_kernel(x_ref, pool_ref, att_ref, nattn_ref, wgate_ref, wbp_ref, wba_ref, wo_ref, nffn_ref, xo_ref, xn_ref):
    _mix_tail(x_ref[...], pool_ref[...], att_ref[...], nattn_ref, wgate_ref, wbp_ref, wba_ref, wo_ref, nffn_ref,
              xo_ref, xn_ref)


def _tail(x, pool_o, att_o, layer, w):
    n = x.shape[0]
    row = lambda width: pl.BlockSpec((n, width), lambda i: (0, 0))
    return pl.pallas_call(
        _tail_kernel,
        grid=(1,),
        in_specs=[row(D_MODEL), row(POOL_W), row(GROUP_W)] + _tail_weight_specs(layer, 1),
        out_specs=[row(D_MODEL), row(D_MODEL)],
        out_shape=[jax.ShapeDtypeStruct((n, D_MODEL), F32), jax.ShapeDtypeStruct((n, D_MODEL), BF16)],
        compiler_params=_params(("arbitrary",)),
        name="tail",
    )(x, pool_o, att_o, *_tail_weights(w))


def _swiglu_step(x, wg_ref, wu_ref, wd_ref, acc_ref):
    h = jax.nn.silu(_dot(x, wg_ref[...])) * _dot(x, wu_ref[...])
    acc_ref[...] += _dot(h.astype(BF16), wd_ref[...])


def _ffn_kernel(x_ref, xn_ref, w1_ref, w3_ref, w2_ref, y_ref, acc_ref):
    f = pl.program_id(1)

    @pl.when(f == 0)
    def _():
        acc_ref[...] = x_ref[...]

    _swiglu_step(xn_ref[...], w1_ref, w3_ref, w2_ref, acc_ref)

    @pl.when(f == pl.num_programs(1) - 1)
    def _():
        y_ref[...] = acc_ref[...]


def _ffn(x, xn, w1, w3, w2, idx, tm, tf):
    n = x.shape[0]
    d_ff = w1.shape[-1]
    return pl.pallas_call(
        _ffn_kernel,
        grid=(n // tm, d_ff // tf),
        in_specs=[pl.BlockSpec((tm, D_MODEL), lambda i, f: (i, 0)),
                  pl.BlockSpec((tm, D_MODEL), lambda i, f: (i, 0)),
                  pl.BlockSpec((None, D_MODEL, tf), lambda i, f: (idx, 0, f)),
                  pl.BlockSpec((None, D_MODEL, tf), lambda i, f: (idx, 0, f)),
                  pl.BlockSpec((None, tf, D_MODEL), lambda i, f: (idx, f, 0))],
        out_specs=pl.BlockSpec((tm, D_MODEL), lambda i, f: (i, 0)),
        out_shape=jax.ShapeDtypeStruct((n, D_MODEL), F32),
        scratch_shapes=[pltpu.VMEM((tm, D_MODEL), F32)],
        compiler_params=_params(("parallel", "arbitrary")),
        name="ffn",
    )(x, xn, w1, w3, w2)


def _router_kernel(x_ref, g_ref, rw_ref, rb_ref, route_ref, xn_ref):
    xn = _rms(x_ref[...], g_ref[...]).astype(BF16)
    xn_ref[...] = xn
    logits = _dot(xn, rw_ref[...]) + rb_ref[...]
    lane = lax.broadcasted_iota(jnp.int32, logits.shape, 1)
    neg = jnp.float32(-jnp.inf)
    logits = jnp.where(lane < N_EXPERTS, logits, neg)
    m1 = jnp.max(logits, axis=1, keepdims=True)
    i1 = jnp.min(jnp.where(logits == m1, lane, LANES), axis=1, keepdims=True)
    rest = jnp.where(lane == i1, neg, logits)
    m2 = jnp.max(rest, axis=1, keepdims=True)
    i2 = jnp.min(jnp.where(rest == m2, lane, LANES), axis=1, keepdims=True)
    e = jnp.exp(m2 - m1)
    g1 = 1.0 / (1.0 + e)
    g2 = e / (1.0 + e)
    route_ref[...] = jnp.where(lane == 0, i1.astype(F32),
                               jnp.where(lane == 1, i2.astype(F32),
                                         jnp.where(lane == 2, g1, jnp.where(lane == 3, g2, 0.0))))


def _router(x, norm_ffn, rw, rb, layer, idx, tm):
    n = x.shape[0]
    return pl.pallas_call(
        _router_kernel,
        grid=(n // tm,),
        in_specs=[pl.BlockSpec((tm, D_MODEL), lambda i: (i, 0)),
                  pl.BlockSpec((None, 1, D_MODEL), lambda i: (layer, 0, 0)),
                  pl.BlockSpec((None, D_MODEL, LANES), lambda i: (idx, 0, 0)),
                  pl.BlockSpec((None, 1, LANES), lambda i: (idx, 0, 0))],
        out_specs=[pl.BlockSpec((tm, LANES), lambda i: (i, 0)), pl.BlockSpec((tm, D_MODEL), lambda i: (i, 0))],
        out_shape=[jax.ShapeDtypeStruct((n, LANES), F32), jax.ShapeDtypeStruct((n, D_MODEL), BF16)],
        compiler_params=_params(("parallel",)),
        name="router",
    )(x, norm_ffn, rw, rb)


def _moe_kernel(blk_ref, xb_ref, wg_ref, wu_ref, wd_ref, yb_ref, acc_ref):
    del blk_ref
    f = pl.program_id(1)

    @pl.when(f == 0)
    def _():
        acc_ref[...] = jnp.zeros_like(acc_ref)

    _swiglu_step(xb_ref[...], wg_ref, wu_ref, wd_ref, acc_ref)

    @pl.when(f == pl.num_programs(1) - 1)
    def _():
        yb_ref[...] = acc_ref[...]


def _moe_rows_per_block(n_assign):
    r = 8
    while r < n_assign // N_EXPERTS and r < MOE_BLOCK_ROWS:
        r *= 2
    return r


def _moe(x, norm_ffn, rw, rb, wg, wu, wd, layer, idx, tm, tf):
    n = x.shape[0]
    route, xn = _router(x, norm_ffn, rw, rb, layer, idx, tm)
    top_e = route[:, 0:TOP_K].astype(jnp.int32)
    gate = route[:, TOP_K:2 * TOP_K]
    a = n * TOP_K
    rows = max(_moe_rows_per_block(a), 16)
    n_blocks = -(-(a + N_EXPERTS * (rows - 1)) // rows)
    e_flat = top_e.reshape(-1)
    order = jnp.argsort(e_flat)
    e_sorted = e_flat[order]
    tok_sorted = (order // TOP_K).astype(jnp.int32)
    g_sorted = gate.reshape(-1)[order]
    counts = jnp.bincount(e_flat, length=N_EXPERTS)
    padded = (counts + rows - 1) // rows * rows
    start = jnp.cumsum(counts) - counts
    pend = jnp.cumsum(padded)
    pstart = pend - padded
    dest = pstart[e_sorted] + jnp.arange(a) - start[e_sorted]
    row_tok = jnp.full((n_blocks * rows,), n, dtype=jnp.int32).at[dest].set(tok_sorted)
    xb = jnp.concatenate([xn, jnp.zeros((1, D_MODEL), xn.dtype)], axis=0)[row_tok]
    blk_e = jnp.minimum(jnp.sum(jnp.arange(n_blocks)[:, None] * rows >= pend[None, :], axis=1),
                        N_EXPERTS - 1).astype(jnp.int32)
    d_ff = wg.shape[-1]
    yb = pl.pallas_call(
        _moe_kernel,
        grid_spec=pltpu.PrefetchScalarGridSpec(
            num_scalar_prefetch=1,
            grid=(n_blocks, d_ff // tf),
            in_specs=[pl.BlockSpec((rows, D_MODEL), lambda i, f, blk: (i, 0)),
                      pl.BlockSpec((None, None, D_MODEL, tf), lambda i, f, blk: (idx, blk[i], 0, f)),
                      pl.BlockSpec((None, None, D_MODEL, tf), lambda i, f, blk: (idx, blk[i], 0, f)),
                      pl.BlockSpec((None, None, tf, D_MODEL), lambda i, f, blk: (idx, blk[i], f, 0))],
            out_specs=pl.BlockSpec((rows, D_MODEL), lambda i, f, blk: (i, 0)),
            scratch_shapes=[pltpu.VMEM((rows, D_MODEL), F32)]),
        out_shape=jax.ShapeDtypeStruct((n_blocks * rows, D_MODEL), F32),
        compiler_params=_params(("parallel", "arbitrary")),
        name="moe",
    )(blk_e, xb, wg, wu, wd)
    y = yb[dest] * g_sorted[:, None]
    return x + jnp.zeros((n, D_MODEL), F32).at[tok_sorted].add(y)


def kernel(x_prompt, x_sample, state_pool, cache_kv_g0, cache_kv_g1, cache_kv_g2, norm_attn, w_in, q_norm, k_norm, w_pool, pool_scale, w_branch_pool, w_branch_attn, w_out, norm_ffn, w1_dense, w3_dense, w2_dense, router_w, router_b, we_gate, we_up, we_down):
    batch, seq, _ = x_prompt.shape
    dec_batch, dec_seq, _ = x_sample.shape
    depth = w_in.shape[0]
    caches = (cache_kv_g0, cache_kv_g1, cache_kv_g2)
    n_p, n_s = batch * seq, dec_batch * dec_seq
    tm_p = 512
    dils = tuple(d for _, d in ATT_GROUPS)
    keeps = tuple(min(win, seq) for win, _ in ATT_GROUPS)

    head_of = np.arange(GROUP_W) // HEAD_DIM
    same_head = (head_of[:, None] == head_of[None, :]).astype(np.float32)
    seg_mean = jnp.asarray(same_head / HEAD_DIM, BF16)
    seg_ones = jnp.asarray(same_head, BF16)
    w_qkv = w_in[:, :, :PROJ_W].astype(BF16)
    qgain = (jnp.tile(q_norm, (1, HEADS)) * (HEAD_DIM ** -0.5)).reshape(depth, 1, GROUP_W)
    kgain = jnp.tile(k_norm, (1, HEADS)).reshape(depth, 1, GROUP_W)
    norm_attn3 = norm_attn.reshape(depth, 1, D_MODEL)
    w = {"w_pool": w_pool.astype(BF16), "pool_scale": pool_scale.reshape(depth, 1, POOL_W),
         "norm_attn": norm_attn3, "w_gate": w_in[:, :, PROJ_W:].astype(BF16),
         "w_bp": w_branch_pool.astype(BF16), "w_ba": w_branch_attn.astype(BF16), "w_o": w_out.astype(BF16),
         "norm_ffn": norm_ffn.reshape(depth, 1, D_MODEL)}
    w1_b, w3_b, w2_b = w1_dense.astype(BF16), w3_dense.astype(BF16), w2_dense.astype(BF16)
    wg_b, wu_b, wd_b = we_gate.astype(BF16), we_up.astype(BF16), we_down.astype(BF16)
    rw = jnp.pad(router_w, ((0, 0), (0, 0), (0, LANES - N_EXPERTS))).astype(BF16)
    rb = jnp.pad(router_b, ((0, 0), (0, LANES - N_EXPERTS))).reshape(-1, 1, LANES)

    xp = x_prompt.reshape(n_p, D_MODEL)
    xs = x_sample.reshape(n_s, D_MODEL)
    pool_p, pool_s = [], []
    kv_s = [[] for _ in ATT_GROUPS]
    kvo_p = ()
    ones = (1,) * N_GROUPS
    for layer in range(depth):
        res = _proj(xp, layer, norm_attn3, w_qkv, qgain, kgain, seg_mean, batch, seq, tm_p, dils, keeps,
                    layer, depth, kvo_p)
        u, qs, kcs, kvo_p = res[0], res[1:4], res[4:7], tuple(res[7:10])
        outs, lses = [], []
        for gi in range(N_GROUPS):
            o, lse = _attention(qs[gi], kcs[gi], gi)
            outs.append(o)
            lses.append(lse)
        pool_p.append(u.reshape(batch, seq, POOL_W)[:, seq - POOL_STATE:])
        xp, xpn = _merge(xp, u, outs, lses, layer, w, batch, seq, tm_p, dils)

        res = _proj(xs, layer, norm_attn3, w_qkv, qgain, kgain, seg_mean, 1, n_s, n_s, ones, (n_s,) * N_GROUPS,
                    0, 1, ())
        u, qs, kvn = res[0], res[1:4], res[7:10]
        pool_o, att_o, new_pool = _sample_mix(state_pool, u, qs, kvn, caches, layer, w, seg_ones, dec_batch, dec_seq)
        for gi in range(N_GROUPS):
            kv_s[gi].append(kvn[gi].reshape(dec_batch, dec_seq, 2, HEADS, HEAD_DIM))
        pool_s.append(new_pool)
        xs, xsn = _tail(xs, pool_o.reshape(n_s, POOL_W), att_o.reshape(n_s, GROUP_W), layer, w)

        i = layer // 2
        if layer % 2 == 0:
            xp = _ffn(xp, xpn, w1_b, w3_b, w2_b, i, 1024, 256)
            xs = _ffn(xs, xsn, w1_b, w3_b, w2_b, i, n_s, 256)
        else:
            xp = _moe(xp, w["norm_ffn"], rw, rb, wg_b, wu_b, wd_b, layer, i, 512, 512)
            xs = _moe(xs, w["norm_ffn"], rw, rb, wg_b, wu_b, wd_b, layer, i, n_s, 512)
    kv_p = [kvo_p[g].reshape(depth, batch, keeps[g], 2, HEADS, HEAD_DIM) for g in range(N_GROUPS)]
    return (xp.reshape(batch, seq, D_MODEL), xs.reshape(dec_batch, dec_seq, D_MODEL),
            jnp.stack(pool_p), jnp.stack(pool_s),
            kv_p[0], jnp.stack(kv_s[0]),
            kv_p[1], jnp.stack(kv_s[1]),
            kv_p[2], jnp.stack(kv_s[2]))
```

```python
import functools

import numpy as np
import jax
import jax.numpy as jnp
from jax import lax
from jax.experimental import pallas as pl
from jax.experimental.pallas import tpu as pltpu

F32 = jnp.float32
BF16 = jnp.bfloat16

D_MODEL = 1024
PAST_LEN = 16384
POOL_WINDOWS = (2, 4, 8, 16)
POOL_GROUP = 128
POOL_W = 512
POOL_STATE = 15
ATT_GROUPS = ((128, 1), (512, 4), (2048, 16))
N_GROUPS = len(ATT_GROUPS)
HEAD_DIM = 64
HEADS = 8
GROUP_W = 512
QKV_W = 1536
Q_BLOCK = 128
ALIBI_MAX = 8.0
N_EXPERTS = 8
TOP_K = 2
RMS_EPS = 1e-6
NEG_INF = -1e30
LANES = 128
LANE_CHUNKS = GROUP_W // LANES
PROJ_W = POOL_W + 3 * QKV_W
VMEM_LIMIT = 56 * 1024 * 1024


def _slopes():
    i = np.arange(1, N_GROUPS * HEADS + 1, dtype=np.float32)
    return np.exp2(-ALIBI_MAX * i / (N_GROUPS * HEADS)).astype(np.float32).reshape(N_GROUPS, HEADS)


def _params(sem):
    return pltpu.CompilerParams(dimension_semantics=sem, vmem_limit_bytes=VMEM_LIMIT)


def _rms(x, gain):
    return x * lax.rsqrt(jnp.mean(x * x, axis=-1, keepdims=True) + RMS_EPS) * gain


def _dot(a, b):
    return jnp.dot(a, b, preferred_element_type=F32)


def _resident(shape, index_map):
    return pl.BlockSpec(shape, index_map, pipeline_mode=pl.Buffered(1))


def _write_classes(dst_ref, col0, val, dil, tmp_ref, slot):
    rows = val.shape[0] // dil
    cols = slice(col0, col0 + GROUP_W)
    if dil == 1:
        dst_ref[0, :, cols] = val.astype(dst_ref.dtype)
        return
    for c in range(LANE_CHUNKS):
        tmp_ref[slot, c] = val[:, c * LANES:(c + 1) * LANES]
    for r in range(dil):
        picked = [tmp_ref[slot, c, pl.ds(r, rows, stride=dil), :] for c in range(LANE_CHUNKS)]
        dst_ref[r, :, cols] = jnp.concatenate(picked, axis=1).astype(dst_ref.dtype)


def _proj_kernel(x_ref, g_ref, w_ref, qg_ref, kg_ref, seg_ref, *rest, tm, dils, keeps, n_alias, norm_terms):
    u_ref, q0_ref, q1_ref, q2_ref, kc0_ref, kc1_ref, kc2_ref, kvo0_ref, kvo1_ref, kvo2_ref, tmp_ref = rest[n_alias:]
    s = pl.program_id(1)
    n_tiles = pl.num_programs(1)
    xn = _rms(x_ref[...], g_ref[...]).astype(BF16)

    def zblk(j):
        return _dot(xn, w_ref[:, j * GROUP_W:(j + 1) * GROUP_W])

    def headnorm(z, gain):
        rem = z * z
        ms = None
        for _ in range(norm_terms):
            part = rem.astype(BF16)
            rem = rem - part.astype(F32)
            ms = _dot(part, seg_ref[...]) if ms is None else ms + _dot(part, seg_ref[...])
        return z * lax.rsqrt(ms + RMS_EPS) * gain

    u_ref[...] = zblk(0)
    q_refs = (q0_ref, q1_ref, q2_ref)
    kc_refs = (kc0_ref, kc1_ref, kc2_ref)
    kvo_refs = (kvo0_ref, kvo1_ref, kvo2_ref)
    slot = 0
    for g in range(N_GROUPS):
        q = headnorm(zblk(1 + g), qg_ref[...])
        k = headnorm(zblk(1 + N_GROUPS + g), kg_ref[...])
        v = zblk(1 + 2 * N_GROUPS + g)
        for dst, col0, val in ((q_refs[g], 0, q), (kc_refs[g], 0, k), (kc_refs[g], GROUP_W, v)):
            _write_classes(dst, col0, val, dils[g], tmp_ref, slot % tmp_ref.shape[0])
            slot += dils[g] > 1
        keep = keeps[g]
        if keep >= tm:
            first = n_tiles - keep // tm

            @pl.when(s >= first)
            def _(k=k, v=v, ref=kvo_refs[g]):
                ref[:, 0:GROUP_W] = k
                ref[:, GROUP_W:2 * GROUP_W] = v
        else:
            @pl.when(s == n_tiles - 1)
            def _(k=k, v=v, ref=kvo_refs[g], keep=keep):
                ref[:, 0:GROUP_W] = k[tm - keep:, :]
                ref[:, GROUP_W:2 * GROUP_W] = v[tm - keep:, :]


def _proj(x, layer, norm_attn, w_qkv, qgain, kgain, seg, batch, seq, tm, dils, keeps, out_layer, out_depth, prev_kvo,
          norm_terms):
    n_tiles = seq // tm
    kvo_specs, kvo_shapes = [], []
    for g in range(N_GROUPS):
        keep = keeps[g]
        assert keep % tm == 0 or (keep < tm and keep % 8 == 0)
        if keep >= tm:
            first = n_tiles - keep // tm
            kvo_specs.append(pl.BlockSpec((None, None, tm, 2 * GROUP_W),
                                          lambda b, s, first=first: (out_layer, b, jnp.maximum(s - first, 0), 0)))
        else:
            kvo_specs.append(pl.BlockSpec((None, None, keep, 2 * GROUP_W), lambda b, s: (out_layer, b, 0, 0)))
        kvo_shapes.append(jax.ShapeDtypeStruct((out_depth, batch, keep, 2 * GROUP_W), F32))
    cls_spec = lambda g, width: pl.BlockSpec((None, dils[g], tm // dils[g], width), lambda b, s: (b, 0, s, 0))
    cls_shape = lambda g, width: jax.ShapeDtypeStruct((batch, dils[g], seq // dils[g], width), BF16)
    n_alias = len(prev_kvo)
    n_in = 6
    return pl.pallas_call(
        functools.partial(_proj_kernel, tm=tm, dils=dils, keeps=keeps, n_alias=n_alias,
                          norm_terms=norm_terms),
        grid=(batch, n_tiles),
        in_specs=[
            pl.BlockSpec((tm, D_MODEL), lambda b, s: (b * n_tiles + s, 0)),
            _resident((None, 1, D_MODEL), lambda b, s: (layer, 0, 0)),
            _resident((None, D_MODEL, PROJ_W), lambda b, s: (layer, 0, 0)),
            _resident((None, 1, GROUP_W), lambda b, s: (layer, 0, 0)),
            _resident((None, 1, GROUP_W), lambda b, s: (layer, 0, 0)),
            _resident((GROUP_W, GROUP_W), lambda b, s: (0, 0)),
        ] + [pl.BlockSpec(memory_space=pl.ANY)] * n_alias,
        out_specs=[pl.BlockSpec((tm, POOL_W), lambda b, s: (b * n_tiles + s, 0))]
                  + [cls_spec(g, GROUP_W) for g in range(N_GROUPS)]
                  + [cls_spec(g, 2 * GROUP_W) for g in range(N_GROUPS)] + kvo_specs,
        out_shape=[jax.ShapeDtypeStruct((batch * seq, POOL_W), F32)]
                  + [cls_shape(g, GROUP_W) for g in range(N_GROUPS)]
                  + [cls_shape(g, 2 * GROUP_W) for g in range(N_GROUPS)] + kvo_shapes,
        scratch_shapes=[pltpu.VMEM((3, LANE_CHUNKS, tm, LANES), F32)],
        input_output_aliases={n_in + g: 1 + 2 * N_GROUPS + g for g in range(n_alias)},
        compiler_params=_params(("parallel", "arbitrary")),
        name="proj",
    )(x, norm_attn, w_qkv, qgain, kgain, seg, *prev_kvo)


def _attn_kernel(q_ref, kvc_ref, *rest, slope_dil, tc, has_halo):
    if has_halo:
        kvh_ref, o_ref, lse_ref, kbuf, vbuf = rest
    else:
        o_ref, lse_ref, kbuf, vbuf = rest
    off = Q_BLOCK if has_halo else 0
    if has_halo:
        kbuf[0:Q_BLOCK, :] = kvh_ref[:, 0:GROUP_W]
        vbuf[0:Q_BLOCK, :] = kvh_ref[:, GROUP_W:2 * GROUP_W]
    kbuf[off:off + tc, :] = kvc_ref[:, 0:GROUP_W]
    vbuf[off:off + tc, :] = kvc_ref[:, GROUP_W:2 * GROUP_W]

    nk = Q_BLOCK + off
    qi = lax.broadcasted_iota(jnp.int32, (Q_BLOCK, nk), 0)
    cj = lax.broadcasted_iota(jnp.int32, (Q_BLOCK, nk), 1)
    dist = qi - cj + off
    distf = dist.astype(F32)
    maskneg = jnp.where((dist >= 0) & (dist <= Q_BLOCK), 0.0, NEG_INF).astype(F32)
    if has_halo:
        first = jnp.where(pl.program_id(2) == 0, NEG_INF, 0.0).astype(F32)
        mask_first = maskneg + jnp.where(cj < Q_BLOCK, first, 0.0)

    for i in range(tc // Q_BLOCK):
        mask = mask_first if (has_halo and i == 0) else maskneg
        rows = slice(i * Q_BLOCK, (i + 1) * Q_BLOCK)
        krows = slice(i * Q_BLOCK, i * Q_BLOCK + nk)
        for hp in range(HEADS // 2):
            outs, lses = [], []
            for h in (2 * hp, 2 * hp + 1):
                cols = slice(h * HEAD_DIM, (h + 1) * HEAD_DIM)
                s = lax.dot_general(q_ref[rows, cols], kbuf[krows, cols],
                                    (((1,), (1,)), ((), ())), preferred_element_type=F32)
                s = s - slope_dil[h] * distf + mask
                m = jnp.max(s, axis=1, keepdims=True)
                p = jnp.exp(s - m)
                l = jnp.sum(p, axis=1, keepdims=True)
                o = _dot(p.astype(BF16), vbuf[krows, cols]) / l
                outs.append(o)
                lses.append(jnp.broadcast_to(m + jnp.log(l), (Q_BLOCK, HEAD_DIM)))
            pc = slice(hp * 2 * HEAD_DIM, (hp + 1) * 2 * HEAD_DIM)
            o_ref[rows, pc] = jnp.concatenate(outs, axis=1).astype(BF16)
            lse_ref[rows, pc] = jnp.concatenate(lses, axis=1)


def _attention(q, kc, gi):
    win, dil = ATT_GROUPS[gi]
    assert win // dil == Q_BLOCK
    batch, _, l, _ = q.shape
    tc = min(l, 512)
    has_halo = l > Q_BLOCK
    slope_dil = tuple(float(s) * dil for s in _slopes()[gi])
    in_specs = [
        pl.BlockSpec((None, None, tc, GROUP_W), lambda b, r, c: (b, r, c, 0)),
        pl.BlockSpec((None, None, tc, 2 * GROUP_W), lambda b, r, c: (b, r, c, 0)),
    ]
    args = [q, kc]
    if has_halo:
        in_specs.append(pl.BlockSpec((None, None, Q_BLOCK, 2 * GROUP_W),
                                     lambda b, r, c: (b, r, jnp.maximum(c * (tc // Q_BLOCK) - 1, 0), 0)))
        args.append(kc)
    nk = tc + (Q_BLOCK if has_halo else 0)
    return pl.pallas_call(
        functools.partial(_attn_kernel, slope_dil=slope_dil, tc=tc, has_halo=has_halo),
        grid=(batch, dil, l // tc),
        in_specs=in_specs,
        out_specs=[pl.BlockSpec((None, None, tc, GROUP_W), lambda b, r, c: (b, r, c, 0))] * 2,
        out_shape=[jax.ShapeDtypeStruct((batch, dil, l, GROUP_W), BF16),
                   jax.ShapeDtypeStruct((batch, dil, l, GROUP_W), F32)],
        scratch_shapes=[pltpu.VMEM((nk, GROUP_W), BF16), pltpu.VMEM((nk, GROUP_W), BF16)],
        compiler_params=_params(("parallel", "parallel", "arbitrary")),
        name=f"attn_g{gi}",
    )(*args)


def _mix_tail(x, pool_o, att_o, nattn_ref, wgate_ref, wbp_ref, wba_ref, wo_ref, nffn_ref, xo_ref, xn_ref):
    gates = jax.nn.sigmoid(_dot(_rms(x, nattn_ref[...]).astype(BF16), wgate_ref[...]))
    hp = _dot(pool_o.astype(BF16), wbp_ref[...])
    ha = _dot(att_o.astype(BF16), wba_ref[...])
    t = gates[:, 0:D_MODEL] * hp + gates[:, D_MODEL:2 * D_MODEL] * ha
    xo = x + _dot(t.astype(BF16), wo_ref[...])
    xo_ref[...] = xo
    xn_ref[...] = _rms(xo, nffn_ref[...]).astype(BF16)


def _group_linear(m, wpool_ref, pscale_ref):
    parts = [_dot(m[:, gi * POOL_GROUP:(gi + 1) * POOL_GROUP].astype(BF16), wpool_ref[gi])
             for gi in range(len(POOL_WINDOWS))]
    return jnp.concatenate(parts, axis=1) * pscale_ref[...]


def _read_classes(src_ref, dil, il_ref, slot):
    if dil == 1:
        return src_ref[0].astype(F32)
    rows = src_ref.shape[1]
    for r in range(dil):
        v = src_ref[r].astype(F32)
        for c in range(LANE_CHUNKS):
            il_ref[slot, c, pl.ds(r, rows, stride=dil), :] = v[:, c * LANES:(c + 1) * LANES]
    return jnp.concatenate([il_ref[slot, c] for c in range(LANE_CHUNKS)], axis=1)


def _merge_kernel(x_ref, u_ref, uh_ref, o0_ref, o1_ref, o2_ref, l0_ref, l1_ref, l2_ref,
                  wpool_ref, pscale_ref, nattn_ref, wgate_ref, wbp_ref, wba_ref, wo_ref, nffn_ref,
                  xo_ref, xn_ref, ext_ref, il_ref, *, tm, dils):
    si = pl.program_id(1)
    halo = POOL_STATE + 1
    ext_ref[0:halo, :] = jnp.where(si == 0, 0.0, uh_ref[...])
    ext_ref[halo:halo + tm, :] = u_ref[...]
    pos = si * tm + lax.broadcasted_iota(jnp.int32, (tm, 1), 0)
    parts = []
    for gi, win in enumerate(POOL_WINDOWS):
        cols = slice(gi * POOL_GROUP, (gi + 1) * POOL_GROUP)
        own = ext_ref[halo:halo + tm, cols]
        acc = own
        for back in range(1, win):
            acc = acc + ext_ref[halo - back:halo - back + tm, cols]
        inv = 1.0 / jnp.minimum(pos + 1, win).astype(F32)
        parts.append(acc * inv - own)
    pool_o = _group_linear(jnp.concatenate(parts, axis=1), wpool_ref, pscale_ref)

    slot = 0
    os_, ls_ = [], []
    for g, (o_ref, l_ref) in enumerate(((o0_ref, l0_ref), (o1_ref, l1_ref), (o2_ref, l2_ref))):
        os_.append(_read_classes(o_ref, dils[g], il_ref, slot))
        slot += dils[g] > 1
        ls_.append(_read_classes(l_ref, dils[g], il_ref, slot))
        slot += dils[g] > 1
    mx = jnp.maximum(ls_[0], jnp.maximum(ls_[1], ls_[2]))
    es = [jnp.exp(v - mx) for v in ls_]
    att_o = (es[0] * os_[0] + es[1] * os_[1] + es[2] * os_[2]) / (es[0] + es[1] + es[2])
    _mix_tail(x_ref[...], pool_o, att_o, nattn_ref, wgate_ref, wbp_ref, wba_ref, wo_ref, nffn_ref, xo_ref, xn_ref)


def _wspec(shape, layer, grid_rank):
    nd = len(shape)
    return _resident((None,) + tuple(shape), lambda *_: (layer,) + (0,) * nd)


def _tail_weight_specs(layer, grid_rank):
    return [_wspec((1, D_MODEL), layer, grid_rank), _wspec((D_MODEL, 2 * D_MODEL), layer, grid_rank),
            _wspec((POOL_W, D_MODEL), layer, grid_rank), _wspec((GROUP_W, D_MODEL), layer, grid_rank),
            _wspec((D_MODEL, D_MODEL), layer, grid_rank), _wspec((1, D_MODEL), layer, grid_rank)]


def _tail_weights(w):
    return (w["norm_attn"], w["w_gate"], w["w_bp"], w["w_ba"], w["w_o"], w["norm_ffn"])


def _merge(x, u, outs, lses, layer, w, batch, seq, tm, dils):
    n = x.shape[0]
    n_tiles = seq // tm
    halo = POOL_STATE + 1
    row = lambda width: pl.BlockSpec((tm, width), lambda b, s: (b * n_tiles + s, 0))
    cls = lambda g: pl.BlockSpec((None, dils[g], tm // dils[g], GROUP_W), lambda b, s: (b, 0, s, 0))
    n_il = 2 * sum(d > 1 for d in dils)
    return pl.pallas_call(
        functools.partial(_merge_kernel, tm=tm, dils=dils),
        grid=(batch, n_tiles),
        in_specs=[row(D_MODEL), row(POOL_W),
                  pl.BlockSpec((halo, POOL_W),
                               lambda b, s: (jnp.maximum((b * n_tiles + s) * (tm // halo) - 1, 0), 0))]
                 + [cls(g) for g in range(N_GROUPS)] * 2
                 + [_wspec((len(POOL_WINDOWS), POOL_GROUP, POOL_GROUP), layer, 2), _wspec((1, POOL_W), layer, 2)]
                 + _tail_weight_specs(layer, 2),
        out_specs=[row(D_MODEL), row(D_MODEL)],
        out_shape=[jax.ShapeDtypeStruct((n, D_MODEL), F32), jax.ShapeDtypeStruct((n, D_MODEL), BF16)],
        scratch_shapes=[pltpu.VMEM((halo + tm, POOL_W), F32), pltpu.VMEM((n_il, LANE_CHUNKS, tm, LANES), F32)],
        compiler_params=_params(("parallel", "parallel")),
        name="merge",
    )(x, u, u, *outs, *lses, w["w_pool"], w["pool_scale"], *_tail_weights(w))


def _sample_bias(dec_seq, cache_rows):
    slopes = _slopes()
    nk = Q_BLOCK + 8
    bias = np.full((N_GROUPS, dec_seq, nk, GROUP_W), NEG_INF, np.float32)
    classes = np.zeros((N_GROUPS, dec_seq), np.int64)
    for gi, (win, dil) in enumerate(ATT_GROUPS):
        band = win // dil
        lc = cache_rows[gi]
        assert lc % dil == 0 and lc // dil == Q_BLOCK
        lane_slope = np.repeat(slopes[gi], HEAD_DIM)
        for t in range(dec_seq):
            cls = (lc + t) % dil
            classes[gi, t] = cls
            r0 = (lc + t - cls) // dil
            for r in range(Q_BLOCK):
                jj = r0 - r
                if 1 <= jj <= band:
                    bias[gi, t, r] = -lane_slope * float(jj * dil)
            for t2 in range(dec_seq):
                if t2 <= t and (t - t2) % dil == 0 and (t - t2) // dil <= band:
                    bias[gi, t, Q_BLOCK + t2] = -lane_slope * float(t - t2)
    return bias, classes


def _sample_mix_kernel(state_ref, u_ref, q0_ref, q1_ref, q2_ref, kn0_ref, kn1_ref, kn2_ref, c0_ref, c1_ref, c2_ref,
                       bias_ref, seg_ref, wpool_ref, pscale_ref,
                       pool_ref, att_ref, newpool_ref, ext_ref, m_ref, *, dec_seq, classes):
    kv_width = 2 * GROUP_W
    ext_ref[...] = jnp.zeros_like(ext_ref)
    ext_ref[0:POOL_STATE, :] = state_ref[...]
    ext_ref[POOL_STATE:POOL_STATE + dec_seq, :] = u_ref[...]
    newpool_ref[...] = ext_ref[dec_seq:dec_seq + POOL_STATE, :]

    m_ref[...] = jnp.zeros_like(m_ref)
    for t in range(dec_seq):
        row = POOL_STATE + t
        for gi, win in enumerate(POOL_WINDOWS):
            cols = slice(gi * POOL_GROUP, (gi + 1) * POOL_GROUP)
            tot = jnp.sum(ext_ref[row - win + 1:row + 1, cols], axis=0, keepdims=True)
            cnt = float(min(PAST_LEN + t + 1, win))
            m_ref[t:t + 1, cols] = tot / cnt - ext_ref[row:row + 1, cols]
    pool_ref[...] = _group_linear(m_ref[...], wpool_ref, pscale_ref)[0:dec_seq]

    pad = jnp.zeros((8 - dec_seq, GROUP_W), F32)
    groups = ((q0_ref, kn0_ref, c0_ref), (q1_ref, kn1_ref, c1_ref), (q2_ref, kn2_ref, c2_ref))
    for t in range(dec_seq):
        outs, lses = [], []
        for gi, (q_ref, kn_ref, c_ref) in enumerate(groups):
            base = int(classes[gi][t]) * kv_width
            keys = jnp.concatenate([c_ref[:, base:base + GROUP_W], kn_ref[:, 0:GROUP_W], pad], axis=0)
            vals = jnp.concatenate([c_ref[:, base + GROUP_W:base + kv_width], kn_ref[:, GROUP_W:kv_width], pad], axis=0)
            prod = keys.astype(BF16).astype(F32) * q_ref[t:t + 1, :].astype(F32)
            hi = prod.astype(BF16)
            lo = (prod - hi.astype(F32)).astype(BF16)
            s = _dot(hi, seg_ref[...]) + _dot(lo, seg_ref[...]) + bias_ref[gi, t]
            mx = jnp.max(s, axis=0, keepdims=True)
            p = jnp.exp(s - mx)
            l = jnp.sum(p, axis=0, keepdims=True)
            pn = (p / l).astype(BF16).astype(F32)
            outs.append(jnp.sum(pn * vals.astype(BF16).astype(F32), axis=0, keepdims=True))
            lses.append(mx + jnp.log(l))
        mx = jnp.maximum(lses[0], jnp.maximum(lses[1], lses[2]))
        es = [jnp.exp(v - mx) for v in lses]
        att_ref[t:t + 1, :] = (es[0] * outs[0] + es[1] * outs[1] + es[2] * outs[2]) / (es[0] + es[1] + es[2])


def _sample_mix(state_pool, u, qs, kvn, caches, layer, w, seg_ones, dec_batch, dec_seq):
    cache_rows = [c.shape[2] for c in caches]
    bias, classes = _sample_bias(dec_seq, cache_rows)
    cviews, cspecs = [], []
    for gi, (win, dil) in enumerate(ATT_GROUPS):
        lc = cache_rows[gi]
        cviews.append(caches[gi].reshape(-1, lc // dil, dil * 2 * GROUP_W))
        ncls = int(classes[gi].max()) + 1
        width = 2 * GROUP_W * min(dil, ncls)
        cspecs.append(pl.BlockSpec((None, lc // dil, width), lambda b: (layer * dec_batch + b, 0, 0)))
    per_tok = lambda width: pl.BlockSpec((None, dec_seq, width), lambda b: (b, 0, 0))
    const = lambda shape: _resident(shape, lambda b: (0,) * len(shape))
    return pl.pallas_call(
        functools.partial(_sample_mix_kernel, dec_seq=dec_seq, classes=tuple(map(tuple, classes))),
        grid=(dec_batch,),
        in_specs=[pl.BlockSpec((None, POOL_STATE, POOL_W), lambda b: (layer * dec_batch + b, 0, 0)),
                  per_tok(POOL_W)] + [per_tok(GROUP_W)] * 3 + [per_tok(2 * GROUP_W)] * 3 + cspecs
                 + [const(bias.shape), const((GROUP_W, GROUP_W)),
                    _wspec((len(POOL_WINDOWS), POOL_GROUP, POOL_GROUP), layer, 1), _wspec((1, POOL_W), layer, 1)],
        out_specs=[per_tok(POOL_W), per_tok(GROUP_W),
                   pl.BlockSpec((None, POOL_STATE, POOL_W), lambda b: (b, 0, 0))],
        out_shape=[jax.ShapeDtypeStruct((dec_batch, dec_seq, POOL_W), F32),
                   jax.ShapeDtypeStruct((dec_batch, dec_seq, GROUP_W), F32),
                   jax.ShapeDtypeStruct((dec_batch, POOL_STATE, POOL_W), F32)],
        scratch_shapes=[pltpu.VMEM((POOL_STATE + dec_seq + 5, POOL_W), F32), pltpu.VMEM((8, POOL_W), F32)],
        compiler_params=_params(("parallel",)),
        name="sample_mix",
    )(state_pool.reshape(-1, POOL_STATE, POOL_W), u.reshape(dec_batch, dec_seq, POOL_W),
      *[q.reshape(dec_batch, dec_seq, GROUP_W) for q in qs],
      *[k.reshape(dec_batch, dec_seq, 2 * GROUP_W) for k in kvn],
      *cviews, jnp.asarray(bias), seg_ones, w["w_pool"], w["pool_scale"])


def _tail_kernel(x_ref, pool_ref, att_ref, nattn_ref, wgate_ref, wbp_ref, wba_ref, wo_ref, nffn_ref, xo_ref, xn_ref):
    _mix_tail(x_ref[...], pool_ref[...], att_ref[...], nattn_ref, wgate_ref, wbp_ref, wba_ref, wo_ref, nffn_ref,
              xo_ref, xn_ref)


def _tail(x, pool_o, att_o, layer, w):
    n = x.shape[0]
    row = lambda width: pl.BlockSpec((n, width), lambda i: (0, 0))
    return pl.pallas_call(
        _tail_kernel,
        grid=(1,),
        in_specs=[row(D_MODEL), row(POOL_W), row(GROUP_W)] + _tail_weight_specs(layer, 1),
        out_specs=[row(D_MODEL), row(D_MODEL)],
        out_shape=[jax.ShapeDtypeStruct((n, D_MODEL), F32), jax.ShapeDtypeStruct((n, D_MODEL), BF16)],
        compiler_params=_params(("arbitrary",)),
        name="tail",
    )(x, pool_o, att_o, *_tail_weights(w))


def _swiglu_step(x, wg_ref, wu_ref, wd_ref, acc_ref):
    h = jax.nn.silu(_dot(x, wg_ref[...])) * _dot(x, wu_ref[...])
    acc_ref[...] += _dot(h.astype(BF16), wd_ref[...])


def _ffn_kernel(x_ref, xn_ref, w1_ref, w3_ref, w2_ref, y_ref, acc_ref):
    f = pl.program_id(1)

    @pl.when(f == 0)
    def _():
        acc_ref[...] = x_ref[...]

    _swiglu_step(xn_ref[...], w1_ref, w3_ref, w2_ref, acc_ref)

    @pl.when(f == pl.num_programs(1) - 1)
    def _():
        y_ref[...] = acc_ref[...]


def _ffn(x, xn, w1, w3, w2, idx, tm, tf):
    n = x.shape[0]
    d_ff = w1.shape[-1]
    return pl.pallas_call(
        _ffn_kernel,
        grid=(n // tm, d_ff // tf),
        in_specs=[pl.BlockSpec((tm, D_MODEL), lambda i, f: (i, 0)),
                  pl.BlockSpec((tm, D_MODEL), lambda i, f: (i, 0)),
                  pl.BlockSpec((None, D_MODEL, tf), lambda i, f: (idx, 0, f)),
                  pl.BlockSpec((None, D_MODEL, tf), lambda i, f: (idx, 0, f)),
                  pl.BlockSpec((None, tf, D_MODEL), lambda i, f: (idx, f, 0))],
        out_specs=pl.BlockSpec((tm, D_MODEL), lambda i, f: (i, 0)),
        out_shape=jax.ShapeDtypeStruct((n, D_MODEL), F32),
        scratch_shapes=[pltpu.VMEM((tm, D_MODEL), F32)],
        compiler_params=_params(("parallel", "arbitrary")),
        name="ffn",
    )(x, xn, w1, w3, w2)


SEG_ALIGN = 16
MOE_ROWS = 512


def _sorted_cap(ts):
    need = TOP_K * ts + N_EXPERTS * (SEG_ALIGN - 1)
    return -(-need // LANES) * LANES if ts >= LANES * 2 else -(-need // SEG_ALIGN) * SEG_ALIGN


def _route_kernel(x_ref, g_ref, rwt_ref, rb_ref, tri_ref, xs_ref, meta_ref, seg_ref, *, ts, cap):
    xn = _rms(x_ref[...], g_ref[...]).astype(BF16)
    logits = lax.dot_general(rwt_ref[...], xn, (((1,), (1,)), ((), ())), preferred_element_type=F32) + rb_ref[...]
    row = lax.broadcasted_iota(jnp.int32, (N_EXPERTS, ts), 0)
    neg = jnp.float32(-jnp.inf)
    m1 = jnp.max(logits, axis=0, keepdims=True)
    i1 = jnp.min(jnp.where(logits == m1, row, N_EXPERTS), axis=0, keepdims=True)
    rest = jnp.where(row == i1, neg, logits)
    m2 = jnp.max(rest, axis=0, keepdims=True)
    i2 = jnp.min(jnp.where(rest == m2, row, N_EXPERTS), axis=0, keepdims=True)
    e = jnp.exp(m2 - m1)
    g1 = 1.0 / (1.0 + e)
    g2 = e / (1.0 + e)
    oh1 = jnp.where(row == i1, 1.0, 0.0)
    oh2 = jnp.where(row == i2, 1.0, 0.0)
    pre1 = _dot(oh1.astype(BF16), tri_ref[...])
    pre2 = _dot(oh2.astype(BF16), tri_ref[...])
    cnt1 = jnp.sum(oh1, axis=1, keepdims=True)
    cnt = cnt1 + jnp.sum(oh2, axis=1, keepdims=True)
    cpad = jnp.floor((cnt + (SEG_ALIGN - 1)) * (1.0 / SEG_ALIGN)) * SEG_ALIGN
    offs = [jnp.zeros((1, 1), F32)]
    for ei in range(1, N_EXPERTS):
        offs.append(offs[-1] + cpad[ei - 1:ei, :])
    off = jnp.concatenate(offs, axis=0)
    d1 = jnp.sum(oh1 * (off + pre1), axis=0, keepdims=True)
    d2 = jnp.sum(oh2 * (off + cnt1 + pre2), axis=0, keepdims=True)
    slot = lax.broadcasted_iota(jnp.int32, (cap, ts), 0).astype(F32)
    p = jnp.where(slot == d1, 1.0, jnp.where(slot == d2, 1.0, 0.0)).astype(BF16)
    xs_ref[...] = _dot(p, xn).astype(BF16)
    meta_t = jnp.concatenate([d1, d2, g1, g2, jnp.zeros((LANES - 4, ts), F32)], axis=0)
    meta_ref[...] = meta_t.T
    lane = lax.broadcasted_iota(jnp.int32, (N_EXPERTS, LANES), 1)
    seg_ref[...] = jnp.where(lane == 0, off, jnp.where(lane == 1, cpad, 0.0)).astype(jnp.int32)


def _route(x, norm_ffn, rwt, rb, layer, idx, ts):
    n = x.shape[0]
    n_sub = n // ts
    cap = _sorted_cap(ts)
    tri = jnp.asarray(np.triu(np.ones((ts, ts), np.float32), 1), BF16)
    return pl.pallas_call(
        functools.partial(_route_kernel, ts=ts, cap=cap),
        grid=(n_sub,),
        in_specs=[pl.BlockSpec((ts, D_MODEL), lambda i: (i, 0)),
                  _resident((None, 1, D_MODEL), lambda i: (layer, 0, 0)),
                  _resident((None, N_EXPERTS, D_MODEL), lambda i: (idx, 0, 0)),
                  _resident((None, N_EXPERTS, 1), lambda i: (idx, 0, 0)),
                  _resident((ts, ts), lambda i: (0, 0))],
        out_specs=[pl.BlockSpec((cap, D_MODEL), lambda i: (i, 0)),
                   pl.BlockSpec((ts, LANES), lambda i: (i, 0)),
                   pl.BlockSpec((None, N_EXPERTS, LANES), lambda i: (i, 0, 0))],
        out_shape=[jax.ShapeDtypeStruct((n_sub * cap, D_MODEL), BF16),
                   jax.ShapeDtypeStruct((n, LANES), F32),
                   jax.ShapeDtypeStruct((n_sub, N_EXPERTS, LANES), jnp.int32)],
        compiler_params=_params(("parallel",)),
        name="route",
    )(x, norm_ffn, rwt, rb, tri)


def _regroup_kernel(src_rows_ref, dst_rows_ref, nseg_ref, src_ref, dst_in_ref, dst_ref, sem, *, n_seg, n_bits):
    del dst_in_ref

    def for_each_copy(i, action):
        n = nseg_ref[i]
        for b in reversed(range(n_bits)):
            size = SEG_ALIGN << b

            @pl.when(((n >> b) & 1) == 1)
            def _(b=b, size=size):
                done = ((n >> (b + 1)) << (b + 1)) * SEG_ALIGN
                s0 = pl.multiple_of(src_rows_ref[i] + done, SEG_ALIGN)
                d0 = pl.multiple_of(dst_rows_ref[i] + done, SEG_ALIGN)
                action(pltpu.make_async_copy(src_ref.at[pl.ds(s0, size)], dst_ref.at[pl.ds(d0, size)], sem))

    def start(i, carry):
        for_each_copy(i, lambda cp: cp.start())
        return carry

    def wait(i, carry):
        for_each_copy(i, lambda cp: cp.wait())
        return carry

    lax.fori_loop(0, n_seg, start, 0)
    lax.fori_loop(0, n_seg, wait, 0)


def _regroup(src_rows, dst_rows, nseg, src, dst, max_seg_rows):
    n_seg = src_rows.shape[0]
    n_bits = int(max_seg_rows // SEG_ALIGN).bit_length()
    return pl.pallas_call(
        functools.partial(_regroup_kernel, n_seg=n_seg, n_bits=n_bits),
        grid_spec=pltpu.PrefetchScalarGridSpec(
            num_scalar_prefetch=3,
            grid=(1,),
            in_specs=[pl.BlockSpec(memory_space=pl.ANY), pl.BlockSpec(memory_space=pl.ANY)],
            out_specs=pl.BlockSpec(memory_space=pl.ANY),
            scratch_shapes=[pltpu.SemaphoreType.DMA(())]),
        out_shape=jax.ShapeDtypeStruct(dst.shape, dst.dtype),
        input_output_aliases={4: 0},
        compiler_params=_params(("arbitrary",)),
        name="regroup",
    )(src_rows, dst_rows, nseg, src, dst)


def _experts_kernel(blk_ref, nreal_ref, xb_ref, wg_ref, wu_ref, wd_ref, yb_ref, acc_ref):
    del blk_ref
    i = pl.program_id(0)
    f = pl.program_id(1)

    @pl.when(f == 0)
    def _():
        acc_ref[...] = jnp.zeros_like(acc_ref)

    @pl.when(i < nreal_ref[0])
    def _():
        _swiglu_step(xb_ref[...], wg_ref, wu_ref, wd_ref, acc_ref)

    @pl.when(f == pl.num_programs(1) - 1)
    def _():
        yb_ref[...] = acc_ref[...].astype(yb_ref.dtype)


def _experts(blk_e, nreal, xb, wg, wu, wd, idx, tf):
    n_blocks = blk_e.shape[0]
    d_ff = wg.shape[-1]
    n_f = d_ff // tf

    def live_block(i, nreal):
        return jnp.minimum(i, nreal[0] - 1)

    def wcol(i, f, blk, nreal):
        return idx, blk[live_block(i, nreal)], 0, jnp.where(i < nreal[0], f, n_f - 1)

    def wrow(i, f, blk, nreal):
        return idx, blk[live_block(i, nreal)], jnp.where(i < nreal[0], f, n_f - 1), 0

    return pl.pallas_call(
        _experts_kernel,
        grid_spec=pltpu.PrefetchScalarGridSpec(
            num_scalar_prefetch=2,
            grid=(n_blocks, n_f),
            in_specs=[pl.BlockSpec((MOE_ROWS, D_MODEL), lambda i, f, blk, nreal: (live_block(i, nreal), 0)),
                      pl.BlockSpec((None, None, D_MODEL, tf), wcol),
                      pl.BlockSpec((None, None, D_MODEL, tf), wcol),
                      pl.BlockSpec((None, None, tf, D_MODEL), wrow)],
            out_specs=pl.BlockSpec((MOE_ROWS, D_MODEL), lambda i, f, blk, nreal: (i, 0)),
            scratch_shapes=[pltpu.VMEM((MOE_ROWS, D_MODEL), F32)]),
        out_shape=jax.ShapeDtypeStruct((n_blocks * MOE_ROWS, D_MODEL), BF16),
        compiler_params=_params(("parallel", "arbitrary")),
        name="experts",
    )(blk_e, nreal, xb, wg, wu, wd)


def _combine_kernel(x_ref, ys_ref, meta_ref, o_ref, *, ts, cap):
    slot = lax.broadcasted_iota(jnp.int32, (ts, cap), 1).astype(F32)
    ys = ys_ref[...]
    q1 = jnp.where(slot == meta_ref[:, 0:1], 1.0, 0.0).astype(BF16)
    q2 = jnp.where(slot == meta_ref[:, 1:2], 1.0, 0.0).astype(BF16)
    o_ref[...] = x_ref[...] + meta_ref[:, 2:3] * _dot(q1, ys) + meta_ref[:, 3:4] * _dot(q2, ys)


def _combine(x, ys, meta, ts):
    n = x.shape[0]
    cap = _sorted_cap(ts)
    return pl.pallas_call(
        functools.partial(_combine_kernel, ts=ts, cap=cap),
        grid=(n // ts,),
        in_specs=[pl.BlockSpec((ts, D_MODEL), lambda i: (i, 0)),
                  pl.BlockSpec((cap, D_MODEL), lambda i: (i, 0)),
                  pl.BlockSpec((ts, LANES), lambda i: (i, 0))],
        out_specs=pl.BlockSpec((ts, D_MODEL), lambda i: (i, 0)),
        out_shape=jax.ShapeDtypeStruct((n, D_MODEL), F32),
        compiler_params=_params(("parallel",)),
        name="combine",
    )(x, ys, meta)


def _moe(xs, tss, norm_ffn, rwt, rb, wg, wu, wd, layer, idx, tf):
    routed = [_route(x, norm_ffn, rwt, rb, layer, idx, ts) for x, ts in zip(xs, tss)]
    caps = [_sorted_cap(ts) for ts in tss]
    off = jnp.concatenate([r[2][:, :, 0] for r in routed], axis=0)
    cpad = jnp.concatenate([r[2][:, :, 1] for r in routed], axis=0)
    n_subs = [r[2].shape[0] for r in routed]
    base = jnp.concatenate([jnp.arange(ns, dtype=jnp.int32) * cap for ns, cap in zip(n_subs, caps)])
    tot = jnp.sum(cpad, axis=0)
    padded = (tot + MOE_ROWS - 1) // MOE_ROWS * MOE_ROWS
    pend = jnp.cumsum(padded)
    within = jnp.cumsum(cpad, axis=0) - cpad
    sorted_rows = (base[:, None] + off).astype(jnp.int32).reshape(-1)
    grouped_rows = ((pend - padded)[None, :] + within).astype(jnp.int32).reshape(-1)
    nseg = (cpad // SEG_ALIGN).astype(jnp.int32).reshape(-1)
    n_assign = TOP_K * sum(x.shape[0] for x in xs)
    n_seg_total = sum(n_subs) * N_EXPERTS
    n_blocks = (n_assign + (SEG_ALIGN - 1) * n_seg_total + N_EXPERTS * (MOE_ROWS - 1)) // MOE_ROWS
    blk_e = jnp.minimum(jnp.sum(jnp.arange(n_blocks)[:, None] * MOE_ROWS >= pend[None, :], axis=1),
                        N_EXPERTS - 1).astype(jnp.int32)
    nreal = (pend[-1:] // MOE_ROWS).astype(jnp.int32)

    xb = jnp.zeros((n_blocks * MOE_ROWS, D_MODEL), BF16)
    lo = 0
    for r, ns, ts in zip(routed, n_subs, tss):
        sl = slice(lo * N_EXPERTS, (lo + ns) * N_EXPERTS)
        xb = _regroup(sorted_rows[sl], grouped_rows[sl], nseg[sl], r[0], xb, TOP_K * ts)
        lo += ns
    yb = _experts(blk_e, nreal, xb, wg, wu, wd, idx, tf)
    outs = []
    lo = 0
    for x, r, ns, ts, cap in zip(xs, routed, n_subs, tss, caps):
        sl = slice(lo * N_EXPERTS, (lo + ns) * N_EXPERTS)
        ys = _regroup(grouped_rows[sl], sorted_rows[sl], nseg[sl], yb, jnp.zeros((ns * cap, D_MODEL), BF16), TOP_K * ts)
        outs.append(_combine(x, ys, r[1], ts))
        lo += ns
    return outs


def kernel(x_prompt, x_sample, state_pool, cache_kv_g0, cache_kv_g1, cache_kv_g2, norm_attn, w_in, q_norm, k_norm, w_pool, pool_scale, w_branch_pool, w_branch_attn, w_out, norm_ffn, w1_dense, w3_dense, w2_dense, router_w, router_b, we_gate, we_up, we_down):
    batch, seq, _ = x_prompt.shape
    dec_batch, dec_seq, _ = x_sample.shape
    depth = w_in.shape[0]
    caches = (cache_kv_g0, cache_kv_g1, cache_kv_g2)
    n_p, n_s = batch * seq, dec_batch * dec_seq
    tm_p = 512
    dils = tuple(d for _, d in ATT_GROUPS)
    keeps = tuple(min(win, seq) for win, _ in ATT_GROUPS)

    head_of = np.arange(GROUP_W) // HEAD_DIM
    same_head = (head_of[:, None] == head_of[None, :]).astype(np.float32)
    seg_mean = jnp.asarray(same_head / HEAD_DIM, BF16)
    seg_ones = jnp.asarray(same_head, BF16)
    w_qkv = w_in[:, :, :PROJ_W].astype(BF16)
    qgain = (jnp.tile(q_norm, (1, HEADS)) * (HEAD_DIM ** -0.5)).reshape(depth, 1, GROUP_W)
    kgain = jnp.tile(k_norm, (1, HEADS)).reshape(depth, 1, GROUP_W)
    norm_attn3 = norm_attn.reshape(depth, 1, D_MODEL)
    w = {"w_pool": w_pool.astype(BF16), "pool_scale": pool_scale.reshape(depth, 1, POOL_W),
         "norm_attn": norm_attn3, "w_gate": w_in[:, :, PROJ_W:].astype(BF16),
         "w_bp": w_branch_pool.astype(BF16), "w_ba": w_branch_attn.astype(BF16), "w_o": w_out.astype(BF16),
         "norm_ffn": norm_ffn.reshape(depth, 1, D_MODEL)}
    w1_b, w3_b, w2_b = w1_dense.astype(BF16), w3_dense.astype(BF16), w2_dense.astype(BF16)
    wg_b, wu_b, wd_b = we_gate.astype(BF16), we_up.astype(BF16), we_down.astype(BF16)
    rwt = jnp.swapaxes(router_w, 1, 2).astype(BF16)
    rb = router_b.reshape(-1, N_EXPERTS, 1)

    xp = x_prompt.reshape(n_p, D_MODEL)
    xs = x_sample.reshape(n_s, D_MODEL)
    pool_p, pool_s = [], []
    kv_s = [[] for _ in ATT_GROUPS]
    kvo_p = tuple(jnp.zeros((depth, batch, keep, 2 * GROUP_W), F32) for keep in keeps)
    ones = (1,) * N_GROUPS
    for layer in range(depth):
        res = _proj(xp, layer, norm_attn3, w_qkv, qgain, kgain, seg_mean, batch, seq, tm_p, dils, keeps,
                    layer, depth, kvo_p, 1)
        u, qs, kcs, kvo_p = res[0], res[1:4], res[4:7], tuple(res[7:10])
        outs, lses = [], []
        for gi in range(N_GROUPS):
            o, lse = _attention(qs[gi], kcs[gi], gi)
            outs.append(o)
            lses.append(lse)
        pool_p.append(u.reshape(batch, seq, POOL_W)[:, seq - POOL_STATE:])
        xp, xpn = _merge(xp, u, outs, lses, layer, w, batch, seq, tm_p, dils)

        res = _proj(xs, layer, norm_attn3, w_qkv, qgain, kgain, seg_mean, 1, n_s, n_s, ones, (n_s,) * N_GROUPS,
                    0, 1, (), 3)
        u, qs, kvn = res[0], res[1:4], res[7:10]
        pool_o, att_o, new_pool = _sample_mix(state_pool, u, qs, kvn, caches, layer, w, seg_ones, dec_batch, dec_seq)
        for gi in range(N_GROUPS):
            kv_s[gi].append(kvn[gi].reshape(dec_batch, dec_seq, 2, HEADS, HEAD_DIM))
        pool_s.append(new_pool)
        xs, xsn = _tail(xs, pool_o.reshape(n_s, POOL_W), att_o.reshape(n_s, GROUP_W), layer, w)

        i = layer // 2
        if layer % 2 == 0:
            xp = _ffn(xp, xpn, w1_b, w3_b, w2_b, i, 1024, 256)
            xs = _ffn(xs, xsn, w1_b, w3_b, w2_b, i, n_s, 256)
        else:
            xp, xs = _moe([xp, xs], [512, n_s], w["norm_ffn"], rwt, rb, wg_b, wu_b, wd_b, layer, i, 512)
    kv_p = [kvo_p[g].reshape(depth, batch, keeps[g], 2, HEADS, HEAD_DIM) for g in range(N_GROUPS)]
    return (xp.reshape(batch, seq, D_MODEL), xs.reshape(dec_batch, dec_seq, D_MODEL),
            jnp.stack(pool_p), jnp.stack(pool_s),
            kv_p[0], jnp.stack(kv_s[0]),
            kv_p[1], jnp.stack(kv_s[1]),
            kv_p[2], jnp.stack(kv_s[2]))
```

```python
import functools

import numpy as np
import jax
import jax.numpy as jnp
from jax import lax
from jax.experimental import pallas as pl
from jax.experimental.pallas import tpu as pltpu

F32 = jnp.float32
BF16 = jnp.bfloat16

D_MODEL = 1024
PAST_LEN = 16384
POOL_WINDOWS = (2, 4, 8, 16)
POOL_GROUP = 128
POOL_W = 512
POOL_STATE = 15
ATT_GROUPS = ((128, 1), (512, 4), (2048, 16))
N_GROUPS = len(ATT_GROUPS)
HEAD_DIM = 64
HEADS = 8
GROUP_W = 512
QKV_W = 1536
Q_BLOCK = 128
ALIBI_MAX = 8.0
N_EXPERTS = 8
TOP_K = 2
RMS_EPS = 1e-6
NEG_INF = -1e30
LANES = 128
LANE_CHUNKS = GROUP_W // LANES
PROJ_W = POOL_W + 3 * QKV_W
VMEM_LIMIT = 56 * 1024 * 1024


def _slopes():
    i = np.arange(1, N_GROUPS * HEADS + 1, dtype=np.float32)
    return np.exp2(-ALIBI_MAX * i / (N_GROUPS * HEADS)).astype(np.float32).reshape(N_GROUPS, HEADS)


def _params(sem):
    return pltpu.CompilerParams(dimension_semantics=sem, vmem_limit_bytes=VMEM_LIMIT)


def _rms(x, gain):
    return x * lax.rsqrt(jnp.mean(x * x, axis=-1, keepdims=True) + RMS_EPS) * gain


def _dot(a, b):
    return jnp.dot(a, b, preferred_element_type=F32)


def _resident(shape, index_map):
    return pl.BlockSpec(shape, index_map, pipeline_mode=pl.Buffered(1))


def _write_classes(dst_ref, col0, val, dil, tmp_ref, slot):
    rows = val.shape[0] // dil
    cols = slice(col0, col0 + GROUP_W)
    if dil == 1:
        dst_ref[0, :, cols] = val.astype(dst_ref.dtype)
        return
    for c in range(LANE_CHUNKS):
        tmp_ref[slot, c] = val[:, c * LANES:(c + 1) * LANES]
    for r in range(dil):
        picked = [tmp_ref[slot, c, pl.ds(r, rows, stride=dil), :] for c in range(LANE_CHUNKS)]
        dst_ref[r, :, cols] = jnp.concatenate(picked, axis=1).astype(dst_ref.dtype)


def _proj_kernel(x_ref, g_ref, w_ref, qg_ref, kg_ref, seg_ref, *rest, tm, dils, keeps, n_alias, norm_terms):
    u_ref, q0_ref, q1_ref, q2_ref, kc0_ref, kc1_ref, kc2_ref, kvo0_ref, kvo1_ref, kvo2_ref, tmp_ref = rest[n_alias:]
    s = pl.program_id(1)
    n_tiles = pl.num_programs(1)
    xn = _rms(x_ref[...], g_ref[...]).astype(BF16)

    def zblk(j):
        return _dot(xn, w_ref[:, j * GROUP_W:(j + 1) * GROUP_W])

    def headnorm(z, gain):
        rem = z * z
        ms = None
        for _ in range(norm_terms):
            part = rem.astype(BF16)
            rem = rem - part.astype(F32)
            ms = _dot(part, seg_ref[...]) if ms is None else ms + _dot(part, seg_ref[...])
        return z * lax.rsqrt(ms + RMS_EPS) * gain

    u_ref[...] = zblk(0)
    q_refs = (q0_ref, q1_ref, q2_ref)
    kc_refs = (kc0_ref, kc1_ref, kc2_ref)
    kvo_refs = (kvo0_ref, kvo1_ref, kvo2_ref)
    slot = 0
    for g in range(N_GROUPS):
        q = headnorm(zblk(1 + g), qg_ref[...])
        k = headnorm(zblk(1 + N_GROUPS + g), kg_ref[...])
        v = zblk(1 + 2 * N_GROUPS + g)
        for dst, col0, val in ((q_refs[g], 0, q), (kc_refs[g], 0, k), (kc_refs[g], GROUP_W, v)):
            _write_classes(dst, col0, val, dils[g], tmp_ref, slot % tmp_ref.shape[0])
            slot += dils[g] > 1
        keep = keeps[g]
        if keep >= tm:
            first = n_tiles - keep // tm

            @pl.when(s >= first)
            def _(k=k, v=v, ref=kvo_refs[g]):
                ref[:, 0:GROUP_W] = k
                ref[:, GROUP_W:2 * GROUP_W] = v
        else:
            @pl.when(s == n_tiles - 1)
            def _(k=k, v=v, ref=kvo_refs[g], keep=keep):
                ref[:, 0:GROUP_W] = k[tm - keep:, :]
                ref[:, GROUP_W:2 * GROUP_W] = v[tm - keep:, :]


def _proj(x, layer, norm_attn, w_qkv, qgain, kgain, seg, batch, seq, tm, dils, keeps, out_layer, out_depth, prev_kvo,
          norm_terms):
    n_tiles = seq // tm
    kvo_specs, kvo_shapes = [], []
    for g in range(N_GROUPS):
        keep = keeps[g]
        assert keep % tm == 0 or (keep < tm and keep % 8 == 0)
        if keep >= tm:
            first = n_tiles - keep // tm
            kvo_specs.append(pl.BlockSpec((None, None, tm, 2 * GROUP_W),
                                          lambda b, s, first=first: (out_layer, b, jnp.maximum(s - first, 0), 0)))
        else:
            kvo_specs.append(pl.BlockSpec((None, None, keep, 2 * GROUP_W), lambda b, s: (out_layer, b, 0, 0)))
        kvo_shapes.append(jax.ShapeDtypeStruct((out_depth, batch, keep, 2 * GROUP_W), F32))
    cls_spec = lambda g, width: pl.BlockSpec((None, dils[g], tm // dils[g], width), lambda b, s: (b, 0, s, 0))
    cls_shape = lambda g, width: jax.ShapeDtypeStruct((batch, dils[g], seq // dils[g], width), BF16)
    n_alias = len(prev_kvo)
    n_in = 6
    return pl.pallas_call(
        functools.partial(_proj_kernel, tm=tm, dils=dils, keeps=keeps, n_alias=n_alias,
                          norm_terms=norm_terms),
        grid=(batch, n_tiles),
        in_specs=[
            pl.BlockSpec((tm, D_MODEL), lambda b, s: (b * n_tiles + s, 0)),
            _resident((None, 1, D_MODEL), lambda b, s: (layer, 0, 0)),
            _resident((None, D_MODEL, PROJ_W), lambda b, s: (layer, 0, 0)),
            _resident((None, 1, GROUP_W), lambda b, s: (layer, 0, 0)),
            _resident((None, 1, GROUP_W), lambda b, s: (layer, 0, 0)),
            _resident((GROUP_W, GROUP_W), lambda b, s: (0, 0)),
        ] + [pl.BlockSpec(memory_space=pl.ANY)] * n_alias,
        out_specs=[pl.BlockSpec((tm, POOL_W), lambda b, s: (b * n_tiles + s, 0))]
                  + [cls_spec(g, GROUP_W) for g in range(N_GROUPS)]
                  + [cls_spec(g, 2 * GROUP_W) for g in range(N_GROUPS)] + kvo_specs,
        out_shape=[jax.ShapeDtypeStruct((batch * seq, POOL_W), F32)]
                  + [cls_shape(g, GROUP_W) for g in range(N_GROUPS)]
                  + [cls_shape(g, 2 * GROUP_W) for g in range(N_GROUPS)] + kvo_shapes,
        scratch_shapes=[pltpu.VMEM((3, LANE_CHUNKS, tm, LANES), F32)],
        input_output_aliases={n_in + g: 1 + 2 * N_GROUPS + g for g in range(n_alias)},
        compiler_params=_params(("parallel", "arbitrary")),
        name="proj",
    )(x, norm_attn, w_qkv, qgain, kgain, seg, *prev_kvo)


def _attn_kernel(q_ref, kvc_ref, *rest, slope_dil, tc, has_halo):
    if has_halo:
        kvh_ref, o_ref, lse_ref, kbuf, vbuf = rest
    else:
        o_ref, lse_ref, kbuf, vbuf = rest
    off = Q_BLOCK if has_halo else 0
    if has_halo:
        kbuf[0:Q_BLOCK, :] = kvh_ref[:, 0:GROUP_W]
        vbuf[0:Q_BLOCK, :] = kvh_ref[:, GROUP_W:2 * GROUP_W]
    kbuf[off:off + tc, :] = kvc_ref[:, 0:GROUP_W]
    vbuf[off:off + tc, :] = kvc_ref[:, GROUP_W:2 * GROUP_W]

    nk = Q_BLOCK + off
    qi = lax.broadcasted_iota(jnp.int32, (Q_BLOCK, nk), 0)
    cj = lax.broadcasted_iota(jnp.int32, (Q_BLOCK, nk), 1)
    dist = qi - cj + off
    distf = dist.astype(F32)
    maskneg = jnp.where((dist >= 0) & (dist <= Q_BLOCK), 0.0, NEG_INF).astype(F32)
    if has_halo:
        first = jnp.where(pl.program_id(2) == 0, NEG_INF, 0.0).astype(F32)
        mask_first = maskneg + jnp.where(cj < Q_BLOCK, first, 0.0)

    for i in range(tc // Q_BLOCK):
        mask = mask_first if (has_halo and i == 0) else maskneg
        rows = slice(i * Q_BLOCK, (i + 1) * Q_BLOCK)
        krows = slice(i * Q_BLOCK, i * Q_BLOCK + nk)
        for hp in range(HEADS // 2):
            outs, lses = [], []
            for h in (2 * hp, 2 * hp + 1):
                cols = slice(h * HEAD_DIM, (h + 1) * HEAD_DIM)
                s = lax.dot_general(q_ref[rows, cols], kbuf[krows, cols],
                                    (((1,), (1,)), ((), ())), preferred_element_type=F32)
                s = s - slope_dil[h] * distf + mask
                m = jnp.max(s, axis=1, keepdims=True)
                p = jnp.exp(s - m)
                l = jnp.sum(p, axis=1, keepdims=True)
                o = _dot(p.astype(BF16), vbuf[krows, cols]) / l
                outs.append(o)
                lses.append(jnp.broadcast_to(m + jnp.log(l), (Q_BLOCK, HEAD_DIM)))
            pc = slice(hp * 2 * HEAD_DIM, (hp + 1) * 2 * HEAD_DIM)
            o_ref[rows, pc] = jnp.concatenate(outs, axis=1).astype(BF16)
            lse_ref[rows, pc] = jnp.concatenate(lses, axis=1)


def _attention(q, kc, gi):
    win, dil = ATT_GROUPS[gi]
    assert win // dil == Q_BLOCK
    batch, _, l, _ = q.shape
    tc = min(l, 512)
    has_halo = l > Q_BLOCK
    slope_dil = tuple(float(s) * dil for s in _slopes()[gi])
    in_specs = [
        pl.BlockSpec((None, None, tc, GROUP_W), lambda b, r, c: (b, r, c, 0)),
        pl.BlockSpec((None, None, tc, 2 * GROUP_W), lambda b, r, c: (b, r, c, 0)),
    ]
    args = [q, kc]
    if has_halo:
        in_specs.append(pl.BlockSpec((None, None, Q_BLOCK, 2 * GROUP_W),
                                     lambda b, r, c: (b, r, jnp.maximum(c * (tc // Q_BLOCK) - 1, 0), 0)))
        args.append(kc)
    nk = tc + (Q_BLOCK if has_halo else 0)
    return pl.pallas_call(
        functools.partial(_attn_kernel, slope_dil=slope_dil, tc=tc, has_halo=has_halo),
        grid=(batch, dil, l // tc),
        in_specs=in_specs,
        out_specs=[pl.BlockSpec((None, None, tc, GROUP_W), lambda b, r, c: (b, r, c, 0))] * 2,
        out_shape=[jax.ShapeDtypeStruct((batch, dil, l, GROUP_W), BF16),
                   jax.ShapeDtypeStruct((batch, dil, l, GROUP_W), F32)],
        scratch_shapes=[pltpu.VMEM((nk, GROUP_W), BF16), pltpu.VMEM((nk, GROUP_W), BF16)],
        compiler_params=_params(("parallel", "parallel", "arbitrary")),
        name=f"attn_g{gi}",
    )(*args)


def _mix_tail(x, pool_o, att_o, nattn_ref, wgate_ref, wbp_ref, wba_ref, wo_ref, nffn_ref, xo_ref, xn_ref):
    gates = jax.nn.sigmoid(_dot(_rms(x, nattn_ref[...]).astype(BF16), wgate_ref[...]))
    hp = _dot(pool_o.astype(BF16), wbp_ref[...])
    ha = _dot(att_o.astype(BF16), wba_ref[...])
    t = gates[:, 0:D_MODEL] * hp + gates[:, D_MODEL:2 * D_MODEL] * ha
    xo = x + _dot(t.astype(BF16), wo_ref[...])
    xo_ref[...] = xo
    xn_ref[...] = _rms(xo, nffn_ref[...]).astype(BF16)


def _group_linear(m, wpool_ref, pscale_ref):
    parts = [_dot(m[:, gi * POOL_GROUP:(gi + 1) * POOL_GROUP].astype(BF16), wpool_ref[gi])
             for gi in range(len(POOL_WINDOWS))]
    return jnp.concatenate(parts, axis=1) * pscale_ref[...]


def _read_classes(src_ref, dil, il_ref, slot):
    if dil == 1:
        return src_ref[0].astype(F32)
    rows = src_ref.shape[1]
    for r in range(dil):
        v = src_ref[r].astype(F32)
        for c in range(LANE_CHUNKS):
            il_ref[slot, c, pl.ds(r, rows, stride=dil), :] = v[:, c * LANES:(c + 1) * LANES]
    return jnp.concatenate([il_ref[slot, c] for c in range(LANE_CHUNKS)], axis=1)


def _merge_kernel(x_ref, u_ref, uh_ref, o0_ref, o1_ref, o2_ref, l0_ref, l1_ref, l2_ref,
                  wpool_ref, pscale_ref, nattn_ref, wgate_ref, wbp_ref, wba_ref, wo_ref, nffn_ref,
                  xo_ref, xn_ref, ext_ref, il_ref, *, tm, dils):
    si = pl.program_id(1)
    halo = POOL_STATE + 1
    ext_ref[0:halo, :] = jnp.where(si == 0, 0.0, uh_ref[...])
    ext_ref[halo:halo + tm, :] = u_ref[...]
    pos = si * tm + lax.broadcasted_iota(jnp.int32, (tm, 1), 0)
    parts = []
    for gi, win in enumerate(POOL_WINDOWS):
        cols = slice(gi * POOL_GROUP, (gi + 1) * POOL_GROUP)
        own = ext_ref[halo:halo + tm, cols]
        acc = own
        for back in range(1, win):
            acc = acc + ext_ref[halo - back:halo - back + tm, cols]
        inv = 1.0 / jnp.minimum(pos + 1, win).astype(F32)
        parts.append(acc * inv - own)
    pool_o = _group_linear(jnp.concatenate(parts, axis=1), wpool_ref, pscale_ref)

    slot = 0
    os_, ls_ = [], []
    for g, (o_ref, l_ref) in enumerate(((o0_ref, l0_ref), (o1_ref, l1_ref), (o2_ref, l2_ref))):
        os_.append(_read_classes(o_ref, dils[g], il_ref, slot))
        slot += dils[g] > 1
        ls_.append(_read_classes(l_ref, dils[g], il_ref, slot))
        slot += dils[g] > 1
    mx = jnp.maximum(ls_[0], jnp.maximum(ls_[1], ls_[2]))
    es = [jnp.exp(v - mx) for v in ls_]
    att_o = (es[0] * os_[0] + es[1] * os_[1] + es[2] * os_[2]) / (es[0] + es[1] + es[2])
    _mix_tail(x_ref[...], pool_o, att_o, nattn_ref, wgate_ref, wbp_ref, wba_ref, wo_ref, nffn_ref, xo_ref, xn_ref)


def _wspec(shape, layer, grid_rank):
    nd = len(shape)
    return _resident((None,) + tuple(shape), lambda *_: (layer,) + (0,) * nd)


def _tail_weight_specs(layer, grid_rank):
    return [_wspec((1, D_MODEL), layer, grid_rank), _wspec((D_MODEL, 2 * D_MODEL), layer, grid_rank),
            _wspec((POOL_W, D_MODEL), layer, grid_rank), _wspec((GROUP_W, D_MODEL), layer, grid_rank),
            _wspec((D_MODEL, D_MODEL), layer, grid_rank), _wspec((1, D_MODEL), layer, grid_rank)]


def _tail_weights(w):
    return (w["norm_attn"], w["w_gate"], w["w_bp"], w["w_ba"], w["w_o"], w["norm_ffn"])


def _merge(x, u, outs, lses, layer, w, batch, seq, tm, dils):
    n = x.shape[0]
    n_tiles = seq // tm
    halo = POOL_STATE + 1
    row = lambda width: pl.BlockSpec((tm, width), lambda b, s: (b * n_tiles + s, 0))
    cls = lambda g: pl.BlockSpec((None, dils[g], tm // dils[g], GROUP_W), lambda b, s: (b, 0, s, 0))
    n_il = 2 * sum(d > 1 for d in dils)
    return pl.pallas_call(
        functools.partial(_merge_kernel, tm=tm, dils=dils),
        grid=(batch, n_tiles),
        in_specs=[row(D_MODEL), row(POOL_W),
                  pl.BlockSpec((halo, POOL_W),
                               lambda b, s: (jnp.maximum((b * n_tiles + s) * (tm // halo) - 1, 0), 0))]
                 + [cls(g) for g in range(N_GROUPS)] * 2
                 + [_wspec((len(POOL_WINDOWS), POOL_GROUP, POOL_GROUP), layer, 2), _wspec((1, POOL_W), layer, 2)]
                 + _tail_weight_specs(layer, 2),
        out_specs=[row(D_MODEL), row(D_MODEL)],
        out_shape=[jax.ShapeDtypeStruct((n, D_MODEL), F32), jax.ShapeDtypeStruct((n, D_MODEL), BF16)],
        scratch_shapes=[pltpu.VMEM((halo + tm, POOL_W), F32), pltpu.VMEM((n_il, LANE_CHUNKS, tm, LANES), F32)],
        compiler_params=_params(("parallel", "parallel")),
        name="merge",
    )(x, u, u, *outs, *lses, w["w_pool"], w["pool_scale"], *_tail_weights(w))


def _sample_bias(dec_seq, cache_rows):
    slopes = _slopes()
    nk = Q_BLOCK + 8
    bias = np.full((N_GROUPS, dec_seq, nk, GROUP_W), NEG_INF, np.float32)
    classes = np.zeros((N_GROUPS, dec_seq), np.int64)
    for gi, (win, dil) in enumerate(ATT_GROUPS):
        band = win // dil
        lc = cache_rows[gi]
        assert lc % dil == 0 and lc // dil == Q_BLOCK
        lane_slope = np.repeat(slopes[gi], HEAD_DIM)
        for t in range(dec_seq):
            cls = (lc + t) % dil
            classes[gi, t] = cls
            r0 = (lc + t - cls) // dil
            for r in range(Q_BLOCK):
                jj = r0 - r
                if 1 <= jj <= band:
                    bias[gi, t, r] = -lane_slope * float(jj * dil)
            for t2 in range(dec_seq):
                if t2 <= t and (t - t2) % dil == 0 and (t - t2) // dil <= band:
                    bias[gi, t, Q_BLOCK + t2] = -lane_slope * float(t - t2)
    return bias, classes


def _sample_mix_kernel(state_ref, u_ref, q0_ref, q1_ref, q2_ref, kn0_ref, kn1_ref, kn2_ref, c0_ref, c1_ref, c2_ref,
                       bias_ref, seg_ref, wpool_ref, pscale_ref,
                       pool_ref, att_ref, newpool_ref, ext_ref, m_ref, *, dec_seq, classes):
    kv_width = 2 * GROUP_W
    ext_ref[...] = jnp.zeros_like(ext_ref)
    ext_ref[0:POOL_STATE, :] = state_ref[...]
    ext_ref[POOL_STATE:POOL_STATE + dec_seq, :] = u_ref[...]
    newpool_ref[...] = ext_ref[dec_seq:dec_seq + POOL_STATE, :]

    m_ref[...] = jnp.zeros_like(m_ref)
    for t in range(dec_seq):
        row = POOL_STATE + t
        for gi, win in enumerate(POOL_WINDOWS):
            cols = slice(gi * POOL_GROUP, (gi + 1) * POOL_GROUP)
            tot = jnp.sum(ext_ref[row - win + 1:row + 1, cols], axis=0, keepdims=True)
            cnt = float(min(PAST_LEN + t + 1, win))
            m_ref[t:t + 1, cols] = tot / cnt - ext_ref[row:row + 1, cols]
    pool_ref[...] = _group_linear(m_ref[...], wpool_ref, pscale_ref)[0:dec_seq]

    pad = jnp.zeros((8 - dec_seq, GROUP_W), F32)
    groups = ((q0_ref, kn0_ref, c0_ref), (q1_ref, kn1_ref, c1_ref), (q2_ref, kn2_ref, c2_ref))
    for t in range(dec_seq):
        outs, lses = [], []
        for gi, (q_ref, kn_ref, c_ref) in enumerate(groups):
            base = int(classes[gi][t]) * kv_width
            keys = jnp.concatenate([c_ref[:, base:base + GROUP_W], kn_ref[:, 0:GROUP_W], pad], axis=0)
            vals = jnp.concatenate([c_ref[:, base + GROUP_W:base + kv_width], kn_ref[:, GROUP_W:kv_width], pad], axis=0)
            prod = keys.astype(BF16).astype(F32) * q_ref[t:t + 1, :].astype(F32)
            hi = prod.astype(BF16)
            lo = (prod - hi.astype(F32)).astype(BF16)
            s = _dot(hi, seg_ref[...]) + _dot(lo, seg_ref[...]) + bias_ref[gi, t]
            mx = jnp.max(s, axis=0, keepdims=True)
            p = jnp.exp(s - mx)
            l = jnp.sum(p, axis=0, keepdims=True)
            pn = (p / l).astype(BF16).astype(F32)
            outs.append(jnp.sum(pn * vals.astype(BF16).astype(F32), axis=0, keepdims=True))
            lses.append(mx + jnp.log(l))
        mx = jnp.maximum(lses[0], jnp.maximum(lses[1], lses[2]))
        es = [jnp.exp(v - mx) for v in lses]
        att_ref[t:t + 1, :] = (es[0] * outs[0] + es[1] * outs[1] + es[2] * outs[2]) / (es[0] + es[1] + es[2])


def _sample_mix(state_pool, u, qs, kvn, caches, layer, w, seg_ones, dec_batch, dec_seq):
    cache_rows = [c.shape[2] for c in caches]
    bias, classes = _sample_bias(dec_seq, cache_rows)
    cviews, cspecs = [], []
    for gi, (win, dil) in enumerate(ATT_GROUPS):
        lc = cache_rows[gi]
        cviews.append(caches[gi].reshape(-1, lc // dil, dil * 2 * GROUP_W))
        ncls = int(classes[gi].max()) + 1
        width = 2 * GROUP_W * min(dil, ncls)
        cspecs.append(pl.BlockSpec((None, lc // dil, width), lambda b: (layer * dec_batch + b, 0, 0)))
    per_tok = lambda width: pl.BlockSpec((None, dec_seq, width), lambda b: (b, 0, 0))
    const = lambda shape: _resident(shape, lambda b: (0,) * len(shape))
    return pl.pallas_call(
        functools.partial(_sample_mix_kernel, dec_seq=dec_seq, classes=tuple(map(tuple, classes))),
        grid=(dec_batch,),
        in_specs=[pl.BlockSpec((None, POOL_STATE, POOL_W), lambda b: (layer * dec_batch + b, 0, 0)),
                  per_tok(POOL_W)] + [per_tok(GROUP_W)] * 3 + [per_tok(2 * GROUP_W)] * 3 + cspecs
                 + [const(bias.shape), const((GROUP_W, GROUP_W)),
                    _wspec((len(POOL_WINDOWS), POOL_GROUP, POOL_GROUP), layer, 1), _wspec((1, POOL_W), layer, 1)],
        out_specs=[per_tok(POOL_W), per_tok(GROUP_W),
                   pl.BlockSpec((None, POOL_STATE, POOL_W), lambda b: (b, 0, 0))],
        out_shape=[jax.ShapeDtypeStruct((dec_batch, dec_seq, POOL_W), F32),
                   jax.ShapeDtypeStruct((dec_batch, dec_seq, GROUP_W), F32),
                   jax.ShapeDtypeStruct((dec_batch, POOL_STATE, POOL_W), F32)],
        scratch_shapes=[pltpu.VMEM((POOL_STATE + dec_seq + 5, POOL_W), F32), pltpu.VMEM((8, POOL_W), F32)],
        compiler_params=_params(("parallel",)),
        name="sample_mix",
    )(state_pool.reshape(-1, POOL_STATE, POOL_W), u.reshape(dec_batch, dec_seq, POOL_W),
      *[q.reshape(dec_batch, dec_seq, GROUP_W) for q in qs],
      *[k.reshape(dec_batch, dec_seq, 2 * GROUP_W) for k in kvn],
      *cviews, jnp.asarray(bias), seg_ones, w["w_pool"], w["pool_scale"])


def _tail_kernel(x_ref, pool_ref, att_ref, nattn_ref, wgate_ref, wbp_ref, wba_ref, wo_ref, nffn_ref, xo_ref, xn_ref):
    _mix_tail(x_ref[...], pool_ref[...], att_ref[...], nattn_ref, wgate_ref, wbp_ref, wba_ref, wo_ref, nffn_ref,
              xo_ref, xn_ref)


def _tail(x, pool_o, att_o, layer, w):
    n = x.shape[0]
    row = lambda width: pl.BlockSpec((n, width), lambda i: (0, 0))
    return pl.pallas_call(
        _tail_kernel,
        grid=(1,),
        in_specs=[row(D_MODEL), row(POOL_W), row(GROUP_W)] + _tail_weight_specs(layer, 1),
        out_specs=[row(D_MODEL), row(D_MODEL)],
        out_shape=[jax.ShapeDtypeStruct((n, D_MODEL), F32), jax.ShapeDtypeStruct((n, D_MODEL), BF16)],
        compiler_params=_params(("arbitrary",)),
        name="tail",
    )(x, pool_o, att_o, *_tail_weights(w))


def _swiglu_step(x, wg_ref, wu_ref, wd_ref, acc_ref):
    h = jax.nn.silu(_dot(x, wg_ref[...])) * _dot(x, wu_ref[...])
    acc_ref[...] += _dot(h.astype(BF16), wd_ref[...])


def _ffn_kernel(x_ref, xn_ref, w1_ref, w3_ref, w2_ref, y_ref, acc_ref):
    f = pl.program_id(1)

    @pl.when(f == 0)
    def _():
        acc_ref[...] = x_ref[...]

    _swiglu_step(xn_ref[...], w1_ref, w3_ref, w2_ref, acc_ref)

    @pl.when(f == pl.num_programs(1) - 1)
    def _():
        y_ref[...] = acc_ref[...]


def _ffn(x, xn, w1, w3, w2, idx, tm, tf):
    n = x.shape[0]
    d_ff = w1.shape[-1]
    return pl.pallas_call(
        _ffn_kernel,
        grid=(n // tm, d_ff // tf),
        in_specs=[pl.BlockSpec((tm, D_MODEL), lambda i, f: (i, 0)),
                  pl.BlockSpec((tm, D_MODEL), lambda i, f: (i, 0)),
                  pl.BlockSpec((None, D_MODEL, tf), lambda i, f: (idx, 0, f)),
                  pl.BlockSpec((None, D_MODEL, tf), lambda i, f: (idx, 0, f)),
                  pl.BlockSpec((None, tf, D_MODEL), lambda i, f: (idx, f, 0))],
        out_specs=pl.BlockSpec((tm, D_MODEL), lambda i, f: (i, 0)),
        out_shape=jax.ShapeDtypeStruct((n, D_MODEL), F32),
        scratch_shapes=[pltpu.VMEM((tm, D_MODEL), F32)],
        compiler_params=_params(("parallel", "arbitrary")),
        name="ffn",
    )(x, xn, w1, w3, w2)


SEG_ALIGN = 16
MOE_ROWS = 512


def _sorted_cap(ts):
    need = TOP_K * ts + N_EXPERTS * (SEG_ALIGN - 1)
    return -(-need // LANES) * LANES if ts >= LANES * 2 else -(-need // SEG_ALIGN) * SEG_ALIGN


def _segment_sizes(oh1, oh2):
    cnt1 = jnp.sum(oh1, axis=1, keepdims=True)
    cnt = cnt1 + jnp.sum(oh2, axis=1, keepdims=True)
    cpad = jnp.floor((cnt + (SEG_ALIGN - 1)) * (1.0 / SEG_ALIGN)) * SEG_ALIGN
    offs = [jnp.zeros((1, 1), F32)]
    for ei in range(1, N_EXPERTS):
        offs.append(offs[-1] + cpad[ei - 1:ei, :])
    return cnt1, cpad, jnp.concatenate(offs, axis=0)


def _gate_kernel(x_ref, g_ref, rwt_ref, rb_ref, route_ref, seg_ref, *, ts):
    xn = _rms(x_ref[...], g_ref[...]).astype(BF16)
    logits = lax.dot_general(rwt_ref[...], xn, (((1,), (1,)), ((), ())), preferred_element_type=F32) + rb_ref[...]
    row = lax.broadcasted_iota(jnp.int32, (N_EXPERTS, ts), 0)
    neg = jnp.float32(-jnp.inf)
    m1 = jnp.max(logits, axis=0, keepdims=True)
    i1 = jnp.min(jnp.where(logits == m1, row, N_EXPERTS), axis=0, keepdims=True)
    rest = jnp.where(row == i1, neg, logits)
    m2 = jnp.max(rest, axis=0, keepdims=True)
    i2 = jnp.min(jnp.where(rest == m2, row, N_EXPERTS), axis=0, keepdims=True)
    e = jnp.exp(m2 - m1)
    zeros = jnp.zeros((N_EXPERTS - 4, ts), F32)
    route_ref[...] = jnp.concatenate([i1.astype(F32), i2.astype(F32), 1.0 / (1.0 + e), e / (1.0 + e), zeros], axis=0)
    _, cpad, off = _segment_sizes(jnp.where(row == i1, 1.0, 0.0), jnp.where(row == i2, 1.0, 0.0))
    lane = lax.broadcasted_iota(jnp.int32, (N_EXPERTS, LANES), 1)
    seg_ref[...] = jnp.where(lane == 0, off, jnp.where(lane == 1, cpad, 0.0)).astype(jnp.int32)


def _gate(x, norm_ffn, rwt, rb, layer, idx, ts):
    n = x.shape[0]
    n_sub = n // ts
    return pl.pallas_call(
        functools.partial(_gate_kernel, ts=ts),
        grid=(n_sub,),
        in_specs=[pl.BlockSpec((ts, D_MODEL), lambda i: (i, 0)),
                  _resident((None, 1, D_MODEL), lambda i: (layer, 0, 0)),
                  _resident((None, N_EXPERTS, D_MODEL), lambda i: (idx, 0, 0)),
                  _resident((None, N_EXPERTS, 1), lambda i: (idx, 0, 0))],
        out_specs=[pl.BlockSpec((None, N_EXPERTS, ts), lambda i: (i, 0, 0)),
                   pl.BlockSpec((None, N_EXPERTS, LANES), lambda i: (i, 0, 0))],
        out_shape=[jax.ShapeDtypeStruct((n_sub, N_EXPERTS, ts), F32),
                   jax.ShapeDtypeStruct((n_sub, N_EXPERTS, LANES), jnp.int32)],
        compiler_params=_params(("parallel",)),
        name="gate",
    )(x, norm_ffn, rwt, rb)


def _segment_copies(i, n_bits, local_ref, off_ref, far_ref, far_rows_ref, nseg_ref, sem, to_far):
    copies = []
    for e in range(N_EXPERTS):
        j = i * N_EXPERTS + e
        n = nseg_ref[j]
        for b in reversed(range(n_bits)):
            size = SEG_ALIGN << b
            done = ((n >> (b + 1)) << (b + 1)) * SEG_ALIGN
            near = local_ref.at[pl.ds(pl.multiple_of(off_ref[j] + done, SEG_ALIGN), size)]
            far = far_ref.at[pl.ds(pl.multiple_of(far_rows_ref[j] + done, SEG_ALIGN), size)]
            cp = pltpu.make_async_copy(near, far, sem) if to_far else pltpu.make_async_copy(far, near, sem)
            copies.append((((n >> b) & 1) == 1, cp))
    return copies


def _scatter_kernel(off_ref, far_rows_ref, nseg_ref, x_ref, g_ref, route_ref, tri_ref, xb_in_ref,
                    meta_ref, xb_ref, xs_ref, sem, *, ts, cap, n_bits):
    del xb_in_ref
    i = pl.program_id(0)
    xn = _rms(x_ref[...], g_ref[...]).astype(BF16)
    row = lax.broadcasted_iota(jnp.int32, (N_EXPERTS, ts), 0).astype(F32)
    oh1 = jnp.where(row == route_ref[0:1, :], 1.0, 0.0)
    oh2 = jnp.where(row == route_ref[1:2, :], 1.0, 0.0)
    cnt1, _, off = _segment_sizes(oh1, oh2)
    pre1 = _dot(oh1.astype(BF16), tri_ref[...])
    pre2 = _dot(oh2.astype(BF16), tri_ref[...])
    d1 = jnp.sum(oh1 * (off + pre1), axis=0, keepdims=True)
    d2 = jnp.sum(oh2 * (off + cnt1 + pre2), axis=0, keepdims=True)
    slot = lax.broadcasted_iota(jnp.int32, (cap, ts), 0).astype(F32)
    p = jnp.where(slot == d1, 1.0, jnp.where(slot == d2, 1.0, 0.0)).astype(BF16)
    xs_ref[...] = _dot(p, xn).astype(BF16)
    meta_t = jnp.concatenate([d1, d2, route_ref[2:4, :], jnp.zeros((LANES - 4, ts), F32)], axis=0)
    meta_ref[...] = meta_t.T
    copies = _segment_copies(i, n_bits, xs_ref, off_ref, xb_ref, far_rows_ref, nseg_ref, sem, True)
    for pred, cp in copies:
        pl.when(pred)(cp.start)
    for pred, cp in copies:
        pl.when(pred)(cp.wait)


def _scatter(off, far_rows, nseg, x, norm_ffn, route, xb, layer, ts):
    n = x.shape[0]
    n_sub = n // ts
    cap = _sorted_cap(ts)
    tri = jnp.asarray(np.triu(np.ones((ts, ts), np.float32), 1), BF16)
    n_bits = int(TOP_K * ts // SEG_ALIGN).bit_length()
    return pl.pallas_call(
        functools.partial(_scatter_kernel, ts=ts, cap=cap, n_bits=n_bits),
        grid_spec=pltpu.PrefetchScalarGridSpec(
            num_scalar_prefetch=3,
            grid=(n_sub,),
            in_specs=[pl.BlockSpec((ts, D_MODEL), lambda i, *_: (i, 0)),
                      _resident((None, 1, D_MODEL), lambda i, *_: (layer, 0, 0)),
                      pl.BlockSpec((None, N_EXPERTS, ts), lambda i, *_: (i, 0, 0)),
                      _resident((ts, ts), lambda i, *_: (0, 0)),
                      pl.BlockSpec(memory_space=pl.ANY)],
            out_specs=[pl.BlockSpec((ts, LANES), lambda i, *_: (i, 0)),
                       pl.BlockSpec(memory_space=pl.ANY)],
            scratch_shapes=[pltpu.VMEM((cap, D_MODEL), BF16), pltpu.SemaphoreType.DMA(())]),
        out_shape=[jax.ShapeDtypeStruct((n, LANES), F32), jax.ShapeDtypeStruct(xb.shape, xb.dtype)],
        input_output_aliases={7: 1},
        compiler_params=_params(("arbitrary",)),
        name="scatter",
    )(off, far_rows, nseg, x, norm_ffn, route, tri, xb)


def _experts_kernel(blk_ref, nreal_ref, xb_ref, wg_ref, wu_ref, wd_ref, yb_ref, acc_ref):
    del blk_ref
    i = pl.program_id(0)
    f = pl.program_id(1)

    @pl.when(f == 0)
    def _():
        acc_ref[...] = jnp.zeros_like(acc_ref)

    @pl.when(i < nreal_ref[0])
    def _():
        _swiglu_step(xb_ref[...], wg_ref, wu_ref, wd_ref, acc_ref)

    @pl.when(f == pl.num_programs(1) - 1)
    def _():
        yb_ref[...] = acc_ref[...].astype(yb_ref.dtype)


def _experts(blk_e, nreal, xb, wg, wu, wd, idx, tf):
    n_blocks = blk_e.shape[0]
    d_ff = wg.shape[-1]
    n_f = d_ff // tf

    def live_block(i, nreal):
        return jnp.minimum(i, nreal[0] - 1)

    def wcol(i, f, blk, nreal):
        return idx, blk[live_block(i, nreal)], 0, jnp.where(i < nreal[0], f, n_f - 1)

    def wrow(i, f, blk, nreal):
        return idx, blk[live_block(i, nreal)], jnp.where(i < nreal[0], f, n_f - 1), 0

    return pl.pallas_call(
        _experts_kernel,
        grid_spec=pltpu.PrefetchScalarGridSpec(
            num_scalar_prefetch=2,
            grid=(n_blocks, n_f),
            in_specs=[pl.BlockSpec((MOE_ROWS, D_MODEL), lambda i, f, blk, nreal: (live_block(i, nreal), 0)),
                      pl.BlockSpec((None, None, D_MODEL, tf), wcol),
                      pl.BlockSpec((None, None, D_MODEL, tf), wcol),
                      pl.BlockSpec((None, None, tf, D_MODEL), wrow)],
            out_specs=pl.BlockSpec((MOE_ROWS, D_MODEL), lambda i, f, blk, nreal: (i, 0)),
            scratch_shapes=[pltpu.VMEM((MOE_ROWS, D_MODEL), F32)]),
        out_shape=jax.ShapeDtypeStruct((n_blocks * MOE_ROWS, D_MODEL), BF16),
        compiler_params=_params(("parallel", "arbitrary")),
        name="experts",
    )(blk_e, nreal, xb, wg, wu, wd)


def _combine_kernel(off_ref, far_rows_ref, nseg_ref, x_ref, meta_ref, yb_ref, o_ref, ys_ref, sem, *, ts, cap, n_bits):
    i = pl.program_id(0)
    ys_ref[...] = jnp.zeros_like(ys_ref)
    copies = _segment_copies(i, n_bits, ys_ref, off_ref, yb_ref, far_rows_ref, nseg_ref, sem, False)
    for pred, cp in copies:
        pl.when(pred)(cp.start)
    for pred, cp in copies:
        pl.when(pred)(cp.wait)
    slot = lax.broadcasted_iota(jnp.int32, (ts, cap), 1).astype(F32)
    ys = ys_ref[...]
    q1 = jnp.where(slot == meta_ref[:, 0:1], 1.0, 0.0).astype(BF16)
    q2 = jnp.where(slot == meta_ref[:, 1:2], 1.0, 0.0).astype(BF16)
    o_ref[...] = x_ref[...] + meta_ref[:, 2:3] * _dot(q1, ys) + meta_ref[:, 3:4] * _dot(q2, ys)


def _combine(off, far_rows, nseg, x, meta, yb, ts):
    n = x.shape[0]
    cap = _sorted_cap(ts)
    n_bits = int(TOP_K * ts // SEG_ALIGN).bit_length()
    return pl.pallas_call(
        functools.partial(_combine_kernel, ts=ts, cap=cap, n_bits=n_bits),
        grid_spec=pltpu.PrefetchScalarGridSpec(
            num_scalar_prefetch=3,
            grid=(n // ts,),
            in_specs=[pl.BlockSpec((ts, D_MODEL), lambda i, *_: (i, 0)),
                      pl.BlockSpec((ts, LANES), lambda i, *_: (i, 0)),
                      pl.BlockSpec(memory_space=pl.ANY)],
            out_specs=pl.BlockSpec((ts, D_MODEL), lambda i, *_: (i, 0)),
            scratch_shapes=[pltpu.VMEM((cap, D_MODEL), BF16), pltpu.SemaphoreType.DMA(())]),
        out_shape=jax.ShapeDtypeStruct((n, D_MODEL), F32),
        compiler_params=_params(("arbitrary",)),
        name="combine",
    )(off, far_rows, nseg, x, meta, yb)


def _moe(xs, tss, norm_ffn, rwt, rb, wg, wu, wd, layer, idx, tf):
    gated = [_gate(x, norm_ffn, rwt, rb, layer, idx, ts) for x, ts in zip(xs, tss)]
    off = jnp.concatenate([g[1][:, :, 0] for g in gated], axis=0)
    cpad = jnp.concatenate([g[1][:, :, 1] for g in gated], axis=0)
    n_subs = [g[1].shape[0] for g in gated]
    tot = jnp.sum(cpad, axis=0)
    padded = (tot + MOE_ROWS - 1) // MOE_ROWS * MOE_ROWS
    pend = jnp.cumsum(padded)
    within = jnp.cumsum(cpad, axis=0) - cpad
    off = off.astype(jnp.int32).reshape(-1)
    grouped_rows = ((pend - padded)[None, :] + within).astype(jnp.int32).reshape(-1)
    nseg = (cpad // SEG_ALIGN).astype(jnp.int32).reshape(-1)
    n_assign = TOP_K * sum(x.shape[0] for x in xs)
    n_blocks = (n_assign + (SEG_ALIGN - 1) * sum(n_subs) * N_EXPERTS + N_EXPERTS * (MOE_ROWS - 1)) // MOE_ROWS
    blk_e = jnp.minimum(jnp.sum(jnp.arange(n_blocks)[:, None] * MOE_ROWS >= pend[None, :], axis=1),
                        N_EXPERTS - 1).astype(jnp.int32)
    nreal = (pend[-1:] // MOE_ROWS).astype(jnp.int32)

    xb = jnp.zeros((n_blocks * MOE_ROWS, D_MODEL), BF16)
    metas, parts = [], []
    lo = 0
    for x, g, ns, ts in zip(xs, gated, n_subs, tss):
        sl = slice(lo * N_EXPERTS, (lo + ns) * N_EXPERTS)
        parts.append((off[sl], grouped_rows[sl], nseg[sl]))
        meta, xb = _scatter(*parts[-1], x, norm_ffn, g[0], xb, layer, ts)
        metas.append(meta)
        lo += ns
    yb = _experts(blk_e, nreal, xb, wg, wu, wd, idx, tf)
    return [_combine(*part, x, meta, yb, ts) for part, x, meta, ts in zip(parts, xs, metas, tss)]


def kernel(x_prompt, x_sample, state_pool, cache_kv_g0, cache_kv_g1, cache_kv_g2, norm_attn, w_in, q_norm, k_norm, w_pool, pool_scale, w_branch_pool, w_branch_attn, w_out, norm_ffn, w1_dense, w3_dense, w2_dense, router_w, router_b, we_gate, we_up, we_down):
    batch, seq, _ = x_prompt.shape
    dec_batch, dec_seq, _ = x_sample.shape
    depth = w_in.shape[0]
    caches = (cache_kv_g0, cache_kv_g1, cache_kv_g2)
    n_p, n_s = batch * seq, dec_batch * dec_seq
    tm_p = 512
    dils = tuple(d for _, d in ATT_GROUPS)
    keeps = tuple(min(win, seq) for win, _ in ATT_GROUPS)

    head_of = np.arange(GROUP_W) // HEAD_DIM
    same_head = (head_of[:, None] == head_of[None, :]).astype(np.float32)
    seg_mean = jnp.asarray(same_head / HEAD_DIM, BF16)
    seg_ones = jnp.asarray(same_head, BF16)
    w_qkv = w_in[:, :, :PROJ_W].astype(BF16)
    qgain = (jnp.tile(q_norm, (1, HEADS)) * (HEAD_DIM ** -0.5)).reshape(depth, 1, GROUP_W)
    kgain = jnp.tile(k_norm, (1, HEADS)).reshape(depth, 1, GROUP_W)
    norm_attn3 = norm_attn.reshape(depth, 1, D_MODEL)
    w = {"w_pool": w_pool.astype(BF16), "pool_scale": pool_scale.reshape(depth, 1, POOL_W),
         "norm_attn": norm_attn3, "w_gate": w_in[:, :, PROJ_W:].astype(BF16),
         "w_bp": w_branch_pool.astype(BF16), "w_ba": w_branch_attn.astype(BF16), "w_o": w_out.astype(BF16),
         "norm_ffn": norm_ffn.reshape(depth, 1, D_MODEL)}
    w1_b, w3_b, w2_b = w1_dense.astype(BF16), w3_dense.astype(BF16), w2_dense.astype(BF16)
    wg_b, wu_b, wd_b = we_gate.astype(BF16), we_up.astype(BF16), we_down.astype(BF16)
    rwt = jnp.swapaxes(router_w, 1, 2).astype(BF16)
    rb = router_b.reshape(-1, N_EXPERTS, 1)

    xp = x_prompt.reshape(n_p, D_MODEL)
    xs = x_sample.reshape(n_s, D_MODEL)
    pool_p, pool_s = [], []
    kv_s = [[] for _ in ATT_GROUPS]
    kvo_p = tuple(jnp.zeros((depth, batch, keep, 2 * GROUP_W), F32) for keep in keeps)
    ones = (1,) * N_GROUPS
    for layer in range(depth):
        res = _proj(xp, layer, norm_attn3, w_qkv, qgain, kgain, seg_mean, batch, seq, tm_p, dils, keeps,
                    layer, depth, kvo_p, 1)
        u, qs, kcs, kvo_p = res[0], res[1:4], res[4:7], tuple(res[7:10])
        outs, lses = [], []
        for gi in range(N_GROUPS):
            o, lse = _attention(qs[gi], kcs[gi], gi)
            outs.append(o)
            lses.append(lse)
        pool_p.append(u.reshape(batch, seq, POOL_W)[:, seq - POOL_STATE:])
        xp, xpn = _merge(xp, u, outs, lses, layer, w, batch, seq, tm_p, dils)

        res = _proj(xs, layer, norm_attn3, w_qkv, qgain, kgain, seg_mean, 1, n_s, n_s, ones, (n_s,) * N_GROUPS,
                    0, 1, (), 3)
        u, qs, kvn = res[0], res[1:4], res[7:10]
        pool_o, att_o, new_pool = _sample_mix(state_pool, u, qs, kvn, caches, layer, w, seg_ones, dec_batch, dec_seq)
        for gi in range(N_GROUPS):
            kv_s[gi].append(kvn[gi].reshape(dec_batch, dec_seq, 2, HEADS, HEAD_DIM))
        pool_s.append(new_pool)
        xs, xsn = _tail(xs, pool_o.reshape(n_s, POOL_W), att_o.reshape(n_s, GROUP_W), layer, w)

        i = layer // 2
        if layer % 2 == 0:
            xp = _ffn(xp, xpn, w1_b, w3_b, w2_b, i, 1024, 256)
            xs = _ffn(xs, xsn, w1_b, w3_b, w2_b, i, n_s, 256)
        else:
            xp, xs = _moe([xp, xs], [512, n_s], w["norm_ffn"], rwt, rb, wg_b, wu_b, wd_b, layer, i, 512)
    kv_p = [kvo_p[g].reshape(depth, batch, keeps[g], 2, HEADS, HEAD_DIM) for g in range(N_GROUPS)]
    return (xp.reshape(batch, seq, D_MODEL), xs.reshape(dec_batch, dec_seq, D_MODEL),
            jnp.stack(pool_p), jnp.stack(pool_s),
            kv_p[0], jnp.stack(kv_s[0]),
            kv_p[1], jnp.stack(kv_s[1]),
            kv_p[2], jnp.stack(kv_s[2]))
```

```python
import functools

import numpy as np
import jax
import jax.numpy as jnp
from jax import lax
from jax.experimental import pallas as pl
from jax.experimental.pallas import tpu as pltpu

F32 = jnp.float32
BF16 = jnp.bfloat16

D_MODEL = 1024
PAST_LEN = 16384
POOL_WINDOWS = (2, 4, 8, 16)
POOL_GROUP = 128
POOL_W = 512
POOL_STATE = 15
ATT_GROUPS = ((128, 1), (512, 4), (2048, 16))
N_GROUPS = len(ATT_GROUPS)
HEAD_DIM = 64
HEADS = 8
GROUP_W = 512
QKV_W = 1536
Q_BLOCK = 128
ALIBI_MAX = 8.0
N_EXPERTS = 8
TOP_K = 2
RMS_EPS = 1e-6
NEG_INF = -1e30
LANES = 128
LANE_CHUNKS = GROUP_W // LANES
PROJ_W = POOL_W + 3 * QKV_W
VMEM_LIMIT = 56 * 1024 * 1024


def _slopes():
    i = np.arange(1, N_GROUPS * HEADS + 1, dtype=np.float32)
    return np.exp2(-ALIBI_MAX * i / (N_GROUPS * HEADS)).astype(np.float32).reshape(N_GROUPS, HEADS)


def _params(sem):
    return pltpu.CompilerParams(dimension_semantics=sem, vmem_limit_bytes=VMEM_LIMIT)


def _rms(x, gain):
    return x * lax.rsqrt(jnp.mean(x * x, axis=-1, keepdims=True) + RMS_EPS) * gain


def _dot(a, b):
    return jnp.dot(a, b, preferred_element_type=F32)


def _resident(shape, index_map):
    return pl.BlockSpec(shape, index_map, pipeline_mode=pl.Buffered(1))


def _write_classes(dst_ref, col0, val, dil, tmp_ref, slot):
    rows = val.shape[0] // dil
    cols = slice(col0, col0 + GROUP_W)
    if dil == 1:
        dst_ref[0, :, cols] = val.astype(dst_ref.dtype)
        return
    for c in range(LANE_CHUNKS):
        tmp_ref[slot, c] = val[:, c * LANES:(c + 1) * LANES]
    for r in range(dil):
        picked = [tmp_ref[slot, c, pl.ds(r, rows, stride=dil), :] for c in range(LANE_CHUNKS)]
        dst_ref[r, :, cols] = jnp.concatenate(picked, axis=1).astype(dst_ref.dtype)


def _proj_kernel(x_ref, g_ref, w_ref, qg_ref, kg_ref, seg_ref, *rest, tm, dils, keeps, n_alias, norm_terms):
    u_ref, q0_ref, q1_ref, q2_ref, kc0_ref, kc1_ref, kc2_ref, kvo0_ref, kvo1_ref, kvo2_ref, tmp_ref = rest[n_alias:]
    s = pl.program_id(1)
    n_tiles = pl.num_programs(1)
    xn = _rms(x_ref[...], g_ref[...]).astype(BF16)

    def zblk(j):
        return _dot(xn, w_ref[:, j * GROUP_W:(j + 1) * GROUP_W])

    def headnorm(z, gain):
        rem = z * z
        ms = None
        for _ in range(norm_terms):
            part = rem.astype(BF16)
            rem = rem - part.astype(F32)
            ms = _dot(part, seg_ref[...]) if ms is None else ms + _dot(part, seg_ref[...])
        return z * lax.rsqrt(ms + RMS_EPS) * gain

    u_ref[...] = zblk(0)
    q_refs = (q0_ref, q1_ref, q2_ref)
    kc_refs = (kc0_ref, kc1_ref, kc2_ref)
    kvo_refs = (kvo0_ref, kvo1_ref, kvo2_ref)
    slot = 0
    for g in range(N_GROUPS):
        q = headnorm(zblk(1 + g), qg_ref[...])
        k = headnorm(zblk(1 + N_GROUPS + g), kg_ref[...])
        v = zblk(1 + 2 * N_GROUPS + g)
        for dst, col0, val in ((q_refs[g], 0, q), (kc_refs[g], 0, k), (kc_refs[g], GROUP_W, v)):
            _write_classes(dst, col0, val, dils[g], tmp_ref, slot % tmp_ref.shape[0])
            slot += dils[g] > 1
        keep = keeps[g]
        if keep >= tm:
            first = n_tiles - keep // tm

            @pl.when(s >= first)
            def _(k=k, v=v, ref=kvo_refs[g]):
                ref[:, 0:GROUP_W] = k
                ref[:, GROUP_W:2 * GROUP_W] = v
        else:
            @pl.when(s == n_tiles - 1)
            def _(k=k, v=v, ref=kvo_refs[g], keep=keep):
                ref[:, 0:GROUP_W] = k[tm - keep:, :]
                ref[:, GROUP_W:2 * GROUP_W] = v[tm - keep:, :]


def _proj(x, layer, norm_attn, w_qkv, qgain, kgain, seg, batch, seq, tm, dils, keeps, out_layer, out_depth, prev_kvo,
          norm_terms):
    n_tiles = seq // tm
    kvo_specs, kvo_shapes = [], []
    for g in range(N_GROUPS):
        keep = keeps[g]
        assert keep % tm == 0 or (keep < tm and keep % 8 == 0)
        if keep >= tm:
            first = n_tiles - keep // tm
            kvo_specs.append(pl.BlockSpec((None, None, tm, 2 * GROUP_W),
                                          lambda b, s, first=first: (out_layer, b, jnp.maximum(s - first, 0), 0)))
        else:
            kvo_specs.append(pl.BlockSpec((None, None, keep, 2 * GROUP_W), lambda b, s: (out_layer, b, 0, 0)))
        kvo_shapes.append(jax.ShapeDtypeStruct((out_depth, batch, keep, 2 * GROUP_W), F32))
    cls_spec = lambda g, width: pl.BlockSpec((None, dils[g], tm // dils[g], width), lambda b, s: (b, 0, s, 0))
    cls_shape = lambda g, width: jax.ShapeDtypeStruct((batch, dils[g], seq // dils[g], width), BF16)
    n_alias = len(prev_kvo)
    n_in = 6
    return pl.pallas_call(
        functools.partial(_proj_kernel, tm=tm, dils=dils, keeps=keeps, n_alias=n_alias,
                          norm_terms=norm_terms),
        grid=(batch, n_tiles),
        in_specs=[
            pl.BlockSpec((tm, D_MODEL), lambda b, s: (b * n_tiles + s, 0)),
            _resident((None, 1, D_MODEL), lambda b, s: (layer, 0, 0)),
            _resident((None, D_MODEL, PROJ_W), lambda b, s: (layer, 0, 0)),
            _resident((None, 1, GROUP_W), lambda b, s: (layer, 0, 0)),
            _resident((None, 1, GROUP_W), lambda b, s: (layer, 0, 0)),
            _resident((GROUP_W, GROUP_W), lambda b, s: (0, 0)),
        ] + [pl.BlockSpec(memory_space=pl.ANY)] * n_alias,
        out_specs=[pl.BlockSpec((tm, POOL_W), lambda b, s: (b * n_tiles + s, 0))]
                  + [cls_spec(g, GROUP_W) for g in range(N_GROUPS)]
                  + [cls_spec(g, 2 * GROUP_W) for g in range(N_GROUPS)] + kvo_specs,
        out_shape=[jax.ShapeDtypeStruct((batch * seq, POOL_W), F32)]
                  + [cls_shape(g, GROUP_W) for g in range(N_GROUPS)]
                  + [cls_shape(g, 2 * GROUP_W) for g in range(N_GROUPS)] + kvo_shapes,
        scratch_shapes=[pltpu.VMEM((3, LANE_CHUNKS, tm, LANES), F32)],
        input_output_aliases={n_in + g: 1 + 2 * N_GROUPS + g for g in range(n_alias)},
        compiler_params=_params(("parallel", "arbitrary")),
        name="proj",
    )(x, norm_attn, w_qkv, qgain, kgain, seg, *prev_kvo)


def _attn_kernel(q_ref, kvc_ref, *rest, slope_dil, tc, has_halo):
    if has_halo:
        kvh_ref, o_ref, lse_ref, kbuf, vbuf = rest
    else:
        o_ref, lse_ref, kbuf, vbuf = rest
    off = Q_BLOCK if has_halo else 0
    if has_halo:
        kbuf[0:Q_BLOCK, :] = kvh_ref[:, 0:GROUP_W]
        vbuf[0:Q_BLOCK, :] = kvh_ref[:, GROUP_W:2 * GROUP_W]
    kbuf[off:off + tc, :] = kvc_ref[:, 0:GROUP_W]
    vbuf[off:off + tc, :] = kvc_ref[:, GROUP_W:2 * GROUP_W]

    nk = Q_BLOCK + off
    qi = lax.broadcasted_iota(jnp.int32, (Q_BLOCK, nk), 0)
    cj = lax.broadcasted_iota(jnp.int32, (Q_BLOCK, nk), 1)
    dist = qi - cj + off
    distf = dist.astype(F32)
    maskneg = jnp.where((dist >= 0) & (dist <= Q_BLOCK), 0.0, NEG_INF).astype(F32)
    if has_halo:
        first = jnp.where(pl.program_id(2) == 0, NEG_INF, 0.0).astype(F32)
        mask_first = maskneg + jnp.where(cj < Q_BLOCK, first, 0.0)

    for i in range(tc // Q_BLOCK):
        mask = mask_first if (has_halo and i == 0) else maskneg
        rows = slice(i * Q_BLOCK, (i + 1) * Q_BLOCK)
        krows = slice(i * Q_BLOCK, i * Q_BLOCK + nk)
        for hp in range(HEADS // 2):
            outs, lses = [], []
            for h in (2 * hp, 2 * hp + 1):
                cols = slice(h * HEAD_DIM, (h + 1) * HEAD_DIM)
                s = lax.dot_general(q_ref[rows, cols], kbuf[krows, cols],
                                    (((1,), (1,)), ((), ())), preferred_element_type=F32)
                s = s - slope_dil[h] * distf + mask
                m = jnp.max(s, axis=1, keepdims=True)
                p = jnp.exp(s - m)
                l = jnp.sum(p, axis=1, keepdims=True)
                o = _dot(p.astype(BF16), vbuf[krows, cols]) / l
                outs.append(o)
                lses.append(jnp.broadcast_to(m + jnp.log(l), (Q_BLOCK, HEAD_DIM)))
            pc = slice(hp * 2 * HEAD_DIM, (hp + 1) * 2 * HEAD_DIM)
            o_ref[rows, pc] = jnp.concatenate(outs, axis=1).astype(BF16)
            lse_ref[rows, pc] = jnp.concatenate(lses, axis=1)


def _attention(q, kc, gi):
    win, dil = ATT_GROUPS[gi]
    assert win // dil == Q_BLOCK
    batch, _, l, _ = q.shape
    tc = min(l, 512)
    has_halo = l > Q_BLOCK
    slope_dil = tuple(float(s) * dil for s in _slopes()[gi])
    in_specs = [
        pl.BlockSpec((None, None, tc, GROUP_W), lambda b, r, c: (b, r, c, 0)),
        pl.BlockSpec((None, None, tc, 2 * GROUP_W), lambda b, r, c: (b, r, c, 0)),
    ]
    args = [q, kc]
    if has_halo:
        in_specs.append(pl.BlockSpec((None, None, Q_BLOCK, 2 * GROUP_W),
                                     lambda b, r, c: (b, r, jnp.maximum(c * (tc // Q_BLOCK) - 1, 0), 0)))
        args.append(kc)
    nk = tc + (Q_BLOCK if has_halo else 0)
    return pl.pallas_call(
        functools.partial(_attn_kernel, slope_dil=slope_dil, tc=tc, has_halo=has_halo),
        grid=(batch, dil, l // tc),
        in_specs=in_specs,
        out_specs=[pl.BlockSpec((None, None, tc, GROUP_W), lambda b, r, c: (b, r, c, 0))] * 2,
        out_shape=[jax.ShapeDtypeStruct((batch, dil, l, GROUP_W), BF16),
                   jax.ShapeDtypeStruct((batch, dil, l, GROUP_W), F32)],
        scratch_shapes=[pltpu.VMEM((nk, GROUP_W), BF16), pltpu.VMEM((nk, GROUP_W), BF16)],
        compiler_params=_params(("parallel", "parallel", "arbitrary")),
        name=f"attn_g{gi}",
    )(*args)


def _mix_tail(x, pool_o, att_o, nattn_ref, wgate_ref, wbp_ref, wba_ref, wo_ref, nffn_ref, xo_ref, xn_ref):
    gates = jax.nn.sigmoid(_dot(_rms(x, nattn_ref[...]).astype(BF16), wgate_ref[...]))
    hp = _dot(pool_o.astype(BF16), wbp_ref[...])
    ha = _dot(att_o.astype(BF16), wba_ref[...])
    t = gates[:, 0:D_MODEL] * hp + gates[:, D_MODEL:2 * D_MODEL] * ha
    xo = x + _dot(t.astype(BF16), wo_ref[...])
    xo_ref[...] = xo
    xn_ref[...] = _rms(xo, nffn_ref[...]).astype(BF16)


def _group_linear(m, wpool_ref, pscale_ref):
    parts = [_dot(m[:, gi * POOL_GROUP:(gi + 1) * POOL_GROUP].astype(BF16), wpool_ref[gi])
             for gi in range(len(POOL_WINDOWS))]
    return jnp.concatenate(parts, axis=1) * pscale_ref[...]


def _read_classes(src_ref, dil, il_ref, slot):
    if dil == 1:
        return src_ref[0].astype(F32)
    rows = src_ref.shape[1]
    for r in range(dil):
        v = src_ref[r].astype(F32)
        for c in range(LANE_CHUNKS):
            il_ref[slot, c, pl.ds(r, rows, stride=dil), :] = v[:, c * LANES:(c + 1) * LANES]
    return jnp.concatenate([il_ref[slot, c] for c in range(LANE_CHUNKS)], axis=1)


def _merge_kernel(x_ref, u_ref, uh_ref, o0_ref, o1_ref, o2_ref, l0_ref, l1_ref, l2_ref,
                  wpool_ref, pscale_ref, nattn_ref, wgate_ref, wbp_ref, wba_ref, wo_ref, nffn_ref,
                  xo_ref, xn_ref, ext_ref, il_ref, *, tm, dils):
    si = pl.program_id(1)
    halo = POOL_STATE + 1
    ext_ref[0:halo, :] = jnp.where(si == 0, 0.0, uh_ref[...])
    ext_ref[halo:halo + tm, :] = u_ref[...]
    pos = si * tm + lax.broadcasted_iota(jnp.int32, (tm, 1), 0)
    parts = []
    for gi, win in enumerate(POOL_WINDOWS):
        cols = slice(gi * POOL_GROUP, (gi + 1) * POOL_GROUP)
        own = ext_ref[halo:halo + tm, cols]
        acc = own
        for back in range(1, win):
            acc = acc + ext_ref[halo - back:halo - back + tm, cols]
        inv = 1.0 / jnp.minimum(pos + 1, win).astype(F32)
        parts.append(acc * inv - own)
    pool_o = _group_linear(jnp.concatenate(parts, axis=1), wpool_ref, pscale_ref)

    slot = 0
    os_, ls_ = [], []
    for g, (o_ref, l_ref) in enumerate(((o0_ref, l0_ref), (o1_ref, l1_ref), (o2_ref, l2_ref))):
        os_.append(_read_classes(o_ref, dils[g], il_ref, slot))
        slot += dils[g] > 1
        ls_.append(_read_classes(l_ref, dils[g], il_ref, slot))
        slot += dils[g] > 1
    mx = jnp.maximum(ls_[0], jnp.maximum(ls_[1], ls_[2]))
    es = [jnp.exp(v - mx) for v in ls_]
    att_o = (es[0] * os_[0] + es[1] * os_[1] + es[2] * os_[2]) / (es[0] + es[1] + es[2])
    _mix_tail(x_ref[...], pool_o, att_o, nattn_ref, wgate_ref, wbp_ref, wba_ref, wo_ref, nffn_ref, xo_ref, xn_ref)


def _wspec(shape, layer, grid_rank):
    nd = len(shape)
    return _resident((None,) + tuple(shape), lambda *_: (layer,) + (0,) * nd)


def _tail_weight_specs(layer, grid_rank):
    return [_wspec((1, D_MODEL), layer, grid_rank), _wspec((D_MODEL, 2 * D_MODEL), layer, grid_rank),
            _wspec((POOL_W, D_MODEL), layer, grid_rank), _wspec((GROUP_W, D_MODEL), layer, grid_rank),
            _wspec((D_MODEL, D_MODEL), layer, grid_rank), _wspec((1, D_MODEL), layer, grid_rank)]


def _tail_weights(w):
    return (w["norm_attn"], w["w_gate"], w["w_bp"], w["w_ba"], w["w_o"], w["norm_ffn"])


def _merge(x, u, outs, lses, layer, w, batch, seq, tm, dils):
    n = x.shape[0]
    n_tiles = seq // tm
    halo = POOL_STATE + 1
    row = lambda width: pl.BlockSpec((tm, width), lambda b, s: (b * n_tiles + s, 0))
    cls = lambda g: pl.BlockSpec((None, dils[g], tm // dils[g], GROUP_W), lambda b, s: (b, 0, s, 0))
    n_il = 2 * sum(d > 1 for d in dils)
    return pl.pallas_call(
        functools.partial(_merge_kernel, tm=tm, dils=dils),
        grid=(batch, n_tiles),
        in_specs=[row(D_MODEL), row(POOL_W),
                  pl.BlockSpec((halo, POOL_W),
                               lambda b, s: (jnp.maximum((b * n_tiles + s) * (tm // halo) - 1, 0), 0))]
                 + [cls(g) for g in range(N_GROUPS)] * 2
                 + [_wspec((len(POOL_WINDOWS), POOL_GROUP, POOL_GROUP), layer, 2), _wspec((1, POOL_W), layer, 2)]
                 + _tail_weight_specs(layer, 2),
        out_specs=[row(D_MODEL), row(D_MODEL)],
        out_shape=[jax.ShapeDtypeStruct((n, D_MODEL), F32), jax.ShapeDtypeStruct((n, D_MODEL), BF16)],
        scratch_shapes=[pltpu.VMEM((halo + tm, POOL_W), F32), pltpu.VMEM((n_il, LANE_CHUNKS, tm, LANES), F32)],
        compiler_params=_params(("parallel", "parallel")),
        name="merge",
    )(x, u, u, *outs, *lses, w["w_pool"], w["pool_scale"], *_tail_weights(w))


def _sample_keys(dec_seq, cache_rows):
    geo = []
    for gi, (win, dil) in enumerate(ATT_GROUPS):
        lc = cache_rows[gi]
        assert lc % dil == 0 and lc // dil == Q_BLOCK
        per_token = []
        for t in range(dec_seq):
            cls = (lc + t) % dil
            new = tuple(t2 for t2 in range(t + 1) if (t - t2) % dil == 0 and (t - t2) // dil <= win // dil)
            per_token.append((cls, (lc + t - cls) // dil, new))
        geo.append(tuple(per_token))
    return tuple(geo)


def _bf16_valued(x):
    return x.astype(BF16).astype(F32)


def _sample_mix_kernel(state_ref, u_ref, q0_ref, q1_ref, q2_ref, kn0_ref, kn1_ref, kn2_ref, c0_ref, c1_ref, c2_ref,
                       slope_ref, wpool_ref, pscale_ref,
                       pool_ref, att_ref, newpool_ref, ext_ref, m_ref, *, dec_seq, geo):
    ext_ref[...] = jnp.zeros_like(ext_ref)
    ext_ref[0:POOL_STATE, :] = state_ref[...]
    ext_ref[POOL_STATE:POOL_STATE + dec_seq, :] = u_ref[...]
    newpool_ref[...] = ext_ref[dec_seq:dec_seq + POOL_STATE, :]

    m_ref[...] = jnp.zeros_like(m_ref)
    for t in range(dec_seq):
        row = POOL_STATE + t
        for gi, win in enumerate(POOL_WINDOWS):
            cols = slice(gi * POOL_GROUP, (gi + 1) * POOL_GROUP)
            tot = jnp.sum(ext_ref[row - win + 1:row + 1, cols], axis=0, keepdims=True)
            cnt = float(min(PAST_LEN + t + 1, win))
            m_ref[t:t + 1, cols] = tot / cnt - ext_ref[row:row + 1, cols]
    pool_ref[...] = _group_linear(m_ref[...], wpool_ref, pscale_ref)[0:dec_seq]

    groups = ((q0_ref, kn0_ref, c0_ref), (q1_ref, kn1_ref, c1_ref), (q2_ref, kn2_ref, c2_ref))
    for t in range(dec_seq):
        outs, lses = [], []
        for gi, (q_ref, kn_ref, c_ref) in enumerate(groups):
            win, dil = ATT_GROUPS[gi]
            cls, r0, new = geo[gi][t]
            slope = slope_ref[gi]
            qv = q_ref[t].astype(F32)
            s = jnp.sum(_bf16_valued(c_ref[:, cls, 0]) * qv[None], axis=-1, keepdims=True)
            back = r0 - lax.broadcasted_iota(jnp.int32, s.shape, 0)
            s = jnp.where((back >= 1) & (back <= win // dil), s - slope[None] * (back * dil).astype(F32), NEG_INF)
            s_new = [jnp.sum(_bf16_valued(kn_ref[t2, 0]) * qv, axis=-1, keepdims=True) - slope * float(t - t2)
                     for t2 in new]
            mx = jnp.max(s, axis=0)
            for v in s_new:
                mx = jnp.maximum(mx, v)
            p = jnp.exp(s - mx[None])
            p_new = [jnp.exp(v - mx) for v in s_new]
            l = jnp.sum(p, axis=0) + sum(p_new)
            o = jnp.sum(_bf16_valued(p / l[None]) * _bf16_valued(c_ref[:, cls, 1]), axis=0)
            for t2, v in zip(new, p_new):
                o = o + _bf16_valued(v / l) * _bf16_valued(kn_ref[t2, 1])
            outs.append(o)
            lses.append(mx + jnp.log(l))
        mx = jnp.maximum(lses[0], jnp.maximum(lses[1], lses[2]))
        es = [jnp.exp(v - mx) for v in lses]
        att_ref[t] = (es[0] * outs[0] + es[1] * outs[1] + es[2] * outs[2]) / (es[0] + es[1] + es[2])


def _sample_mix(state_pool, u, qs, kvn, caches, layer, w, dec_batch, dec_seq):
    cache_rows = [c.shape[2] for c in caches]
    geo = _sample_keys(dec_seq, cache_rows)
    kv_tail = (2, HEADS, HEAD_DIM)
    cviews, cspecs = [], []
    for gi, (win, dil) in enumerate(ATT_GROUPS):
        lc = cache_rows[gi]
        cviews.append(caches[gi].reshape((-1, lc // dil, dil) + kv_tail))
        ncls = max(cls for cls, _, _ in geo[gi]) + 1
        cspecs.append(pl.BlockSpec((None, lc // dil, min(dil, ncls)) + kv_tail,
                                   lambda b: (layer * dec_batch + b, 0, 0, 0, 0, 0)))
    per_tok = lambda *tail: pl.BlockSpec((None, dec_seq) + tail, lambda b: (b, 0) + (0,) * len(tail))
    return pl.pallas_call(
        functools.partial(_sample_mix_kernel, dec_seq=dec_seq, geo=geo),
        grid=(dec_batch,),
        in_specs=[pl.BlockSpec((None, POOL_STATE, POOL_W), lambda b: (layer * dec_batch + b, 0, 0)),
                  per_tok(POOL_W)] + [per_tok(HEADS, HEAD_DIM)] * 3 + [per_tok(*kv_tail)] * 3 + cspecs
                 + [_resident((N_GROUPS, HEADS, 1), lambda b: (0, 0, 0)),
                    _wspec((len(POOL_WINDOWS), POOL_GROUP, POOL_GROUP), layer, 1), _wspec((1, POOL_W), layer, 1)],
        out_specs=[per_tok(POOL_W), per_tok(HEADS, HEAD_DIM),
                   pl.BlockSpec((None, POOL_STATE, POOL_W), lambda b: (b, 0, 0))],
        out_shape=[jax.ShapeDtypeStruct((dec_batch, dec_seq, POOL_W), F32),
                   jax.ShapeDtypeStruct((dec_batch, dec_seq, HEADS, HEAD_DIM), F32),
                   jax.ShapeDtypeStruct((dec_batch, POOL_STATE, POOL_W), F32)],
        scratch_shapes=[pltpu.VMEM((POOL_STATE + dec_seq + 5, POOL_W), F32), pltpu.VMEM((8, POOL_W), F32)],
        compiler_params=_params(("parallel",)),
        name="sample_mix",
    )(state_pool.reshape(-1, POOL_STATE, POOL_W), u.reshape(dec_batch, dec_seq, POOL_W), *qs, *kvn, *cviews,
      jnp.asarray(_slopes()).reshape(N_GROUPS, HEADS, 1), w["w_pool"], w["pool_scale"])


def _tail_kernel(x_ref, pool_ref, att_ref, nattn_ref, wgate_ref, wbp_ref, wba_ref, wo_ref, nffn_ref, xo_ref, xn_ref):
    _mix_tail(x_ref[...], pool_ref[...], att_ref[...], nattn_ref, wgate_ref, wbp_ref, wba_ref, wo_ref, nffn_ref,
              xo_ref, xn_ref)


def _tail(x, pool_o, att_o, layer, w):
    n = x.shape[0]
    row = lambda width: pl.BlockSpec((n, width), lambda i: (0, 0))
    return pl.pallas_call(
        _tail_kernel,
        grid=(1,),
        in_specs=[row(D_MODEL), row(POOL_W), row(GROUP_W)] + _tail_weight_specs(layer, 1),
        out_specs=[row(D_MODEL), row(D_MODEL)],
        out_shape=[jax.ShapeDtypeStruct((n, D_MODEL), F32), jax.ShapeDtypeStruct((n, D_MODEL), BF16)],
        compiler_params=_params(("arbitrary",)),
        name="tail",
    )(x, pool_o, att_o, *_tail_weights(w))


def _swiglu_step(x, wg_ref, wu_ref, wd_ref, acc_ref):
    h = jax.nn.silu(_dot(x, wg_ref[...])) * _dot(x, wu_ref[...])
    acc_ref[...] += _dot(h.astype(BF16), wd_ref[...])


def _ffn_kernel(x_ref, xn_ref, w1_ref, w3_ref, w2_ref, y_ref, acc_ref):
    f = pl.program_id(1)

    @pl.when(f == 0)
    def _():
        acc_ref[...] = x_ref[...]

    _swiglu_step(xn_ref[...], w1_ref, w3_ref, w2_ref, acc_ref)

    @pl.when(f == pl.num_programs(1) - 1)
    def _():
        y_ref[...] = acc_ref[...]


def _ffn(x, xn, w1, w3, w2, idx, tm, tf):
    n = x.shape[0]
    d_ff = w1.shape[-1]
    return pl.pallas_call(
        _ffn_kernel,
        grid=(n // tm, d_ff // tf),
        in_specs=[pl.BlockSpec((tm, D_MODEL), lambda i, f: (i, 0)),
                  pl.BlockSpec((tm, D_MODEL), lambda i, f: (i, 0)),
                  pl.BlockSpec((None, D_MODEL, tf), lambda i, f: (idx, 0, f)),
                  pl.BlockSpec((None, D_MODEL, tf), lambda i, f: (idx, 0, f)),
                  pl.BlockSpec((None, tf, D_MODEL), lambda i, f: (idx, f, 0))],
        out_specs=pl.BlockSpec((tm, D_MODEL), lambda i, f: (i, 0)),
        out_shape=jax.ShapeDtypeStruct((n, D_MODEL), F32),
        scratch_shapes=[pltpu.VMEM((tm, D_MODEL), F32)],
        compiler_params=_params(("parallel", "arbitrary")),
        name="ffn",
    )(x, xn, w1, w3, w2)


SEG_ALIGN = 16
MOE_ROWS = 512


def _sorted_cap(ts):
    need = TOP_K * ts + N_EXPERTS * (SEG_ALIGN - 1)
    return -(-need // LANES) * LANES if ts >= LANES * 2 else -(-need // SEG_ALIGN) * SEG_ALIGN


def _segment_sizes(oh1, oh2):
    cnt1 = jnp.sum(oh1, axis=1, keepdims=True)
    cnt = cnt1 + jnp.sum(oh2, axis=1, keepdims=True)
    cpad = jnp.floor((cnt + (SEG_ALIGN - 1)) * (1.0 / SEG_ALIGN)) * SEG_ALIGN
    offs = [jnp.zeros((1, 1), F32)]
    for ei in range(1, N_EXPERTS):
        offs.append(offs[-1] + cpad[ei - 1:ei, :])
    return cnt1, cpad, jnp.concatenate(offs, axis=0)


def _gate_kernel(x_ref, g_ref, rwt_ref, rb_ref, route_ref, seg_ref, *, ts):
    xn = _rms(x_ref[...], g_ref[...]).astype(BF16)
    logits = lax.dot_general(rwt_ref[...], xn, (((1,), (1,)), ((), ())), preferred_element_type=F32) + rb_ref[...]
    row = lax.broadcasted_iota(jnp.int32, (N_EXPERTS, ts), 0)
    neg = jnp.float32(-jnp.inf)
    m1 = jnp.max(logits, axis=0, keepdims=True)
    i1 = jnp.min(jnp.where(logits == m1, row, N_EXPERTS), axis=0, keepdims=True)
    rest = jnp.where(row == i1, neg, logits)
    m2 = jnp.max(rest, axis=0, keepdims=True)
    i2 = jnp.min(jnp.where(rest == m2, row, N_EXPERTS), axis=0, keepdims=True)
    e = jnp.exp(m2 - m1)
    zeros = jnp.zeros((N_EXPERTS - 4, ts), F32)
    route_ref[...] = jnp.concatenate([i1.astype(F32), i2.astype(F32), 1.0 / (1.0 + e), e / (1.0 + e), zeros], axis=0)
    _, cpad, off = _segment_sizes(jnp.where(row == i1, 1.0, 0.0), jnp.where(row == i2, 1.0, 0.0))
    lane = lax.broadcasted_iota(jnp.int32, (N_EXPERTS, LANES), 1)
    seg_ref[...] = jnp.where(lane == 0, off, jnp.where(lane == 1, cpad, 0.0)).astype(jnp.int32)


def _gate(x, norm_ffn, rwt, rb, layer, idx, ts):
    n = x.shape[0]
    n_sub = n // ts
    return pl.pallas_call(
        functools.partial(_gate_kernel, ts=ts),
        grid=(n_sub,),
        in_specs=[pl.BlockSpec((ts, D_MODEL), lambda i: (i, 0)),
                  _resident((None, 1, D_MODEL), lambda i: (layer, 0, 0)),
                  _resident((None, N_EXPERTS, D_MODEL), lambda i: (idx, 0, 0)),
                  _resident((None, N_EXPERTS, 1), lambda i: (idx, 0, 0))],
        out_specs=[pl.BlockSpec((None, N_EXPERTS, ts), lambda i: (i, 0, 0)),
                   pl.BlockSpec((None, N_EXPERTS, LANES), lambda i: (i, 0, 0))],
        out_shape=[jax.ShapeDtypeStruct((n_sub, N_EXPERTS, ts), F32),
                   jax.ShapeDtypeStruct((n_sub, N_EXPERTS, LANES), jnp.int32)],
        compiler_params=_params(("parallel",)),
        name="gate",
    )(x, norm_ffn, rwt, rb)


def _segment_copies(i, n_bits, local_ref, off_ref, far_ref, far_rows_ref, nseg_ref, sem, to_far):
    copies = []
    for e in range(N_EXPERTS):
        j = i * N_EXPERTS + e
        n = nseg_ref[j]
        for b in reversed(range(n_bits)):
            size = SEG_ALIGN << b
            done = ((n >> (b + 1)) << (b + 1)) * SEG_ALIGN
            near = local_ref.at[pl.ds(pl.multiple_of(off_ref[j] + done, SEG_ALIGN), size)]
            far = far_ref.at[pl.ds(pl.multiple_of(far_rows_ref[j] + done, SEG_ALIGN), size)]
            cp = pltpu.make_async_copy(near, far, sem) if to_far else pltpu.make_async_copy(far, near, sem)
            copies.append((((n >> b) & 1) == 1, cp))
    return copies


def _scatter_kernel(off_ref, far_rows_ref, nseg_ref, x_ref, g_ref, route_ref, tri_ref, xb_in_ref,
                    meta_ref, xb_ref, xs_ref, sem, *, ts, cap, n_bits):
    del xb_in_ref
    i = pl.program_id(0)
    xn = _rms(x_ref[...], g_ref[...]).astype(BF16)
    row = lax.broadcasted_iota(jnp.int32, (N_EXPERTS, ts), 0).astype(F32)
    oh1 = jnp.where(row == route_ref[0:1, :], 1.0, 0.0)
    oh2 = jnp.where(row == route_ref[1:2, :], 1.0, 0.0)
    cnt1, _, off = _segment_sizes(oh1, oh2)
    pre1 = _dot(oh1.astype(BF16), tri_ref[...])
    pre2 = _dot(oh2.astype(BF16), tri_ref[...])
    d1 = jnp.sum(oh1 * (off + pre1), axis=0, keepdims=True)
    d2 = jnp.sum(oh2 * (off + cnt1 + pre2), axis=0, keepdims=True)
    slot = lax.broadcasted_iota(jnp.int32, (cap, ts), 0).astype(F32)
    p = jnp.where(slot == d1, 1.0, jnp.where(slot == d2, 1.0, 0.0)).astype(BF16)
    xs_ref[...] = _dot(p, xn).astype(BF16)
    meta_t = jnp.concatenate([d1, d2, route_ref[2:4, :], jnp.zeros((LANES - 4, ts), F32)], axis=0)
    meta_ref[...] = meta_t.T
    copies = _segment_copies(i, n_bits, xs_ref, off_ref, xb_ref, far_rows_ref, nseg_ref, sem, True)
    for pred, cp in copies:
        pl.when(pred)(cp.start)
    for pred, cp in copies:
        pl.when(pred)(cp.wait)


def _scatter(off, far_rows, nseg, x, norm_ffn, route, xb, layer, ts):
    n = x.shape[0]
    n_sub = n // ts
    cap = _sorted_cap(ts)
    tri = jnp.asarray(np.triu(np.ones((ts, ts), np.float32), 1), BF16)
    n_bits = int(TOP_K * ts // SEG_ALIGN).bit_length()
    return pl.pallas_call(
        functools.partial(_scatter_kernel, ts=ts, cap=cap, n_bits=n_bits),
        grid_spec=pltpu.PrefetchScalarGridSpec(
            num_scalar_prefetch=3,
            grid=(n_sub,),
            in_specs=[pl.BlockSpec((ts, D_MODEL), lambda i, *_: (i, 0)),
                      _resident((None, 1, D_MODEL), lambda i, *_: (layer, 0, 0)),
                      pl.BlockSpec((None, N_EXPERTS, ts), lambda i, *_: (i, 0, 0)),
                      _resident((ts, ts), lambda i, *_: (0, 0)),
                      pl.BlockSpec(memory_space=pl.ANY)],
            out_specs=[pl.BlockSpec((ts, LANES), lambda i, *_: (i, 0)),
                       pl.BlockSpec(memory_space=pl.ANY)],
            scratch_shapes=[pltpu.VMEM((cap, D_MODEL), BF16), pltpu.SemaphoreType.DMA(())]),
        out_shape=[jax.ShapeDtypeStruct((n, LANES), F32), jax.ShapeDtypeStruct(xb.shape, xb.dtype)],
        input_output_aliases={7: 1},
        compiler_params=_params(("arbitrary",)),
        name="scatter",
    )(off, far_rows, nseg, x, norm_ffn, route, tri, xb)


def _experts_kernel(blk_ref, nreal_ref, xb_ref, wg_ref, wu_ref, wd_ref, yb_ref, acc_ref):
    del blk_ref
    i = pl.program_id(0)
    f = pl.program_id(1)

    @pl.when(f == 0)
    def _():
        acc_ref[...] = jnp.zeros_like(acc_ref)

    @pl.when(i < nreal_ref[0])
    def _():
        _swiglu_step(xb_ref[...], wg_ref, wu_ref, wd_ref, acc_ref)

    @pl.when(f == pl.num_programs(1) - 1)
    def _():
        yb_ref[...] = acc_ref[...].astype(yb_ref.dtype)


def _experts(blk_e, nreal, xb, wg, wu, wd, idx, tf):
    n_blocks = blk_e.shape[0]
    d_ff = wg.shape[-1]
    n_f = d_ff // tf

    def live_block(i, nreal):
        return jnp.minimum(i, nreal[0] - 1)

    def wcol(i, f, blk, nreal):
        return idx, blk[live_block(i, nreal)], 0, jnp.where(i < nreal[0], f, n_f - 1)

    def wrow(i, f, blk, nreal):
        return idx, blk[live_block(i, nreal)], jnp.where(i < nreal[0], f, n_f - 1), 0

    return pl.pallas_call(
        _experts_kernel,
        grid_spec=pltpu.PrefetchScalarGridSpec(
            num_scalar_prefetch=2,
            grid=(n_blocks, n_f),
            in_specs=[pl.BlockSpec((MOE_ROWS, D_MODEL), lambda i, f, blk, nreal: (live_block(i, nreal), 0)),
                      pl.BlockSpec((None, None, D_MODEL, tf), wcol),
                      pl.BlockSpec((None, None, D_MODEL, tf), wcol),
                      pl.BlockSpec((None, None, tf, D_MODEL), wrow)],
            out_specs=pl.BlockSpec((MOE_ROWS, D_MODEL), lambda i, f, blk, nreal: (i, 0)),
            scratch_shapes=[pltpu.VMEM((MOE_ROWS, D_MODEL), F32)]),
        out_shape=jax.ShapeDtypeStruct((n_blocks * MOE_ROWS, D_MODEL), BF16),
        compiler_params=_params(("parallel", "arbitrary")),
        name="experts",
    )(blk_e, nreal, xb, wg, wu, wd)


def _combine_kernel(off_ref, far_rows_ref, nseg_ref, x_ref, meta_ref, yb_ref, o_ref, ys_ref, sem, *, ts, cap, n_bits):
    i = pl.program_id(0)
    ys_ref[...] = jnp.zeros_like(ys_ref)
    copies = _segment_copies(i, n_bits, ys_ref, off_ref, yb_ref, far_rows_ref, nseg_ref, sem, False)
    for pred, cp in copies:
        pl.when(pred)(cp.start)
    for pred, cp in copies:
        pl.when(pred)(cp.wait)
    slot = lax.broadcasted_iota(jnp.int32, (ts, cap), 1).astype(F32)
    ys = ys_ref[...]
    q1 = jnp.where(slot == meta_ref[:, 0:1], 1.0, 0.0).astype(BF16)
    q2 = jnp.where(slot == meta_ref[:, 1:2], 1.0, 0.0).astype(BF16)
    o_ref[...] = x_ref[...] + meta_ref[:, 2:3] * _dot(q1, ys) + meta_ref[:, 3:4] * _dot(q2, ys)


def _combine(off, far_rows, nseg, x, meta, yb, ts):
    n = x.shape[0]
    cap = _sorted_cap(ts)
    n_bits = int(TOP_K * ts // SEG_ALIGN).bit_length()
    return pl.pallas_call(
        functools.partial(_combine_kernel, ts=ts, cap=cap, n_bits=n_bits),
        grid_spec=pltpu.PrefetchScalarGridSpec(
            num_scalar_prefetch=3,
            grid=(n // ts,),
            in_specs=[pl.BlockSpec((ts, D_MODEL), lambda i, *_: (i, 0)),
                      pl.BlockSpec((ts, LANES), lambda i, *_: (i, 0)),
                      pl.BlockSpec(memory_space=pl.ANY)],
            out_specs=pl.BlockSpec((ts, D_MODEL), lambda i, *_: (i, 0)),
            scratch_shapes=[pltpu.VMEM((cap, D_MODEL), BF16), pltpu.SemaphoreType.DMA(())]),
        out_shape=jax.ShapeDtypeStruct((n, D_MODEL), F32),
        compiler_params=_params(("arbitrary",)),
        name="combine",
    )(off, far_rows, nseg, x, meta, yb)


def _moe(xs, tss, norm_ffn, rwt, rb, wg, wu, wd, layer, idx, tf):
    gated = [_gate(x, norm_ffn, rwt, rb, layer, idx, ts) for x, ts in zip(xs, tss)]
    off = jnp.concatenate([g[1][:, :, 0] for g in gated], axis=0)
    cpad = jnp.concatenate([g[1][:, :, 1] for g in gated], axis=0)
    n_subs = [g[1].shape[0] for g in gated]
    tot = jnp.sum(cpad, axis=0)
    padded = (tot + MOE_ROWS - 1) // MOE_ROWS * MOE_ROWS
    pend = jnp.cumsum(padded)
    within = jnp.cumsum(cpad, axis=0) - cpad
    off = off.astype(jnp.int32).reshape(-1)
    grouped_rows = ((pend - padded)[None, :] + within).astype(jnp.int32).reshape(-1)
    nseg = (cpad // SEG_ALIGN).astype(jnp.int32).reshape(-1)
    n_assign = TOP_K * sum(x.shape[0] for x in xs)
    n_blocks = (n_assign + (SEG_ALIGN - 1) * sum(n_subs) * N_EXPERTS + N_EXPERTS * (MOE_ROWS - 1)) // MOE_ROWS
    blk_e = jnp.minimum(jnp.sum(jnp.arange(n_blocks)[:, None] * MOE_ROWS >= pend[None, :], axis=1),
                        N_EXPERTS - 1).astype(jnp.int32)
    nreal = (pend[-1:] // MOE_ROWS).astype(jnp.int32)

    xb = jnp.zeros((n_blocks * MOE_ROWS, D_MODEL), BF16)
    metas, parts = [], []
    lo = 0
    for x, g, ns, ts in zip(xs, gated, n_subs, tss):
        sl = slice(lo * N_EXPERTS, (lo + ns) * N_EXPERTS)
        parts.append((off[sl], grouped_rows[sl], nseg[sl]))
        meta, xb = _scatter(*parts[-1], x, norm_ffn, g[0], xb, layer, ts)
        metas.append(meta)
        lo += ns
    yb = _experts(blk_e, nreal, xb, wg, wu, wd, idx, tf)
    return [_combine(*part, x, meta, yb, ts) for part, x, meta, ts in zip(parts, xs, metas, tss)]


def kernel(x_prompt, x_sample, state_pool, cache_kv_g0, cache_kv_g1, cache_kv_g2, norm_attn, w_in, q_norm, k_norm, w_pool, pool_scale, w_branch_pool, w_branch_attn, w_out, norm_ffn, w1_dense, w3_dense, w2_dense, router_w, router_b, we_gate, we_up, we_down):
    batch, seq, _ = x_prompt.shape
    dec_batch, dec_seq, _ = x_sample.shape
    depth = w_in.shape[0]
    caches = (cache_kv_g0, cache_kv_g1, cache_kv_g2)
    n_p, n_s = batch * seq, dec_batch * dec_seq
    tm_p = 512
    dils = tuple(d for _, d in ATT_GROUPS)
    keeps = tuple(min(win, seq) for win, _ in ATT_GROUPS)

    head_of = np.arange(GROUP_W) // HEAD_DIM
    same_head = (head_of[:, None] == head_of[None, :]).astype(np.float32)
    seg_mean = jnp.asarray(same_head / HEAD_DIM, BF16)
    w_qkv = w_in[:, :, :PROJ_W].astype(BF16)
    qgain = (jnp.tile(q_norm, (1, HEADS)) * (HEAD_DIM ** -0.5)).reshape(depth, 1, GROUP_W)
    kgain = jnp.tile(k_norm, (1, HEADS)).reshape(depth, 1, GROUP_W)
    norm_attn3 = norm_attn.reshape(depth, 1, D_MODEL)
    w = {"w_pool": w_pool.astype(BF16), "pool_scale": pool_scale.reshape(depth, 1, POOL_W),
         "norm_attn": norm_attn3, "w_gate": w_in[:, :, PROJ_W:].astype(BF16),
         "w_bp": w_branch_pool.astype(BF16), "w_ba": w_branch_attn.astype(BF16), "w_o": w_out.astype(BF16),
         "norm_ffn": norm_ffn.reshape(depth, 1, D_MODEL)}
    w1_b, w3_b, w2_b = w1_dense.astype(BF16), w3_dense.astype(BF16), w2_dense.astype(BF16)
    wg_b, wu_b, wd_b = we_gate.astype(BF16), we_up.astype(BF16), we_down.astype(BF16)
    rwt = jnp.swapaxes(router_w, 1, 2).astype(BF16)
    rb = router_b.reshape(-1, N_EXPERTS, 1)

    xp = x_prompt.reshape(n_p, D_MODEL)
    xs = x_sample.reshape(n_s, D_MODEL)
    pool_p, pool_s = [], []
    kv_s = [[] for _ in ATT_GROUPS]
    kvo_p = tuple(jnp.zeros((depth, batch, keep, 2 * GROUP_W), F32) for keep in keeps)
    ones = (1,) * N_GROUPS
    for layer in range(depth):
        res = _proj(xp, layer, norm_attn3, w_qkv, qgain, kgain, seg_mean, batch, seq, tm_p, dils, keeps,
                    layer, depth, kvo_p, 1)
        u, qs, kcs, kvo_p = res[0], res[1:4], res[4:7], tuple(res[7:10])
        outs, lses = [], []
        for gi in range(N_GROUPS):
            o, lse = _attention(qs[gi], kcs[gi], gi)
            outs.append(o)
            lses.append(lse)
        pool_p.append(u.reshape(batch, seq, POOL_W)[:, seq - POOL_STATE:])
        xp, xpn = _merge(xp, u, outs, lses, layer, w, batch, seq, tm_p, dils)

        res = _proj(xs, layer, norm_attn3, w_qkv, qgain, kgain, seg_mean, 1, n_s, n_s, ones, (n_s,) * N_GROUPS,
                    0, 1, (), 3)
        u = res[0]
        qs = [q.reshape(dec_batch, dec_seq, HEADS, HEAD_DIM) for q in res[1:4]]
        kvn = [kv.reshape(dec_batch, dec_seq, 2, HEADS, HEAD_DIM) for kv in res[7:10]]
        pool_o, att_o, new_pool = _sample_mix(state_pool, u, qs, kvn, caches, layer, w, dec_batch, dec_seq)
        for gi in range(N_GROUPS):
            kv_s[gi].append(kvn[gi])
        pool_s.append(new_pool)
        xs, xsn = _tail(xs, pool_o.reshape(n_s, POOL_W), att_o.reshape(n_s, GROUP_W), layer, w)

        i = layer // 2
        if layer % 2 == 0:
            xp = _ffn(xp, xpn, w1_b, w3_b, w2_b, i, 1024, 256)
            xs = _ffn(xs, xsn, w1_b, w3_b, w2_b, i, n_s, 256)
        else:
            xp, xs = _moe([xp, xs], [512, n_s], w["norm_ffn"], rwt, rb, wg_b, wu_b, wd_b, layer, i, 512)
    kv_p = [kvo_p[g].reshape(depth, batch, keeps[g], 2, HEADS, HEAD_DIM) for g in range(N_GROUPS)]
    return (xp.reshape(batch, seq, D_MODEL), xs.reshape(dec_batch, dec_seq, D_MODEL),
            jnp.stack(pool_p), jnp.stack(pool_s),
            kv_p[0], jnp.stack(kv_s[0]),
            kv_p[1], jnp.stack(kv_s[1]),
            kv_p[2], jnp.stack(kv_s[2]))
```

```python
import functools

import numpy as np
import jax
import jax.numpy as jnp
from jax import lax
from jax.experimental import pallas as pl
from jax.experimental.pallas import tpu as pltpu

F32 = jnp.float32
BF16 = jnp.bfloat16

D_MODEL = 1024
PAST_LEN = 16384
POOL_WINDOWS = (2, 4, 8, 16)
POOL_GROUP = 128
POOL_W = 512
POOL_STATE = 15
ATT_GROUPS = ((128, 1), (512, 4), (2048, 16))
N_GROUPS = len(ATT_GROUPS)
HEAD_DIM = 64
HEADS = 8
GROUP_W = 512
QKV_W = 1536
Q_BLOCK = 128
ALIBI_MAX = 8.0
N_EXPERTS = 8
TOP_K = 2
RMS_EPS = 1e-6
NEG_INF = -1e30
LANES = 128
LANE_CHUNKS = GROUP_W // LANES
PROJ_W = POOL_W + 3 * QKV_W
VMEM_LIMIT = 56 * 1024 * 1024


def _slopes():
    i = np.arange(1, N_GROUPS * HEADS + 1, dtype=np.float32)
    return np.exp2(-ALIBI_MAX * i / (N_GROUPS * HEADS)).astype(np.float32).reshape(N_GROUPS, HEADS)


def _params(sem):
    return pltpu.CompilerParams(dimension_semantics=sem, vmem_limit_bytes=VMEM_LIMIT)


def _rms(x, gain):
    return x * lax.rsqrt(jnp.mean(x * x, axis=-1, keepdims=True) + RMS_EPS) * gain


def _dot(a, b):
    return jnp.dot(a, b, preferred_element_type=F32)


def _resident(shape, index_map):
    return pl.BlockSpec(shape, index_map, pipeline_mode=pl.Buffered(1))


def _write_classes(dst_ref, col0, val, dil, tmp_ref, slot):
    rows = val.shape[0] // dil
    cols = slice(col0, col0 + GROUP_W)
    if dil == 1:
        dst_ref[0, :, cols] = val.astype(dst_ref.dtype)
        return
    for c in range(LANE_CHUNKS):
        tmp_ref[slot, c] = val[:, c * LANES:(c + 1) * LANES]
    for r in range(dil):
        picked = [tmp_ref[slot, c, pl.ds(r, rows, stride=dil), :] for c in range(LANE_CHUNKS)]
        dst_ref[r, :, cols] = jnp.concatenate(picked, axis=1).astype(dst_ref.dtype)


def _proj_kernel(x_ref, g_ref, w_ref, qg_ref, kg_ref, seg_ref, *rest, tm, dils, keeps, n_alias, norm_terms,
                 kv_positions_minor):
    u_ref, q0_ref, q1_ref, q2_ref, kc0_ref, kc1_ref, kc2_ref, kvo0_ref, kvo1_ref, kvo2_ref, tmp_ref = rest[n_alias:]
    s = pl.program_id(1)
    n_tiles = pl.num_programs(1)
    xn = _rms(x_ref[...], g_ref[...]).astype(BF16)

    def zblk(j):
        return _dot(xn, w_ref[:, j * GROUP_W:(j + 1) * GROUP_W])

    def headnorm(z, gain):
        rem = z * z
        ms = None
        for _ in range(norm_terms):
            part = rem.astype(BF16)
            rem = rem - part.astype(F32)
            ms = _dot(part, seg_ref[...]) if ms is None else ms + _dot(part, seg_ref[...])
        return z * lax.rsqrt(ms + RMS_EPS) * gain

    u_ref[...] = zblk(0)
    q_refs = (q0_ref, q1_ref, q2_ref)
    kc_refs = (kc0_ref, kc1_ref, kc2_ref)
    kvo_refs = (kvo0_ref, kvo1_ref, kvo2_ref)
    slot = 0
    for g in range(N_GROUPS):
        q = headnorm(zblk(1 + g), qg_ref[...])
        k = headnorm(zblk(1 + N_GROUPS + g), kg_ref[...])
        v = zblk(1 + 2 * N_GROUPS + g)
        for dst, col0, val in ((q_refs[g], 0, q), (kc_refs[g], 0, k), (kc_refs[g], GROUP_W, v)):
            _write_classes(dst, col0, val, dils[g], tmp_ref, slot % tmp_ref.shape[0])
            slot += dils[g] > 1
        keep = keeps[g]
        first = n_tiles - max(keep // tm, 1)
        rows = min(keep, tm)

        @pl.when(s >= first)
        def _(k=k, v=v, ref=kvo_refs[g], rows=rows):
            if kv_positions_minor:
                ref[0] = k[tm - rows:, :].T
                ref[1] = v[tm - rows:, :].T
            else:
                ref[:, 0:GROUP_W] = k[tm - rows:, :]
                ref[:, GROUP_W:2 * GROUP_W] = v[tm - rows:, :]


def _proj(x, layer, norm_attn, w_qkv, qgain, kgain, seg, batch, seq, tm, dils, keeps, out_layer, out_depth, prev_kvo,
          norm_terms, kv_positions_minor):
    n_tiles = seq // tm
    kvo_specs, kvo_shapes = [], []
    for g in range(N_GROUPS):
        keep = keeps[g]
        assert keep % tm == 0 or (keep < tm and keep % LANES == 0)
        first = n_tiles - max(keep // tm, 1)
        rows = min(keep, tm)
        if kv_positions_minor:
            kvo_specs.append(pl.BlockSpec((None, None, 2, GROUP_W, rows),
                                          lambda b, s, first=first: (out_layer, b, 0, 0, jnp.maximum(s - first, 0))))
            kvo_shapes.append(jax.ShapeDtypeStruct((out_depth, batch, 2, GROUP_W, keep), F32))
        else:
            kvo_specs.append(pl.BlockSpec((None, None, rows, 2 * GROUP_W),
                                          lambda b, s, first=first: (out_layer, b, jnp.maximum(s - first, 0), 0)))
            kvo_shapes.append(jax.ShapeDtypeStruct((out_depth, batch, keep, 2 * GROUP_W), F32))
    cls_spec = lambda g, width: pl.BlockSpec((None, dils[g], tm // dils[g], width), lambda b, s: (b, 0, s, 0))
    cls_shape = lambda g, width: jax.ShapeDtypeStruct((batch, dils[g], seq // dils[g], width), BF16)
    n_alias = len(prev_kvo)
    n_in = 6
    return pl.pallas_call(
        functools.partial(_proj_kernel, tm=tm, dils=dils, keeps=keeps, n_alias=n_alias,
                          norm_terms=norm_terms, kv_positions_minor=kv_positions_minor),
        grid=(batch, n_tiles),
        in_specs=[
            pl.BlockSpec((tm, D_MODEL), lambda b, s: (b * n_tiles + s, 0)),
            _resident((None, 1, D_MODEL), lambda b, s: (layer, 0, 0)),
            _resident((None, D_MODEL, PROJ_W), lambda b, s: (layer, 0, 0)),
            _resident((None, 1, GROUP_W), lambda b, s: (layer, 0, 0)),
            _resident((None, 1, GROUP_W), lambda b, s: (layer, 0, 0)),
            _resident((GROUP_W, GROUP_W), lambda b, s: (0, 0)),
        ] + [pl.BlockSpec(memory_space=pl.ANY)] * n_alias,
        out_specs=[pl.BlockSpec((tm, POOL_W), lambda b, s: (b * n_tiles + s, 0))]
                  + [cls_spec(g, GROUP_W) for g in range(N_GROUPS)]
                  + [cls_spec(g, 2 * GROUP_W) for g in range(N_GROUPS)] + kvo_specs,
        out_shape=[jax.ShapeDtypeStruct((batch * seq, POOL_W), F32)]
                  + [cls_shape(g, GROUP_W) for g in range(N_GROUPS)]
                  + [cls_shape(g, 2 * GROUP_W) for g in range(N_GROUPS)] + kvo_shapes,
        scratch_shapes=[pltpu.VMEM((3, LANE_CHUNKS, tm, LANES), F32)],
        input_output_aliases={n_in + g: 1 + 2 * N_GROUPS + g for g in range(n_alias)},
        compiler_params=_params(("parallel", "arbitrary")),
        name="proj",
    )(x, norm_attn, w_qkv, qgain, kgain, seg, *prev_kvo)


def _attn_kernel(q_ref, kvc_ref, *rest, slope_dil, tc, has_halo):
    if has_halo:
        kvh_ref, o_ref, lse_ref, kbuf, vbuf = rest
    else:
        o_ref, lse_ref, kbuf, vbuf = rest
    off = Q_BLOCK if has_halo else 0
    if has_halo:
        kbuf[0:Q_BLOCK, :] = kvh_ref[:, 0:GROUP_W]
        vbuf[0:Q_BLOCK, :] = kvh_ref[:, GROUP_W:2 * GROUP_W]
    kbuf[off:off + tc, :] = kvc_ref[:, 0:GROUP_W]
    vbuf[off:off + tc, :] = kvc_ref[:, GROUP_W:2 * GROUP_W]

    nk = Q_BLOCK + off
    qi = lax.broadcasted_iota(jnp.int32, (Q_BLOCK, nk), 0)
    cj = lax.broadcasted_iota(jnp.int32, (Q_BLOCK, nk), 1)
    dist = qi - cj + off
    distf = dist.astype(F32)
    maskneg = jnp.where((dist >= 0) & (dist <= Q_BLOCK), 0.0, NEG_INF).astype(F32)
    if has_halo:
        first = jnp.where(pl.program_id(2) == 0, NEG_INF, 0.0).astype(F32)
        mask_first = maskneg + jnp.where(cj < Q_BLOCK, first, 0.0)

    for i in range(tc // Q_BLOCK):
        mask = mask_first if (has_halo and i == 0) else maskneg
        rows = slice(i * Q_BLOCK, (i + 1) * Q_BLOCK)
        krows = slice(i * Q_BLOCK, i * Q_BLOCK + nk)
        for hp in range(HEADS // 2):
            outs, lses = [], []
            for h in (2 * hp, 2 * hp + 1):
                cols = slice(h * HEAD_DIM, (h + 1) * HEAD_DIM)
                s = lax.dot_general(q_ref[rows, cols], kbuf[krows, cols],
                                    (((1,), (1,)), ((), ())), preferred_element_type=F32)
                s = s - slope_dil[h] * distf + mask
                m = jnp.max(s, axis=1, keepdims=True)
                p = jnp.exp(s - m)
                l = jnp.sum(p, axis=1, keepdims=True)
                o = _dot(p.astype(BF16), vbuf[krows, cols]) / l
                outs.append(o)
                lses.append(jnp.broadcast_to(m + jnp.log(l), (Q_BLOCK, HEAD_DIM)))
            pc = slice(hp * 2 * HEAD_DIM, (hp + 1) * 2 * HEAD_DIM)
            o_ref[rows, pc] = jnp.concatenate(outs, axis=1).astype(BF16)
            lse_ref[rows, pc] = jnp.concatenate(lses, axis=1)


def _attention(q, kc, gi):
    win, dil = ATT_GROUPS[gi]
    assert win // dil == Q_BLOCK
    batch, _, l, _ = q.shape
    tc = min(l, 512)
    has_halo = l > Q_BLOCK
    slope_dil = tuple(float(s) * dil for s in _slopes()[gi])
    in_specs = [
        pl.BlockSpec((None, None, tc, GROUP_W), lambda b, r, c: (b, r, c, 0)),
        pl.BlockSpec((None, None, tc, 2 * GROUP_W), lambda b, r, c: (b, r, c, 0)),
    ]
    args = [q, kc]
    if has_halo:
        in_specs.append(pl.BlockSpec((None, None, Q_BLOCK, 2 * GROUP_W),
                                     lambda b, r, c: (b, r, jnp.maximum(c * (tc // Q_BLOCK) - 1, 0), 0)))
        args.append(kc)
    nk = tc + (Q_BLOCK if has_halo else 0)
    return pl.pallas_call(
        functools.partial(_attn_kernel, slope_dil=slope_dil, tc=tc, has_halo=has_halo),
        grid=(batch, dil, l // tc),
        in_specs=in_specs,
        out_specs=[pl.BlockSpec((None, None, tc, GROUP_W), lambda b, r, c: (b, r, c, 0))] * 2,
        out_shape=[jax.ShapeDtypeStruct((batch, dil, l, GROUP_W), BF16),
                   jax.ShapeDtypeStruct((batch, dil, l, GROUP_W), F32)],
        scratch_shapes=[pltpu.VMEM((nk, GROUP_W), BF16), pltpu.VMEM((nk, GROUP_W), BF16)],
        compiler_params=_params(("parallel", "parallel", "arbitrary")),
        name=f"attn_g{gi}",
    )(*args)


def _mix_tail(x, pool_o, att_o, nattn_ref, wgate_ref, wbp_ref, wba_ref, wo_ref, nffn_ref, xo_ref, xn_ref):
    gates = jax.nn.sigmoid(_dot(_rms(x, nattn_ref[...]).astype(BF16), wgate_ref[...]))
    hp = _dot(pool_o.astype(BF16), wbp_ref[...])
    ha = _dot(att_o.astype(BF16), wba_ref[...])
    t = gates[:, 0:D_MODEL] * hp + gates[:, D_MODEL:2 * D_MODEL] * ha
    xo = x + _dot(t.astype(BF16), wo_ref[...])
    xo_ref[...] = xo
    xn_ref[...] = _rms(xo, nffn_ref[...]).astype(BF16)


def _group_linear(m, wpool_ref, pscale_ref):
    parts = [_dot(m[:, gi * POOL_GROUP:(gi + 1) * POOL_GROUP].astype(BF16), wpool_ref[gi])
             for gi in range(len(POOL_WINDOWS))]
    return jnp.concatenate(parts, axis=1) * pscale_ref[...]


def _read_classes(src_ref, dil, il_ref, slot):
    if dil == 1:
        return src_ref[0].astype(F32)
    rows = src_ref.shape[1]
    for r in range(dil):
        v = src_ref[r].astype(F32)
        for c in range(LANE_CHUNKS):
            il_ref[slot, c, pl.ds(r, rows, stride=dil), :] = v[:, c * LANES:(c + 1) * LANES]
    return jnp.concatenate([il_ref[slot, c] for c in range(LANE_CHUNKS)], axis=1)


def _merge_kernel(x_ref, u_ref, uh_ref, o0_ref, o1_ref, o2_ref, l0_ref, l1_ref, l2_ref,
                  wpool_ref, pscale_ref, nattn_ref, wgate_ref, wbp_ref, wba_ref, wo_ref, nffn_ref,
                  xo_ref, xn_ref, ext_ref, il_ref, *, tm, dils):
    si = pl.program_id(1)
    halo = POOL_STATE + 1
    ext_ref[0:halo, :] = jnp.where(si == 0, 0.0, uh_ref[...])
    ext_ref[halo:halo + tm, :] = u_ref[...]
    pos = si * tm + lax.broadcasted_iota(jnp.int32, (tm, 1), 0)
    parts = []
    for gi, win in enumerate(POOL_WINDOWS):
        cols = slice(gi * POOL_GROUP, (gi + 1) * POOL_GROUP)
        own = ext_ref[halo:halo + tm, cols]
        acc = own
        for back in range(1, win):
            acc = acc + ext_ref[halo - back:halo - back + tm, cols]
        inv = 1.0 / jnp.minimum(pos + 1, win).astype(F32)
        parts.append(acc * inv - own)
    pool_o = _group_linear(jnp.concatenate(parts, axis=1), wpool_ref, pscale_ref)

    slot = 0
    os_, ls_ = [], []
    for g, (o_ref, l_ref) in enumerate(((o0_ref, l0_ref), (o1_ref, l1_ref), (o2_ref, l2_ref))):
        os_.append(_read_classes(o_ref, dils[g], il_ref, slot))
        slot += dils[g] > 1
        ls_.append(_read_classes(l_ref, dils[g], il_ref, slot))
        slot += dils[g] > 1
    mx = jnp.maximum(ls_[0], jnp.maximum(ls_[1], ls_[2]))
    es = [jnp.exp(v - mx) for v in ls_]
    att_o = (es[0] * os_[0] + es[1] * os_[1] + es[2] * os_[2]) / (es[0] + es[1] + es[2])
    _mix_tail(x_ref[...], pool_o, att_o, nattn_ref, wgate_ref, wbp_ref, wba_ref, wo_ref, nffn_ref, xo_ref, xn_ref)


def _wspec(shape, layer, grid_rank):
    nd = len(shape)
    return _resident((None,) + tuple(shape), lambda *_: (layer,) + (0,) * nd)


def _tail_weight_specs(layer, grid_rank):
    return [_wspec((1, D_MODEL), layer, grid_rank), _wspec((D_MODEL, 2 * D_MODEL), layer, grid_rank),
            _wspec((POOL_W, D_MODEL), layer, grid_rank), _wspec((GROUP_W, D_MODEL), layer, grid_rank),
            _wspec((D_MODEL, D_MODEL), layer, grid_rank), _wspec((1, D_MODEL), layer, grid_rank)]


def _tail_weights(w):
    return (w["norm_attn"], w["w_gate"], w["w_bp"], w["w_ba"], w["w_o"], w["norm_ffn"])


def _merge(x, u, outs, lses, layer, w, batch, seq, tm, dils):
    n = x.shape[0]
    n_tiles = seq // tm
    halo = POOL_STATE + 1
    row = lambda width: pl.BlockSpec((tm, width), lambda b, s: (b * n_tiles + s, 0))
    cls = lambda g: pl.BlockSpec((None, dils[g], tm // dils[g], GROUP_W), lambda b, s: (b, 0, s, 0))
    n_il = 2 * sum(d > 1 for d in dils)
    return pl.pallas_call(
        functools.partial(_merge_kernel, tm=tm, dils=dils),
        grid=(batch, n_tiles),
        in_specs=[row(D_MODEL), row(POOL_W),
                  pl.BlockSpec((halo, POOL_W),
                               lambda b, s: (jnp.maximum((b * n_tiles + s) * (tm // halo) - 1, 0), 0))]
                 + [cls(g) for g in range(N_GROUPS)] * 2
                 + [_wspec((len(POOL_WINDOWS), POOL_GROUP, POOL_GROUP), layer, 2), _wspec((1, POOL_W), layer, 2)]
                 + _tail_weight_specs(layer, 2),
        out_specs=[row(D_MODEL), row(D_MODEL)],
        out_shape=[jax.ShapeDtypeStruct((n, D_MODEL), F32), jax.ShapeDtypeStruct((n, D_MODEL), BF16)],
        scratch_shapes=[pltpu.VMEM((halo + tm, POOL_W), F32), pltpu.VMEM((n_il, LANE_CHUNKS, tm, LANES), F32)],
        compiler_params=_params(("parallel", "parallel")),
        name="merge",
    )(x, u, u, *outs, *lses, w["w_pool"], w["pool_scale"], *_tail_weights(w))


def _sample_keys(dec_seq, cache_rows):
    geo = []
    for gi, (win, dil) in enumerate(ATT_GROUPS):
        lc = cache_rows[gi]
        assert lc % dil == 0 and lc // dil == Q_BLOCK
        per_token = []
        for t in range(dec_seq):
            cls = (lc + t) % dil
            new = tuple(t2 for t2 in range(t + 1) if (t - t2) % dil == 0 and (t - t2) // dil <= win // dil)
            per_token.append((cls, (lc + t - cls) // dil, new))
        geo.append(tuple(per_token))
    return tuple(geo)


def _bf16_valued(x):
    return x.astype(BF16).astype(F32)


def _sample_mix_kernel(state_ref, u_ref, q0_ref, q1_ref, q2_ref, kn0_ref, kn1_ref, kn2_ref, c0_ref, c1_ref, c2_ref,
                       slope_ref, wpool_ref, pscale_ref,
                       pool_ref, att_ref, newpool_ref, ext_ref, m_ref, *, dec_seq, geo):
    ext_ref[...] = jnp.zeros_like(ext_ref)
    ext_ref[0:POOL_STATE, :] = state_ref[...]
    ext_ref[POOL_STATE:POOL_STATE + dec_seq, :] = u_ref[...]
    newpool_ref[...] = ext_ref[dec_seq:dec_seq + POOL_STATE, :]

    m_ref[...] = jnp.zeros_like(m_ref)
    for t in range(dec_seq):
        row = POOL_STATE + t
        for gi, win in enumerate(POOL_WINDOWS):
            cols = slice(gi * POOL_GROUP, (gi + 1) * POOL_GROUP)
            tot = jnp.sum(ext_ref[row - win + 1:row + 1, cols], axis=0, keepdims=True)
            cnt = float(min(PAST_LEN + t + 1, win))
            m_ref[t:t + 1, cols] = tot / cnt - ext_ref[row:row + 1, cols]
    pool_ref[...] = _group_linear(m_ref[...], wpool_ref, pscale_ref)[0:dec_seq]

    groups = ((q0_ref, kn0_ref, c0_ref), (q1_ref, kn1_ref, c1_ref), (q2_ref, kn2_ref, c2_ref))
    for t in range(dec_seq):
        outs, lses = [], []
        for gi, (q_ref, kn_ref, c_ref) in enumerate(groups):
            win, dil = ATT_GROUPS[gi]
            cls, r0, new = geo[gi][t]
            slope = slope_ref[gi]
            qv = q_ref[t].astype(F32)
            s = jnp.sum(_bf16_valued(c_ref[:, cls, 0]) * qv[None], axis=-1, keepdims=True)
            back = r0 - lax.broadcasted_iota(jnp.int32, s.shape, 0)
            s = jnp.where((back >= 1) & (back <= win // dil), s - slope[None] * (back * dil).astype(F32), NEG_INF)
            s_new = [jnp.sum(_bf16_valued(kn_ref[t2, 0]) * qv, axis=-1, keepdims=True) - slope * float(t - t2)
                     for t2 in new]
            mx = jnp.max(s, axis=0)
            for v in s_new:
                mx = jnp.maximum(mx, v)
            p = jnp.exp(s - mx[None])
            p_new = [jnp.exp(v - mx) for v in s_new]
            l = jnp.sum(p, axis=0) + sum(p_new)
            o = jnp.sum(_bf16_valued(p / l[None]) * _bf16_valued(c_ref[:, cls, 1]), axis=0)
            for t2, v in zip(new, p_new):
                o = o + _bf16_valued(v / l) * _bf16_valued(kn_ref[t2, 1])
            outs.append(o)
            lses.append(mx + jnp.log(l))
        mx = jnp.maximum(lses[0], jnp.maximum(lses[1], lses[2]))
        es = [jnp.exp(v - mx) for v in lses]
        att_ref[t] = (es[0] * outs[0] + es[1] * outs[1] + es[2] * outs[2]) / (es[0] + es[1] + es[2])


def _sample_mix(state_pool, u, qs, kvn, caches, layer, w, dec_batch, dec_seq):
    cache_rows = [c.shape[2] for c in caches]
    geo = _sample_keys(dec_seq, cache_rows)
    kv_tail = (2, HEADS, HEAD_DIM)
    cviews, cspecs = [], []
    for gi, (win, dil) in enumerate(ATT_GROUPS):
        lc = cache_rows[gi]
        cviews.append(caches[gi].reshape((-1, lc // dil, dil) + kv_tail))
        ncls = max(cls for cls, _, _ in geo[gi]) + 1
        cspecs.append(pl.BlockSpec((None, lc // dil, min(dil, ncls)) + kv_tail,
                                   lambda b: (layer * dec_batch + b, 0, 0, 0, 0, 0)))
    per_tok = lambda *tail: pl.BlockSpec((None, dec_seq) + tail, lambda b: (b, 0) + (0,) * len(tail))
    return pl.pallas_call(
        functools.partial(_sample_mix_kernel, dec_seq=dec_seq, geo=geo),
        grid=(dec_batch,),
        in_specs=[pl.BlockSpec((None, POOL_STATE, POOL_W), lambda b: (layer * dec_batch + b, 0, 0)),
                  per_tok(POOL_W)] + [per_tok(HEADS, HEAD_DIM)] * 3 + [per_tok(*kv_tail)] * 3 + cspecs
                 + [_resident((N_GROUPS, HEADS, 1), lambda b: (0, 0, 0)),
                    _wspec((len(POOL_WINDOWS), POOL_GROUP, POOL_GROUP), layer, 1), _wspec((1, POOL_W), layer, 1)],
        out_specs=[per_tok(POOL_W), per_tok(HEADS, HEAD_DIM),
                   pl.BlockSpec((None, POOL_STATE, POOL_W), lambda b: (b, 0, 0))],
        out_shape=[jax.ShapeDtypeStruct((dec_batch, dec_seq, POOL_W), F32),
                   jax.ShapeDtypeStruct((dec_batch, dec_seq, HEADS, HEAD_DIM), F32),
                   jax.ShapeDtypeStruct((dec_batch, POOL_STATE, POOL_W), F32)],
        scratch_shapes=[pltpu.VMEM((POOL_STATE + dec_seq + 5, POOL_W), F32), pltpu.VMEM((8, POOL_W), F32)],
        compiler_params=_params(("parallel",)),
        name="sample_mix",
    )(state_pool.reshape(-1, POOL_STATE, POOL_W), u.reshape(dec_batch, dec_seq, POOL_W), *qs, *kvn, *cviews,
      jnp.asarray(_slopes()).reshape(N_GROUPS, HEADS, 1), w["w_pool"], w["pool_scale"])


def _tail_kernel(x_ref, pool_ref, att_ref, nattn_ref, wgate_ref, wbp_ref, wba_ref, wo_ref, nffn_ref, xo_ref, xn_ref):
    _mix_tail(x_ref[...], pool_ref[...], att_ref[...], nattn_ref, wgate_ref, wbp_ref, wba_ref, wo_ref, nffn_ref,
              xo_ref, xn_ref)


def _tail(x, pool_o, att_o, layer, w):
    n = x.shape[0]
    row = lambda width: pl.BlockSpec((n, width), lambda i: (0, 0))
    return pl.pallas_call(
        _tail_kernel,
        grid=(1,),
        in_specs=[row(D_MODEL), row(POOL_W), row(GROUP_W)] + _tail_weight_specs(layer, 1),
        out_specs=[row(D_MODEL), row(D_MODEL)],
        out_shape=[jax.ShapeDtypeStruct((n, D_MODEL), F32), jax.ShapeDtypeStruct((n, D_MODEL), BF16)],
        compiler_params=_params(("arbitrary",)),
        name="tail",
    )(x, pool_o, att_o, *_tail_weights(w))


def _swiglu_step(x, wg_ref, wu_ref, wd_ref, acc_ref):
    h = jax.nn.silu(_dot(x, wg_ref[...])) * _dot(x, wu_ref[...])
    acc_ref[...] += _dot(h.astype(BF16), wd_ref[...])


def _ffn_kernel(x_ref, xn_ref, w1_ref, w3_ref, w2_ref, y_ref, acc_ref):
    f = pl.program_id(1)

    @pl.when(f == 0)
    def _():
        acc_ref[...] = x_ref[...]

    _swiglu_step(xn_ref[...], w1_ref, w3_ref, w2_ref, acc_ref)

    @pl.when(f == pl.num_programs(1) - 1)
    def _():
        y_ref[...] = acc_ref[...]


def _ffn(x, xn, w1, w3, w2, idx, tm, tf):
    n = x.shape[0]
    d_ff = w1.shape[-1]
    return pl.pallas_call(
        _ffn_kernel,
        grid=(n // tm, d_ff // tf),
        in_specs=[pl.BlockSpec((tm, D_MODEL), lambda i, f: (i, 0)),
                  pl.BlockSpec((tm, D_MODEL), lambda i, f: (i, 0)),
                  pl.BlockSpec((None, D_MODEL, tf), lambda i, f: (idx, 0, f)),
                  pl.BlockSpec((None, D_MODEL, tf), lambda i, f: (idx, 0, f)),
                  pl.BlockSpec((None, tf, D_MODEL), lambda i, f: (idx, f, 0))],
        out_specs=pl.BlockSpec((tm, D_MODEL), lambda i, f: (i, 0)),
        out_shape=jax.ShapeDtypeStruct((n, D_MODEL), F32),
        scratch_shapes=[pltpu.VMEM((tm, D_MODEL), F32)],
        compiler_params=_params(("parallel", "arbitrary")),
        name="ffn",
    )(x, xn, w1, w3, w2)


SEG_ALIGN = 16
MOE_ROWS = 512


def _sorted_cap(ts):
    need = TOP_K * ts + N_EXPERTS * (SEG_ALIGN - 1)
    return -(-need // LANES) * LANES if ts >= LANES * 2 else -(-need // SEG_ALIGN) * SEG_ALIGN


def _segment_sizes(oh1, oh2):
    cnt1 = jnp.sum(oh1, axis=1, keepdims=True)
    cnt = cnt1 + jnp.sum(oh2, axis=1, keepdims=True)
    cpad = jnp.floor((cnt + (SEG_ALIGN - 1)) * (1.0 / SEG_ALIGN)) * SEG_ALIGN
    offs = [jnp.zeros((1, 1), F32)]
    for ei in range(1, N_EXPERTS):
        offs.append(offs[-1] + cpad[ei - 1:ei, :])
    return cnt1, cpad, jnp.concatenate(offs, axis=0)


def _gate_kernel(x_ref, g_ref, rwt_ref, rb_ref, route_ref, seg_ref, *, ts):
    xn = _rms(x_ref[...], g_ref[...]).astype(BF16)
    logits = lax.dot_general(rwt_ref[...], xn, (((1,), (1,)), ((), ())), preferred_element_type=F32) + rb_ref[...]
    row = lax.broadcasted_iota(jnp.int32, (N_EXPERTS, ts), 0)
    neg = jnp.float32(-jnp.inf)
    m1 = jnp.max(logits, axis=0, keepdims=True)
    i1 = jnp.min(jnp.where(logits == m1, row, N_EXPERTS), axis=0, keepdims=True)
    rest = jnp.where(row == i1, neg, logits)
    m2 = jnp.max(rest, axis=0, keepdims=True)
    i2 = jnp.min(jnp.where(rest == m2, row, N_EXPERTS), axis=0, keepdims=True)
    e = jnp.exp(m2 - m1)
    zeros = jnp.zeros((N_EXPERTS - 4, ts), F32)
    route_ref[...] = jnp.concatenate([i1.astype(F32), i2.astype(F32), 1.0 / (1.0 + e), e / (1.0 + e), zeros], axis=0)
    _, cpad, off = _segment_sizes(jnp.where(row == i1, 1.0, 0.0), jnp.where(row == i2, 1.0, 0.0))
    lane = lax.broadcasted_iota(jnp.int32, (N_EXPERTS, LANES), 1)
    seg_ref[...] = jnp.where(lane == 0, off, jnp.where(lane == 1, cpad, 0.0)).astype(jnp.int32)


def _gate(x, norm_ffn, rwt, rb, layer, idx, ts):
    n = x.shape[0]
    n_sub = n // ts
    return pl.pallas_call(
        functools.partial(_gate_kernel, ts=ts),
        grid=(n_sub,),
        in_specs=[pl.BlockSpec((ts, D_MODEL), lambda i: (i, 0)),
                  _resident((None, 1, D_MODEL), lambda i: (layer, 0, 0)),
                  _resident((None, N_EXPERTS, D_MODEL), lambda i: (idx, 0, 0)),
                  _resident((None, N_EXPERTS, 1), lambda i: (idx, 0, 0))],
        out_specs=[pl.BlockSpec((None, N_EXPERTS, ts), lambda i: (i, 0, 0)),
                   pl.BlockSpec((None, N_EXPERTS, LANES), lambda i: (i, 0, 0))],
        out_shape=[jax.ShapeDtypeStruct((n_sub, N_EXPERTS, ts), F32),
                   jax.ShapeDtypeStruct((n_sub, N_EXPERTS, LANES), jnp.int32)],
        compiler_params=_params(("parallel",)),
        name="gate",
    )(x, norm_ffn, rwt, rb)


def _segment_copies(i, n_bits, local_ref, off_ref, far_ref, far_rows_ref, nseg_ref, sem, to_far):
    copies = []
    for e in range(N_EXPERTS):
        j = i * N_EXPERTS + e
        n = nseg_ref[j]
        for b in reversed(range(n_bits)):
            size = SEG_ALIGN << b
            done = ((n >> (b + 1)) << (b + 1)) * SEG_ALIGN
            near = local_ref.at[pl.ds(pl.multiple_of(off_ref[j] + done, SEG_ALIGN), size)]
            far = far_ref.at[pl.ds(pl.multiple_of(far_rows_ref[j] + done, SEG_ALIGN), size)]
            cp = pltpu.make_async_copy(near, far, sem) if to_far else pltpu.make_async_copy(far, near, sem)
            copies.append((((n >> b) & 1) == 1, cp))
    return copies


def _scatter_kernel(off_ref, far_rows_ref, nseg_ref, x_ref, g_ref, route_ref, tri_ref, xb_in_ref,
                    meta_ref, xb_ref, xs_ref, sem, *, ts, cap, n_bits):
    del xb_in_ref
    i = pl.program_id(0)
    xn = _rms(x_ref[...], g_ref[...]).astype(BF16)
    row = lax.broadcasted_iota(jnp.int32, (N_EXPERTS, ts), 0).astype(F32)
    oh1 = jnp.where(row == route_ref[0:1, :], 1.0, 0.0)
    oh2 = jnp.where(row == route_ref[1:2, :], 1.0, 0.0)
    cnt1, _, off = _segment_sizes(oh1, oh2)
    pre1 = _dot(oh1.astype(BF16), tri_ref[...])
    pre2 = _dot(oh2.astype(BF16), tri_ref[...])
    d1 = jnp.sum(oh1 * (off + pre1), axis=0, keepdims=True)
    d2 = jnp.sum(oh2 * (off + cnt1 + pre2), axis=0, keepdims=True)
    slot = lax.broadcasted_iota(jnp.int32, (cap, ts), 0).astype(F32)
    p = jnp.where(slot == d1, 1.0, jnp.where(slot == d2, 1.0, 0.0)).astype(BF16)
    xs_ref[...] = _dot(p, xn).astype(BF16)
    meta_t = jnp.concatenate([d1, d2, route_ref[2:4, :], jnp.zeros((LANES - 4, ts), F32)], axis=0)
    meta_ref[...] = meta_t.T
    copies = _segment_copies(i, n_bits, xs_ref, off_ref, xb_ref, far_rows_ref, nseg_ref, sem, True)
    for pred, cp in copies:
        pl.when(pred)(cp.start)
    for pred, cp in copies:
        pl.when(pred)(cp.wait)


def _scatter(off, far_rows, nseg, x, norm_ffn, route, xb, layer, ts):
    n = x.shape[0]
    n_sub = n // ts
    cap = _sorted_cap(ts)
    tri = jnp.asarray(np.triu(np.ones((ts, ts), np.float32), 1), BF16)
    n_bits = int(TOP_K * ts // SEG_ALIGN).bit_length()
    return pl.pallas_call(
        functools.partial(_scatter_kernel, ts=ts, cap=cap, n_bits=n_bits),
        grid_spec=pltpu.PrefetchScalarGridSpec(
            num_scalar_prefetch=3,
            grid=(n_sub,),
            in_specs=[pl.BlockSpec((ts, D_MODEL), lambda i, *_: (i, 0)),
                      _resident((None, 1, D_MODEL), lambda i, *_: (layer, 0, 0)),
                      pl.BlockSpec((None, N_EXPERTS, ts), lambda i, *_: (i, 0, 0)),
                      _resident((ts, ts), lambda i, *_: (0, 0)),
                      pl.BlockSpec(memory_space=pl.ANY)],
            out_specs=[pl.BlockSpec((ts, LANES), lambda i, *_: (i, 0)),
                       pl.BlockSpec(memory_space=pl.ANY)],
            scratch_shapes=[pltpu.VMEM((cap, D_MODEL), BF16), pltpu.SemaphoreType.DMA(())]),
        out_shape=[jax.ShapeDtypeStruct((n, LANES), F32), jax.ShapeDtypeStruct(xb.shape, xb.dtype)],
        input_output_aliases={7: 1},
        compiler_params=_params(("arbitrary",)),
        name="scatter",
    )(off, far_rows, nseg, x, norm_ffn, route, tri, xb)


def _experts_kernel(blk_ref, nreal_ref, xb_ref, wg_ref, wu_ref, wd_ref, yb_ref, acc_ref):
    del blk_ref
    i = pl.program_id(0)
    f = pl.program_id(1)

    @pl.when(f == 0)
    def _():
        acc_ref[...] = jnp.zeros_like(acc_ref)

    @pl.when(i < nreal_ref[0])
    def _():
        _swiglu_step(xb_ref[...], wg_ref, wu_ref, wd_ref, acc_ref)

    @pl.when(f == pl.num_programs(1) - 1)
    def _():
        yb_ref[...] = acc_ref[...].astype(yb_ref.dtype)


def _experts(blk_e, nreal, xb, wg, wu, wd, idx, tf):
    n_blocks = blk_e.shape[0]
    d_ff = wg.shape[-1]
    n_f = d_ff // tf

    def live_block(i, nreal):
        return jnp.minimum(i, nreal[0] - 1)

    def wcol(i, f, blk, nreal):
        return idx, blk[live_block(i, nreal)], 0, jnp.where(i < nreal[0], f, n_f - 1)

    def wrow(i, f, blk, nreal):
        return idx, blk[live_block(i, nreal)], jnp.where(i < nreal[0], f, n_f - 1), 0

    return pl.pallas_call(
        _experts_kernel,
        grid_spec=pltpu.PrefetchScalarGridSpec(
            num_scalar_prefetch=2,
            grid=(n_blocks, n_f),
            in_specs=[pl.BlockSpec((MOE_ROWS, D_MODEL), lambda i, f, blk, nreal: (live_block(i, nreal), 0)),
                      pl.BlockSpec((None, None, D_MODEL, tf), wcol),
                      pl.BlockSpec((None, None, D_MODEL, tf), wcol),
                      pl.BlockSpec((None, None, tf, D_MODEL), wrow)],
            out_specs=pl.BlockSpec((MOE_ROWS, D_MODEL), lambda i, f, blk, nreal: (i, 0)),
            scratch_shapes=[pltpu.VMEM((MOE_ROWS, D_MODEL), F32)]),
        out_shape=jax.ShapeDtypeStruct((n_blocks * MOE_ROWS, D_MODEL), BF16),
        compiler_params=_params(("parallel", "arbitrary")),
        name="experts",
    )(blk_e, nreal, xb, wg, wu, wd)


def _combine_kernel(off_ref, far_rows_ref, nseg_ref, x_ref, meta_ref, yb_ref, o_ref, ys_ref, sem, *, ts, cap, n_bits):
    i = pl.program_id(0)
    ys_ref[...] = jnp.zeros_like(ys_ref)
    copies = _segment_copies(i, n_bits, ys_ref, off_ref, yb_ref, far_rows_ref, nseg_ref, sem, False)
    for pred, cp in copies:
        pl.when(pred)(cp.start)
    for pred, cp in copies:
        pl.when(pred)(cp.wait)
    slot = lax.broadcasted_iota(jnp.int32, (ts, cap), 1).astype(F32)
    ys = ys_ref[...]
    q1 = jnp.where(slot == meta_ref[:, 0:1], 1.0, 0.0).astype(BF16)
    q2 = jnp.where(slot == meta_ref[:, 1:2], 1.0, 0.0).astype(BF16)
    o_ref[...] = x_ref[...] + meta_ref[:, 2:3] * _dot(q1, ys) + meta_ref[:, 3:4] * _dot(q2, ys)


def _combine(off, far_rows, nseg, x, meta, yb, ts):
    n = x.shape[0]
    cap = _sorted_cap(ts)
    n_bits = int(TOP_K * ts // SEG_ALIGN).bit_length()
    return pl.pallas_call(
        functools.partial(_combine_kernel, ts=ts, cap=cap, n_bits=n_bits),
        grid_spec=pltpu.PrefetchScalarGridSpec(
            num_scalar_prefetch=3,
            grid=(n // ts,),
            in_specs=[pl.BlockSpec((ts, D_MODEL), lambda i, *_: (i, 0)),
                      pl.BlockSpec((ts, LANES), lambda i, *_: (i, 0)),
                      pl.BlockSpec(memory_space=pl.ANY)],
            out_specs=pl.BlockSpec((ts, D_MODEL), lambda i, *_: (i, 0)),
            scratch_shapes=[pltpu.VMEM((cap, D_MODEL), BF16), pltpu.SemaphoreType.DMA(())]),
        out_shape=jax.ShapeDtypeStruct((n, D_MODEL), F32),
        compiler_params=_params(("arbitrary",)),
        name="combine",
    )(off, far_rows, nseg, x, meta, yb)


def _moe(xs, tss, norm_ffn, rwt, rb, wg, wu, wd, layer, idx, tf):
    gated = [_gate(x, norm_ffn, rwt, rb, layer, idx, ts) for x, ts in zip(xs, tss)]
    off = jnp.concatenate([g[1][:, :, 0] for g in gated], axis=0)
    cpad = jnp.concatenate([g[1][:, :, 1] for g in gated], axis=0)
    n_subs = [g[1].shape[0] for g in gated]
    tot = jnp.sum(cpad, axis=0)
    padded = (tot + MOE_ROWS - 1) // MOE_ROWS * MOE_ROWS
    pend = jnp.cumsum(padded)
    within = jnp.cumsum(cpad, axis=0) - cpad
    off = off.astype(jnp.int32).reshape(-1)
    grouped_rows = ((pend - padded)[None, :] + within).astype(jnp.int32).reshape(-1)
    nseg = (cpad // SEG_ALIGN).astype(jnp.int32).reshape(-1)
    n_assign = TOP_K * sum(x.shape[0] for x in xs)
    n_blocks = (n_assign + (SEG_ALIGN - 1) * sum(n_subs) * N_EXPERTS + N_EXPERTS * (MOE_ROWS - 1)) // MOE_ROWS
    blk_e = jnp.minimum(jnp.sum(jnp.arange(n_blocks)[:, None] * MOE_ROWS >= pend[None, :], axis=1),
                        N_EXPERTS - 1).astype(jnp.int32)
    nreal = (pend[-1:] // MOE_ROWS).astype(jnp.int32)

    xb = jnp.zeros((n_blocks * MOE_ROWS, D_MODEL), BF16)
    metas, parts = [], []
    lo = 0
    for x, g, ns, ts in zip(xs, gated, n_subs, tss):
        sl = slice(lo * N_EXPERTS, (lo + ns) * N_EXPERTS)
        parts.append((off[sl], grouped_rows[sl], nseg[sl]))
        meta, xb = _scatter(*parts[-1], x, norm_ffn, g[0], xb, layer, ts)
        metas.append(meta)
        lo += ns
    yb = _experts(blk_e, nreal, xb, wg, wu, wd, idx, tf)
    return [_combine(*part, x, meta, yb, ts) for part, x, meta, ts in zip(parts, xs, metas, tss)]


def kernel(x_prompt, x_sample, state_pool, cache_kv_g0, cache_kv_g1, cache_kv_g2, norm_attn, w_in, q_norm, k_norm, w_pool, pool_scale, w_branch_pool, w_branch_attn, w_out, norm_ffn, w1_dense, w3_dense, w2_dense, router_w, router_b, we_gate, we_up, we_down):
    batch, seq, _ = x_prompt.shape
    dec_batch, dec_seq, _ = x_sample.shape
    depth = w_in.shape[0]
    caches = (cache_kv_g0, cache_kv_g1, cache_kv_g2)
    n_p, n_s = batch * seq, dec_batch * dec_seq
    tm_p = 512
    dils = tuple(d for _, d in ATT_GROUPS)
    keeps = tuple(min(win, seq) for win, _ in ATT_GROUPS)

    head_of = np.arange(GROUP_W) // HEAD_DIM
    same_head = (head_of[:, None] == head_of[None, :]).astype(np.float32)
    seg_mean = jnp.asarray(same_head / HEAD_DIM, BF16)
    w_qkv = w_in[:, :, :PROJ_W].astype(BF16)
    qgain = (jnp.tile(q_norm, (1, HEADS)) * (HEAD_DIM ** -0.5)).reshape(depth, 1, GROUP_W)
    kgain = jnp.tile(k_norm, (1, HEADS)).reshape(depth, 1, GROUP_W)
    norm_attn3 = norm_attn.reshape(depth, 1, D_MODEL)
    w = {"w_pool": w_pool.astype(BF16), "pool_scale": pool_scale.reshape(depth, 1, POOL_W),
         "norm_attn": norm_attn3, "w_gate": w_in[:, :, PROJ_W:].astype(BF16),
         "w_bp": w_branch_pool.astype(BF16), "w_ba": w_branch_attn.astype(BF16), "w_o": w_out.astype(BF16),
         "norm_ffn": norm_ffn.reshape(depth, 1, D_MODEL)}
    w1_b, w3_b, w2_b = w1_dense.astype(BF16), w3_dense.astype(BF16), w2_dense.astype(BF16)
    wg_b, wu_b, wd_b = we_gate.astype(BF16), we_up.astype(BF16), we_down.astype(BF16)
    rwt = jnp.swapaxes(router_w, 1, 2).astype(BF16)
    rb = router_b.reshape(-1, N_EXPERTS, 1)

    xp = x_prompt.reshape(n_p, D_MODEL)
    xs = x_sample.reshape(n_s, D_MODEL)
    pool_p, pool_s = [], []
    kv_s = [[] for _ in ATT_GROUPS]
    kvo_p = tuple(jnp.zeros((depth, batch, 2, GROUP_W, keep), F32) for keep in keeps)
    ones = (1,) * N_GROUPS
    for layer in range(depth):
        res = _proj(xp, layer, norm_attn3, w_qkv, qgain, kgain, seg_mean, batch, seq, tm_p, dils, keeps,
                    layer, depth, kvo_p, 1, True)
        u, qs, kcs, kvo_p = res[0], res[1:4], res[4:7], tuple(res[7:10])
        outs, lses = [], []
        for gi in range(N_GROUPS):
            o, lse = _attention(qs[gi], kcs[gi], gi)
            outs.append(o)
            lses.append(lse)
        pool_p.append(u.reshape(batch, seq, POOL_W)[:, seq - POOL_STATE:])
        xp, xpn = _merge(xp, u, outs, lses, layer, w, batch, seq, tm_p, dils)

        res = _proj(xs, layer, norm_attn3, w_qkv, qgain, kgain, seg_mean, 1, n_s, n_s, ones, (n_s,) * N_GROUPS,
                    0, 1, (), 3, False)
        u = res[0]
        qs = [q.reshape(dec_batch, dec_seq, HEADS, HEAD_DIM) for q in res[1:4]]
        kvn = [kv.reshape(dec_batch, dec_seq, 2, HEADS, HEAD_DIM) for kv in res[7:10]]
        pool_o, att_o, new_pool = _sample_mix(state_pool, u, qs, kvn, caches, layer, w, dec_batch, dec_seq)
        for gi in range(N_GROUPS):
            kv_s[gi].append(kvn[gi])
        pool_s.append(new_pool)
        xs, xsn = _tail(xs, pool_o.reshape(n_s, POOL_W), att_o.reshape(n_s, GROUP_W), layer, w)

        i = layer // 2
        if layer % 2 == 0:
            xp = _ffn(xp, xpn, w1_b, w3_b, w2_b, i, 1024, 256)
            xs = _ffn(xs, xsn, w1_b, w3_b, w2_b, i, n_s, 256)
        else:
            xp, xs = _moe([xp, xs], [512, n_s], w["norm_ffn"], rwt, rb, wg_b, wu_b, wd_b, layer, i, 512)
    kv_p = [kvo_p[g].reshape(depth, batch, 2, HEADS, HEAD_DIM, keeps[g]).transpose(0, 1, 5, 2, 3, 4)
            for g in range(N_GROUPS)]
    return (xp.reshape(batch, seq, D_MODEL), xs.reshape(dec_batch, dec_seq, D_MODEL),
            jnp.stack(pool_p), jnp.stack(pool_s),
            kv_p[0], jnp.stack(kv_s[0]),
            kv_p[1], jnp.stack(kv_s[1]),
            kv_p[2], jnp.stack(kv_s[2]))
```

```python
import functools

import numpy as np
import jax
import jax.numpy as jnp
from jax import lax
from jax.experimental import pallas as pl
from jax.experimental.pallas import tpu as pltpu

F32 = jnp.float32
BF16 = jnp.bfloat16

D_MODEL = 1024
PAST_LEN = 16384
POOL_WINDOWS = (2, 4, 8, 16)
POOL_GROUP = 128
POOL_W = 512
POOL_STATE = 15
ATT_GROUPS = ((128, 1), (512, 4), (2048, 16))
N_GROUPS = len(ATT_GROUPS)
HEAD_DIM = 64
HEADS = 8
GROUP_W = 512
QKV_W = 1536
Q_BLOCK = 128
ALIBI_MAX = 8.0
N_EXPERTS = 8
TOP_K = 2
RMS_EPS = 1e-6
NEG_INF = -1e30
LANES = 128
LANE_CHUNKS = GROUP_W // LANES
PROJ_W = POOL_W + 3 * QKV_W
VMEM_LIMIT = 56 * 1024 * 1024


def _slopes():
    i = np.arange(1, N_GROUPS * HEADS + 1, dtype=np.float32)
    return np.exp2(-ALIBI_MAX * i / (N_GROUPS * HEADS)).astype(np.float32).reshape(N_GROUPS, HEADS)


def _params(sem):
    return pltpu.CompilerParams(dimension_semantics=sem, vmem_limit_bytes=VMEM_LIMIT)


def _rms(x, gain):
    return x * lax.rsqrt(jnp.mean(x * x, axis=-1, keepdims=True) + RMS_EPS) * gain


def _dot(a, b):
    return jnp.dot(a, b, preferred_element_type=F32)


def _resident(shape, index_map):
    return pl.BlockSpec(shape, index_map, pipeline_mode=pl.Buffered(1))


def _write_classes(dst_ref, col0, val, dil, tmp_ref, slot):
    rows = val.shape[0] // dil
    cols = slice(col0, col0 + GROUP_W)
    if dil == 1:
        dst_ref[0, :, cols] = val.astype(dst_ref.dtype)
        return
    for c in range(LANE_CHUNKS):
        tmp_ref[slot, c] = val[:, c * LANES:(c + 1) * LANES]
    for r in range(dil):
        picked = [tmp_ref[slot, c, pl.ds(r, rows, stride=dil), :] for c in range(LANE_CHUNKS)]
        dst_ref[r, :, cols] = jnp.concatenate(picked, axis=1).astype(dst_ref.dtype)


def _proj_kernel(x_ref, g_ref, w_ref, qg_ref, kg_ref, seg_ref, *rest, tm, dils, keeps, n_alias, norm_terms,
                 kv_positions_minor):
    u_ref, q0_ref, q1_ref, q2_ref, kc0_ref, kc1_ref, kc2_ref, kvo0_ref, kvo1_ref, kvo2_ref, tmp_ref = rest[n_alias:]
    s = pl.program_id(1)
    n_tiles = pl.num_programs(1)
    xn = _rms(x_ref[...], g_ref[...]).astype(BF16)

    def zblk(j):
        return _dot(xn, w_ref[:, j * GROUP_W:(j + 1) * GROUP_W])

    def headnorm(z, gain):
        rem = z * z
        ms = None
        for _ in range(norm_terms):
            part = rem.astype(BF16)
            rem = rem - part.astype(F32)
            ms = _dot(part, seg_ref[...]) if ms is None else ms + _dot(part, seg_ref[...])
        return z * lax.rsqrt(ms + RMS_EPS) * gain

    u_ref[...] = zblk(0)
    q_refs = (q0_ref, q1_ref, q2_ref)
    kc_refs = (kc0_ref, kc1_ref, kc2_ref)
    kvo_refs = (kvo0_ref, kvo1_ref, kvo2_ref)
    slot = 0
    for g in range(N_GROUPS):
        q = headnorm(zblk(1 + g), qg_ref[...])
        k = headnorm(zblk(1 + N_GROUPS + g), kg_ref[...])
        v = zblk(1 + 2 * N_GROUPS + g)
        for dst, col0, val in ((q_refs[g], 0, q), (kc_refs[g], 0, k), (kc_refs[g], GROUP_W, v)):
            _write_classes(dst, col0, val, dils[g], tmp_ref, slot % tmp_ref.shape[0])
            slot += dils[g] > 1
        keep = keeps[g]
        first = n_tiles - max(keep // tm, 1)
        rows = min(keep, tm)

        @pl.when(s >= first)
        def _(k=k, v=v, ref=kvo_refs[g], rows=rows):
            if kv_positions_minor:
                ref[0] = k[tm - rows:, :].T
                ref[1] = v[tm - rows:, :].T
            else:
                ref[:, 0:GROUP_W] = k[tm - rows:, :]
                ref[:, GROUP_W:2 * GROUP_W] = v[tm - rows:, :]


def _proj(x, layer, norm_attn, w_qkv, qgain, kgain, seg, batch, seq, tm, dils, keeps, out_layer, out_depth, prev_kvo,
          norm_terms, kv_positions_minor):
    n_tiles = seq // tm
    kvo_specs, kvo_shapes = [], []
    for g in range(N_GROUPS):
        keep = keeps[g]
        assert keep % tm == 0 or (keep < tm and keep % LANES == 0)
        first = n_tiles - max(keep // tm, 1)
        rows = min(keep, tm)
        if kv_positions_minor:
            kvo_specs.append(pl.BlockSpec((None, None, 2, GROUP_W, rows),
                                          lambda b, s, first=first: (out_layer, b, 0, 0, jnp.maximum(s - first, 0))))
            kvo_shapes.append(jax.ShapeDtypeStruct((out_depth, batch, 2, GROUP_W, keep), F32))
        else:
            kvo_specs.append(pl.BlockSpec((None, None, rows, 2 * GROUP_W),
                                          lambda b, s, first=first: (out_layer, b, jnp.maximum(s - first, 0), 0)))
            kvo_shapes.append(jax.ShapeDtypeStruct((out_depth, batch, keep, 2 * GROUP_W), F32))
    cls_spec = lambda g, width: pl.BlockSpec((None, dils[g], tm // dils[g], width), lambda b, s: (b, 0, s, 0))
    cls_shape = lambda g, width: jax.ShapeDtypeStruct((batch, dils[g], seq // dils[g], width), BF16)
    n_alias = len(prev_kvo)
    n_in = 6
    return pl.pallas_call(
        functools.partial(_proj_kernel, tm=tm, dils=dils, keeps=keeps, n_alias=n_alias,
                          norm_terms=norm_terms, kv_positions_minor=kv_positions_minor),
        grid=(batch, n_tiles),
        in_specs=[
            pl.BlockSpec((tm, D_MODEL), lambda b, s: (b * n_tiles + s, 0)),
            _resident((None, 1, D_MODEL), lambda b, s: (layer, 0, 0)),
            _resident((None, D_MODEL, PROJ_W), lambda b, s: (layer, 0, 0)),
            _resident((None, 1, GROUP_W), lambda b, s: (layer, 0, 0)),
            _resident((None, 1, GROUP_W), lambda b, s: (layer, 0, 0)),
            _resident((GROUP_W, GROUP_W), lambda b, s: (0, 0)),
        ] + [pl.BlockSpec(memory_space=pl.ANY)] * n_alias,
        out_specs=[pl.BlockSpec((tm, POOL_W), lambda b, s: (b * n_tiles + s, 0))]
                  + [cls_spec(g, GROUP_W) for g in range(N_GROUPS)]
                  + [cls_spec(g, 2 * GROUP_W) for g in range(N_GROUPS)] + kvo_specs,
        out_shape=[jax.ShapeDtypeStruct((batch * seq, POOL_W), F32)]
                  + [cls_shape(g, GROUP_W) for g in range(N_GROUPS)]
                  + [cls_shape(g, 2 * GROUP_W) for g in range(N_GROUPS)] + kvo_shapes,
        scratch_shapes=[pltpu.VMEM((3, LANE_CHUNKS, tm, LANES), F32)],
        input_output_aliases={n_in + g: 1 + 2 * N_GROUPS + g for g in range(n_alias)},
        compiler_params=_params(("parallel", "arbitrary")),
        name="proj",
    )(x, norm_attn, w_qkv, qgain, kgain, seg, *prev_kvo)


def _attn_kernel(q_ref, kvc_ref, *rest, slope_dil, tc, has_halo):
    if has_halo:
        kvh_ref, o_ref, lse_ref, kbuf, vbuf = rest
    else:
        o_ref, lse_ref, kbuf, vbuf = rest
    off = Q_BLOCK if has_halo else 0
    if has_halo:
        kbuf[0:Q_BLOCK, :] = kvh_ref[:, 0:GROUP_W]
        vbuf[0:Q_BLOCK, :] = kvh_ref[:, GROUP_W:2 * GROUP_W]
    kbuf[off:off + tc, :] = kvc_ref[:, 0:GROUP_W]
    vbuf[off:off + tc, :] = kvc_ref[:, GROUP_W:2 * GROUP_W]

    nk = Q_BLOCK + off
    qi = lax.broadcasted_iota(jnp.int32, (Q_BLOCK, nk), 0)
    cj = lax.broadcasted_iota(jnp.int32, (Q_BLOCK, nk), 1)
    dist = qi - cj + off
    distf = dist.astype(F32)
    maskneg = jnp.where((dist >= 0) & (dist <= Q_BLOCK), 0.0, NEG_INF).astype(F32)
    if has_halo:
        first = jnp.where(pl.program_id(2) == 0, NEG_INF, 0.0).astype(F32)
        mask_first = maskneg + jnp.where(cj < Q_BLOCK, first, 0.0)

    for i in range(tc // Q_BLOCK):
        mask = mask_first if (has_halo and i == 0) else maskneg
        rows = slice(i * Q_BLOCK, (i + 1) * Q_BLOCK)
        krows = slice(i * Q_BLOCK, i * Q_BLOCK + nk)
        for hp in range(HEADS // 2):
            outs, lses = [], []
            for h in (2 * hp, 2 * hp + 1):
                cols = slice(h * HEAD_DIM, (h + 1) * HEAD_DIM)
                s = lax.dot_general(q_ref[rows, cols], kbuf[krows, cols],
                                    (((1,), (1,)), ((), ())), preferred_element_type=F32)
                s = s - slope_dil[h] * distf + mask
                m = jnp.max(s, axis=1, keepdims=True)
                p = jnp.exp(s - m)
                l = jnp.sum(p, axis=1, keepdims=True)
                o = _dot(p.astype(BF16), vbuf[krows, cols]) / l
                outs.append(o)
                lses.append(jnp.broadcast_to(m + jnp.log(l), (Q_BLOCK, HEAD_DIM)))
            pc = slice(hp * 2 * HEAD_DIM, (hp + 1) * 2 * HEAD_DIM)
            o_ref[rows, pc] = jnp.concatenate(outs, axis=1).astype(BF16)
            lse_ref[rows, pc] = jnp.concatenate(lses, axis=1)


def _attention(q, kc, gi):
    win, dil = ATT_GROUPS[gi]
    assert win // dil == Q_BLOCK
    batch, _, l, _ = q.shape
    tc = min(l, 512)
    has_halo = l > Q_BLOCK
    slope_dil = tuple(float(s) * dil for s in _slopes()[gi])
    in_specs = [
        pl.BlockSpec((None, None, tc, GROUP_W), lambda b, r, c: (b, r, c, 0)),
        pl.BlockSpec((None, None, tc, 2 * GROUP_W), lambda b, r, c: (b, r, c, 0)),
    ]
    args = [q, kc]
    if has_halo:
        in_specs.append(pl.BlockSpec((None, None, Q_BLOCK, 2 * GROUP_W),
                                     lambda b, r, c: (b, r, jnp.maximum(c * (tc // Q_BLOCK) - 1, 0), 0)))
        args.append(kc)
    nk = tc + (Q_BLOCK if has_halo else 0)
    return pl.pallas_call(
        functools.partial(_attn_kernel, slope_dil=slope_dil, tc=tc, has_halo=has_halo),
        grid=(batch, dil, l // tc),
        in_specs=in_specs,
        out_specs=[pl.BlockSpec((None, None, tc, GROUP_W), lambda b, r, c: (b, r, c, 0))] * 2,
        out_shape=[jax.ShapeDtypeStruct((batch, dil, l, GROUP_W), BF16),
                   jax.ShapeDtypeStruct((batch, dil, l, GROUP_W), F32)],
        scratch_shapes=[pltpu.VMEM((nk, GROUP_W), BF16), pltpu.VMEM((nk, GROUP_W), BF16)],
        compiler_params=_params(("parallel", "parallel", "arbitrary")),
        name=f"attn_g{gi}",
    )(*args)


def _mix_tail(x, pool_o, att_o, nattn_ref, wgate_ref, wbp_ref, wba_ref, wo_ref, nffn_ref, xo_ref, xn_ref):
    gates = jax.nn.sigmoid(_dot(_rms(x, nattn_ref[...]).astype(BF16), wgate_ref[...]))
    hp = _dot(pool_o.astype(BF16), wbp_ref[...])
    ha = _dot(att_o.astype(BF16), wba_ref[...])
    t = gates[:, 0:D_MODEL] * hp + gates[:, D_MODEL:2 * D_MODEL] * ha
    xo = x + _dot(t.astype(BF16), wo_ref[...])
    xo_ref[...] = xo
    xn_ref[...] = _rms(xo, nffn_ref[...]).astype(BF16)


def _group_linear(m, wpool_ref, pscale_ref):
    parts = [_dot(m[:, gi * POOL_GROUP:(gi + 1) * POOL_GROUP].astype(BF16), wpool_ref[gi])
             for gi in range(len(POOL_WINDOWS))]
    return jnp.concatenate(parts, axis=1) * pscale_ref[...]


def _read_classes(src_ref, dil, il_ref, slot):
    if dil == 1:
        return src_ref[0].astype(F32)
    rows = src_ref.shape[1]
    for r in range(dil):
        v = src_ref[r].astype(F32)
        for c in range(LANE_CHUNKS):
            il_ref[slot, c, pl.ds(r, rows, stride=dil), :] = v[:, c * LANES:(c + 1) * LANES]
    return jnp.concatenate([il_ref[slot, c] for c in range(LANE_CHUNKS)], axis=1)


def _merge_kernel(x_ref, u_ref, uh_ref, o0_ref, o1_ref, o2_ref, l0_ref, l1_ref, l2_ref,
                  wpool_ref, pscale_ref, nattn_ref, wgate_ref, wbp_ref, wba_ref, wo_ref, nffn_ref,
                  xo_ref, xn_ref, ext_ref, il_ref, *, tm, dils):
    si = pl.program_id(1)
    halo = POOL_STATE + 1
    ext_ref[0:halo, :] = jnp.where(si == 0, 0.0, uh_ref[...])
    ext_ref[halo:halo + tm, :] = u_ref[...]
    pos = si * tm + lax.broadcasted_iota(jnp.int32, (tm, 1), 0)
    parts = []
    for gi, win in enumerate(POOL_WINDOWS):
        cols = slice(gi * POOL_GROUP, (gi + 1) * POOL_GROUP)
        own = ext_ref[halo:halo + tm, cols]
        acc = own
        for back in range(1, win):
            acc = acc + ext_ref[halo - back:halo - back + tm, cols]
        inv = 1.0 / jnp.minimum(pos + 1, win).astype(F32)
        parts.append(acc * inv - own)
    pool_o = _group_linear(jnp.concatenate(parts, axis=1), wpool_ref, pscale_ref)

    slot = 0
    os_, ls_ = [], []
    for g, (o_ref, l_ref) in enumerate(((o0_ref, l0_ref), (o1_ref, l1_ref), (o2_ref, l2_ref))):
        os_.append(_read_classes(o_ref, dils[g], il_ref, slot))
        slot += dils[g] > 1
        ls_.append(_read_classes(l_ref, dils[g], il_ref, slot))
        slot += dils[g] > 1
    mx = jnp.maximum(ls_[0], jnp.maximum(ls_[1], ls_[2]))
    es = [jnp.exp(v - mx) for v in ls_]
    att_o = (es[0] * os_[0] + es[1] * os_[1] + es[2] * os_[2]) / (es[0] + es[1] + es[2])
    _mix_tail(x_ref[...], pool_o, att_o, nattn_ref, wgate_ref, wbp_ref, wba_ref, wo_ref, nffn_ref, xo_ref, xn_ref)


def _wspec(shape, layer, grid_rank):
    nd = len(shape)
    return _resident((None,) + tuple(shape), lambda *_: (layer,) + (0,) * nd)


def _tail_weight_specs(layer, grid_rank):
    return [_wspec((1, D_MODEL), layer, grid_rank), _wspec((D_MODEL, 2 * D_MODEL), layer, grid_rank),
            _wspec((POOL_W, D_MODEL), layer, grid_rank), _wspec((GROUP_W, D_MODEL), layer, grid_rank),
            _wspec((D_MODEL, D_MODEL), layer, grid_rank), _wspec((1, D_MODEL), layer, grid_rank)]


def _tail_weights(w):
    return (w["norm_attn"], w["w_gate"], w["w_bp"], w["w_ba"], w["w_o"], w["norm_ffn"])


def _merge(x, u, outs, lses, layer, w, batch, seq, tm, dils):
    n = x.shape[0]
    n_tiles = seq // tm
    halo = POOL_STATE + 1
    row = lambda width: pl.BlockSpec((tm, width), lambda b, s: (b * n_tiles + s, 0))
    cls = lambda g: pl.BlockSpec((None, dils[g], tm // dils[g], GROUP_W), lambda b, s: (b, 0, s, 0))
    n_il = 2 * sum(d > 1 for d in dils)
    return pl.pallas_call(
        functools.partial(_merge_kernel, tm=tm, dils=dils),
        grid=(batch, n_tiles),
        in_specs=[row(D_MODEL), row(POOL_W),
                  pl.BlockSpec((halo, POOL_W),
                               lambda b, s: (jnp.maximum((b * n_tiles + s) * (tm // halo) - 1, 0), 0))]
                 + [cls(g) for g in range(N_GROUPS)] * 2
                 + [_wspec((len(POOL_WINDOWS), POOL_GROUP, POOL_GROUP), layer, 2), _wspec((1, POOL_W), layer, 2)]
                 + _tail_weight_specs(layer, 2),
        out_specs=[row(D_MODEL), row(D_MODEL)],
        out_shape=[jax.ShapeDtypeStruct((n, D_MODEL), F32), jax.ShapeDtypeStruct((n, D_MODEL), BF16)],
        scratch_shapes=[pltpu.VMEM((halo + tm, POOL_W), F32), pltpu.VMEM((n_il, LANE_CHUNKS, tm, LANES), F32)],
        compiler_params=_params(("parallel", "parallel")),
        name="merge",
    )(x, u, u, *outs, *lses, w["w_pool"], w["pool_scale"], *_tail_weights(w))


def _sample_bias(dec_seq, cache_rows):
    out = []
    for gi, (win, dil) in enumerate(ATT_GROUPS):
        lc = cache_rows[gi]
        lane = np.arange(lc + LANES)
        t = np.arange(dec_seq)[:, None]
        back = np.where(lane < lc, lc + t - lane, t - (lane - lc))
        valid = (back >= 0) & (back % dil == 0) & (back // dil <= win // dil) & (lane < lc + dec_seq)
        neg_back = np.where(valid, -back, 0).astype(np.float32)[:, None, :]
        mask = np.where(valid, 0.0, NEG_INF).astype(np.float32)[:, None, :]
        out.append((jnp.asarray(neg_back), jnp.asarray(mask)))
    return out


def _bf16_valued(x):
    return x.astype(BF16).astype(F32)


def _sample_mix_kernel(state_ref, u_ref, q0_ref, q1_ref, q2_ref, kn0_ref, kn1_ref, kn2_ref, c0_ref, c1_ref, c2_ref,
                       nb0_ref, nb1_ref, nb2_ref, mk0_ref, mk1_ref, mk2_ref, slope_ref, wpool_ref, pscale_ref,
                       pool_ref, att_ref, newpool_ref, ext_ref, m_ref, *, dec_seq):
    ext_ref[...] = jnp.zeros_like(ext_ref)
    ext_ref[0:POOL_STATE, :] = state_ref[...]
    ext_ref[POOL_STATE:POOL_STATE + dec_seq, :] = u_ref[...]
    newpool_ref[...] = ext_ref[dec_seq:dec_seq + POOL_STATE, :]

    m_ref[...] = jnp.zeros_like(m_ref)
    for t in range(dec_seq):
        row = POOL_STATE + t
        for gi, win in enumerate(POOL_WINDOWS):
            cols = slice(gi * POOL_GROUP, (gi + 1) * POOL_GROUP)
            tot = jnp.sum(ext_ref[row - win + 1:row + 1, cols], axis=0, keepdims=True)
            cnt = float(min(PAST_LEN + t + 1, win))
            m_ref[t:t + 1, cols] = tot / cnt - ext_ref[row:row + 1, cols]
    pool_ref[...] = _group_linear(m_ref[...], wpool_ref, pscale_ref)[0:dec_seq]

    groups = ((q0_ref, kn0_ref, c0_ref, nb0_ref, mk0_ref), (q1_ref, kn1_ref, c1_ref, nb1_ref, mk1_ref),
              (q2_ref, kn2_ref, c2_ref, nb2_ref, mk2_ref))

    def per_head(h, carry):
        outs, lses = [], []
        for gi, (q_ref, kn_ref, c_ref, nb_ref, mk_ref) in enumerate(groups):
            keys = jnp.concatenate([_bf16_valued(c_ref[0, h]), _bf16_valued(kn_ref[0, h])], axis=1)
            vals = jnp.concatenate([_bf16_valued(c_ref[1, h]), _bf16_valued(kn_ref[1, h])], axis=1)
            q_all = q_ref[h]
            slope = slope_ref[gi, h]
            o_g, lse_g = [], []
            for t in range(dec_seq):
                s = jnp.sum(keys * q_all[:, t:t + 1], axis=0, keepdims=True) + slope * nb_ref[t] + mk_ref[t]
                mx = jnp.max(s, axis=1, keepdims=True)
                p = jnp.exp(s - mx)
                l = jnp.sum(p, axis=1, keepdims=True)
                o_g.append(jnp.sum(vals * _bf16_valued(p / l), axis=1, keepdims=True))
                lse_g.append(mx + jnp.log(l))
            outs.append(o_g)
            lses.append(lse_g)
        lane = lax.broadcasted_iota(jnp.int32, (HEAD_DIM, LANES), 1)
        att = jnp.zeros((HEAD_DIM, LANES), F32)
        for t in range(dec_seq):
            mx = jnp.maximum(lses[0][t], jnp.maximum(lses[1][t], lses[2][t]))
            es = [jnp.exp(lses[g][t] - mx) for g in range(N_GROUPS)]
            col = (es[0] * outs[0][t] + es[1] * outs[1][t] + es[2] * outs[2][t]) / (es[0] + es[1] + es[2])
            att = jnp.where(lane == t, col, att)
        att_ref[h] = att
        return carry

    lax.fori_loop(0, HEADS, per_head, 0)


def _sample_mix(state_pool, u, qs, kvn, caches, layer, w, dec_batch, dec_seq):
    cache_rows = [c.shape[2] for c in caches]
    bias = _sample_bias(dec_seq, cache_rows)
    kv_tail = (2, HEADS, HEAD_DIM)
    cviews = [jnp.transpose(c, (0, 1, 3, 4, 5, 2)).reshape((-1,) + kv_tail + (c.shape[2],)) for c in caches]
    cspecs = [pl.BlockSpec((None,) + kv_tail + (lc,), lambda b: (layer * dec_batch + b, 0, 0, 0, 0))
              for lc in cache_rows]
    per_batch = lambda *tail: pl.BlockSpec((None,) + tail, lambda b: (b,) + (0,) * len(tail))
    const = lambda shape: _resident(shape, lambda b: (0,) * len(shape))
    return pl.pallas_call(
        functools.partial(_sample_mix_kernel, dec_seq=dec_seq),
        grid=(dec_batch,),
        in_specs=[pl.BlockSpec((None, POOL_STATE, POOL_W), lambda b: (layer * dec_batch + b, 0, 0)),
                  per_batch(dec_seq, POOL_W)] + [per_batch(HEADS, HEAD_DIM, LANES)] * 3
                 + [per_batch(*kv_tail, LANES)] * 3 + cspecs
                 + [const(nb.shape) for nb, _ in bias] + [const(mk.shape) for _, mk in bias]
                 + [const((N_GROUPS, HEADS, 1, 1)),
                    _wspec((len(POOL_WINDOWS), POOL_GROUP, POOL_GROUP), layer, 1), _wspec((1, POOL_W), layer, 1)],
        out_specs=[per_batch(dec_seq, POOL_W), per_batch(HEADS, HEAD_DIM, LANES),
                   pl.BlockSpec((None, POOL_STATE, POOL_W), lambda b: (b, 0, 0))],
        out_shape=[jax.ShapeDtypeStruct((dec_batch, dec_seq, POOL_W), F32),
                   jax.ShapeDtypeStruct((dec_batch, HEADS, HEAD_DIM, LANES), F32),
                   jax.ShapeDtypeStruct((dec_batch, POOL_STATE, POOL_W), F32)],
        scratch_shapes=[pltpu.VMEM((POOL_STATE + dec_seq + 5, POOL_W), F32), pltpu.VMEM((8, POOL_W), F32)],
        compiler_params=_params(("parallel",)),
        name="sample_mix",
    )(state_pool.reshape(-1, POOL_STATE, POOL_W), u.reshape(dec_batch, dec_seq, POOL_W), *qs, *kvn, *cviews,
      *[nb for nb, _ in bias], *[mk for _, mk in bias],
      jnp.asarray(_slopes()).reshape(N_GROUPS, HEADS, 1, 1), w["w_pool"], w["pool_scale"])


def _tail_kernel(x_ref, pool_ref, att_ref, nattn_ref, wgate_ref, wbp_ref, wba_ref, wo_ref, nffn_ref, xo_ref, xn_ref):
    _mix_tail(x_ref[...], pool_ref[...], att_ref[...], nattn_ref, wgate_ref, wbp_ref, wba_ref, wo_ref, nffn_ref,
              xo_ref, xn_ref)


def _tail(x, pool_o, att_o, layer, w):
    n = x.shape[0]
    row = lambda width: pl.BlockSpec((n, width), lambda i: (0, 0))
    return pl.pallas_call(
        _tail_kernel,
        grid=(1,),
        in_specs=[row(D_MODEL), row(POOL_W), row(GROUP_W)] + _tail_weight_specs(layer, 1),
        out_specs=[row(D_MODEL), row(D_MODEL)],
        out_shape=[jax.ShapeDtypeStruct((n, D_MODEL), F32), jax.ShapeDtypeStruct((n, D_MODEL), BF16)],
        compiler_params=_params(("arbitrary",)),
        name="tail",
    )(x, pool_o, att_o, *_tail_weights(w))


def _swiglu_step(x, wg_ref, wu_ref, wd_ref, acc_ref):
    h = jax.nn.silu(_dot(x, wg_ref[...])) * _dot(x, wu_ref[...])
    acc_ref[...] += _dot(h.astype(BF16), wd_ref[...])


def _ffn_kernel(x_ref, xn_ref, w1_ref, w3_ref, w2_ref, y_ref, acc_ref):
    f = pl.program_id(1)

    @pl.when(f == 0)
    def _():
        acc_ref[...] = x_ref[...]

    _swiglu_step(xn_ref[...], w1_ref, w3_ref, w2_ref, acc_ref)

    @pl.when(f == pl.num_programs(1) - 1)
    def _():
        y_ref[...] = acc_ref[...]


def _ffn(x, xn, w1, w3, w2, idx, tm, tf):
    n = x.shape[0]
    d_ff = w1.shape[-1]
    return pl.pallas_call(
        _ffn_kernel,
        grid=(n // tm, d_ff // tf),
        in_specs=[pl.BlockSpec((tm, D_MODEL), lambda i, f: (i, 0)),
                  pl.BlockSpec((tm, D_MODEL), lambda i, f: (i, 0)),
                  pl.BlockSpec((None, D_MODEL, tf), lambda i, f: (idx, 0, f)),
                  pl.BlockSpec((None, D_MODEL, tf), lambda i, f: (idx, 0, f)),
                  pl.BlockSpec((None, tf, D_MODEL), lambda i, f: (idx, f, 0))],
        out_specs=pl.BlockSpec((tm, D_MODEL), lambda i, f: (i, 0)),
        out_shape=jax.ShapeDtypeStruct((n, D_MODEL), F32),
        scratch_shapes=[pltpu.VMEM((tm, D_MODEL), F32)],
        compiler_params=_params(("parallel", "arbitrary")),
        name="ffn",
    )(x, xn, w1, w3, w2)


SEG_ALIGN = 16
MOE_ROWS = 512


def _sorted_cap(ts):
    need = TOP_K * ts + N_EXPERTS * (SEG_ALIGN - 1)
    return -(-need // LANES) * LANES if ts >= LANES * 2 else -(-need // SEG_ALIGN) * SEG_ALIGN


def _segment_sizes(oh1, oh2):
    cnt1 = jnp.sum(oh1, axis=1, keepdims=True)
    cnt = cnt1 + jnp.sum(oh2, axis=1, keepdims=True)
    cpad = jnp.floor((cnt + (SEG_ALIGN - 1)) * (1.0 / SEG_ALIGN)) * SEG_ALIGN
    offs = [jnp.zeros((1, 1), F32)]
    for ei in range(1, N_EXPERTS):
        offs.append(offs[-1] + cpad[ei - 1:ei, :])
    return cnt1, cpad, jnp.concatenate(offs, axis=0)


def _gate_kernel(x_ref, g_ref, rwt_ref, rb_ref, route_ref, seg_ref, *, ts):
    xn = _rms(x_ref[...], g_ref[...]).astype(BF16)
    logits = lax.dot_general(rwt_ref[...], xn, (((1,), (1,)), ((), ())), preferred_element_type=F32) + rb_ref[...]
    row = lax.broadcasted_iota(jnp.int32, (N_EXPERTS, ts), 0)
    neg = jnp.float32(-jnp.inf)
    m1 = jnp.max(logits, axis=0, keepdims=True)
    i1 = jnp.min(jnp.where(logits == m1, row, N_EXPERTS), axis=0, keepdims=True)
    rest = jnp.where(row == i1, neg, logits)
    m2 = jnp.max(rest, axis=0, keepdims=True)
    i2 = jnp.min(jnp.where(rest == m2, row, N_EXPERTS), axis=0, keepdims=True)
    e = jnp.exp(m2 - m1)
    zeros = jnp.zeros((N_EXPERTS - 4, ts), F32)
    route_ref[...] = jnp.concatenate([i1.astype(F32), i2.astype(F32), 1.0 / (1.0 + e), e / (1.0 + e), zeros], axis=0)
    _, cpad, off = _segment_sizes(jnp.where(row == i1, 1.0, 0.0), jnp.where(row == i2, 1.0, 0.0))
    lane = lax.broadcasted_iota(jnp.int32, (N_EXPERTS, LANES), 1)
    seg_ref[...] = jnp.where(lane == 0, off, jnp.where(lane == 1, cpad, 0.0)).astype(jnp.int32)


def _gate(x, norm_ffn, rwt, rb, layer, idx, ts):
    n = x.shape[0]
    n_sub = n // ts
    return pl.pallas_call(
        functools.partial(_gate_kernel, ts=ts),
        grid=(n_sub,),
        in_specs=[pl.BlockSpec((ts, D_MODEL), lambda i: (i, 0)),
                  _resident((None, 1, D_MODEL), lambda i: (layer, 0, 0)),
                  _resident((None, N_EXPERTS, D_MODEL), lambda i: (idx, 0, 0)),
                  _resident((None, N_EXPERTS, 1), lambda i: (idx, 0, 0))],
        out_specs=[pl.BlockSpec((None, N_EXPERTS, ts), lambda i: (i, 0, 0)),
                   pl.BlockSpec((None, N_EXPERTS, LANES), lambda i: (i, 0, 0))],
        out_shape=[jax.ShapeDtypeStruct((n_sub, N_EXPERTS, ts), F32),
                   jax.ShapeDtypeStruct((n_sub, N_EXPERTS, LANES), jnp.int32)],
        compiler_params=_params(("parallel",)),
        name="gate",
    )(x, norm_ffn, rwt, rb)


def _segment_copies(i, n_bits, local_ref, off_ref, far_ref, far_rows_ref, nseg_ref, sem, to_far):
    copies = []
    for e in range(N_EXPERTS):
        j = i * N_EXPERTS + e
        n = nseg_ref[j]
        for b in reversed(range(n_bits)):
            size = SEG_ALIGN << b
            done = ((n >> (b + 1)) << (b + 1)) * SEG_ALIGN
            near = local_ref.at[pl.ds(pl.multiple_of(off_ref[j] + done, SEG_ALIGN), size)]
            far = far_ref.at[pl.ds(pl.multiple_of(far_rows_ref[j] + done, SEG_ALIGN), size)]
            cp = pltpu.make_async_copy(near, far, sem) if to_far else pltpu.make_async_copy(far, near, sem)
            copies.append((((n >> b) & 1) == 1, cp))
    return copies


def _scatter_kernel(off_ref, far_rows_ref, nseg_ref, x_ref, g_ref, route_ref, tri_ref, xb_in_ref,
                    meta_ref, xb_ref, xs_ref, sem, *, ts, cap, n_bits):
    del xb_in_ref
    i = pl.program_id(0)
    xn = _rms(x_ref[...], g_ref[...]).astype(BF16)
    row = lax.broadcasted_iota(jnp.int32, (N_EXPERTS, ts), 0).astype(F32)
    oh1 = jnp.where(row == route_ref[0:1, :], 1.0, 0.0)
    oh2 = jnp.where(row == route_ref[1:2, :], 1.0, 0.0)
    cnt1, _, off = _segment_sizes(oh1, oh2)
    pre1 = _dot(oh1.astype(BF16), tri_ref[...])
    pre2 = _dot(oh2.astype(BF16), tri_ref[...])
    d1 = jnp.sum(oh1 * (off + pre1), axis=0, keepdims=True)
    d2 = jnp.sum(oh2 * (off + cnt1 + pre2), axis=0, keepdims=True)
    slot = lax.broadcasted_iota(jnp.int32, (cap, ts), 0).astype(F32)
    p = jnp.where(slot == d1, 1.0, jnp.where(slot == d2, 1.0, 0.0)).astype(BF16)
    xs_ref[...] = _dot(p, xn).astype(BF16)
    meta_t = jnp.concatenate([d1, d2, route_ref[2:4, :], jnp.zeros((LANES - 4, ts), F32)], axis=0)
    meta_ref[...] = meta_t.T
    copies = _segment_copies(i, n_bits, xs_ref, off_ref, xb_ref, far_rows_ref, nseg_ref, sem, True)
    for pred, cp in copies:
        pl.when(pred)(cp.start)
    for pred, cp in copies:
        pl.when(pred)(cp.wait)


def _scatter(off, far_rows, nseg, x, norm_ffn, route, xb, layer, ts):
    n = x.shape[0]
    n_sub = n // ts
    cap = _sorted_cap(ts)
    tri = jnp.asarray(np.triu(np.ones((ts, ts), np.float32), 1), BF16)
    n_bits = int(TOP_K * ts // SEG_ALIGN).bit_length()
    return pl.pallas_call(
        functools.partial(_scatter_kernel, ts=ts, cap=cap, n_bits=n_bits),
        grid_spec=pltpu.PrefetchScalarGridSpec(
            num_scalar_prefetch=3,
            grid=(n_sub,),
            in_specs=[pl.BlockSpec((ts, D_MODEL), lambda i, *_: (i, 0)),
                      _resident((None, 1, D_MODEL), lambda i, *_: (layer, 0, 0)),
                      pl.BlockSpec((None, N_EXPERTS, ts), lambda i, *_: (i, 0, 0)),
                      _resident((ts, ts), lambda i, *_: (0, 0)),
                      pl.BlockSpec(memory_space=pl.ANY)],
            out_specs=[pl.BlockSpec((ts, LANES), lambda i, *_: (i, 0)),
                       pl.BlockSpec(memory_space=pl.ANY)],
            scratch_shapes=[pltpu.VMEM((cap, D_MODEL), BF16), pltpu.SemaphoreType.DMA(())]),
        out_shape=[jax.ShapeDtypeStruct((n, LANES), F32), jax.ShapeDtypeStruct(xb.shape, xb.dtype)],
        input_output_aliases={7: 1},
        compiler_params=_params(("arbitrary",)),
        name="scatter",
    )(off, far_rows, nseg, x, norm_ffn, route, tri, xb)


def _experts_kernel(blk_ref, nreal_ref, xb_ref, wg_ref, wu_ref, wd_ref, yb_ref, acc_ref):
    del blk_ref
    i = pl.program_id(0)
    f = pl.program_id(1)

    @pl.when(f == 0)
    def _():
        acc_ref[...] = jnp.zeros_like(acc_ref)

    @pl.when(i < nreal_ref[0])
    def _():
        _swiglu_step(xb_ref[...], wg_ref, wu_ref, wd_ref, acc_ref)

    @pl.when(f == pl.num_programs(1) - 1)
    def _():
        yb_ref[...] = acc_ref[...].astype(yb_ref.dtype)


def _experts(blk_e, nreal, xb, wg, wu, wd, idx, tf):
    n_blocks = blk_e.shape[0]
    d_ff = wg.shape[-1]
    n_f = d_ff // tf

    def live_block(i, nreal):
        return jnp.minimum(i, nreal[0] - 1)

    def wcol(i, f, blk, nreal):
        return idx, blk[live_block(i, nreal)], 0, jnp.where(i < nreal[0], f, n_f - 1)

    def wrow(i, f, blk, nreal):
        return idx, blk[live_block(i, nreal)], jnp.where(i < nreal[0], f, n_f - 1), 0

    return pl.pallas_call(
        _experts_kernel,
        grid_spec=pltpu.PrefetchScalarGridSpec(
            num_scalar_prefetch=2,
            grid=(n_blocks, n_f),
            in_specs=[pl.BlockSpec((MOE_ROWS, D_MODEL), lambda i, f, blk, nreal: (live_block(i, nreal), 0)),
                      pl.BlockSpec((None, None, D_MODEL, tf), wcol),
                      pl.BlockSpec((None, None, D_MODEL, tf), wcol),
                      pl.BlockSpec((None, None, tf, D_MODEL), wrow)],
            out_specs=pl.BlockSpec((MOE_ROWS, D_MODEL), lambda i, f, blk, nreal: (i, 0)),
            scratch_shapes=[pltpu.VMEM((MOE_ROWS, D_MODEL), F32)]),
        out_shape=jax.ShapeDtypeStruct((n_blocks * MOE_ROWS, D_MODEL), BF16),
        compiler_params=_params(("parallel", "arbitrary")),
        name="experts",
    )(blk_e, nreal, xb, wg, wu, wd)


def _combine_kernel(off_ref, far_rows_ref, nseg_ref, x_ref, meta_ref, yb_ref, o_ref, ys_ref, sem, *, ts, cap, n_bits):
    i = pl.program_id(0)
    ys_ref[...] = jnp.zeros_like(ys_ref)
    copies = _segment_copies(i, n_bits, ys_ref, off_ref, yb_ref, far_rows_ref, nseg_ref, sem, False)
    for pred, cp in copies:
        pl.when(pred)(cp.start)
    for pred, cp in copies:
        pl.when(pred)(cp.wait)
    slot = lax.broadcasted_iota(jnp.int32, (ts, cap), 1).astype(F32)
    ys = ys_ref[...]
    q1 = jnp.where(slot == meta_ref[:, 0:1], 1.0, 0.0).astype(BF16)
    q2 = jnp.where(slot == meta_ref[:, 1:2], 1.0, 0.0).astype(BF16)
    o_ref[...] = x_ref[...] + meta_ref[:, 2:3] * _dot(q1, ys) + meta_ref[:, 3:4] * _dot(q2, ys)


def _combine(off, far_rows, nseg, x, meta, yb, ts):
    n = x.shape[0]
    cap = _sorted_cap(ts)
    n_bits = int(TOP_K * ts // SEG_ALIGN).bit_length()
    return pl.pallas_call(
        functools.partial(_combine_kernel, ts=ts, cap=cap, n_bits=n_bits),
        grid_spec=pltpu.PrefetchScalarGridSpec(
            num_scalar_prefetch=3,
            grid=(n // ts,),
            in_specs=[pl.BlockSpec((ts, D_MODEL), lambda i, *_: (i, 0)),
                      pl.BlockSpec((ts, LANES), lambda i, *_: (i, 0)),
                      pl.BlockSpec(memory_space=pl.ANY)],
            out_specs=pl.BlockSpec((ts, D_MODEL), lambda i, *_: (i, 0)),
            scratch_shapes=[pltpu.VMEM((cap, D_MODEL), BF16), pltpu.SemaphoreType.DMA(())]),
        out_shape=jax.ShapeDtypeStruct((n, D_MODEL), F32),
        compiler_params=_params(("arbitrary",)),
        name="combine",
    )(off, far_rows, nseg, x, meta, yb)


def _moe(xs, tss, norm_ffn, rwt, rb, wg, wu, wd, layer, idx, tf):
    gated = [_gate(x, norm_ffn, rwt, rb, layer, idx, ts) for x, ts in zip(xs, tss)]
    off = jnp.concatenate([g[1][:, :, 0] for g in gated], axis=0)
    cpad = jnp.concatenate([g[1][:, :, 1] for g in gated], axis=0)
    n_subs = [g[1].shape[0] for g in gated]
    tot = jnp.sum(cpad, axis=0)
    padded = (tot + MOE_ROWS - 1) // MOE_ROWS * MOE_ROWS
    pend = jnp.cumsum(padded)
    within = jnp.cumsum(cpad, axis=0) - cpad
    off = off.astype(jnp.int32).reshape(-1)
    grouped_rows = ((pend - padded)[None, :] + within).astype(jnp.int32).reshape(-1)
    nseg = (cpad // SEG_ALIGN).astype(jnp.int32).reshape(-1)
    n_assign = TOP_K * sum(x.shape[0] for x in xs)
    n_blocks = (n_assign + (SEG_ALIGN - 1) * sum(n_subs) * N_EXPERTS + N_EXPERTS * (MOE_ROWS - 1)) // MOE_ROWS
    blk_e = jnp.minimum(jnp.sum(jnp.arange(n_blocks)[:, None] * MOE_ROWS >= pend[None, :], axis=1),
                        N_EXPERTS - 1).astype(jnp.int32)
    nreal = (pend[-1:] // MOE_ROWS).astype(jnp.int32)

    xb = jnp.zeros((n_blocks * MOE_ROWS, D_MODEL), BF16)
    metas, parts = [], []
    lo = 0
    for x, g, ns, ts in zip(xs, gated, n_subs, tss):
        sl = slice(lo * N_EXPERTS, (lo + ns) * N_EXPERTS)
        parts.append((off[sl], grouped_rows[sl], nseg[sl]))
        meta, xb = _scatter(*parts[-1], x, norm_ffn, g[0], xb, layer, ts)
        metas.append(meta)
        lo += ns
    yb = _experts(blk_e, nreal, xb, wg, wu, wd, idx, tf)
    return [_combine(*part, x, meta, yb, ts) for part, x, meta, ts in zip(parts, xs, metas, tss)]


def kernel(x_prompt, x_sample, state_pool, cache_kv_g0, cache_kv_g1, cache_kv_g2, norm_attn, w_in, q_norm, k_norm, w_pool, pool_scale, w_branch_pool, w_branch_attn, w_out, norm_ffn, w1_dense, w3_dense, w2_dense, router_w, router_b, we_gate, we_up, we_down):
    batch, seq, _ = x_prompt.shape
    dec_batch, dec_seq, _ = x_sample.shape
    depth = w_in.shape[0]
    caches = (cache_kv_g0, cache_kv_g1, cache_kv_g2)
    n_p, n_s = batch * seq, dec_batch * dec_seq
    tm_p = 512
    dils = tuple(d for _, d in ATT_GROUPS)
    keeps = tuple(min(win, seq) for win, _ in ATT_GROUPS)

    head_of = np.arange(GROUP_W) // HEAD_DIM
    same_head = (head_of[:, None] == head_of[None, :]).astype(np.float32)
    seg_mean = jnp.asarray(same_head / HEAD_DIM, BF16)
    w_qkv = w_in[:, :, :PROJ_W].astype(BF16)
    qgain = (jnp.tile(q_norm, (1, HEADS)) * (HEAD_DIM ** -0.5)).reshape(depth, 1, GROUP_W)
    kgain = jnp.tile(k_norm, (1, HEADS)).reshape(depth, 1, GROUP_W)
    norm_attn3 = norm_attn.reshape(depth, 1, D_MODEL)
    w = {"w_pool": w_pool.astype(BF16), "pool_scale": pool_scale.reshape(depth, 1, POOL_W),
         "norm_attn": norm_attn3, "w_gate": w_in[:, :, PROJ_W:].astype(BF16),
         "w_bp": w_branch_pool.astype(BF16), "w_ba": w_branch_attn.astype(BF16), "w_o": w_out.astype(BF16),
         "norm_ffn": norm_ffn.reshape(depth, 1, D_MODEL)}
    w1_b, w3_b, w2_b = w1_dense.astype(BF16), w3_dense.astype(BF16), w2_dense.astype(BF16)
    wg_b, wu_b, wd_b = we_gate.astype(BF16), we_up.astype(BF16), we_down.astype(BF16)
    rwt = jnp.swapaxes(router_w, 1, 2).astype(BF16)
    rb = router_b.reshape(-1, N_EXPERTS, 1)

    xp = x_prompt.reshape(n_p, D_MODEL)
    xs = x_sample.reshape(n_s, D_MODEL)
    pool_p, pool_s = [], []
    kv_s = [[] for _ in ATT_GROUPS]
    kvo_p = tuple(jnp.zeros((depth, batch, 2, GROUP_W, keep), F32) for keep in keeps)
    ones = (1,) * N_GROUPS
    for layer in range(depth):
        res = _proj(xp, layer, norm_attn3, w_qkv, qgain, kgain, seg_mean, batch, seq, tm_p, dils, keeps,
                    layer, depth, kvo_p, 1, True)
        u, qs, kcs, kvo_p = res[0], res[1:4], res[4:7], tuple(res[7:10])
        outs, lses = [], []
        for gi in range(N_GROUPS):
            o, lse = _attention(qs[gi], kcs[gi], gi)
            outs.append(o)
            lses.append(lse)
        pool_p.append(u.reshape(batch, seq, POOL_W)[:, seq - POOL_STATE:])
        xp, xpn = _merge(xp, u, outs, lses, layer, w, batch, seq, tm_p, dils)

        res = _proj(xs, layer, norm_attn3, w_qkv, qgain, kgain, seg_mean, 1, n_s, n_s, ones, (n_s,) * N_GROUPS,
                    0, 1, (), 3, False)
        u = res[0]
        tokens_to_lanes = ((0, 0),) * 3 + ((0, LANES - dec_seq),)
        qs = [jnp.pad(q.reshape(dec_batch, dec_seq, HEADS, HEAD_DIM).transpose(0, 2, 3, 1).astype(F32),
                      tokens_to_lanes) for q in res[1:4]]
        kv_new = [kv.reshape(dec_batch, dec_seq, 2, HEADS, HEAD_DIM) for kv in res[7:10]]
        kvn = [jnp.pad(kv.transpose(0, 2, 3, 4, 1), ((0, 0),) + tokens_to_lanes) for kv in kv_new]
        pool_o, att_o, new_pool = _sample_mix(state_pool, u, qs, kvn, caches, layer, w, dec_batch, dec_seq)
        att_o = att_o[..., :dec_seq].transpose(0, 3, 1, 2).reshape(n_s, GROUP_W)
        for gi in range(N_GROUPS):
            kv_s[gi].append(kv_new[gi])
        pool_s.append(new_pool)
        xs, xsn = _tail(xs, pool_o.reshape(n_s, POOL_W), att_o, layer, w)

        i = layer // 2
        if layer % 2 == 0:
            xp = _ffn(xp, xpn, w1_b, w3_b, w2_b, i, 1024, 256)
            xs = _ffn(xs, xsn, w1_b, w3_b, w2_b, i, n_s, 256)
        else:
            xp, xs = _moe([xp, xs], [512, n_s], w["norm_ffn"], rwt, rb, wg_b, wu_b, wd_b, layer, i, 512)
    kv_p = [kvo_p[g].reshape(depth, batch, 2, HEADS, HEAD_DIM, keeps[g]).transpose(0, 1, 5, 2, 3, 4)
            for g in range(N_GROUPS)]
    return (xp.reshape(batch, seq, D_MODEL), xs.reshape(dec_batch, dec_seq, D_MODEL),
            jnp.stack(pool_p), jnp.stack(pool_s),
            kv_p[0], jnp.stack(kv_s[0]),
            kv_p[1], jnp.stack(kv_s[1]),
            kv_p[2], jnp.stack(kv_s[2]))
```

```python
import functools

import numpy as np
import jax
import jax.numpy as jnp
from jax import lax
from jax.experimental import pallas as pl
from jax.experimental.pallas import tpu as pltpu

F32 = jnp.float32
BF16 = jnp.bfloat16

D_MODEL = 1024
PAST_LEN = 16384
POOL_WINDOWS = (2, 4, 8, 16)
POOL_GROUP = 128
POOL_W = 512
POOL_STATE = 15
ATT_GROUPS = ((128, 1), (512, 4), (2048, 16))
N_GROUPS = len(ATT_GROUPS)
HEAD_DIM = 64
HEADS = 8
GROUP_W = 512
QKV_W = 1536
Q_BLOCK = 128
ALIBI_MAX = 8.0
N_EXPERTS = 8
TOP_K = 2
RMS_EPS = 1e-6
NEG_INF = -1e30
LANES = 128
LANE_CHUNKS = GROUP_W // LANES
PROJ_W = POOL_W + 3 * QKV_W
VMEM_LIMIT = 56 * 1024 * 1024


def _slopes():
    i = np.arange(1, N_GROUPS * HEADS + 1, dtype=np.float32)
    return np.exp2(-ALIBI_MAX * i / (N_GROUPS * HEADS)).astype(np.float32).reshape(N_GROUPS, HEADS)


def _params(sem):
    return pltpu.CompilerParams(dimension_semantics=sem, vmem_limit_bytes=VMEM_LIMIT)


def _rms(x, gain):
    return x * lax.rsqrt(jnp.mean(x * x, axis=-1, keepdims=True) + RMS_EPS) * gain


def _dot(a, b):
    return jnp.dot(a, b, preferred_element_type=F32)


def _resident(shape, index_map):
    return pl.BlockSpec(shape, index_map, pipeline_mode=pl.Buffered(1))


def _write_classes(dst_ref, col0, val, dil, tmp_ref, slot):
    rows = val.shape[0] // dil
    cols = slice(col0, col0 + GROUP_W)
    if dil == 1:
        dst_ref[0, :, cols] = val.astype(dst_ref.dtype)
        return
    for c in range(LANE_CHUNKS):
        tmp_ref[slot, c] = val[:, c * LANES:(c + 1) * LANES]
    for r in range(dil):
        picked = [tmp_ref[slot, c, pl.ds(r, rows, stride=dil), :] for c in range(LANE_CHUNKS)]
        dst_ref[r, :, cols] = jnp.concatenate(picked, axis=1).astype(dst_ref.dtype)


def _proj_kernel(x_ref, g_ref, w_ref, qg_ref, kg_ref, seg_ref, *rest, tm, dils, keeps, n_alias, norm_terms,
                 kv_positions_minor):
    u_ref, q0_ref, q1_ref, q2_ref, kc0_ref, kc1_ref, kc2_ref, kvo0_ref, kvo1_ref, kvo2_ref, tmp_ref = rest[n_alias:]
    s = pl.program_id(1)
    n_tiles = pl.num_programs(1)
    xn = _rms(x_ref[...], g_ref[...]).astype(BF16)

    def zblk(j):
        return _dot(xn, w_ref[:, j * GROUP_W:(j + 1) * GROUP_W])

    def headnorm(z, gain):
        rem = z * z
        ms = None
        for _ in range(norm_terms):
            part = rem.astype(BF16)
            rem = rem - part.astype(F32)
            ms = _dot(part, seg_ref[...]) if ms is None else ms + _dot(part, seg_ref[...])
        return z * lax.rsqrt(ms + RMS_EPS) * gain

    u_ref[...] = zblk(0)
    q_refs = (q0_ref, q1_ref, q2_ref)
    kc_refs = (kc0_ref, kc1_ref, kc2_ref)
    kvo_refs = (kvo0_ref, kvo1_ref, kvo2_ref)
    slot = 0
    for g in range(N_GROUPS):
        q = headnorm(zblk(1 + g), qg_ref[...])
        k = headnorm(zblk(1 + N_GROUPS + g), kg_ref[...])
        v = zblk(1 + 2 * N_GROUPS + g)
        for dst, col0, val in ((q_refs[g], 0, q), (kc_refs[g], 0, k), (kc_refs[g], GROUP_W, v)):
            _write_classes(dst, col0, val, dils[g], tmp_ref, slot % tmp_ref.shape[0])
            slot += dils[g] > 1
        keep = keeps[g]
        first = n_tiles - max(keep // tm, 1)
        rows = min(keep, tm)

        @pl.when(s >= first)
        def _(k=k, v=v, ref=kvo_refs[g], rows=rows):
            if kv_positions_minor:
                ref[0] = k[tm - rows:, :].T
                ref[1] = v[tm - rows:, :].T
            else:
                ref[:, 0:GROUP_W] = k[tm - rows:, :]
                ref[:, GROUP_W:2 * GROUP_W] = v[tm - rows:, :]


def _proj(x, layer, norm_attn, w_qkv, qgain, kgain, seg, batch, seq, tm, dils, keeps, out_layer, out_depth, prev_kvo,
          norm_terms, kv_positions_minor):
    n_tiles = seq // tm
    kvo_specs, kvo_shapes = [], []
    for g in range(N_GROUPS):
        keep = keeps[g]
        assert keep % tm == 0 or (keep < tm and keep % LANES == 0)
        first = n_tiles - max(keep // tm, 1)
        rows = min(keep, tm)
        if kv_positions_minor:
            kvo_specs.append(pl.BlockSpec((None, None, 2, GROUP_W, rows),
                                          lambda b, s, first=first: (out_layer, b, 0, 0, jnp.maximum(s - first, 0))))
            kvo_shapes.append(jax.ShapeDtypeStruct((out_depth, batch, 2, GROUP_W, keep), F32))
        else:
            kvo_specs.append(pl.BlockSpec((None, None, rows, 2 * GROUP_W),
                                          lambda b, s, first=first: (out_layer, b, jnp.maximum(s - first, 0), 0)))
            kvo_shapes.append(jax.ShapeDtypeStruct((out_depth, batch, keep, 2 * GROUP_W), F32))
    cls_spec = lambda g, width: pl.BlockSpec((None, dils[g], tm // dils[g], width), lambda b, s: (b, 0, s, 0))
    cls_shape = lambda g, width: jax.ShapeDtypeStruct((batch, dils[g], seq // dils[g], width), BF16)
    n_alias = len(prev_kvo)
    n_in = 6
    return pl.pallas_call(
        functools.partial(_proj_kernel, tm=tm, dils=dils, keeps=keeps, n_alias=n_alias,
                          norm_terms=norm_terms, kv_positions_minor=kv_positions_minor),
        grid=(batch, n_tiles),
        in_specs=[
            pl.BlockSpec((tm, D_MODEL), lambda b, s: (b * n_tiles + s, 0)),
            _resident((None, 1, D_MODEL), lambda b, s: (layer, 0, 0)),
            _resident((None, D_MODEL, PROJ_W), lambda b, s: (layer, 0, 0)),
            _resident((None, 1, GROUP_W), lambda b, s: (layer, 0, 0)),
            _resident((None, 1, GROUP_W), lambda b, s: (layer, 0, 0)),
            _resident((GROUP_W, GROUP_W), lambda b, s: (0, 0)),
        ] + [pl.BlockSpec(memory_space=pl.ANY)] * n_alias,
        out_specs=[pl.BlockSpec((tm, POOL_W), lambda b, s: (b * n_tiles + s, 0))]
                  + [cls_spec(g, GROUP_W) for g in range(N_GROUPS)]
                  + [cls_spec(g, 2 * GROUP_W) for g in range(N_GROUPS)] + kvo_specs,
        out_shape=[jax.ShapeDtypeStruct((batch * seq, POOL_W), F32)]
                  + [cls_shape(g, GROUP_W) for g in range(N_GROUPS)]
                  + [cls_shape(g, 2 * GROUP_W) for g in range(N_GROUPS)] + kvo_shapes,
        scratch_shapes=[pltpu.VMEM((3, LANE_CHUNKS, tm, LANES), F32)],
        input_output_aliases={n_in + g: 1 + 2 * N_GROUPS + g for g in range(n_alias)},
        compiler_params=_params(("parallel", "arbitrary")),
        name="proj",
    )(x, norm_attn, w_qkv, qgain, kgain, seg, *prev_kvo)


def _attn_kernel(q_ref, kvc_ref, *rest, slope_dil, tc, has_halo):
    if has_halo:
        kvh_ref, o_ref, lse_ref, kbuf, vbuf = rest
    else:
        o_ref, lse_ref, kbuf, vbuf = rest
    off = Q_BLOCK if has_halo else 0
    if has_halo:
        kbuf[0:Q_BLOCK, :] = kvh_ref[:, 0:GROUP_W]
        vbuf[0:Q_BLOCK, :] = kvh_ref[:, GROUP_W:2 * GROUP_W]
    kbuf[off:off + tc, :] = kvc_ref[:, 0:GROUP_W]
    vbuf[off:off + tc, :] = kvc_ref[:, GROUP_W:2 * GROUP_W]

    nk = Q_BLOCK + off
    qi = lax.broadcasted_iota(jnp.int32, (Q_BLOCK, nk), 0)
    cj = lax.broadcasted_iota(jnp.int32, (Q_BLOCK, nk), 1)
    dist = qi - cj + off
    distf = dist.astype(F32)
    maskneg = jnp.where((dist >= 0) & (dist <= Q_BLOCK), 0.0, NEG_INF).astype(F32)
    if has_halo:
        first = jnp.where(pl.program_id(2) == 0, NEG_INF, 0.0).astype(F32)
        mask_first = maskneg + jnp.where(cj < Q_BLOCK, first, 0.0)

    for i in range(tc // Q_BLOCK):
        mask = mask_first if (has_halo and i == 0) else maskneg
        rows = slice(i * Q_BLOCK, (i + 1) * Q_BLOCK)
        krows = slice(i * Q_BLOCK, i * Q_BLOCK + nk)
        for hp in range(HEADS // 2):
            outs, lses = [], []
            for h in (2 * hp, 2 * hp + 1):
                cols = slice(h * HEAD_DIM, (h + 1) * HEAD_DIM)
                s = lax.dot_general(q_ref[rows, cols], kbuf[krows, cols],
                                    (((1,), (1,)), ((), ())), preferred_element_type=F32)
                s = s - slope_dil[h] * distf + mask
                m = jnp.max(s, axis=1, keepdims=True)
                p = jnp.exp(s - m)
                l = jnp.sum(p, axis=1, keepdims=True)
                o = _dot(p.astype(BF16), vbuf[krows, cols]) / l
                outs.append(o)
                lses.append(jnp.broadcast_to(m + jnp.log(l), (Q_BLOCK, HEAD_DIM)))
            pc = slice(hp * 2 * HEAD_DIM, (hp + 1) * 2 * HEAD_DIM)
            o_ref[rows, pc] = jnp.concatenate(outs, axis=1).astype(BF16)
            lse_ref[rows, pc] = jnp.concatenate(lses, axis=1)


def _attention(q, kc, gi):
    win, dil = ATT_GROUPS[gi]
    assert win // dil == Q_BLOCK
    batch, _, l, _ = q.shape
    tc = min(l, 512)
    has_halo = l > Q_BLOCK
    slope_dil = tuple(float(s) * dil for s in _slopes()[gi])
    in_specs = [
        pl.BlockSpec((None, None, tc, GROUP_W), lambda b, r, c: (b, r, c, 0)),
        pl.BlockSpec((None, None, tc, 2 * GROUP_W), lambda b, r, c: (b, r, c, 0)),
    ]
    args = [q, kc]
    if has_halo:
        in_specs.append(pl.BlockSpec((None, None, Q_BLOCK, 2 * GROUP_W),
                                     lambda b, r, c: (b, r, jnp.maximum(c * (tc // Q_BLOCK) - 1, 0), 0)))
        args.append(kc)
    nk = tc + (Q_BLOCK if has_halo else 0)
    return pl.pallas_call(
        functools.partial(_attn_kernel, slope_dil=slope_dil, tc=tc, has_halo=has_halo),
        grid=(batch, dil, l // tc),
        in_specs=in_specs,
        out_specs=[pl.BlockSpec((None, None, tc, GROUP_W), lambda b, r, c: (b, r, c, 0))] * 2,
        out_shape=[jax.ShapeDtypeStruct((batch, dil, l, GROUP_W), BF16),
                   jax.ShapeDtypeStruct((batch, dil, l, GROUP_W), F32)],
        scratch_shapes=[pltpu.VMEM((nk, GROUP_W), BF16), pltpu.VMEM((nk, GROUP_W), BF16)],
        compiler_params=_params(("parallel", "parallel", "arbitrary")),
        name=f"attn_g{gi}",
    )(*args)


def _mix_tail(x, pool_o, att_o, nattn_ref, wgate_ref, wbp_ref, wba_ref, wo_ref, nffn_ref, xo_ref, xn_ref):
    gates = jax.nn.sigmoid(_dot(_rms(x, nattn_ref[...]).astype(BF16), wgate_ref[...]))
    hp = _dot(pool_o.astype(BF16), wbp_ref[...])
    ha = _dot(att_o.astype(BF16), wba_ref[...])
    t = gates[:, 0:D_MODEL] * hp + gates[:, D_MODEL:2 * D_MODEL] * ha
    xo = x + _dot(t.astype(BF16), wo_ref[...])
    xo_ref[...] = xo
    xn_ref[...] = _rms(xo, nffn_ref[...]).astype(BF16)


def _group_linear(m, wpool_ref, pscale_ref):
    parts = [_dot(m[:, gi * POOL_GROUP:(gi + 1) * POOL_GROUP].astype(BF16), wpool_ref[gi])
             for gi in range(len(POOL_WINDOWS))]
    return jnp.concatenate(parts, axis=1) * pscale_ref[...]


def _read_classes(src_ref, dil, il_ref, slot):
    if dil == 1:
        return src_ref[0].astype(F32)
    rows = src_ref.shape[1]
    for r in range(dil):
        v = src_ref[r].astype(F32)
        for c in range(LANE_CHUNKS):
            il_ref[slot, c, pl.ds(r, rows, stride=dil), :] = v[:, c * LANES:(c + 1) * LANES]
    return jnp.concatenate([il_ref[slot, c] for c in range(LANE_CHUNKS)], axis=1)


def _merge_kernel(x_ref, u_ref, uh_ref, o0_ref, o1_ref, o2_ref, l0_ref, l1_ref, l2_ref,
                  wpool_ref, pscale_ref, nattn_ref, wgate_ref, wbp_ref, wba_ref, wo_ref, nffn_ref,
                  xo_ref, xn_ref, ext_ref, il_ref, *, tm, dils):
    si = pl.program_id(1)
    halo = POOL_STATE + 1
    ext_ref[0:halo, :] = jnp.where(si == 0, 0.0, uh_ref[...])
    ext_ref[halo:halo + tm, :] = u_ref[...]
    pos = si * tm + lax.broadcasted_iota(jnp.int32, (tm, 1), 0)
    parts = []
    for gi, win in enumerate(POOL_WINDOWS):
        cols = slice(gi * POOL_GROUP, (gi + 1) * POOL_GROUP)
        own = ext_ref[halo:halo + tm, cols]
        acc = own
        for back in range(1, win):
            acc = acc + ext_ref[halo - back:halo - back + tm, cols]
        inv = 1.0 / jnp.minimum(pos + 1, win).astype(F32)
        parts.append(acc * inv - own)
    pool_o = _group_linear(jnp.concatenate(parts, axis=1), wpool_ref, pscale_ref)

    slot = 0
    os_, ls_ = [], []
    for g, (o_ref, l_ref) in enumerate(((o0_ref, l0_ref), (o1_ref, l1_ref), (o2_ref, l2_ref))):
        os_.append(_read_classes(o_ref, dils[g], il_ref, slot))
        slot += dils[g] > 1
        ls_.append(_read_classes(l_ref, dils[g], il_ref, slot))
        slot += dils[g] > 1
    mx = jnp.maximum(ls_[0], jnp.maximum(ls_[1], ls_[2]))
    es = [jnp.exp(v - mx) for v in ls_]
    att_o = (es[0] * os_[0] + es[1] * os_[1] + es[2] * os_[2]) / (es[0] + es[1] + es[2])
    _mix_tail(x_ref[...], pool_o, att_o, nattn_ref, wgate_ref, wbp_ref, wba_ref, wo_ref, nffn_ref, xo_ref, xn_ref)


def _wspec(shape, layer, grid_rank):
    nd = len(shape)
    return _resident((None,) + tuple(shape), lambda *_: (layer,) + (0,) * nd)


def _tail_weight_specs(layer, grid_rank):
    return [_wspec((1, D_MODEL), layer, grid_rank), _wspec((D_MODEL, 2 * D_MODEL), layer, grid_rank),
            _wspec((POOL_W, D_MODEL), layer, grid_rank), _wspec((GROUP_W, D_MODEL), layer, grid_rank),
            _wspec((D_MODEL, D_MODEL), layer, grid_rank), _wspec((1, D_MODEL), layer, grid_rank)]


def _tail_weights(w):
    return (w["norm_attn"], w["w_gate"], w["w_bp"], w["w_ba"], w["w_o"], w["norm_ffn"])


def _merge(x, u, outs, lses, layer, w, batch, seq, tm, dils):
    n = x.shape[0]
    n_tiles = seq // tm
    halo = POOL_STATE + 1
    row = lambda width: pl.BlockSpec((tm, width), lambda b, s: (b * n_tiles + s, 0))
    cls = lambda g: pl.BlockSpec((None, dils[g], tm // dils[g], GROUP_W), lambda b, s: (b, 0, s, 0))
    n_il = 2 * sum(d > 1 for d in dils)
    return pl.pallas_call(
        functools.partial(_merge_kernel, tm=tm, dils=dils),
        grid=(batch, n_tiles),
        in_specs=[row(D_MODEL), row(POOL_W),
                  pl.BlockSpec((halo, POOL_W),
                               lambda b, s: (jnp.maximum((b * n_tiles + s) * (tm // halo) - 1, 0), 0))]
                 + [cls(g) for g in range(N_GROUPS)] * 2
                 + [_wspec((len(POOL_WINDOWS), POOL_GROUP, POOL_GROUP), layer, 2), _wspec((1, POOL_W), layer, 2)]
                 + _tail_weight_specs(layer, 2),
        out_specs=[row(D_MODEL), row(D_MODEL)],
        out_shape=[jax.ShapeDtypeStruct((n, D_MODEL), F32), jax.ShapeDtypeStruct((n, D_MODEL), BF16)],
        scratch_shapes=[pltpu.VMEM((halo + tm, POOL_W), F32), pltpu.VMEM((n_il, LANE_CHUNKS, tm, LANES), F32)],
        compiler_params=_params(("parallel", "parallel")),
        name="merge",
    )(x, u, u, *outs, *lses, w["w_pool"], w["pool_scale"], *_tail_weights(w))


def _head_mask():
    return jnp.asarray(np.arange(GROUP_W)[None, :] // HEAD_DIM == np.arange(HEADS)[:, None], F32)


def _sample_keys(dec_seq, cache_rows):
    slopes = _slopes()
    geo = []
    for gi, (win, dil) in enumerate(ATT_GROUPS):
        lc = cache_rows[gi]
        assert lc % dil == 0 and lc // dil == LANES
        cls = (lc + np.arange(dec_seq)) % dil
        n_cached = (cls.max() + 1) * LANES
        lane = np.arange(n_cached + LANES)
        sel = (np.arange(lc)[:, None] == (lane[:n_cached] % LANES) * dil + lane[:n_cached] // LANES)
        t = np.arange(dec_seq)[:, None]
        back = np.where(lane < n_cached, lc + t - ((lane % LANES) * dil + lane // LANES), t - (lane - n_cached))
        valid = (back >= 0) & (back % dil == 0) & (back // dil <= win // dil) & (lane < n_cached + dec_seq)
        bias = np.where(valid[:, None, :], -slopes[gi][None, :, None] * back[:, None, :], NEG_INF)
        geo.append((jnp.asarray(sel, BF16), jnp.asarray(bias.reshape(dec_seq * HEADS, -1), F32)))
    return geo


def _sample_mix_kernel(state_ref, u_ref, q0_ref, q1_ref, q2_ref, kn0_ref, kn1_ref, kn2_ref, c0_ref, c1_ref, c2_ref,
                       sel0_ref, sel1_ref, sel2_ref, bias0_ref, bias1_ref, bias2_ref, headmask_ref,
                       wpool_ref, pscale_ref, pool_ref, att_ref, newpool_ref, ext_ref, m_ref, *, dec_seq):
    ext_ref[...] = jnp.zeros_like(ext_ref)
    ext_ref[0:POOL_STATE, :] = state_ref[...]
    ext_ref[POOL_STATE:POOL_STATE + dec_seq, :] = u_ref[...]
    newpool_ref[...] = ext_ref[dec_seq:dec_seq + POOL_STATE, :]

    m_ref[...] = jnp.zeros_like(m_ref)
    for t in range(dec_seq):
        row = POOL_STATE + t
        for gi, win in enumerate(POOL_WINDOWS):
            cols = slice(gi * POOL_GROUP, (gi + 1) * POOL_GROUP)
            tot = jnp.sum(ext_ref[row - win + 1:row + 1, cols], axis=0, keepdims=True)
            cnt = float(min(PAST_LEN + t + 1, win))
            m_ref[t:t + 1, cols] = tot / cnt - ext_ref[row:row + 1, cols]
    pool_ref[...] = _group_linear(m_ref[...], wpool_ref, pscale_ref)[0:dec_seq]

    groups = ((q0_ref, kn0_ref, c0_ref, sel0_ref, bias0_ref), (q1_ref, kn1_ref, c1_ref, sel1_ref, bias1_ref),
              (q2_ref, kn2_ref, c2_ref, sel2_ref, bias2_ref))
    k_rows = HEADS * HEAD_DIM
    outs, lses = [], []
    for q_ref, kn_ref, c_ref, sel_ref, bias_ref in groups:
        flat = c_ref[...].reshape(2 * k_rows, c_ref.shape[-1]).astype(BF16)
        cached = _dot(flat, sel_ref[...]).astype(BF16)
        new = kn_ref[...].reshape(2 * k_rows, LANES).astype(BF16)
        keys = jnp.concatenate([cached[0:k_rows], new[0:k_rows]], axis=1)
        vals = jnp.concatenate([cached[k_rows:], new[k_rows:]], axis=1)
        s = _dot(q_ref[...], keys) + bias_ref[...]
        mx = jnp.max(s, axis=1, keepdims=True)
        p = jnp.exp(s - mx)
        l = jnp.sum(p, axis=1, keepdims=True)
        outs.append(lax.dot_general((p / l).astype(BF16), vals, (((1,), (1,)), ((), ())),
                                    preferred_element_type=F32))
        lses.append(mx + jnp.log(l))
    mx = jnp.maximum(lses[0], jnp.maximum(lses[1], lses[2]))
    es = [jnp.exp(v - mx) for v in lses]
    mixed = (es[0] * outs[0] + es[1] * outs[1] + es[2] * outs[2]) / (es[0] + es[1] + es[2])
    att_ref[...] = jnp.sum(mixed.reshape(dec_seq, HEADS, k_rows) * headmask_ref[...][None], axis=1)


def _sample_mix(state_pool, u, qs, kvn, caches, layer, w, dec_batch, dec_seq):
    cache_rows = [c.shape[2] for c in caches]
    geo = _sample_keys(dec_seq, cache_rows)
    consts = [g[i] for i in (0, 1) for g in geo] + [_head_mask()]
    kv_tail = (2, HEADS, HEAD_DIM)
    cviews = [jnp.transpose(c, (0, 1, 3, 4, 5, 2)).reshape((-1,) + kv_tail + (c.shape[2],)) for c in caches]
    cspecs = [pl.BlockSpec((None,) + kv_tail + (lc,), lambda b: (layer * dec_batch + b, 0, 0, 0, 0))
              for lc in cache_rows]
    per_batch = lambda *tail: pl.BlockSpec((None,) + tail, lambda b: (b,) + (0,) * len(tail))
    const = lambda shape: _resident(shape, lambda b: (0,) * len(shape))
    return pl.pallas_call(
        functools.partial(_sample_mix_kernel, dec_seq=dec_seq),
        grid=(dec_batch,),
        in_specs=[pl.BlockSpec((None, POOL_STATE, POOL_W), lambda b: (layer * dec_batch + b, 0, 0)),
                  per_batch(dec_seq, POOL_W)] + [per_batch(dec_seq * HEADS, GROUP_W)] * 3
                 + [per_batch(*kv_tail, LANES)] * 3 + cspecs + [const(c.shape) for c in consts]
                 + [_wspec((len(POOL_WINDOWS), POOL_GROUP, POOL_GROUP), layer, 1), _wspec((1, POOL_W), layer, 1)],
        out_specs=[per_batch(dec_seq, POOL_W), per_batch(dec_seq, GROUP_W),
                   pl.BlockSpec((None, POOL_STATE, POOL_W), lambda b: (b, 0, 0))],
        out_shape=[jax.ShapeDtypeStruct((dec_batch, dec_seq, POOL_W), F32),
                   jax.ShapeDtypeStruct((dec_batch, dec_seq, GROUP_W), F32),
                   jax.ShapeDtypeStruct((dec_batch, POOL_STATE, POOL_W), F32)],
        scratch_shapes=[pltpu.VMEM((POOL_STATE + dec_seq + 5, POOL_W), F32), pltpu.VMEM((8, POOL_W), F32)],
        compiler_params=_params(("parallel",)),
        name="sample_mix",
    )(state_pool.reshape(-1, POOL_STATE, POOL_W), u.reshape(dec_batch, dec_seq, POOL_W), *qs, *kvn, *cviews,
      *consts, w["w_pool"], w["pool_scale"])


def _tail_kernel(x_ref, pool_ref, att_ref, nattn_ref, wgate_ref, wbp_ref, wba_ref, wo_ref, nffn_ref, xo_ref, xn_ref):
    _mix_tail(x_ref[...], pool_ref[...], att_ref[...], nattn_ref, wgate_ref, wbp_ref, wba_ref, wo_ref, nffn_ref,
              xo_ref, xn_ref)


def _tail(x, pool_o, att_o, layer, w):
    n = x.shape[0]
    row = lambda width: pl.BlockSpec((n, width), lambda i: (0, 0))
    return pl.pallas_call(
        _tail_kernel,
        grid=(1,),
        in_specs=[row(D_MODEL), row(POOL_W), row(GROUP_W)] + _tail_weight_specs(layer, 1),
        out_specs=[row(D_MODEL), row(D_MODEL)],
        out_shape=[jax.ShapeDtypeStruct((n, D_MODEL), F32), jax.ShapeDtypeStruct((n, D_MODEL), BF16)],
        compiler_params=_params(("arbitrary",)),
        name="tail",
    )(x, pool_o, att_o, *_tail_weights(w))


def _swiglu_step(x, wg_ref, wu_ref, wd_ref, acc_ref):
    h = jax.nn.silu(_dot(x, wg_ref[...])) * _dot(x, wu_ref[...])
    acc_ref[...] += _dot(h.astype(BF16), wd_ref[...])


def _ffn_kernel(x_ref, xn_ref, w1_ref, w3_ref, w2_ref, y_ref, acc_ref):
    f = pl.program_id(1)

    @pl.when(f == 0)
    def _():
        acc_ref[...] = x_ref[...]

    _swiglu_step(xn_ref[...], w1_ref, w3_ref, w2_ref, acc_ref)

    @pl.when(f == pl.num_programs(1) - 1)
    def _():
        y_ref[...] = acc_ref[...]


def _ffn(x, xn, w1, w3, w2, idx, tm, tf):
    n = x.shape[0]
    d_ff = w1.shape[-1]
    return pl.pallas_call(
        _ffn_kernel,
        grid=(n // tm, d_ff // tf),
        in_specs=[pl.BlockSpec((tm, D_MODEL), lambda i, f: (i, 0)),
                  pl.BlockSpec((tm, D_MODEL), lambda i, f: (i, 0)),
                  pl.BlockSpec((None, D_MODEL, tf), lambda i, f: (idx, 0, f)),
                  pl.BlockSpec((None, D_MODEL, tf), lambda i, f: (idx, 0, f)),
                  pl.BlockSpec((None, tf, D_MODEL), lambda i, f: (idx, f, 0))],
        out_specs=pl.BlockSpec((tm, D_MODEL), lambda i, f: (i, 0)),
        out_shape=jax.ShapeDtypeStruct((n, D_MODEL), F32),
        scratch_shapes=[pltpu.VMEM((tm, D_MODEL), F32)],
        compiler_params=_params(("parallel", "arbitrary")),
        name="ffn",
    )(x, xn, w1, w3, w2)


SEG_ALIGN = 16
MOE_ROWS = 512


def _sorted_cap(ts):
    need = TOP_K * ts + N_EXPERTS * (SEG_ALIGN - 1)
    return -(-need // LANES) * LANES if ts >= LANES * 2 else -(-need // SEG_ALIGN) * SEG_ALIGN


def _segment_sizes(oh1, oh2):
    cnt1 = jnp.sum(oh1, axis=1, keepdims=True)
    cnt = cnt1 + jnp.sum(oh2, axis=1, keepdims=True)
    cpad = jnp.floor((cnt + (SEG_ALIGN - 1)) * (1.0 / SEG_ALIGN)) * SEG_ALIGN
    offs = [jnp.zeros((1, 1), F32)]
    for ei in range(1, N_EXPERTS):
        offs.append(offs[-1] + cpad[ei - 1:ei, :])
    return cnt1, cpad, jnp.concatenate(offs, axis=0)


def _gate_kernel(x_ref, g_ref, rwt_ref, rb_ref, route_ref, seg_ref, *, ts):
    xn = _rms(x_ref[...], g_ref[...]).astype(BF16)
    logits = lax.dot_general(rwt_ref[...], xn, (((1,), (1,)), ((), ())), preferred_element_type=F32) + rb_ref[...]
    row = lax.broadcasted_iota(jnp.int32, (N_EXPERTS, ts), 0)
    neg = jnp.float32(-jnp.inf)
    m1 = jnp.max(logits, axis=0, keepdims=True)
    i1 = jnp.min(jnp.where(logits == m1, row, N_EXPERTS), axis=0, keepdims=True)
    rest = jnp.where(row == i1, neg, logits)
    m2 = jnp.max(rest, axis=0, keepdims=True)
    i2 = jnp.min(jnp.where(rest == m2, row, N_EXPERTS), axis=0, keepdims=True)
    e = jnp.exp(m2 - m1)
    zeros = jnp.zeros((N_EXPERTS - 4, ts), F32)
    route_ref[...] = jnp.concatenate([i1.astype(F32), i2.astype(F32), 1.0 / (1.0 + e), e / (1.0 + e), zeros], axis=0)
    _, cpad, off = _segment_sizes(jnp.where(row == i1, 1.0, 0.0), jnp.where(row == i2, 1.0, 0.0))
    lane = lax.broadcasted_iota(jnp.int32, (N_EXPERTS, LANES), 1)
    seg_ref[...] = jnp.where(lane == 0, off, jnp.where(lane == 1, cpad, 0.0)).astype(jnp.int32)


def _gate(x, norm_ffn, rwt, rb, layer, idx, ts):
    n = x.shape[0]
    n_sub = n // ts
    return pl.pallas_call(
        functools.partial(_gate_kernel, ts=ts),
        grid=(n_sub,),
        in_specs=[pl.BlockSpec((ts, D_MODEL), lambda i: (i, 0)),
                  _resident((None, 1, D_MODEL), lambda i: (layer, 0, 0)),
                  _resident((None, N_EXPERTS, D_MODEL), lambda i: (idx, 0, 0)),
                  _resident((None, N_EXPERTS, 1), lambda i: (idx, 0, 0))],
        out_specs=[pl.BlockSpec((None, N_EXPERTS, ts), lambda i: (i, 0, 0)),
                   pl.BlockSpec((None, N_EXPERTS, LANES), lambda i: (i, 0, 0))],
        out_shape=[jax.ShapeDtypeStruct((n_sub, N_EXPERTS, ts), F32),
                   jax.ShapeDtypeStruct((n_sub, N_EXPERTS, LANES), jnp.int32)],
        compiler_params=_params(("parallel",)),
        name="gate",
    )(x, norm_ffn, rwt, rb)


def _segment_copies(i, n_bits, local_ref, off_ref, far_ref, far_rows_ref, nseg_ref, sem, to_far):
    copies = []
    for e in range(N_EXPERTS):
        j = i * N_EXPERTS + e
        n = nseg_ref[j]
        for b in reversed(range(n_bits)):
            size = SEG_ALIGN << b
            done = ((n >> (b + 1)) << (b + 1)) * SEG_ALIGN
            near = local_ref.at[pl.ds(pl.multiple_of(off_ref[j] + done, SEG_ALIGN), size)]
            far = far_ref.at[pl.ds(pl.multiple_of(far_rows_ref[j] + done, SEG_ALIGN), size)]
            cp = pltpu.make_async_copy(near, far, sem) if to_far else pltpu.make_async_copy(far, near, sem)
            copies.append((((n >> b) & 1) == 1, cp))
    return copies


def _scatter_kernel(off_ref, far_rows_ref, nseg_ref, x_ref, g_ref, route_ref, tri_ref, xb_in_ref,
                    meta_ref, xb_ref, xs_ref, sem, *, ts, cap, n_bits):
    del xb_in_ref
    i = pl.program_id(0)
    xn = _rms(x_ref[...], g_ref[...]).astype(BF16)
    row = lax.broadcasted_iota(jnp.int32, (N_EXPERTS, ts), 0).astype(F32)
    oh1 = jnp.where(row == route_ref[0:1, :], 1.0, 0.0)
    oh2 = jnp.where(row == route_ref[1:2, :], 1.0, 0.0)
    cnt1, _, off = _segment_sizes(oh1, oh2)
    pre1 = _dot(oh1.astype(BF16), tri_ref[...])
    pre2 = _dot(oh2.astype(BF16), tri_ref[...])
    d1 = jnp.sum(oh1 * (off + pre1), axis=0, keepdims=True)
    d2 = jnp.sum(oh2 * (off + cnt1 + pre2), axis=0, keepdims=True)
    slot = lax.broadcasted_iota(jnp.int32, (cap, ts), 0).astype(F32)
    p = jnp.where(slot == d1, 1.0, jnp.where(slot == d2, 1.0, 0.0)).astype(BF16)
    xs_ref[...] = _dot(p, xn).astype(BF16)
    meta_t = jnp.concatenate([d1, d2, route_ref[2:4, :], jnp.zeros((LANES - 4, ts), F32)], axis=0)
    meta_ref[...] = meta_t.T
    copies = _segment_copies(i, n_bits, xs_ref, off_ref, xb_ref, far_rows_ref, nseg_ref, sem, True)
    for pred, cp in copies:
        pl.when(pred)(cp.start)
    for pred, cp in copies:
        pl.when(pred)(cp.wait)


def _scatter(off, far_rows, nseg, x, norm_ffn, route, xb, layer, ts):
    n = x.shape[0]
    n_sub = n // ts
    cap = _sorted_cap(ts)
    tri = jnp.asarray(np.triu(np.ones((ts, ts), np.float32), 1), BF16)
    n_bits = int(TOP_K * ts // SEG_ALIGN).bit_length()
    return pl.pallas_call(
        functools.partial(_scatter_kernel, ts=ts, cap=cap, n_bits=n_bits),
        grid_spec=pltpu.PrefetchScalarGridSpec(
            num_scalar_prefetch=3,
            grid=(n_sub,),
            in_specs=[pl.BlockSpec((ts, D_MODEL), lambda i, *_: (i, 0)),
                      _resident((None, 1, D_MODEL), lambda i, *_: (layer, 0, 0)),
                      pl.BlockSpec((None, N_EXPERTS, ts), lambda i, *_: (i, 0, 0)),
                      _resident((ts, ts), lambda i, *_: (0, 0)),
                      pl.BlockSpec(memory_space=pl.ANY)],
            out_specs=[pl.BlockSpec((ts, LANES), lambda i, *_: (i, 0)),
                       pl.BlockSpec(memory_space=pl.ANY)],
            scratch_shapes=[pltpu.VMEM((cap, D_MODEL), BF16), pltpu.SemaphoreType.DMA(())]),
        out_shape=[jax.ShapeDtypeStruct((n, LANES), F32), jax.ShapeDtypeStruct(xb.shape, xb.dtype)],
        input_output_aliases={7: 1},
        compiler_params=_params(("arbitrary",)),
        name="scatter",
    )(off, far_rows, nseg, x, norm_ffn, route, tri, xb)


def _experts_kernel(blk_ref, nreal_ref, xb_ref, wg_ref, wu_ref, wd_ref, yb_ref, acc_ref):
    del blk_ref
    i = pl.program_id(0)
    f = pl.program_id(1)

    @pl.when(f == 0)
    def _():
        acc_ref[...] = jnp.zeros_like(acc_ref)

    @pl.when(i < nreal_ref[0])
    def _():
        _swiglu_step(xb_ref[...], wg_ref, wu_ref, wd_ref, acc_ref)

    @pl.when(f == pl.num_programs(1) - 1)
    def _():
        yb_ref[...] = acc_ref[...].astype(yb_ref.dtype)


def _experts(blk_e, nreal, xb, wg, wu, wd, idx, tf):
    n_blocks = blk_e.shape[0]
    d_ff = wg.shape[-1]
    n_f = d_ff // tf

    def live_block(i, nreal):
        return jnp.minimum(i, nreal[0] - 1)

    def wcol(i, f, blk, nreal):
        return idx, blk[live_block(i, nreal)], 0, jnp.where(i < nreal[0], f, n_f - 1)

    def wrow(i, f, blk, nreal):
        return idx, blk[live_block(i, nreal)], jnp.where(i < nreal[0], f, n_f - 1), 0

    return pl.pallas_call(
        _experts_kernel,
        grid_spec=pltpu.PrefetchScalarGridSpec(
            num_scalar_prefetch=2,
            grid=(n_blocks, n_f),
            in_specs=[pl.BlockSpec((MOE_ROWS, D_MODEL), lambda i, f, blk, nreal: (live_block(i, nreal), 0)),
                      pl.BlockSpec((None, None, D_MODEL, tf), wcol),
                      pl.BlockSpec((None, None, D_MODEL, tf), wcol),
                      pl.BlockSpec((None, None, tf, D_MODEL), wrow)],
            out_specs=pl.BlockSpec((MOE_ROWS, D_MODEL), lambda i, f, blk, nreal: (i, 0)),
            scratch_shapes=[pltpu.VMEM((MOE_ROWS, D_MODEL), F32)]),
        out_shape=jax.ShapeDtypeStruct((n_blocks * MOE_ROWS, D_MODEL), BF16),
        compiler_params=_params(("parallel", "arbitrary")),
        name="experts",
    )(blk_e, nreal, xb, wg, wu, wd)


def _combine_kernel(off_ref, far_rows_ref, nseg_ref, x_ref, meta_ref, yb_ref, o_ref, ys_ref, sem, *, ts, cap, n_bits):
    i = pl.program_id(0)
    ys_ref[...] = jnp.zeros_like(ys_ref)
    copies = _segment_copies(i, n_bits, ys_ref, off_ref, yb_ref, far_rows_ref, nseg_ref, sem, False)
    for pred, cp in copies:
        pl.when(pred)(cp.start)
    for pred, cp in copies:
        pl.when(pred)(cp.wait)
    slot = lax.broadcasted_iota(jnp.int32, (ts, cap), 1).astype(F32)
    ys = ys_ref[...]
    q1 = jnp.where(slot == meta_ref[:, 0:1], 1.0, 0.0).astype(BF16)
    q2 = jnp.where(slot == meta_ref[:, 1:2], 1.0, 0.0).astype(BF16)
    o_ref[...] = x_ref[...] + meta_ref[:, 2:3] * _dot(q1, ys) + meta_ref[:, 3:4] * _dot(q2, ys)


def _combine(off, far_rows, nseg, x, meta, yb, ts):
    n = x.shape[0]
    cap = _sorted_cap(ts)
    n_bits = int(TOP_K * ts // SEG_ALIGN).bit_length()
    return pl.pallas_call(
        functools.partial(_combine_kernel, ts=ts, cap=cap, n_bits=n_bits),
        grid_spec=pltpu.PrefetchScalarGridSpec(
            num_scalar_prefetch=3,
            grid=(n // ts,),
            in_specs=[pl.BlockSpec((ts, D_MODEL), lambda i, *_: (i, 0)),
                      pl.BlockSpec((ts, LANES), lambda i, *_: (i, 0)),
                      pl.BlockSpec(memory_space=pl.ANY)],
            out_specs=pl.BlockSpec((ts, D_MODEL), lambda i, *_: (i, 0)),
            scratch_shapes=[pltpu.VMEM((cap, D_MODEL), BF16), pltpu.SemaphoreType.DMA(())]),
        out_shape=jax.ShapeDtypeStruct((n, D_MODEL), F32),
        compiler_params=_params(("arbitrary",)),
        name="combine",
    )(off, far_rows, nseg, x, meta, yb)


def _moe(xs, tss, norm_ffn, rwt, rb, wg, wu, wd, layer, idx, tf):
    gated = [_gate(x, norm_ffn, rwt, rb, layer, idx, ts) for x, ts in zip(xs, tss)]
    off = jnp.concatenate([g[1][:, :, 0] for g in gated], axis=0)
    cpad = jnp.concatenate([g[1][:, :, 1] for g in gated], axis=0)
    n_subs = [g[1].shape[0] for g in gated]
    tot = jnp.sum(cpad, axis=0)
    padded = (tot + MOE_ROWS - 1) // MOE_ROWS * MOE_ROWS
    pend = jnp.cumsum(padded)
    within = jnp.cumsum(cpad, axis=0) - cpad
    off = off.astype(jnp.int32).reshape(-1)
    grouped_rows = ((pend - padded)[None, :] + within).astype(jnp.int32).reshape(-1)
    nseg = (cpad // SEG_ALIGN).astype(jnp.int32).reshape(-1)
    n_assign = TOP_K * sum(x.shape[0] for x in xs)
    n_blocks = (n_assign + (SEG_ALIGN - 1) * sum(n_subs) * N_EXPERTS + N_EXPERTS * (MOE_ROWS - 1)) // MOE_ROWS
    blk_e = jnp.minimum(jnp.sum(jnp.arange(n_blocks)[:, None] * MOE_ROWS >= pend[None, :], axis=1),
                        N_EXPERTS - 1).astype(jnp.int32)
    nreal = (pend[-1:] // MOE_ROWS).astype(jnp.int32)

    xb = jnp.zeros((n_blocks * MOE_ROWS, D_MODEL), BF16)
    metas, parts = [], []
    lo = 0
    for x, g, ns, ts in zip(xs, gated, n_subs, tss):
        sl = slice(lo * N_EXPERTS, (lo + ns) * N_EXPERTS)
        parts.append((off[sl], grouped_rows[sl], nseg[sl]))
        meta, xb = _scatter(*parts[-1], x, norm_ffn, g[0], xb, layer, ts)
        metas.append(meta)
        lo += ns
    yb = _experts(blk_e, nreal, xb, wg, wu, wd, idx, tf)
    return [_combine(*part, x, meta, yb, ts) for part, x, meta, ts in zip(parts, xs, metas, tss)]


def kernel(x_prompt, x_sample, state_pool, cache_kv_g0, cache_kv_g1, cache_kv_g2, norm_attn, w_in, q_norm, k_norm, w_pool, pool_scale, w_branch_pool, w_branch_attn, w_out, norm_ffn, w1_dense, w3_dense, w2_dense, router_w, router_b, we_gate, we_up, we_down):
    batch, seq, _ = x_prompt.shape
    dec_batch, dec_seq, _ = x_sample.shape
    depth = w_in.shape[0]
    caches = (cache_kv_g0, cache_kv_g1, cache_kv_g2)
    n_p, n_s = batch * seq, dec_batch * dec_seq
    tm_p = 512
    dils = tuple(d for _, d in ATT_GROUPS)
    keeps = tuple(min(win, seq) for win, _ in ATT_GROUPS)

    head_of = np.arange(GROUP_W) // HEAD_DIM
    same_head = (head_of[:, None] == head_of[None, :]).astype(np.float32)
    seg_mean = jnp.asarray(same_head / HEAD_DIM, BF16)
    w_qkv = w_in[:, :, :PROJ_W].astype(BF16)
    qgain = (jnp.tile(q_norm, (1, HEADS)) * (HEAD_DIM ** -0.5)).reshape(depth, 1, GROUP_W)
    kgain = jnp.tile(k_norm, (1, HEADS)).reshape(depth, 1, GROUP_W)
    norm_attn3 = norm_attn.reshape(depth, 1, D_MODEL)
    w = {"w_pool": w_pool.astype(BF16), "pool_scale": pool_scale.reshape(depth, 1, POOL_W),
         "norm_attn": norm_attn3, "w_gate": w_in[:, :, PROJ_W:].astype(BF16),
         "w_bp": w_branch_pool.astype(BF16), "w_ba": w_branch_attn.astype(BF16), "w_o": w_out.astype(BF16),
         "norm_ffn": norm_ffn.reshape(depth, 1, D_MODEL)}
    w1_b, w3_b, w2_b = w1_dense.astype(BF16), w3_dense.astype(BF16), w2_dense.astype(BF16)
    wg_b, wu_b, wd_b = we_gate.astype(BF16), we_up.astype(BF16), we_down.astype(BF16)
    rwt = jnp.swapaxes(router_w, 1, 2).astype(BF16)
    rb = router_b.reshape(-1, N_EXPERTS, 1)

    xp = x_prompt.reshape(n_p, D_MODEL)
    xs = x_sample.reshape(n_s, D_MODEL)
    pool_p, pool_s = [], []
    kv_s = [[] for _ in ATT_GROUPS]
    kvo_p = tuple(jnp.zeros((depth, batch, 2, GROUP_W, keep), F32) for keep in keeps)
    ones = (1,) * N_GROUPS
    for layer in range(depth):
        res = _proj(xp, layer, norm_attn3, w_qkv, qgain, kgain, seg_mean, batch, seq, tm_p, dils, keeps,
                    layer, depth, kvo_p, 1, True)
        u, qs, kcs, kvo_p = res[0], res[1:4], res[4:7], tuple(res[7:10])
        outs, lses = [], []
        for gi in range(N_GROUPS):
            o, lse = _attention(qs[gi], kcs[gi], gi)
            outs.append(o)
            lses.append(lse)
        pool_p.append(u.reshape(batch, seq, POOL_W)[:, seq - POOL_STATE:])
        xp, xpn = _merge(xp, u, outs, lses, layer, w, batch, seq, tm_p, dils)

        res = _proj(xs, layer, norm_attn3, w_qkv, qgain, kgain, seg_mean, 1, n_s, n_s, ones, (n_s,) * N_GROUPS,
                    0, 1, (), 3, False)
        u = res[0]
        qs = [(q.reshape(dec_batch, dec_seq, 1, GROUP_W) * _head_mask().astype(BF16))
              .reshape(dec_batch, dec_seq * HEADS, GROUP_W) for q in res[1:4]]
        kv_new = [kv.reshape(dec_batch, dec_seq, 2, HEADS, HEAD_DIM) for kv in res[7:10]]
        tokens_to_lanes = ((0, 0),) * 4 + ((0, LANES - dec_seq),)
        kvn = [jnp.pad(kv.transpose(0, 2, 3, 4, 1), tokens_to_lanes) for kv in kv_new]
        pool_o, att_o, new_pool = _sample_mix(state_pool, u, qs, kvn, caches, layer, w, dec_batch, dec_seq)
        att_o = att_o.reshape(n_s, GROUP_W)
        for gi in range(N_GROUPS):
            kv_s[gi].append(kv_new[gi])
        pool_s.append(new_pool)
        xs, xsn = _tail(xs, pool_o.reshape(n_s, POOL_W), att_o, layer, w)

        i = layer // 2
        if layer % 2 == 0:
            xp = _ffn(xp, xpn, w1_b, w3_b, w2_b, i, 512, w1_b.shape[-1] // 2)
            xs = _ffn(xs, xsn, w1_b, w3_b, w2_b, i, n_s, w1_b.shape[-1] // 2)
        else:
            xp, xs = _moe([xp, xs], [512, n_s], w["norm_ffn"], rwt, rb, wg_b, wu_b, wd_b, layer, i,
                           wg_b.shape[-1] // 2)
    kv_p = [kvo_p[g].reshape(depth, batch, 2, HEADS, HEAD_DIM, keeps[g]).transpose(0, 1, 5, 2, 3, 4)
            for g in range(N_GROUPS)]
    return (xp.reshape(batch, seq, D_MODEL), xs.reshape(dec_batch, dec_seq, D_MODEL),
            jnp.stack(pool_p), jnp.stack(pool_s),
            kv_p[0], jnp.stack(kv_s[0]),
            kv_p[1], jnp.stack(kv_s[1]),
            kv_p[2], jnp.stack(kv_s[2]))
```

```python
import functools

import numpy as np
import jax
import jax.numpy as jnp
from jax import lax
from jax.experimental import pallas as pl
from jax.experimental.pallas import tpu as pltpu

F32 = jnp.float32
BF16 = jnp.bfloat16

D_MODEL = 1024
PAST_LEN = 16384
POOL_WINDOWS = (2, 4, 8, 16)
POOL_GROUP = 128
POOL_W = 512
POOL_STATE = 15
ATT_GROUPS = ((128, 1), (512, 4), (2048, 16))
N_GROUPS = len(ATT_GROUPS)
HEAD_DIM = 64
HEADS = 8
GROUP_W = 512
QKV_W = 1536
Q_BLOCK = 128
ALIBI_MAX = 8.0
N_EXPERTS = 8
TOP_K = 2
RMS_EPS = 1e-6
NEG_INF = -1e30
LANES = 128
LANE_CHUNKS = GROUP_W // LANES
PROJ_W = POOL_W + 3 * QKV_W
VMEM_LIMIT = 56 * 1024 * 1024


def _slopes():
    i = np.arange(1, N_GROUPS * HEADS + 1, dtype=np.float32)
    return np.exp2(-ALIBI_MAX * i / (N_GROUPS * HEADS)).astype(np.float32).reshape(N_GROUPS, HEADS)


def _params(sem):
    return pltpu.CompilerParams(dimension_semantics=sem, vmem_limit_bytes=VMEM_LIMIT)


def _rms(x, gain):
    return x * lax.rsqrt(jnp.mean(x * x, axis=-1, keepdims=True) + RMS_EPS) * gain


def _dot(a, b):
    return jnp.dot(a, b, preferred_element_type=F32)


def _resident(shape, index_map):
    return pl.BlockSpec(shape, index_map, pipeline_mode=pl.Buffered(1))


def _write_classes(dst_ref, col0, val, dil, tmp_ref, slot):
    rows = val.shape[0] // dil
    cols = slice(col0, col0 + GROUP_W)
    if dil == 1:
        dst_ref[0, :, cols] = val.astype(dst_ref.dtype)
        return
    for c in range(LANE_CHUNKS):
        tmp_ref[slot, c] = val[:, c * LANES:(c + 1) * LANES]
    for r in range(dil):
        picked = [tmp_ref[slot, c, pl.ds(r, rows, stride=dil), :] for c in range(LANE_CHUNKS)]
        dst_ref[r, :, cols] = jnp.concatenate(picked, axis=1).astype(dst_ref.dtype)


def _proj_kernel(x_ref, g_ref, w_ref, qg_ref, kg_ref, seg_ref, *rest, tm, dils, keeps, n_alias, norm_terms,
                 kv_positions_minor):
    u_ref, q0_ref, q1_ref, q2_ref, kc0_ref, kc1_ref, kc2_ref, kvo0_ref, kvo1_ref, kvo2_ref, tmp_ref = rest[n_alias:]
    s = pl.program_id(1)
    n_tiles = pl.num_programs(1)
    xn = _rms(x_ref[...], g_ref[...]).astype(BF16)

    def zblk(j):
        return _dot(xn, w_ref[:, j * GROUP_W:(j + 1) * GROUP_W])

    def headnorm(z, gain):
        rem = z * z
        ms = None
        for _ in range(norm_terms):
            part = rem.astype(BF16)
            rem = rem - part.astype(F32)
            ms = _dot(part, seg_ref[...]) if ms is None else ms + _dot(part, seg_ref[...])
        return z * lax.rsqrt(ms + RMS_EPS) * gain

    u_ref[...] = zblk(0)
    q_refs = (q0_ref, q1_ref, q2_ref)
    kc_refs = (kc0_ref, kc1_ref, kc2_ref)
    kvo_refs = (kvo0_ref, kvo1_ref, kvo2_ref)
    slot = 0
    for g in range(N_GROUPS):
        q = headnorm(zblk(1 + g), qg_ref[...])
        k = headnorm(zblk(1 + N_GROUPS + g), kg_ref[...])
        v = zblk(1 + 2 * N_GROUPS + g)
        for dst, col0, val in ((q_refs[g], 0, q), (kc_refs[g], 0, k), (kc_refs[g], GROUP_W, v)):
            _write_classes(dst, col0, val, dils[g], tmp_ref, slot % tmp_ref.shape[0])
            slot += dils[g] > 1
        keep = keeps[g]
        first = n_tiles - max(keep // tm, 1)
        rows = min(keep, tm)

        @pl.when(s >= first)
        def _(k=k, v=v, ref=kvo_refs[g], rows=rows):
            if kv_positions_minor:
                ref[0] = k[tm - rows:, :].T
                ref[1] = v[tm - rows:, :].T
            else:
                ref[:, 0:GROUP_W] = k[tm - rows:, :]
                ref[:, GROUP_W:2 * GROUP_W] = v[tm - rows:, :]


def _proj(x, layer, norm_attn, w_qkv, qgain, kgain, seg, batch, seq, tm, dils, keeps, out_layer, out_depth, prev_kvo,
          norm_terms, kv_positions_minor):
    n_tiles = seq // tm
    kvo_specs, kvo_shapes = [], []
    for g in range(N_GROUPS):
        keep = keeps[g]
        assert keep % tm == 0 or (keep < tm and keep % LANES == 0)
        first = n_tiles - max(keep // tm, 1)
        rows = min(keep, tm)
        if kv_positions_minor:
            kvo_specs.append(pl.BlockSpec((None, None, 2, GROUP_W, rows),
                                          lambda b, s, first=first: (out_layer, b, 0, 0, jnp.maximum(s - first, 0))))
            kvo_shapes.append(jax.ShapeDtypeStruct((out_depth, batch, 2, GROUP_W, keep), F32))
        else:
            kvo_specs.append(pl.BlockSpec((None, None, rows, 2 * GROUP_W),
                                          lambda b, s, first=first: (out_layer, b, jnp.maximum(s - first, 0), 0)))
            kvo_shapes.append(jax.ShapeDtypeStruct((out_depth, batch, keep, 2 * GROUP_W), F32))
    cls_spec = lambda g, width: pl.BlockSpec((None, dils[g], tm // dils[g], width), lambda b, s: (b, 0, s, 0))
    cls_shape = lambda g, width: jax.ShapeDtypeStruct((batch, dils[g], seq // dils[g], width), BF16)
    n_alias = len(prev_kvo)
    n_in = 6
    return pl.pallas_call(
        functools.partial(_proj_kernel, tm=tm, dils=dils, keeps=keeps, n_alias=n_alias,
                          norm_terms=norm_terms, kv_positions_minor=kv_positions_minor),
        grid=(batch, n_tiles),
        in_specs=[
            pl.BlockSpec((tm, D_MODEL), lambda b, s: (b * n_tiles + s, 0)),
            _resident((None, 1, D_MODEL), lambda b, s: (layer, 0, 0)),
            _resident((None, D_MODEL, PROJ_W), lambda b, s: (layer, 0, 0)),
            _resident((None, 1, GROUP_W), lambda b, s: (layer, 0, 0)),
            _resident((None, 1, GROUP_W), lambda b, s: (layer, 0, 0)),
            _resident((GROUP_W, GROUP_W), lambda b, s: (0, 0)),
        ] + [pl.BlockSpec(memory_space=pl.ANY)] * n_alias,
        out_specs=[pl.BlockSpec((tm, POOL_W), lambda b, s: (b * n_tiles + s, 0))]
                  + [cls_spec(g, GROUP_W) for g in range(N_GROUPS)]
                  + [cls_spec(g, 2 * GROUP_W) for g in range(N_GROUPS)] + kvo_specs,
        out_shape=[jax.ShapeDtypeStruct((batch * seq, POOL_W), F32)]
                  + [cls_shape(g, GROUP_W) for g in range(N_GROUPS)]
                  + [cls_shape(g, 2 * GROUP_W) for g in range(N_GROUPS)] + kvo_shapes,
        scratch_shapes=[pltpu.VMEM((3, LANE_CHUNKS, tm, LANES), F32)],
        input_output_aliases={n_in + g: 1 + 2 * N_GROUPS + g for g in range(n_alias)},
        compiler_params=_params(("parallel", "arbitrary")),
        name="proj",
    )(x, norm_attn, w_qkv, qgain, kgain, seg, *prev_kvo)


def _attn_kernel(q_ref, kvc_ref, *rest, slope_dil, tc, has_halo):
    if has_halo:
        kvh_ref, o_ref, lse_ref, kbuf, vbuf = rest
    else:
        o_ref, lse_ref, kbuf, vbuf = rest
    off = Q_BLOCK if has_halo else 0
    if has_halo:
        kbuf[0:Q_BLOCK, :] = kvh_ref[:, 0:GROUP_W]
        vbuf[0:Q_BLOCK, :] = kvh_ref[:, GROUP_W:2 * GROUP_W]
    kbuf[off:off + tc, :] = kvc_ref[:, 0:GROUP_W]
    vbuf[off:off + tc, :] = kvc_ref[:, GROUP_W:2 * GROUP_W]

    nk = Q_BLOCK + off
    kj = lax.broadcasted_iota(jnp.int32, (nk, Q_BLOCK), 0)
    qi = lax.broadcasted_iota(jnp.int32, (nk, Q_BLOCK), 1)
    dist = qi - kj + off
    distf = dist.astype(F32)
    maskneg = jnp.where((dist >= 0) & (dist <= Q_BLOCK), 0.0, NEG_INF).astype(F32)
    if has_halo:
        first = jnp.where(pl.program_id(2) == 0, NEG_INF, 0.0).astype(F32)
        mask_first = maskneg + jnp.where(kj < Q_BLOCK, first, 0.0)

    for i in range(tc // Q_BLOCK):
        mask = mask_first if (has_halo and i == 0) else maskneg
        rows = slice(i * Q_BLOCK, (i + 1) * Q_BLOCK)
        krows = slice(i * Q_BLOCK, i * Q_BLOCK + nk)
        for hp in range(HEADS // 2):
            outs, lses = [], []
            for h in (2 * hp, 2 * hp + 1):
                cols = slice(h * HEAD_DIM, (h + 1) * HEAD_DIM)
                s = lax.dot_general(kbuf[krows, cols], q_ref[rows, cols],
                                    (((1,), (1,)), ((), ())), preferred_element_type=F32)
                s = s - slope_dil[h] * distf + mask
                m = jnp.max(s, axis=0, keepdims=True)
                p = jnp.exp(s - m)
                l = jnp.sum(p, axis=0, keepdims=True)
                outs.append(lax.dot_general(vbuf[krows, cols], p.astype(BF16), (((0,), (0,)), ((), ())),
                                            preferred_element_type=F32) / l)
                lses.append(jnp.broadcast_to(m + jnp.log(l), (HEAD_DIM, Q_BLOCK)))
            pc = slice(hp * 2 * HEAD_DIM, (hp + 1) * 2 * HEAD_DIM)
            o_ref[rows, pc] = jnp.concatenate(outs, axis=0).T.astype(BF16)
            lse_ref[rows, pc] = jnp.concatenate(lses, axis=0).T


def _attention(q, kc, gi):
    win, dil = ATT_GROUPS[gi]
    assert win // dil == Q_BLOCK
    batch, _, l, _ = q.shape
    out_dims = (batch, dil, l, GROUP_W)
    has_halo = l > Q_BLOCK
    slope_dil = tuple(float(s) * dil for s in _slopes()[gi])
    if not has_halo:
        q, kc = q.reshape(batch, 1, dil * l, GROUP_W), kc.reshape(batch, 1, dil * l, 2 * GROUP_W)
        dil, l = 1, dil * l
    tc = min(l, 512)
    in_specs = [
        pl.BlockSpec((None, None, tc, GROUP_W), lambda b, r, c: (b, r, c, 0)),
        pl.BlockSpec((None, None, tc, 2 * GROUP_W), lambda b, r, c: (b, r, c, 0)),
    ]
    args = [q, kc]
    if has_halo:
        in_specs.append(pl.BlockSpec((None, None, Q_BLOCK, 2 * GROUP_W),
                                     lambda b, r, c: (b, r, jnp.maximum(c * (tc // Q_BLOCK) - 1, 0), 0)))
        args.append(kc)
    nk = tc + (Q_BLOCK if has_halo else 0)
    o, lse = pl.pallas_call(
        functools.partial(_attn_kernel, slope_dil=slope_dil, tc=tc, has_halo=has_halo),
        grid=(batch, dil, l // tc),
        in_specs=in_specs,
        out_specs=[pl.BlockSpec((None, None, tc, GROUP_W), lambda b, r, c: (b, r, c, 0))] * 2,
        out_shape=[jax.ShapeDtypeStruct((batch, dil, l, GROUP_W), BF16),
                   jax.ShapeDtypeStruct((batch, dil, l, GROUP_W), F32)],
        scratch_shapes=[pltpu.VMEM((nk, GROUP_W), BF16), pltpu.VMEM((nk, GROUP_W), BF16)],
        compiler_params=_params(("parallel", "parallel", "arbitrary")),
        name=f"attn_g{gi}",
    )(*args)
    return o.reshape(out_dims), lse.reshape(out_dims)


def _mix_tail(x, pool_o, att_o, nattn_ref, wgate_ref, wbp_ref, wba_ref, wo_ref, nffn_ref, xo_ref, xn_ref):
    gates = jax.nn.sigmoid(_dot(_rms(x, nattn_ref[...]).astype(BF16), wgate_ref[...]))
    hp = _dot(pool_o.astype(BF16), wbp_ref[...])
    ha = _dot(att_o.astype(BF16), wba_ref[...])
    t = gates[:, 0:D_MODEL] * hp + gates[:, D_MODEL:2 * D_MODEL] * ha
    xo = x + _dot(t.astype(BF16), wo_ref[...])
    xo_ref[...] = xo
    xn_ref[...] = _rms(xo, nffn_ref[...]).astype(BF16)


def _group_linear(m, wpool_ref, pscale_ref):
    parts = [_dot(m[:, gi * POOL_GROUP:(gi + 1) * POOL_GROUP].astype(BF16), wpool_ref[gi])
             for gi in range(len(POOL_WINDOWS))]
    return jnp.concatenate(parts, axis=1) * pscale_ref[...]


def _read_classes(src_ref, dil, il_ref, slot):
    if dil == 1:
        return src_ref[0].astype(F32)
    rows = src_ref.shape[1]
    for r in range(dil):
        v = src_ref[r].astype(F32)
        for c in range(LANE_CHUNKS):
            il_ref[slot, c, pl.ds(r, rows, stride=dil), :] = v[:, c * LANES:(c + 1) * LANES]
    return jnp.concatenate([il_ref[slot, c] for c in range(LANE_CHUNKS)], axis=1)


def _merge_kernel(x_ref, u_ref, uh_ref, o0_ref, o1_ref, o2_ref, l0_ref, l1_ref, l2_ref,
                  wpool_ref, pscale_ref, nattn_ref, wgate_ref, wbp_ref, wba_ref, wo_ref, nffn_ref,
                  xo_ref, xn_ref, ext_ref, il_ref, *, tm, dils):
    si = pl.program_id(1)
    halo = POOL_STATE + 1
    ext_ref[0:halo, :] = jnp.where(si == 0, 0.0, uh_ref[...])
    ext_ref[halo:halo + tm, :] = u_ref[...]
    pos = si * tm + lax.broadcasted_iota(jnp.int32, (tm, 1), 0)
    parts = []
    for gi, win in enumerate(POOL_WINDOWS):
        cols = slice(gi * POOL_GROUP, (gi + 1) * POOL_GROUP)
        own = ext_ref[halo:halo + tm, cols]
        acc = own
        for back in range(1, win):
            acc = acc + ext_ref[halo - back:halo - back + tm, cols]
        inv = 1.0 / jnp.minimum(pos + 1, win).astype(F32)
        parts.append(acc * inv - own)
    pool_o = _group_linear(jnp.concatenate(parts, axis=1), wpool_ref, pscale_ref)

    slot = 0
    os_, ls_ = [], []
    for g, (o_ref, l_ref) in enumerate(((o0_ref, l0_ref), (o1_ref, l1_ref), (o2_ref, l2_ref))):
        os_.append(_read_classes(o_ref, dils[g], il_ref, slot))
        slot += dils[g] > 1
        ls_.append(_read_classes(l_ref, dils[g], il_ref, slot))
        slot += dils[g] > 1
    mx = jnp.maximum(ls_[0], jnp.maximum(ls_[1], ls_[2]))
    es = [jnp.exp(v - mx) for v in ls_]
    att_o = (es[0] * os_[0] + es[1] * os_[1] + es[2] * os_[2]) / (es[0] + es[1] + es[2])
    _mix_tail(x_ref[...], pool_o, att_o, nattn_ref, wgate_ref, wbp_ref, wba_ref, wo_ref, nffn_ref, xo_ref, xn_ref)


def _wspec(shape, layer, grid_rank):
    nd = len(shape)
    return _resident((None,) + tuple(shape), lambda *_: (layer,) + (0,) * nd)


def _tail_weight_specs(layer, grid_rank):
    return [_wspec((1, D_MODEL), layer, grid_rank), _wspec((D_MODEL, 2 * D_MODEL), layer, grid_rank),
            _wspec((POOL_W, D_MODEL), layer, grid_rank), _wspec((GROUP_W, D_MODEL), layer, grid_rank),
            _wspec((D_MODEL, D_MODEL), layer, grid_rank), _wspec((1, D_MODEL), layer, grid_rank)]


def _tail_weights(w):
    return (w["norm_attn"], w["w_gate"], w["w_bp"], w["w_ba"], w["w_o"], w["norm_ffn"])


def _merge(x, u, outs, lses, layer, w, batch, seq, tm, dils):
    n = x.shape[0]
    n_tiles = seq // tm
    halo = POOL_STATE + 1
    row = lambda width: pl.BlockSpec((tm, width), lambda b, s: (b * n_tiles + s, 0))
    cls = lambda g: pl.BlockSpec((None, dils[g], tm // dils[g], GROUP_W), lambda b, s: (b, 0, s, 0))
    n_il = 2 * sum(d > 1 for d in dils)
    return pl.pallas_call(
        functools.partial(_merge_kernel, tm=tm, dils=dils),
        grid=(batch, n_tiles),
        in_specs=[row(D_MODEL), row(POOL_W),
                  pl.BlockSpec((halo, POOL_W),
                               lambda b, s: (jnp.maximum((b * n_tiles + s) * (tm // halo) - 1, 0), 0))]
                 + [cls(g) for g in range(N_GROUPS)] * 2
                 + [_wspec((len(POOL_WINDOWS), POOL_GROUP, POOL_GROUP), layer, 2), _wspec((1, POOL_W), layer, 2)]
                 + _tail_weight_specs(layer, 2),
        out_specs=[row(D_MODEL), row(D_MODEL)],
        out_shape=[jax.ShapeDtypeStruct((n, D_MODEL), F32), jax.ShapeDtypeStruct((n, D_MODEL), BF16)],
        scratch_shapes=[pltpu.VMEM((halo + tm, POOL_W), F32), pltpu.VMEM((n_il, LANE_CHUNKS, tm, LANES), F32)],
        compiler_params=_params(("parallel", "parallel")),
        name="merge",
    )(x, u, u, *outs, *lses, w["w_pool"], w["pool_scale"], *_tail_weights(w))


def _head_mask():
    return jnp.asarray(np.arange(GROUP_W)[None, :] // HEAD_DIM == np.arange(HEADS)[:, None], F32)


def _sample_keys(dec_seq, cache_rows):
    slopes = _slopes()
    geo = []
    for gi, (win, dil) in enumerate(ATT_GROUPS):
        lc = cache_rows[gi]
        assert lc % dil == 0 and lc // dil == LANES
        cls = (lc + np.arange(dec_seq)) % dil
        n_cached = (cls.max() + 1) * LANES
        lane = np.arange(n_cached + LANES)
        sel = (np.arange(lc)[:, None] == (lane[:n_cached] % LANES) * dil + lane[:n_cached] // LANES)
        t = np.arange(dec_seq)[:, None]
        back = np.where(lane < n_cached, lc + t - ((lane % LANES) * dil + lane // LANES), t - (lane - n_cached))
        valid = (back >= 0) & (back % dil == 0) & (back // dil <= win // dil) & (lane < n_cached + dec_seq)
        bias = np.where(valid[:, None, :], -slopes[gi][None, :, None] * back[:, None, :], NEG_INF)
        geo.append((jnp.asarray(sel, BF16), jnp.asarray(bias.reshape(dec_seq * HEADS, -1), F32)))
    return geo


def _sample_mix_kernel(state_ref, u_ref, q0_ref, q1_ref, q2_ref, kn0_ref, kn1_ref, kn2_ref, c0_ref, c1_ref, c2_ref,
                       sel0_ref, sel1_ref, sel2_ref, bias0_ref, bias1_ref, bias2_ref, headmask_ref,
                       wpool_ref, pscale_ref, pool_ref, att_ref, newpool_ref, ext_ref, m_ref, *, dec_seq):
    ext_ref[...] = jnp.zeros_like(ext_ref)
    ext_ref[0:POOL_STATE, :] = state_ref[...]
    ext_ref[POOL_STATE:POOL_STATE + dec_seq, :] = u_ref[...]
    newpool_ref[...] = ext_ref[dec_seq:dec_seq + POOL_STATE, :]

    m_ref[...] = jnp.zeros_like(m_ref)
    for t in range(dec_seq):
        row = POOL_STATE + t
        for gi, win in enumerate(POOL_WINDOWS):
            cols = slice(gi * POOL_GROUP, (gi + 1) * POOL_GROUP)
            tot = jnp.sum(ext_ref[row - win + 1:row + 1, cols], axis=0, keepdims=True)
            cnt = float(min(PAST_LEN + t + 1, win))
            m_ref[t:t + 1, cols] = tot / cnt - ext_ref[row:row + 1, cols]
    pool_ref[...] = _group_linear(m_ref[...], wpool_ref, pscale_ref)[0:dec_seq]

    groups = ((q0_ref, kn0_ref, c0_ref, sel0_ref, bias0_ref), (q1_ref, kn1_ref, c1_ref, sel1_ref, bias1_ref),
              (q2_ref, kn2_ref, c2_ref, sel2_ref, bias2_ref))
    k_rows = HEADS * HEAD_DIM
    outs, lses = [], []
    for q_ref, kn_ref, c_ref, sel_ref, bias_ref in groups:
        flat = c_ref[...].reshape(2 * k_rows, c_ref.shape[-1]).astype(BF16)
        cached = _dot(flat, sel_ref[...]).astype(BF16)
        new = kn_ref[...].reshape(2 * k_rows, LANES).astype(BF16)
        keys = jnp.concatenate([cached[0:k_rows], new[0:k_rows]], axis=1)
        vals = jnp.concatenate([cached[k_rows:], new[k_rows:]], axis=1)
        s = _dot(q_ref[...], keys) + bias_ref[...]
        mx = jnp.max(s, axis=1, keepdims=True)
        p = jnp.exp(s - mx)
        l = jnp.sum(p, axis=1, keepdims=True)
        outs.append(lax.dot_general((p / l).astype(BF16), vals, (((1,), (1,)), ((), ())),
                                    preferred_element_type=F32))
        lses.append(mx + jnp.log(l))
    mx = jnp.maximum(lses[0], jnp.maximum(lses[1], lses[2]))
    es = [jnp.exp(v - mx) for v in lses]
    mixed = (es[0] * outs[0] + es[1] * outs[1] + es[2] * outs[2]) / (es[0] + es[1] + es[2])
    att_ref[...] = jnp.sum(mixed.reshape(dec_seq, HEADS, k_rows) * headmask_ref[...][None], axis=1)


def _sample_mix(state_pool, u, qs, kvn, caches, layer, w, dec_batch, dec_seq):
    cache_rows = [c.shape[2] for c in caches]
    geo = _sample_keys(dec_seq, cache_rows)
    consts = [g[i] for i in (0, 1) for g in geo] + [_head_mask()]
    kv_tail = (2, HEADS, HEAD_DIM)
    cviews = [jnp.transpose(c, (0, 1, 3, 4, 5, 2)).reshape((-1,) + kv_tail + (c.shape[2],)) for c in caches]
    cspecs = [pl.BlockSpec((None,) + kv_tail + (lc,), lambda b: (layer * dec_batch + b, 0, 0, 0, 0))
              for lc in cache_rows]
    per_batch = lambda *tail: pl.BlockSpec((None,) + tail, lambda b: (b,) + (0,) * len(tail))
    const = lambda shape: _resident(shape, lambda b: (0,) * len(shape))
    return pl.pallas_call(
        functools.partial(_sample_mix_kernel, dec_seq=dec_seq),
        grid=(dec_batch,),
        in_specs=[pl.BlockSpec((None, POOL_STATE, POOL_W), lambda b: (layer * dec_batch + b, 0, 0)),
                  per_batch(dec_seq, POOL_W)] + [per_batch(dec_seq * HEADS, GROUP_W)] * 3
                 + [per_batch(*kv_tail, LANES)] * 3 + cspecs + [const(c.shape) for c in consts]
                 + [_wspec((len(POOL_WINDOWS), POOL_GROUP, POOL_GROUP), layer, 1), _wspec((1, POOL_W), layer, 1)],
        out_specs=[per_batch(dec_seq, POOL_W), per_batch(dec_seq, GROUP_W),
                   pl.BlockSpec((None, POOL_STATE, POOL_W), lambda b: (b, 0, 0))],
        out_shape=[jax.ShapeDtypeStruct((dec_batch, dec_seq, POOL_W), F32),
                   jax.ShapeDtypeStruct((dec_batch, dec_seq, GROUP_W), F32),
                   jax.ShapeDtypeStruct((dec_batch, POOL_STATE, POOL_W), F32)],
        scratch_shapes=[pltpu.VMEM((POOL_STATE + dec_seq + 5, POOL_W), F32), pltpu.VMEM((8, POOL_W), F32)],
        compiler_params=_params(("parallel",)),
        name="sample_mix",
    )(state_pool.reshape(-1, POOL_STATE, POOL_W), u.reshape(dec_batch, dec_seq, POOL_W), *qs, *kvn, *cviews,
      *consts, w["w_pool"], w["pool_scale"])


def _tail_kernel(x_ref, pool_ref, att_ref, nattn_ref, wgate_ref, wbp_ref, wba_ref, wo_ref, nffn_ref, xo_ref, xn_ref):
    _mix_tail(x_ref[...], pool_ref[...], att_ref[...], nattn_ref, wgate_ref, wbp_ref, wba_ref, wo_ref, nffn_ref,
              xo_ref, xn_ref)


def _tail(x, pool_o, att_o, layer, w):
    n = x.shape[0]
    row = lambda width: pl.BlockSpec((n, width), lambda i: (0, 0))
    return pl.pallas_call(
        _tail_kernel,
        grid=(1,),
        in_specs=[row(D_MODEL), row(POOL_W), row(GROUP_W)] + _tail_weight_specs(layer, 1),
        out_specs=[row(D_MODEL), row(D_MODEL)],
        out_shape=[jax.ShapeDtypeStruct((n, D_MODEL), F32), jax.ShapeDtypeStruct((n, D_MODEL), BF16)],
        compiler_params=_params(("arbitrary",)),
        name="tail",
    )(x, pool_o, att_o, *_tail_weights(w))


def _swiglu_step(x, wg_ref, wu_ref, wd_ref, acc_ref):
    h = jax.nn.silu(_dot(x, wg_ref[...])) * _dot(x, wu_ref[...])
    acc_ref[...] += _dot(h.astype(BF16), wd_ref[...])


def _ffn_kernel(x_ref, xn_ref, w1_ref, w3_ref, w2_ref, y_ref, acc_ref):
    f = pl.program_id(1)

    @pl.when(f == 0)
    def _():
        acc_ref[...] = x_ref[...]

    _swiglu_step(xn_ref[...], w1_ref, w3_ref, w2_ref, acc_ref)

    @pl.when(f == pl.num_programs(1) - 1)
    def _():
        y_ref[...] = acc_ref[...]


def _ffn(x, xn, w1, w3, w2, idx, tm, tf):
    n = x.shape[0]
    d_ff = w1.shape[-1]
    return pl.pallas_call(
        _ffn_kernel,
        grid=(n // tm, d_ff // tf),
        in_specs=[pl.BlockSpec((tm, D_MODEL), lambda i, f: (i, 0)),
                  pl.BlockSpec((tm, D_MODEL), lambda i, f: (i, 0)),
                  pl.BlockSpec((None, D_MODEL, tf), lambda i, f: (idx, 0, f)),
                  pl.BlockSpec((None, D_MODEL, tf), lambda i, f: (idx, 0, f)),
                  pl.BlockSpec((None, tf, D_MODEL), lambda i, f: (idx, f, 0))],
        out_specs=pl.BlockSpec((tm, D_MODEL), lambda i, f: (i, 0)),
        out_shape=jax.ShapeDtypeStruct((n, D_MODEL), F32),
        scratch_shapes=[pltpu.VMEM((tm, D_MODEL), F32)],
        compiler_params=_params(("parallel", "arbitrary")),
        name="ffn",
    )(x, xn, w1, w3, w2)


SEG_ALIGN = 16
MOE_ROWS = 512


def _sorted_cap(ts):
    need = TOP_K * ts + N_EXPERTS * (SEG_ALIGN - 1)
    return -(-need // LANES) * LANES if ts >= LANES * 2 else -(-need // SEG_ALIGN) * SEG_ALIGN


def _segment_sizes(oh1, oh2):
    cnt1 = jnp.sum(oh1, axis=1, keepdims=True)
    cnt = cnt1 + jnp.sum(oh2, axis=1, keepdims=True)
    cpad = jnp.floor((cnt + (SEG_ALIGN - 1)) * (1.0 / SEG_ALIGN)) * SEG_ALIGN
    offs = [jnp.zeros((1, 1), F32)]
    for ei in range(1, N_EXPERTS):
        offs.append(offs[-1] + cpad[ei - 1:ei, :])
    return cnt1, cpad, jnp.concatenate(offs, axis=0)


def _gate_kernel(x_ref, g_ref, rwt_ref, rb_ref, route_ref, seg_ref, *, ts):
    xn = _rms(x_ref[...], g_ref[...]).astype(BF16)
    logits = lax.dot_general(rwt_ref[...], xn, (((1,), (1,)), ((), ())), preferred_element_type=F32) + rb_ref[...]
    row = lax.broadcasted_iota(jnp.int32, (N_EXPERTS, ts), 0)
    neg = jnp.float32(-jnp.inf)
    m1 = jnp.max(logits, axis=0, keepdims=True)
    i1 = jnp.min(jnp.where(logits == m1, row, N_EXPERTS), axis=0, keepdims=True)
    rest = jnp.where(row == i1, neg, logits)
    m2 = jnp.max(rest, axis=0, keepdims=True)
    i2 = jnp.min(jnp.where(rest == m2, row, N_EXPERTS), axis=0, keepdims=True)
    e = jnp.exp(m2 - m1)
    zeros = jnp.zeros((N_EXPERTS - 4, ts), F32)
    route_ref[...] = jnp.concatenate([i1.astype(F32), i2.astype(F32), 1.0 / (1.0 + e), e / (1.0 + e), zeros], axis=0)
    _, cpad, off = _segment_sizes(jnp.where(row == i1, 1.0, 0.0), jnp.where(row == i2, 1.0, 0.0))
    lane = lax.broadcasted_iota(jnp.int32, (N_EXPERTS, LANES), 1)
    seg_ref[...] = jnp.where(lane == 0, off, jnp.where(lane == 1, cpad, 0.0)).astype(jnp.int32)


def _gate(x, norm_ffn, rwt, rb, layer, idx, ts):
    n = x.shape[0]
    n_sub = n // ts
    return pl.pallas_call(
        functools.partial(_gate_kernel, ts=ts),
        grid=(n_sub,),
        in_specs=[pl.BlockSpec((ts, D_MODEL), lambda i: (i, 0)),
                  _resident((None, 1, D_MODEL), lambda i: (layer, 0, 0)),
                  _resident((None, N_EXPERTS, D_MODEL), lambda i: (idx, 0, 0)),
                  _resident((None, N_EXPERTS, 1), lambda i: (idx, 0, 0))],
        out_specs=[pl.BlockSpec((None, N_EXPERTS, ts), lambda i: (i, 0, 0)),
                   pl.BlockSpec((None, N_EXPERTS, LANES), lambda i: (i, 0, 0))],
        out_shape=[jax.ShapeDtypeStruct((n_sub, N_EXPERTS, ts), F32),
                   jax.ShapeDtypeStruct((n_sub, N_EXPERTS, LANES), jnp.int32)],
        compiler_params=_params(("parallel",)),
        name="gate",
    )(x, norm_ffn, rwt, rb)


def _segment_copies(i, n_bits, local_ref, off_ref, far_ref, far_rows_ref, nseg_ref, sem, to_far):
    copies = []
    for e in range(N_EXPERTS):
        j = i * N_EXPERTS + e
        n = nseg_ref[j]
        for b in reversed(range(n_bits)):
            size = SEG_ALIGN << b
            done = ((n >> (b + 1)) << (b + 1)) * SEG_ALIGN
            near = local_ref.at[pl.ds(pl.multiple_of(off_ref[j] + done, SEG_ALIGN), size)]
            far = far_ref.at[pl.ds(pl.multiple_of(far_rows_ref[j] + done, SEG_ALIGN), size)]
            cp = pltpu.make_async_copy(near, far, sem) if to_far else pltpu.make_async_copy(far, near, sem)
            copies.append((((n >> b) & 1) == 1, cp))
    return copies


def _scatter_kernel(off_ref, far_rows_ref, nseg_ref, x_ref, g_ref, route_ref, tri_ref, xb_in_ref,
                    meta_ref, xb_ref, xs_ref, sem, *, ts, cap, n_bits):
    del xb_in_ref
    i = pl.program_id(0)
    xn = _rms(x_ref[...], g_ref[...]).astype(BF16)
    row = lax.broadcasted_iota(jnp.int32, (N_EXPERTS, ts), 0).astype(F32)
    oh1 = jnp.where(row == route_ref[0:1, :], 1.0, 0.0)
    oh2 = jnp.where(row == route_ref[1:2, :], 1.0, 0.0)
    cnt1, _, off = _segment_sizes(oh1, oh2)
    pre1 = _dot(oh1.astype(BF16), tri_ref[...])
    pre2 = _dot(oh2.astype(BF16), tri_ref[...])
    d1 = jnp.sum(oh1 * (off + pre1), axis=0, keepdims=True)
    d2 = jnp.sum(oh2 * (off + cnt1 + pre2), axis=0, keepdims=True)
    slot = lax.broadcasted_iota(jnp.int32, (cap, ts), 0).astype(F32)
    p = jnp.where(slot == d1, 1.0, jnp.where(slot == d2, 1.0, 0.0)).astype(BF16)
    xs_ref[...] = _dot(p, xn).astype(BF16)
    meta_t = jnp.concatenate([d1, d2, route_ref[2:4, :], jnp.zeros((LANES - 4, ts), F32)], axis=0)
    meta_ref[...] = meta_t.T
    copies = _segment_copies(i, n_bits, xs_ref, off_ref, xb_ref, far_rows_ref, nseg_ref, sem, True)
    for pred, cp in copies:
        pl.when(pred)(cp.start)
    for pred, cp in copies:
        pl.when(pred)(cp.wait)


def _scatter(off, far_rows, nseg, x, norm_ffn, route, xb, layer, ts):
    n = x.shape[0]
    n_sub = n // ts
    cap = _sorted_cap(ts)
    tri = jnp.asarray(np.triu(np.ones((ts, ts), np.float32), 1), BF16)
    n_bits = int(TOP_K * ts // SEG_ALIGN).bit_length()
    return pl.pallas_call(
        functools.partial(_scatter_kernel, ts=ts, cap=cap, n_bits=n_bits),
        grid_spec=pltpu.PrefetchScalarGridSpec(
            num_scalar_prefetch=3,
            grid=(n_sub,),
            in_specs=[pl.BlockSpec((ts, D_MODEL), lambda i, *_: (i, 0)),
                      _resident((None, 1, D_MODEL), lambda i, *_: (layer, 0, 0)),
                      pl.BlockSpec((None, N_EXPERTS, ts), lambda i, *_: (i, 0, 0)),
                      _resident((ts, ts), lambda i, *_: (0, 0)),
                      pl.BlockSpec(memory_space=pl.ANY)],
            out_specs=[pl.BlockSpec((ts, LANES), lambda i, *_: (i, 0)),
                       pl.BlockSpec(memory_space=pl.ANY)],
            scratch_shapes=[pltpu.VMEM((cap, D_MODEL), BF16), pltpu.SemaphoreType.DMA(())]),
        out_shape=[jax.ShapeDtypeStruct((n, LANES), F32), jax.ShapeDtypeStruct(xb.shape, xb.dtype)],
        input_output_aliases={7: 1},
        compiler_params=_params(("arbitrary",)),
        name="scatter",
    )(off, far_rows, nseg, x, norm_ffn, route, tri, xb)


def _experts_kernel(blk_ref, nreal_ref, xb_ref, wg_ref, wu_ref, wd_ref, yb_ref, acc_ref):
    del blk_ref
    i = pl.program_id(0)
    f = pl.program_id(1)

    @pl.when(f == 0)
    def _():
        acc_ref[...] = jnp.zeros_like(acc_ref)

    @pl.when(i < nreal_ref[0])
    def _():
        _swiglu_step(xb_ref[...], wg_ref, wu_ref, wd_ref, acc_ref)

    @pl.when(f == pl.num_programs(1) - 1)
    def _():
        yb_ref[...] = acc_ref[...].astype(yb_ref.dtype)


def _experts(blk_e, nreal, xb, wg, wu, wd, idx, tf):
    n_blocks = blk_e.shape[0]
    d_ff = wg.shape[-1]
    n_f = d_ff // tf

    def live_block(i, nreal):
        return jnp.minimum(i, nreal[0] - 1)

    def wcol(i, f, blk, nreal):
        return idx, blk[live_block(i, nreal)], 0, jnp.where(i < nreal[0], f, n_f - 1)

    def wrow(i, f, blk, nreal):
        return idx, blk[live_block(i, nreal)], jnp.where(i < nreal[0], f, n_f - 1), 0

    return pl.pallas_call(
        _experts_kernel,
        grid_spec=pltpu.PrefetchScalarGridSpec(
            num_scalar_prefetch=2,
            grid=(n_blocks, n_f),
            in_specs=[pl.BlockSpec((MOE_ROWS, D_MODEL), lambda i, f, blk, nreal: (live_block(i, nreal), 0)),
                      pl.BlockSpec((None, None, D_MODEL, tf), wcol),
                      pl.BlockSpec((None, None, D_MODEL, tf), wcol),
                      pl.BlockSpec((None, None, tf, D_MODEL), wrow)],
            out_specs=pl.BlockSpec((MOE_ROWS, D_MODEL), lambda i, f, blk, nreal: (i, 0)),
            scratch_shapes=[pltpu.VMEM((MOE_ROWS, D_MODEL), F32)]),
        out_shape=jax.ShapeDtypeStruct((n_blocks * MOE_ROWS, D_MODEL), BF16),
        compiler_params=_params(("parallel", "arbitrary")),
        name="experts",
    )(blk_e, nreal, xb, wg, wu, wd)


def _combine_kernel(off_ref, far_rows_ref, nseg_ref, x_ref, meta_ref, yb_ref, o_ref, ys_ref, sem, *, ts, cap, n_bits):
    i = pl.program_id(0)
    ys_ref[...] = jnp.zeros_like(ys_ref)
    copies = _segment_copies(i, n_bits, ys_ref, off_ref, yb_ref, far_rows_ref, nseg_ref, sem, False)
    for pred, cp in copies:
        pl.when(pred)(cp.start)
    for pred, cp in copies:
        pl.when(pred)(cp.wait)
    slot = lax.broadcasted_iota(jnp.int32, (ts, cap), 1).astype(F32)
    ys = ys_ref[...]
    q1 = jnp.where(slot == meta_ref[:, 0:1], 1.0, 0.0).astype(BF16)
    q2 = jnp.where(slot == meta_ref[:, 1:2], 1.0, 0.0).astype(BF16)
    o_ref[...] = x_ref[...] + meta_ref[:, 2:3] * _dot(q1, ys) + meta_ref[:, 3:4] * _dot(q2, ys)


def _combine(off, far_rows, nseg, x, meta, yb, ts):
    n = x.shape[0]
    cap = _sorted_cap(ts)
    n_bits = int(TOP_K * ts // SEG_ALIGN).bit_length()
    return pl.pallas_call(
        functools.partial(_combine_kernel, ts=ts, cap=cap, n_bits=n_bits),
        grid_spec=pltpu.PrefetchScalarGridSpec(
            num_scalar_prefetch=3,
            grid=(n // ts,),
            in_specs=[pl.BlockSpec((ts, D_MODEL), lambda i, *_: (i, 0)),
                      pl.BlockSpec((ts, LANES), lambda i, *_: (i, 0)),
                      pl.BlockSpec(memory_space=pl.ANY)],
            out_specs=pl.BlockSpec((ts, D_MODEL), lambda i, *_: (i, 0)),
            scratch_shapes=[pltpu.VMEM((cap, D_MODEL), BF16), pltpu.SemaphoreType.DMA(())]),
        out_shape=jax.ShapeDtypeStruct((n, D_MODEL), F32),
        compiler_params=_params(("arbitrary",)),
        name="combine",
    )(off, far_rows, nseg, x, meta, yb)


def _moe(xs, tss, norm_ffn, rwt, rb, wg, wu, wd, layer, idx, tf):
    gated = [_gate(x, norm_ffn, rwt, rb, layer, idx, ts) for x, ts in zip(xs, tss)]
    off = jnp.concatenate([g[1][:, :, 0] for g in gated], axis=0)
    cpad = jnp.concatenate([g[1][:, :, 1] for g in gated], axis=0)
    n_subs = [g[1].shape[0] for g in gated]
    tot = jnp.sum(cpad, axis=0)
    padded = (tot + MOE_ROWS - 1) // MOE_ROWS * MOE_ROWS
    pend = jnp.cumsum(padded)
    within = jnp.cumsum(cpad, axis=0) - cpad
    off = off.astype(jnp.int32).reshape(-1)
    grouped_rows = ((pend - padded)[None, :] + within).astype(jnp.int32).reshape(-1)
    nseg = (cpad // SEG_ALIGN).astype(jnp.int32).reshape(-1)
    n_assign = TOP_K * sum(x.shape[0] for x in xs)
    n_blocks = (n_assign + (SEG_ALIGN - 1) * sum(n_subs) * N_EXPERTS + N_EXPERTS * (MOE_ROWS - 1)) // MOE_ROWS
    blk_e = jnp.minimum(jnp.sum(jnp.arange(n_blocks)[:, None] * MOE_ROWS >= pend[None, :], axis=1),
                        N_EXPERTS - 1).astype(jnp.int32)
    nreal = (pend[-1:] // MOE_ROWS).astype(jnp.int32)

    xb = jnp.zeros((n_blocks * MOE_ROWS, D_MODEL), BF16)
    metas, parts = [], []
    lo = 0
    for x, g, ns, ts in zip(xs, gated, n_subs, tss):
        sl = slice(lo * N_EXPERTS, (lo + ns) * N_EXPERTS)
        parts.append((off[sl], grouped_rows[sl], nseg[sl]))
        meta, xb = _scatter(*parts[-1], x, norm_ffn, g[0], xb, layer, ts)
        metas.append(meta)
        lo += ns
    yb = _experts(blk_e, nreal, xb, wg, wu, wd, idx, tf)
    return [_combine(*part, x, meta, yb, ts) for part, x, meta, ts in zip(parts, xs, metas, tss)]


def kernel(x_prompt, x_sample, state_pool, cache_kv_g0, cache_kv_g1, cache_kv_g2, norm_attn, w_in, q_norm, k_norm, w_pool, pool_scale, w_branch_pool, w_branch_attn, w_out, norm_ffn, w1_dense, w3_dense, w2_dense, router_w, router_b, we_gate, we_up, we_down):
    batch, seq, _ = x_prompt.shape
    dec_batch, dec_seq, _ = x_sample.shape
    depth = w_in.shape[0]
    caches = (cache_kv_g0, cache_kv_g1, cache_kv_g2)
    n_p, n_s = batch * seq, dec_batch * dec_seq
    tm_p = 512
    dils = tuple(d for _, d in ATT_GROUPS)
    keeps = tuple(min(win, seq) for win, _ in ATT_GROUPS)

    head_of = np.arange(GROUP_W) // HEAD_DIM
    same_head = (head_of[:, None] == head_of[None, :]).astype(np.float32)
    seg_mean = jnp.asarray(same_head / HEAD_DIM, BF16)
    w_qkv = w_in[:, :, :PROJ_W].astype(BF16)
    qgain = (jnp.tile(q_norm, (1, HEADS)) * (HEAD_DIM ** -0.5)).reshape(depth, 1, GROUP_W)
    kgain = jnp.tile(k_norm, (1, HEADS)).reshape(depth, 1, GROUP_W)
    norm_attn3 = norm_attn.reshape(depth, 1, D_MODEL)
    w = {"w_pool": w_pool.astype(BF16), "pool_scale": pool_scale.reshape(depth, 1, POOL_W),
         "norm_attn": norm_attn3, "w_gate": w_in[:, :, PROJ_W:].astype(BF16),
         "w_bp": w_branch_pool.astype(BF16), "w_ba": w_branch_attn.astype(BF16), "w_o": w_out.astype(BF16),
         "norm_ffn": norm_ffn.reshape(depth, 1, D_MODEL)}
    w1_b, w3_b, w2_b = w1_dense.astype(BF16), w3_dense.astype(BF16), w2_dense.astype(BF16)
    wg_b, wu_b, wd_b = we_gate.astype(BF16), we_up.astype(BF16), we_down.astype(BF16)
    rwt = jnp.swapaxes(router_w, 1, 2).astype(BF16)
    rb = router_b.reshape(-1, N_EXPERTS, 1)

    xp = x_prompt.reshape(n_p, D_MODEL)
    xs = x_sample.reshape(n_s, D_MODEL)
    pool_p, pool_s = [], []
    kv_s = [[] for _ in ATT_GROUPS]
    kvo_p = tuple(jnp.zeros((depth, batch, 2, GROUP_W, keep), F32) for keep in keeps)
    ones = (1,) * N_GROUPS
    for layer in range(depth):
        res = _proj(xp, layer, norm_attn3, w_qkv, qgain, kgain, seg_mean, batch, seq, tm_p, dils, keeps,
                    layer, depth, kvo_p, 1, True)
        u, qs, kcs, kvo_p = res[0], res[1:4], res[4:7], tuple(res[7:10])
        outs, lses = [], []
        for gi in range(N_GROUPS):
            o, lse = _attention(qs[gi], kcs[gi], gi)
            outs.append(o)
            lses.append(lse)
        pool_p.append(u.reshape(batch, seq, POOL_W)[:, seq - POOL_STATE:])
        xp, xpn = _merge(xp, u, outs, lses, layer, w, batch, seq, tm_p, dils)

        res = _proj(xs, layer, norm_attn3, w_qkv, qgain, kgain, seg_mean, 1, n_s, n_s, ones, (n_s,) * N_GROUPS,
                    0, 1, (), 3, False)
        u = res[0]
        qs = [(q.reshape(dec_batch, dec_seq, 1, GROUP_W) * _head_mask().astype(BF16))
              .reshape(dec_batch, dec_seq * HEADS, GROUP_W) for q in res[1:4]]
        kv_new = [kv.reshape(dec_batch, dec_seq, 2, HEADS, HEAD_DIM) for kv in res[7:10]]
        tokens_to_lanes = ((0, 0),) * 4 + ((0, LANES - dec_seq),)
        kvn = [jnp.pad(kv.transpose(0, 2, 3, 4, 1), tokens_to_lanes) for kv in kv_new]
        pool_o, att_o, new_pool = _sample_mix(state_pool, u, qs, kvn, caches, layer, w, dec_batch, dec_seq)
        att_o = att_o.reshape(n_s, GROUP_W)
        for gi in range(N_GROUPS):
            kv_s[gi].append(kv_new[gi])
        pool_s.append(new_pool)
        xs, xsn = _tail(xs, pool_o.reshape(n_s, POOL_W), att_o, layer, w)

        i = layer // 2
        if layer % 2 == 0:
            xp = _ffn(xp, xpn, w1_b, w3_b, w2_b, i, 512, w1_b.shape[-1] // 2)
            xs = _ffn(xs, xsn, w1_b, w3_b, w2_b, i, n_s, w1_b.shape[-1] // 2)
        else:
            xp, xs = _moe([xp, xs], [512, n_s], w["norm_ffn"], rwt, rb, wg_b, wu_b, wd_b, layer, i,
                           wg_b.shape[-1] // 2)
    kv_p = [kvo_p[g].reshape(depth, batch, 2, HEADS, HEAD_DIM, keeps[g]).transpose(0, 1, 5, 2, 3, 4)
            for g in range(N_GROUPS)]
    return (xp.reshape(batch, seq, D_MODEL), xs.reshape(dec_batch, dec_seq, D_MODEL),
            jnp.stack(pool_p), jnp.stack(pool_s),
            kv_p[0], jnp.stack(kv_s[0]),
            kv_p[1], jnp.stack(kv_s[1]),
            kv_p[2], jnp.stack(kv_s[2]))
```

```python
import functools

import numpy as np
import jax
import jax.numpy as jnp
from jax import lax
from jax.experimental import pallas as pl
from jax.experimental.pallas import tpu as pltpu

F32 = jnp.float32
BF16 = jnp.bfloat16

D_MODEL = 1024
PAST_LEN = 16384
POOL_WINDOWS = (2, 4, 8, 16)
POOL_GROUP = 128
POOL_W = 512
POOL_STATE = 15
ATT_GROUPS = ((128, 1), (512, 4), (2048, 16))
N_GROUPS = len(ATT_GROUPS)
HEAD_DIM = 64
HEADS = 8
GROUP_W = 512
QKV_W = 1536
Q_BLOCK = 128
ALIBI_MAX = 8.0
N_EXPERTS = 8
TOP_K = 2
RMS_EPS = 1e-6
NEG_INF = -1e30
LANES = 128
LANE_CHUNKS = GROUP_W // LANES
PROJ_W = POOL_W + 3 * QKV_W
VMEM_LIMIT = 56 * 1024 * 1024


def _slopes():
    i = np.arange(1, N_GROUPS * HEADS + 1, dtype=np.float32)
    return np.exp2(-ALIBI_MAX * i / (N_GROUPS * HEADS)).astype(np.float32).reshape(N_GROUPS, HEADS)


def _params(sem):
    return pltpu.CompilerParams(dimension_semantics=sem, vmem_limit_bytes=VMEM_LIMIT)


def _rms(x, gain):
    return x * lax.rsqrt(jnp.mean(x * x, axis=-1, keepdims=True) + RMS_EPS) * gain


def _dot(a, b):
    return jnp.dot(a, b, preferred_element_type=F32)


def _resident(shape, index_map):
    return pl.BlockSpec(shape, index_map, pipeline_mode=pl.Buffered(1))


def _write_classes(dst_ref, col0, val, dil, tmp_ref, slot):
    rows = val.shape[0] // dil
    cols = slice(col0, col0 + GROUP_W)
    if dil == 1:
        dst_ref[0, :, cols] = val.astype(dst_ref.dtype)
        return
    for c in range(LANE_CHUNKS):
        tmp_ref[slot, c] = val[:, c * LANES:(c + 1) * LANES]
    for r in range(dil):
        picked = [tmp_ref[slot, c, pl.ds(r, rows, stride=dil), :] for c in range(LANE_CHUNKS)]
        dst_ref[r, :, cols] = jnp.concatenate(picked, axis=1).astype(dst_ref.dtype)


def _proj_kernel(x_ref, g_ref, w_ref, qg_ref, kg_ref, seg_ref, *rest, tm, n_tiles, dils, keeps, n_alias,
                 norm_terms, kv_positions_minor):
    u_ref, q0_ref, q1_ref, q2_ref, kc0_ref, kc1_ref, kc2_ref, kvo0_ref, kvo1_ref, kvo2_ref, tmp_ref = rest[n_alias:]
    s = pl.program_id(1)
    xn = _rms(x_ref[...], g_ref[...]).astype(BF16)

    def zblk(j):
        return _dot(xn, w_ref[:, j * GROUP_W:(j + 1) * GROUP_W])

    def headnorm(z, gain):
        rem = z * z
        ms = None
        for _ in range(norm_terms):
            part = rem.astype(BF16)
            rem = rem - part.astype(F32)
            ms = _dot(part, seg_ref[...]) if ms is None else ms + _dot(part, seg_ref[...])
        return z * lax.rsqrt(ms + RMS_EPS) * gain

    u_ref[...] = zblk(0)
    q_refs = (q0_ref, q1_ref, q2_ref)
    kc_refs = (kc0_ref, kc1_ref, kc2_ref)
    kvo_refs = (kvo0_ref, kvo1_ref, kvo2_ref)
    slot = 0
    kept = {}
    for g in range(N_GROUPS):
        q = headnorm(zblk(1 + g), qg_ref[...])
        k = headnorm(zblk(1 + N_GROUPS + g), kg_ref[...])
        v = zblk(1 + 2 * N_GROUPS + g)
        for dst, col0, val in ((q_refs[g], 0, q), (kc_refs[g], 0, k), (kc_refs[g], GROUP_W, v)):
            _write_classes(dst, col0, val, dils[g], tmp_ref, slot % tmp_ref.shape[0])
            slot += dils[g] > 1
        kept.setdefault(n_tiles - max(keeps[g] // tm, 1), []).append((kvo_refs[g], min(keeps[g], tm), k, v))

    def write_kept(items):
        for ref, rows, k, v in items:
            if kv_positions_minor:
                ref[0] = k[tm - rows:, :].T
                ref[1] = v[tm - rows:, :].T
            else:
                ref[:, 0:GROUP_W] = k[tm - rows:, :]
                ref[:, GROUP_W:2 * GROUP_W] = v[tm - rows:, :]

    for first, items in kept.items():
        if first == 0:
            write_kept(items)
        else:
            pl.when(s >= first)(functools.partial(write_kept, items))


def _proj(x, layer, norm_attn, w_qkv, qgain, kgain, seg, batch, seq, tm, dils, keeps, out_layer, out_depth, prev_kvo,
          norm_terms, kv_positions_minor):
    n_tiles = seq // tm
    kvo_specs, kvo_shapes = [], []
    for g in range(N_GROUPS):
        keep = keeps[g]
        assert keep % tm == 0 or (keep < tm and keep % LANES == 0)
        first = n_tiles - max(keep // tm, 1)
        rows = min(keep, tm)
        if kv_positions_minor:
            kvo_specs.append(pl.BlockSpec((None, None, 2, GROUP_W, rows),
                                          lambda b, s, first=first: (out_layer, b, 0, 0, jnp.maximum(s - first, 0))))
            kvo_shapes.append(jax.ShapeDtypeStruct((out_depth, batch, 2, GROUP_W, keep), F32))
        else:
            kvo_specs.append(pl.BlockSpec((None, None, rows, 2 * GROUP_W),
                                          lambda b, s, first=first: (out_layer, b, jnp.maximum(s - first, 0), 0)))
            kvo_shapes.append(jax.ShapeDtypeStruct((out_depth, batch, keep, 2 * GROUP_W), F32))
    cls_spec = lambda g, width: pl.BlockSpec((None, dils[g], tm // dils[g], width), lambda b, s: (b, 0, s, 0))
    cls_shape = lambda g, width: jax.ShapeDtypeStruct((batch, dils[g], seq // dils[g], width), BF16)
    n_alias = len(prev_kvo)
    n_in = 6
    return pl.pallas_call(
        functools.partial(_proj_kernel, tm=tm, n_tiles=n_tiles, dils=dils, keeps=keeps, n_alias=n_alias,
                          norm_terms=norm_terms, kv_positions_minor=kv_positions_minor),
        grid=(batch, n_tiles),
        in_specs=[
            pl.BlockSpec((tm, D_MODEL), lambda b, s: (b * n_tiles + s, 0)),
            _resident((None, 1, D_MODEL), lambda b, s: (layer, 0, 0)),
            _resident((None, D_MODEL, PROJ_W), lambda b, s: (layer, 0, 0)),
            _resident((None, 1, GROUP_W), lambda b, s: (layer, 0, 0)),
            _resident((None, 1, GROUP_W), lambda b, s: (layer, 0, 0)),
            _resident((GROUP_W, GROUP_W), lambda b, s: (0, 0)),
        ] + [pl.BlockSpec(memory_space=pl.ANY)] * n_alias,
        out_specs=[pl.BlockSpec((tm, POOL_W), lambda b, s: (b * n_tiles + s, 0))]
                  + [cls_spec(g, GROUP_W) for g in range(N_GROUPS)]
                  + [cls_spec(g, 2 * GROUP_W) for g in range(N_GROUPS)] + kvo_specs,
        out_shape=[jax.ShapeDtypeStruct((batch * seq, POOL_W), F32)]
                  + [cls_shape(g, GROUP_W) for g in range(N_GROUPS)]
                  + [cls_shape(g, 2 * GROUP_W) for g in range(N_GROUPS)] + kvo_shapes,
        scratch_shapes=[pltpu.VMEM((3, LANE_CHUNKS, tm, LANES), F32)],
        input_output_aliases={n_in + g: 1 + 2 * N_GROUPS + g for g in range(n_alias)},
        compiler_params=_params(("parallel", "arbitrary")),
        name="proj",
    )(x, norm_attn, w_qkv, qgain, kgain, seg, *prev_kvo)


def _attn_kernel(q_ref, kvc_ref, kvh_ref, o_ref, lse_ref, kbuf, vbuf, *, slope_dil, tc, isolated_tiles):
    off = Q_BLOCK
    kbuf[0:off, :] = kvh_ref[:, 0:GROUP_W]
    vbuf[0:off, :] = kvh_ref[:, GROUP_W:2 * GROUP_W]
    kbuf[off:off + tc, :] = kvc_ref[:, 0:GROUP_W]
    vbuf[off:off + tc, :] = kvc_ref[:, GROUP_W:2 * GROUP_W]

    nk = Q_BLOCK + off
    kj = lax.broadcasted_iota(jnp.int32, (nk, Q_BLOCK), 0)
    qi = lax.broadcasted_iota(jnp.int32, (nk, Q_BLOCK), 1)
    dist = qi - kj + off
    distf = dist.astype(F32)
    maskneg = jnp.where((dist >= 0) & (dist <= Q_BLOCK), 0.0, NEG_INF).astype(F32)
    first = NEG_INF if isolated_tiles else jnp.where(pl.program_id(2) == 0, NEG_INF, 0.0).astype(F32)
    mask_first = maskneg + jnp.where(kj < Q_BLOCK, first, 0.0)

    for i in range(tc // Q_BLOCK):
        mask = mask_first if (isolated_tiles or i == 0) else maskneg
        rows = slice(i * Q_BLOCK, (i + 1) * Q_BLOCK)
        krows = slice(i * Q_BLOCK, i * Q_BLOCK + nk)
        for hp in range(HEADS // 2):
            outs, lses = [], []
            for h in (2 * hp, 2 * hp + 1):
                cols = slice(h * HEAD_DIM, (h + 1) * HEAD_DIM)
                s = lax.dot_general(kbuf[krows, cols], q_ref[rows, cols],
                                    (((1,), (1,)), ((), ())), preferred_element_type=F32)
                s = s - slope_dil[h] * distf + mask
                m = jnp.max(s, axis=0, keepdims=True)
                p = jnp.exp(s - m)
                l = jnp.sum(p, axis=0, keepdims=True)
                outs.append(lax.dot_general(vbuf[krows, cols], p.astype(BF16), (((0,), (0,)), ((), ())),
                                            preferred_element_type=F32) / l)
                lses.append(jnp.broadcast_to(m + jnp.log(l), (HEAD_DIM, Q_BLOCK)))
            pc = slice(hp * 2 * HEAD_DIM, (hp + 1) * 2 * HEAD_DIM)
            o_ref[rows, pc] = jnp.concatenate(outs, axis=0).T.astype(BF16)
            lse_ref[rows, pc] = jnp.concatenate(lses, axis=0).T


def _attention(q, kc, gi):
    win, dil = ATT_GROUPS[gi]
    assert win // dil == Q_BLOCK
    batch, _, l, _ = q.shape
    out_dims = (batch, dil, l, GROUP_W)
    isolated_tiles = l == Q_BLOCK
    slope_dil = tuple(float(s) * dil for s in _slopes()[gi])
    if isolated_tiles:
        q, kc = q.reshape(batch, 1, dil * l, GROUP_W), kc.reshape(batch, 1, dil * l, 2 * GROUP_W)
        dil, l = 1, dil * l
    tc = min(l, 512)
    nk = tc + Q_BLOCK
    o, lse = pl.pallas_call(
        functools.partial(_attn_kernel, slope_dil=slope_dil, tc=tc, isolated_tiles=isolated_tiles),
        grid=(batch, dil, l // tc),
        in_specs=[pl.BlockSpec((None, None, tc, GROUP_W), lambda b, r, c: (b, r, c, 0)),
                  pl.BlockSpec((None, None, tc, 2 * GROUP_W), lambda b, r, c: (b, r, c, 0)),
                  pl.BlockSpec((None, None, Q_BLOCK, 2 * GROUP_W),
                               lambda b, r, c: (b, r, jnp.maximum(c * (tc // Q_BLOCK) - 1, 0), 0))],
        out_specs=[pl.BlockSpec((None, None, tc, GROUP_W), lambda b, r, c: (b, r, c, 0))] * 2,
        out_shape=[jax.ShapeDtypeStruct((batch, dil, l, GROUP_W), BF16),
                   jax.ShapeDtypeStruct((batch, dil, l, GROUP_W), F32)],
        scratch_shapes=[pltpu.VMEM((nk, GROUP_W), BF16), pltpu.VMEM((nk, GROUP_W), BF16)],
        compiler_params=_params(("parallel", "parallel", "arbitrary")),
        name=f"attn_g{gi}",
    )(q, kc, kc)
    return o.reshape(out_dims), lse.reshape(out_dims)


def _mix_tail(x, pool_o, att_o, nattn_ref, wgate_ref, wbp_ref, wba_ref, wo_ref, nffn_ref, xo_ref, xn_ref):
    gates = jax.nn.sigmoid(_dot(_rms(x, nattn_ref[...]).astype(BF16), wgate_ref[...]))
    hp = _dot(pool_o.astype(BF16), wbp_ref[...])
    ha = _dot(att_o.astype(BF16), wba_ref[...])
    t = gates[:, 0:D_MODEL] * hp + gates[:, D_MODEL:2 * D_MODEL] * ha
    xo = x + _dot(t.astype(BF16), wo_ref[...])
    xo_ref[...] = xo
    xn_ref[...] = _rms(xo, nffn_ref[...]).astype(BF16)


def _group_linear(m, wpool_ref, pscale_ref):
    parts = [_dot(m[:, gi * POOL_GROUP:(gi + 1) * POOL_GROUP].astype(BF16), wpool_ref[gi])
             for gi in range(len(POOL_WINDOWS))]
    return jnp.concatenate(parts, axis=1) * pscale_ref[...]


def _read_classes(src_ref, dil, il_ref, slot):
    if dil == 1:
        return src_ref[0].astype(F32)
    rows = src_ref.shape[1]
    for r in range(dil):
        v = src_ref[r].astype(F32)
        for c in range(LANE_CHUNKS):
            il_ref[slot, c, pl.ds(r, rows, stride=dil), :] = v[:, c * LANES:(c + 1) * LANES]
    return jnp.concatenate([il_ref[slot, c] for c in range(LANE_CHUNKS)], axis=1)


def _merge_kernel(x_ref, u_ref, uh_ref, o0_ref, o1_ref, o2_ref, l0_ref, l1_ref, l2_ref,
                  wpool_ref, pscale_ref, nattn_ref, wgate_ref, wbp_ref, wba_ref, wo_ref, nffn_ref,
                  xo_ref, xn_ref, ext_ref, il_ref, *, tm, dils):
    si = pl.program_id(1)
    halo = POOL_STATE + 1
    ext_ref[0:halo, :] = jnp.where(si == 0, 0.0, uh_ref[...])
    ext_ref[halo:halo + tm, :] = u_ref[...]
    pos = si * tm + lax.broadcasted_iota(jnp.int32, (tm, 1), 0)
    parts = []
    for gi, win in enumerate(POOL_WINDOWS):
        cols = slice(gi * POOL_GROUP, (gi + 1) * POOL_GROUP)
        own = ext_ref[halo:halo + tm, cols]
        acc = own
        for back in range(1, win):
            acc = acc + ext_ref[halo - back:halo - back + tm, cols]
        inv = 1.0 / jnp.minimum(pos + 1, win).astype(F32)
        parts.append(acc * inv - own)
    pool_o = _group_linear(jnp.concatenate(parts, axis=1), wpool_ref, pscale_ref)

    slot = 0
    os_, ls_ = [], []
    for g, (o_ref, l_ref) in enumerate(((o0_ref, l0_ref), (o1_ref, l1_ref), (o2_ref, l2_ref))):
        os_.append(_read_classes(o_ref, dils[g], il_ref, slot))
        slot += dils[g] > 1
        ls_.append(_read_classes(l_ref, dils[g], il_ref, slot))
        slot += dils[g] > 1
    mx = jnp.maximum(ls_[0], jnp.maximum(ls_[1], ls_[2]))
    es = [jnp.exp(v - mx) for v in ls_]
    att_o = (es[0] * os_[0] + es[1] * os_[1] + es[2] * os_[2]) / (es[0] + es[1] + es[2])
    _mix_tail(x_ref[...], pool_o, att_o, nattn_ref, wgate_ref, wbp_ref, wba_ref, wo_ref, nffn_ref, xo_ref, xn_ref)


def _wspec(shape, layer, grid_rank):
    nd = len(shape)
    return _resident((None,) + tuple(shape), lambda *_: (layer,) + (0,) * nd)


def _tail_weight_specs(layer, grid_rank):
    return [_wspec((1, D_MODEL), layer, grid_rank), _wspec((D_MODEL, 2 * D_MODEL), layer, grid_rank),
            _wspec((POOL_W, D_MODEL), layer, grid_rank), _wspec((GROUP_W, D_MODEL), layer, grid_rank),
            _wspec((D_MODEL, D_MODEL), layer, grid_rank), _wspec((1, D_MODEL), layer, grid_rank)]


def _tail_weights(w):
    return (w["norm_attn"], w["w_gate"], w["w_bp"], w["w_ba"], w["w_o"], w["norm_ffn"])


def _merge(x, u, outs, lses, layer, w, batch, seq, tm, dils):
    n = x.shape[0]
    n_tiles = seq // tm
    halo = POOL_STATE + 1
    row = lambda width: pl.BlockSpec((tm, width), lambda b, s: (b * n_tiles + s, 0))
    cls = lambda g: pl.BlockSpec((None, dils[g], tm // dils[g], GROUP_W), lambda b, s: (b, 0, s, 0))
    n_il = 2 * sum(d > 1 for d in dils)
    return pl.pallas_call(
        functools.partial(_merge_kernel, tm=tm, dils=dils),
        grid=(batch, n_tiles),
        in_specs=[row(D_MODEL), row(POOL_W),
                  pl.BlockSpec((halo, POOL_W),
                               lambda b, s: (jnp.maximum((b * n_tiles + s) * (tm // halo) - 1, 0), 0))]
                 + [cls(g) for g in range(N_GROUPS)] * 2
                 + [_wspec((len(POOL_WINDOWS), POOL_GROUP, POOL_GROUP), layer, 2), _wspec((1, POOL_W), layer, 2)]
                 + _tail_weight_specs(layer, 2),
        out_specs=[row(D_MODEL), row(D_MODEL)],
        out_shape=[jax.ShapeDtypeStruct((n, D_MODEL), F32), jax.ShapeDtypeStruct((n, D_MODEL), BF16)],
        scratch_shapes=[pltpu.VMEM((halo + tm, POOL_W), F32), pltpu.VMEM((n_il, LANE_CHUNKS, tm, LANES), F32)],
        compiler_params=_params(("parallel", "parallel")),
        name="merge",
    )(x, u, u, *outs, *lses, w["w_pool"], w["pool_scale"], *_tail_weights(w))


def _head_mask():
    return jnp.asarray(np.arange(GROUP_W)[None, :] // HEAD_DIM == np.arange(HEADS)[:, None], F32)


def _sample_keys(dec_seq, cache_rows):
    slopes = _slopes()
    geo = []
    for gi, (win, dil) in enumerate(ATT_GROUPS):
        lc = cache_rows[gi]
        assert lc % dil == 0 and lc // dil == LANES
        cls = (lc + np.arange(dec_seq)) % dil
        n_cached = (cls.max() + 1) * LANES
        lane = np.arange(n_cached + LANES)
        sel = (np.arange(lc)[:, None] == (lane[:n_cached] % LANES) * dil + lane[:n_cached] // LANES)
        t = np.arange(dec_seq)[:, None]
        back = np.where(lane < n_cached, lc + t - ((lane % LANES) * dil + lane // LANES), t - (lane - n_cached))
        valid = (back >= 0) & (back % dil == 0) & (back // dil <= win // dil) & (lane < n_cached + dec_seq)
        bias = np.where(valid[:, None, :], -slopes[gi][None, :, None] * back[:, None, :], NEG_INF)
        geo.append((jnp.asarray(sel, BF16), jnp.asarray(bias.reshape(dec_seq * HEADS, -1), F32)))
    return geo


def _sample_mix_kernel(state_ref, u_ref, q0_ref, q1_ref, q2_ref, kn0_ref, kn1_ref, kn2_ref, c0_ref, c1_ref, c2_ref,
                       sel0_ref, sel1_ref, sel2_ref, bias0_ref, bias1_ref, bias2_ref, headmask_ref,
                       wpool_ref, pscale_ref, pool_ref, att_ref, newpool_ref, ext_ref, m_ref, *, dec_seq):
    ext_ref[...] = jnp.zeros_like(ext_ref)
    ext_ref[0:POOL_STATE, :] = state_ref[...]
    ext_ref[POOL_STATE:POOL_STATE + dec_seq, :] = u_ref[...]
    newpool_ref[...] = ext_ref[dec_seq:dec_seq + POOL_STATE, :]

    m_ref[...] = jnp.zeros_like(m_ref)
    for t in range(dec_seq):
        row = POOL_STATE + t
        for gi, win in enumerate(POOL_WINDOWS):
            cols = slice(gi * POOL_GROUP, (gi + 1) * POOL_GROUP)
            tot = jnp.sum(ext_ref[row - win + 1:row + 1, cols], axis=0, keepdims=True)
            cnt = float(min(PAST_LEN + t + 1, win))
            m_ref[t:t + 1, cols] = tot / cnt - ext_ref[row:row + 1, cols]
    pool_ref[...] = _group_linear(m_ref[...], wpool_ref, pscale_ref)[0:dec_seq]

    groups = ((q0_ref, kn0_ref, c0_ref, sel0_ref, bias0_ref), (q1_ref, kn1_ref, c1_ref, sel1_ref, bias1_ref),
              (q2_ref, kn2_ref, c2_ref, sel2_ref, bias2_ref))
    k_rows = HEADS * HEAD_DIM
    outs, lses = [], []
    for q_ref, kn_ref, c_ref, sel_ref, bias_ref in groups:
        flat = c_ref[...].reshape(2 * k_rows, c_ref.shape[-1]).astype(BF16)
        cached = _dot(flat, sel_ref[...]).astype(BF16)
        new = kn_ref[...].reshape(2 * k_rows, LANES).astype(BF16)
        keys = jnp.concatenate([cached[0:k_rows], new[0:k_rows]], axis=1)
        vals = jnp.concatenate([cached[k_rows:], new[k_rows:]], axis=1)
        s = _dot(q_ref[...], keys) + bias_ref[...]
        mx = jnp.max(s, axis=1, keepdims=True)
        p = jnp.exp(s - mx)
        l = jnp.sum(p, axis=1, keepdims=True)
        outs.append(lax.dot_general((p / l).astype(BF16), vals, (((1,), (1,)), ((), ())),
                                    preferred_element_type=F32))
        lses.append(mx + jnp.log(l))
    mx = jnp.maximum(lses[0], jnp.maximum(lses[1], lses[2]))
    es = [jnp.exp(v - mx) for v in lses]
    mixed = (es[0] * outs[0] + es[1] * outs[1] + es[2] * outs[2]) / (es[0] + es[1] + es[2])
    att_ref[...] = jnp.sum(mixed.reshape(dec_seq, HEADS, k_rows) * headmask_ref[...][None], axis=1)


def _sample_mix(state_pool, u, qs, kvn, caches, layer, w, dec_batch, dec_seq):
    cache_rows = [c.shape[2] for c in caches]
    geo = _sample_keys(dec_seq, cache_rows)
    consts = [g[i] for i in (0, 1) for g in geo] + [_head_mask()]
    kv_tail = (2, HEADS, HEAD_DIM)
    cviews = [jnp.transpose(c, (0, 1, 3, 4, 5, 2)).reshape((-1,) + kv_tail + (c.shape[2],)) for c in caches]
    cspecs = [pl.BlockSpec((None,) + kv_tail + (lc,), lambda b: (layer * dec_batch + b, 0, 0, 0, 0))
              for lc in cache_rows]
    per_batch = lambda *tail: pl.BlockSpec((None,) + tail, lambda b: (b,) + (0,) * len(tail))
    const = lambda shape: _resident(shape, lambda b: (0,) * len(shape))
    return pl.pallas_call(
        functools.partial(_sample_mix_kernel, dec_seq=dec_seq),
        grid=(dec_batch,),
        in_specs=[pl.BlockSpec((None, POOL_STATE, POOL_W), lambda b: (layer * dec_batch + b, 0, 0)),
                  per_batch(dec_seq, POOL_W)] + [per_batch(dec_seq * HEADS, GROUP_W)] * 3
                 + [per_batch(*kv_tail, LANES)] * 3 + cspecs + [const(c.shape) for c in consts]
                 + [_wspec((len(POOL_WINDOWS), POOL_GROUP, POOL_GROUP), layer, 1), _wspec((1, POOL_W), layer, 1)],
        out_specs=[per_batch(dec_seq, POOL_W), per_batch(dec_seq, GROUP_W),
                   pl.BlockSpec((None, POOL_STATE, POOL_W), lambda b: (b, 0, 0))],
        out_shape=[jax.ShapeDtypeStruct((dec_batch, dec_seq, POOL_W), F32),
                   jax.ShapeDtypeStruct((dec_batch, dec_seq, GROUP_W), F32),
                   jax.ShapeDtypeStruct((dec_batch, POOL_STATE, POOL_W), F32)],
        scratch_shapes=[pltpu.VMEM((POOL_STATE + dec_seq + 5, POOL_W), F32), pltpu.VMEM((8, POOL_W), F32)],
        compiler_params=_params(("parallel",)),
        name="sample_mix",
    )(state_pool.reshape(-1, POOL_STATE, POOL_W), u.reshape(dec_batch, dec_seq, POOL_W), *qs, *kvn, *cviews,
      *consts, w["w_pool"], w["pool_scale"])


def _tail_kernel(x_ref, pool_ref, att_ref, nattn_ref, wgate_ref, wbp_ref, wba_ref, wo_ref, nffn_ref, xo_ref, xn_ref):
    _mix_tail(x_ref[...], pool_ref[...], att_ref[...], nattn_ref, wgate_ref, wbp_ref, wba_ref, wo_ref, nffn_ref,
              xo_ref, xn_ref)


def _tail(x, pool_o, att_o, layer, w):
    n = x.shape[0]
    row = lambda width: pl.BlockSpec((n, width), lambda i: (0, 0))
    return pl.pallas_call(
        _tail_kernel,
        grid=(1,),
        in_specs=[row(D_MODEL), row(POOL_W), row(GROUP_W)] + _tail_weight_specs(layer, 1),
        out_specs=[row(D_MODEL), row(D_MODEL)],
        out_shape=[jax.ShapeDtypeStruct((n, D_MODEL), F32), jax.ShapeDtypeStruct((n, D_MODEL), BF16)],
        compiler_params=_params(("arbitrary",)),
        name="tail",
    )(x, pool_o, att_o, *_tail_weights(w))


def _swiglu_step(x, wg_ref, wu_ref, wd_ref, acc_ref):
    h = jax.nn.silu(_dot(x, wg_ref[...])) * _dot(x, wu_ref[...])
    acc_ref[...] += _dot(h.astype(BF16), wd_ref[...])


def _ffn_kernel(x_ref, xn_ref, w1_ref, w3_ref, w2_ref, y_ref, acc_ref):
    f = pl.program_id(1)

    @pl.when(f == 0)
    def _():
        acc_ref[...] = x_ref[...]

    _swiglu_step(xn_ref[...], w1_ref, w3_ref, w2_ref, acc_ref)

    @pl.when(f == pl.num_programs(1) - 1)
    def _():
        y_ref[...] = acc_ref[...]


def _ffn(x, xn, w1, w3, w2, idx, tm, tf):
    n = x.shape[0]
    d_ff = w1.shape[-1]
    return pl.pallas_call(
        _ffn_kernel,
        grid=(n // tm, d_ff // tf),
        in_specs=[pl.BlockSpec((tm, D_MODEL), lambda i, f: (i, 0)),
                  pl.BlockSpec((tm, D_MODEL), lambda i, f: (i, 0)),
                  pl.BlockSpec((None, D_MODEL, tf), lambda i, f: (idx, 0, f)),
                  pl.BlockSpec((None, D_MODEL, tf), lambda i, f: (idx, 0, f)),
                  pl.BlockSpec((None, tf, D_MODEL), lambda i, f: (idx, f, 0))],
        out_specs=pl.BlockSpec((tm, D_MODEL), lambda i, f: (i, 0)),
        out_shape=jax.ShapeDtypeStruct((n, D_MODEL), F32),
        scratch_shapes=[pltpu.VMEM((tm, D_MODEL), F32)],
        compiler_params=_params(("parallel", "arbitrary")),
        name="ffn",
    )(x, xn, w1, w3, w2)


SEG_ALIGN = 16
MOE_ROWS = 512


def _sorted_cap(ts):
    need = TOP_K * ts + N_EXPERTS * (SEG_ALIGN - 1)
    return -(-need // LANES) * LANES if ts >= LANES * 2 else -(-need // SEG_ALIGN) * SEG_ALIGN


def _segment_sizes(oh1, oh2):
    cnt1 = jnp.sum(oh1, axis=1, keepdims=True)
    cnt = cnt1 + jnp.sum(oh2, axis=1, keepdims=True)
    cpad = jnp.floor((cnt + (SEG_ALIGN - 1)) * (1.0 / SEG_ALIGN)) * SEG_ALIGN
    offs = [jnp.zeros((1, 1), F32)]
    for ei in range(1, N_EXPERTS):
        offs.append(offs[-1] + cpad[ei - 1:ei, :])
    return cnt1, cpad, jnp.concatenate(offs, axis=0)


def _gate_kernel(x_ref, g_ref, rwt_ref, rb_ref, route_ref, seg_ref, *, ts):
    xn = _rms(x_ref[...], g_ref[...]).astype(BF16)
    logits = lax.dot_general(rwt_ref[...], xn, (((1,), (1,)), ((), ())), preferred_element_type=F32) + rb_ref[...]
    row = lax.broadcasted_iota(jnp.int32, (N_EXPERTS, ts), 0)
    neg = jnp.float32(-jnp.inf)
    m1 = jnp.max(logits, axis=0, keepdims=True)
    i1 = jnp.min(jnp.where(logits == m1, row, N_EXPERTS), axis=0, keepdims=True)
    rest = jnp.where(row == i1, neg, logits)
    m2 = jnp.max(rest, axis=0, keepdims=True)
    i2 = jnp.min(jnp.where(rest == m2, row, N_EXPERTS), axis=0, keepdims=True)
    e = jnp.exp(m2 - m1)
    zeros = jnp.zeros((N_EXPERTS - 4, ts), F32)
    route_ref[...] = jnp.concatenate([i1.astype(F32), i2.astype(F32), 1.0 / (1.0 + e), e / (1.0 + e), zeros], axis=0)
    _, cpad, off = _segment_sizes(jnp.where(row == i1, 1.0, 0.0), jnp.where(row == i2, 1.0, 0.0))
    lane = lax.broadcasted_iota(jnp.int32, (N_EXPERTS, LANES), 1)
    seg_ref[...] = jnp.where(lane == 0, off, jnp.where(lane == 1, cpad, 0.0)).astype(jnp.int32)


def _gate(x, norm_ffn, rwt, rb, layer, idx, ts):
    n = x.shape[0]
    n_sub = n // ts
    return pl.pallas_call(
        functools.partial(_gate_kernel, ts=ts),
        grid=(n_sub,),
        in_specs=[pl.BlockSpec((ts, D_MODEL), lambda i: (i, 0)),
                  _resident((None, 1, D_MODEL), lambda i: (layer, 0, 0)),
                  _resident((None, N_EXPERTS, D_MODEL), lambda i: (idx, 0, 0)),
                  _resident((None, N_EXPERTS, 1), lambda i: (idx, 0, 0))],
        out_specs=[pl.BlockSpec((None, N_EXPERTS, ts), lambda i: (i, 0, 0)),
                   pl.BlockSpec((None, N_EXPERTS, LANES), lambda i: (i, 0, 0))],
        out_shape=[jax.ShapeDtypeStruct((n_sub, N_EXPERTS, ts), F32),
                   jax.ShapeDtypeStruct((n_sub, N_EXPERTS, LANES), jnp.int32)],
        compiler_params=_params(("parallel",)),
        name="gate",
    )(x, norm_ffn, rwt, rb)


def _segment_copies(i, n_bits, local_ref, off_ref, far_ref, far_rows_ref, nseg_ref, sem, to_far):
    copies = []
    for e in range(N_EXPERTS):
        j = i * N_EXPERTS + e
        n = nseg_ref[j]
        for b in reversed(range(n_bits)):
            size = SEG_ALIGN << b
            done = ((n >> (b + 1)) << (b + 1)) * SEG_ALIGN
            near = local_ref.at[pl.ds(pl.multiple_of(off_ref[j] + done, SEG_ALIGN), size)]
            far = far_ref.at[pl.ds(pl.multiple_of(far_rows_ref[j] + done, SEG_ALIGN), size)]
            cp = pltpu.make_async_copy(near, far, sem) if to_far else pltpu.make_async_copy(far, near, sem)
            copies.append((((n >> b) & 1) == 1, cp))
    return copies


def _scatter_kernel(off_ref, far_rows_ref, nseg_ref, x_ref, g_ref, route_ref, tri_ref, xb_in_ref,
                    meta_ref, xb_ref, xs_ref, sem, *, ts, cap, n_bits):
    del xb_in_ref
    i = pl.program_id(0)
    xn = _rms(x_ref[...], g_ref[...]).astype(BF16)
    row = lax.broadcasted_iota(jnp.int32, (N_EXPERTS, ts), 0).astype(F32)
    oh1 = jnp.where(row == route_ref[0:1, :], 1.0, 0.0)
    oh2 = jnp.where(row == route_ref[1:2, :], 1.0, 0.0)
    cnt1, _, off = _segment_sizes(oh1, oh2)
    pre1 = _dot(oh1.astype(BF16), tri_ref[...])
    pre2 = _dot(oh2.astype(BF16), tri_ref[...])
    d1 = jnp.sum(oh1 * (off + pre1), axis=0, keepdims=True)
    d2 = jnp.sum(oh2 * (off + cnt1 + pre2), axis=0, keepdims=True)
    slot = lax.broadcasted_iota(jnp.int32, (cap, ts), 0).astype(F32)
    p = jnp.where(slot == d1, 1.0, jnp.where(slot == d2, 1.0, 0.0)).astype(BF16)
    xs_ref[...] = _dot(p, xn).astype(BF16)
    meta_t = jnp.concatenate([d1, d2, route_ref[2:4, :], jnp.zeros((LANES - 4, ts), F32)], axis=0)
    meta_ref[...] = meta_t.T
    copies = _segment_copies(i, n_bits, xs_ref, off_ref, xb_ref, far_rows_ref, nseg_ref, sem, True)
    for pred, cp in copies:
        pl.when(pred)(cp.start)
    for pred, cp in copies:
        pl.when(pred)(cp.wait)


def _scatter(off, far_rows, nseg, x, norm_ffn, route, xb, layer, ts):
    n = x.shape[0]
    n_sub = n // ts
    cap = _sorted_cap(ts)
    tri = jnp.asarray(np.triu(np.ones((ts, ts), np.float32), 1), BF16)
    n_bits = int(TOP_K * ts // SEG_ALIGN).bit_length()
    return pl.pallas_call(
        functools.partial(_scatter_kernel, ts=ts, cap=cap, n_bits=n_bits),
        grid_spec=pltpu.PrefetchScalarGridSpec(
            num_scalar_prefetch=3,
            grid=(n_sub,),
            in_specs=[pl.BlockSpec((ts, D_MODEL), lambda i, *_: (i, 0)),
                      _resident((None, 1, D_MODEL), lambda i, *_: (layer, 0, 0)),
                      pl.BlockSpec((None, N_EXPERTS, ts), lambda i, *_: (i, 0, 0)),
                      _resident((ts, ts), lambda i, *_: (0, 0)),
                      pl.BlockSpec(memory_space=pl.ANY)],
            out_specs=[pl.BlockSpec((ts, LANES), lambda i, *_: (i, 0)),
                       pl.BlockSpec(memory_space=pl.ANY)],
            scratch_shapes=[pltpu.VMEM((cap, D_MODEL), BF16), pltpu.SemaphoreType.DMA(())]),
        out_shape=[jax.ShapeDtypeStruct((n, LANES), F32), jax.ShapeDtypeStruct(xb.shape, xb.dtype)],
        input_output_aliases={7: 1},
        compiler_params=_params(("arbitrary",)),
        name="scatter",
    )(off, far_rows, nseg, x, norm_ffn, route, tri, xb)


def _experts_kernel(blk_ref, nreal_ref, xb_ref, wg_ref, wu_ref, wd_ref, yb_ref, acc_ref):
    del blk_ref
    i = pl.program_id(0)
    f = pl.program_id(1)

    @pl.when(f == 0)
    def _():
        acc_ref[...] = jnp.zeros_like(acc_ref)

    @pl.when(i < nreal_ref[0])
    def _():
        _swiglu_step(xb_ref[...], wg_ref, wu_ref, wd_ref, acc_ref)

    @pl.when(f == pl.num_programs(1) - 1)
    def _():
        yb_ref[...] = acc_ref[...].astype(yb_ref.dtype)


def _experts(blk_e, nreal, xb, wg, wu, wd, idx, tf):
    n_blocks = blk_e.shape[0]
    d_ff = wg.shape[-1]
    n_f = d_ff // tf

    def live_block(i, nreal):
        return jnp.minimum(i, nreal[0] - 1)

    def wcol(i, f, blk, nreal):
        return idx, blk[live_block(i, nreal)], 0, jnp.where(i < nreal[0], f, n_f - 1)

    def wrow(i, f, blk, nreal):
        return idx, blk[live_block(i, nreal)], jnp.where(i < nreal[0], f, n_f - 1), 0

    return pl.pallas_call(
        _experts_kernel,
        grid_spec=pltpu.PrefetchScalarGridSpec(
            num_scalar_prefetch=2,
            grid=(n_blocks, n_f),
            in_specs=[pl.BlockSpec((MOE_ROWS, D_MODEL), lambda i, f, blk, nreal: (live_block(i, nreal), 0)),
                      pl.BlockSpec((None, None, D_MODEL, tf), wcol),
                      pl.BlockSpec((None, None, D_MODEL, tf), wcol),
                      pl.BlockSpec((None, None, tf, D_MODEL), wrow)],
            out_specs=pl.BlockSpec((MOE_ROWS, D_MODEL), lambda i, f, blk, nreal: (i, 0)),
            scratch_shapes=[pltpu.VMEM((MOE_ROWS, D_MODEL), F32)]),
        out_shape=jax.ShapeDtypeStruct((n_blocks * MOE_ROWS, D_MODEL), BF16),
        compiler_params=_params(("parallel", "arbitrary")),
        name="experts",
    )(blk_e, nreal, xb, wg, wu, wd)


def _combine_kernel(off_ref, far_rows_ref, nseg_ref, x_ref, meta_ref, yb_ref, o_ref, ys_ref, sem, *, ts, cap, n_bits):
    i = pl.program_id(0)
    ys_ref[...] = jnp.zeros_like(ys_ref)
    copies = _segment_copies(i, n_bits, ys_ref, off_ref, yb_ref, far_rows_ref, nseg_ref, sem, False)
    for pred, cp in copies:
        pl.when(pred)(cp.start)
    for pred, cp in copies:
        pl.when(pred)(cp.wait)
    slot = lax.broadcasted_iota(jnp.int32, (ts, cap), 1).astype(F32)
    ys = ys_ref[...]
    q1 = jnp.where(slot == meta_ref[:, 0:1], 1.0, 0.0).astype(BF16)
    q2 = jnp.where(slot == meta_ref[:, 1:2], 1.0, 0.0).astype(BF16)
    o_ref[...] = x_ref[...] + meta_ref[:, 2:3] * _dot(q1, ys) + meta_ref[:, 3:4] * _dot(q2, ys)


def _combine(off, far_rows, nseg, x, meta, yb, ts):
    n = x.shape[0]
    cap = _sorted_cap(ts)
    n_bits = int(TOP_K * ts // SEG_ALIGN).bit_length()
    return pl.pallas_call(
        functools.partial(_combine_kernel, ts=ts, cap=cap, n_bits=n_bits),
        grid_spec=pltpu.PrefetchScalarGridSpec(
            num_scalar_prefetch=3,
            grid=(n // ts,),
            in_specs=[pl.BlockSpec((ts, D_MODEL), lambda i, *_: (i, 0)),
                      pl.BlockSpec((ts, LANES), lambda i, *_: (i, 0)),
                      pl.BlockSpec(memory_space=pl.ANY)],
            out_specs=pl.BlockSpec((ts, D_MODEL), lambda i, *_: (i, 0)),
            scratch_shapes=[pltpu.VMEM((cap, D_MODEL), BF16), pltpu.SemaphoreType.DMA(())]),
        out_shape=jax.ShapeDtypeStruct((n, D_MODEL), F32),
        compiler_params=_params(("arbitrary",)),
        name="combine",
    )(off, far_rows, nseg, x, meta, yb)


def _moe(xs, tss, norm_ffn, rwt, rb, wg, wu, wd, layer, idx, tf):
    gated = [_gate(x, norm_ffn, rwt, rb, layer, idx, ts) for x, ts in zip(xs, tss)]
    off = jnp.concatenate([g[1][:, :, 0] for g in gated], axis=0)
    cpad = jnp.concatenate([g[1][:, :, 1] for g in gated], axis=0)
    n_subs = [g[1].shape[0] for g in gated]
    tot = jnp.sum(cpad, axis=0)
    padded = (tot + MOE_ROWS - 1) // MOE_ROWS * MOE_ROWS
    pend = jnp.cumsum(padded)
    within = jnp.cumsum(cpad, axis=0) - cpad
    off = off.astype(jnp.int32).reshape(-1)
    grouped_rows = ((pend - padded)[None, :] + within).astype(jnp.int32).reshape(-1)
    nseg = (cpad // SEG_ALIGN).astype(jnp.int32).reshape(-1)
    n_assign = TOP_K * sum(x.shape[0] for x in xs)
    n_blocks = (n_assign + (SEG_ALIGN - 1) * sum(n_subs) * N_EXPERTS + N_EXPERTS * (MOE_ROWS - 1)) // MOE_ROWS
    blk_e = jnp.minimum(jnp.sum(jnp.arange(n_blocks)[:, None] * MOE_ROWS >= pend[None, :], axis=1),
                        N_EXPERTS - 1).astype(jnp.int32)
    nreal = (pend[-1:] // MOE_ROWS).astype(jnp.int32)

    xb = jnp.zeros((n_blocks * MOE_ROWS, D_MODEL), BF16)
    metas, parts = [], []
    lo = 0
    for x, g, ns, ts in zip(xs, gated, n_subs, tss):
        sl = slice(lo * N_EXPERTS, (lo + ns) * N_EXPERTS)
        parts.append((off[sl], grouped_rows[sl], nseg[sl]))
        meta, xb = _scatter(*parts[-1], x, norm_ffn, g[0], xb, layer, ts)
        metas.append(meta)
        lo += ns
    yb = _experts(blk_e, nreal, xb, wg, wu, wd, idx, tf)
    return [_combine(*part, x, meta, yb, ts) for part, x, meta, ts in zip(parts, xs, metas, tss)]


def kernel(x_prompt, x_sample, state_pool, cache_kv_g0, cache_kv_g1, cache_kv_g2, norm_attn, w_in, q_norm, k_norm, w_pool, pool_scale, w_branch_pool, w_branch_attn, w_out, norm_ffn, w1_dense, w3_dense, w2_dense, router_w, router_b, we_gate, we_up, we_down):
    batch, seq, _ = x_prompt.shape
    dec_batch, dec_seq, _ = x_sample.shape
    depth = w_in.shape[0]
    caches = (cache_kv_g0, cache_kv_g1, cache_kv_g2)
    n_p, n_s = batch * seq, dec_batch * dec_seq
    tm_p = 512
    dils = tuple(d for _, d in ATT_GROUPS)
    keeps = tuple(min(win, seq) for win, _ in ATT_GROUPS)

    head_of = np.arange(GROUP_W) // HEAD_DIM
    same_head = (head_of[:, None] == head_of[None, :]).astype(np.float32)
    seg_mean = jnp.asarray(same_head / HEAD_DIM, BF16)
    w_qkv = w_in[:, :, :PROJ_W].astype(BF16)
    qgain = (jnp.tile(q_norm, (1, HEADS)) * (HEAD_DIM ** -0.5)).reshape(depth, 1, GROUP_W)
    kgain = jnp.tile(k_norm, (1, HEADS)).reshape(depth, 1, GROUP_W)
    norm_attn3 = norm_attn.reshape(depth, 1, D_MODEL)
    w = {"w_pool": w_pool.astype(BF16), "pool_scale": pool_scale.reshape(depth, 1, POOL_W),
         "norm_attn": norm_attn3, "w_gate": w_in[:, :, PROJ_W:].astype(BF16),
         "w_bp": w_branch_pool.astype(BF16), "w_ba": w_branch_attn.astype(BF16), "w_o": w_out.astype(BF16),
         "norm_ffn": norm_ffn.reshape(depth, 1, D_MODEL)}
    w1_b, w3_b, w2_b = w1_dense.astype(BF16), w3_dense.astype(BF16), w2_dense.astype(BF16)
    wg_b, wu_b, wd_b = we_gate.astype(BF16), we_up.astype(BF16), we_down.astype(BF16)
    rwt = jnp.swapaxes(router_w, 1, 2).astype(BF16)
    rb = router_b.reshape(-1, N_EXPERTS, 1)

    xp = x_prompt.reshape(n_p, D_MODEL)
    xs = x_sample.reshape(n_s, D_MODEL)
    pool_p, pool_s = [], []
    kv_s = [[] for _ in ATT_GROUPS]
    kvo_p = tuple(jnp.zeros((depth, batch, 2, GROUP_W, keep), F32) for keep in keeps)
    ones = (1,) * N_GROUPS
    for layer in range(depth):
        res = _proj(xp, layer, norm_attn3, w_qkv, qgain, kgain, seg_mean, batch, seq, tm_p, dils, keeps,
                    layer, depth, kvo_p, 1, True)
        u, qs, kcs, kvo_p = res[0], res[1:4], res[4:7], tuple(res[7:10])
        outs, lses = [], []
        for gi in range(N_GROUPS):
            o, lse = _attention(qs[gi], kcs[gi], gi)
            outs.append(o)
            lses.append(lse)
        pool_p.append(u.reshape(batch, seq, POOL_W)[:, seq - POOL_STATE:])
        xp, xpn = _merge(xp, u, outs, lses, layer, w, batch, seq, tm_p, dils)

        res = _proj(xs, layer, norm_attn3, w_qkv, qgain, kgain, seg_mean, 1, n_s, n_s, ones, (n_s,) * N_GROUPS,
                    0, 1, (), 3, False)
        u = res[0]
        qs = [(q.reshape(dec_batch, dec_seq, 1, GROUP_W) * _head_mask().astype(BF16))
              .reshape(dec_batch, dec_seq * HEADS, GROUP_W) for q in res[1:4]]
        kv_new = [kv.reshape(dec_batch, dec_seq, 2, HEADS, HEAD_DIM) for kv in res[7:10]]
        tokens_to_lanes = ((0, 0),) * 4 + ((0, LANES - dec_seq),)
        kvn = [jnp.pad(kv.transpose(0, 2, 3, 4, 1), tokens_to_lanes) for kv in kv_new]
        pool_o, att_o, new_pool = _sample_mix(state_pool, u, qs, kvn, caches, layer, w, dec_batch, dec_seq)
        att_o = att_o.reshape(n_s, GROUP_W)
        for gi in range(N_GROUPS):
            kv_s[gi].append(kv_new[gi])
        pool_s.append(new_pool)
        xs, xsn = _tail(xs, pool_o.reshape(n_s, POOL_W), att_o, layer, w)

        i = layer // 2
        if layer % 2 == 0:
            xp = _ffn(xp, xpn, w1_b, w3_b, w2_b, i, 512, w1_b.shape[-1] // 2)
            xs = _ffn(xs, xsn, w1_b, w3_b, w2_b, i, n_s, w1_b.shape[-1] // 2)
        else:
            xp, xs = _moe([xp, xs], [512, n_s], w["norm_ffn"], rwt, rb, wg_b, wu_b, wd_b, layer, i,
                           wg_b.shape[-1] // 2)
    kv_p = [kvo_p[g].reshape(depth, batch, 2, HEADS, HEAD_DIM, keeps[g]).transpose(0, 1, 5, 2, 3, 4)
            for g in range(N_GROUPS)]
    return (xp.reshape(batch, seq, D_MODEL), xs.reshape(dec_batch, dec_seq, D_MODEL),
            jnp.stack(pool_p), jnp.stack(pool_s),
            kv_p[0], jnp.stack(kv_s[0]),
            kv_p[1], jnp.stack(kv_s[1]),
            kv_p[2], jnp.stack(kv_s[2]))
```

```python
import functools

import numpy as np
import jax
import jax.numpy as jnp
from jax import lax
from jax.experimental import pallas as pl
from jax.experimental.pallas import tpu as pltpu

F32 = jnp.float32
BF16 = jnp.bfloat16

D_MODEL = 1024
PAST_LEN = 16384
POOL_WINDOWS = (2, 4, 8, 16)
POOL_GROUP = 128
POOL_W = 512
POOL_STATE = 15
ATT_GROUPS = ((128, 1), (512, 4), (2048, 16))
N_GROUPS = len(ATT_GROUPS)
HEAD_DIM = 64
HEADS = 8
GROUP_W = 512
QKV_W = 1536
Q_BLOCK = 128
ALIBI_MAX = 8.0
N_EXPERTS = 8
TOP_K = 2
RMS_EPS = 1e-6
NEG_INF = -1e30
LANES = 128
LANE_CHUNKS = GROUP_W // LANES
PROJ_W = POOL_W + 3 * QKV_W
VMEM_LIMIT = 56 * 1024 * 1024


def _slopes():
    i = np.arange(1, N_GROUPS * HEADS + 1, dtype=np.float32)
    return np.exp2(-ALIBI_MAX * i / (N_GROUPS * HEADS)).astype(np.float32).reshape(N_GROUPS, HEADS)


def _params(sem):
    return pltpu.CompilerParams(dimension_semantics=sem, vmem_limit_bytes=VMEM_LIMIT)


def _rms(x, gain):
    return x * lax.rsqrt(jnp.mean(x * x, axis=-1, keepdims=True) + RMS_EPS) * gain


def _dot(a, b):
    return jnp.dot(a, b, preferred_element_type=F32)


def _resident(shape, index_map):
    return pl.BlockSpec(shape, index_map, pipeline_mode=pl.Buffered(1))


def _write_classes(dst_ref, col0, val, dil, tmp_ref, slot):
    rows = val.shape[0] // dil
    cols = slice(col0, col0 + GROUP_W)
    if dil == 1:
        dst_ref[0, :, cols] = val.astype(dst_ref.dtype)
        return
    for c in range(LANE_CHUNKS):
        tmp_ref[slot, c] = val[:, c * LANES:(c + 1) * LANES]
    for r in range(dil):
        picked = [tmp_ref[slot, c, pl.ds(r, rows, stride=dil), :] for c in range(LANE_CHUNKS)]
        dst_ref[r, :, cols] = jnp.concatenate(picked, axis=1).astype(dst_ref.dtype)


def _proj_kernel(x_ref, g_ref, w_ref, qg_ref, kg_ref, seg_ref, *rest, tm, n_tiles, dils, keeps, n_alias,
                 norm_terms, kv_positions_minor):
    u_ref, q0_ref, q1_ref, q2_ref, kc0_ref, kc1_ref, kc2_ref, kvo0_ref, kvo1_ref, kvo2_ref, tmp_ref = rest[n_alias:]
    s = pl.program_id(1)
    xn = _rms(x_ref[...], g_ref[...]).astype(BF16)

    def zblk(j):
        return _dot(xn, w_ref[:, j * GROUP_W:(j + 1) * GROUP_W])

    def headnorm(z, gain):
        rem = z * z
        ms = None
        for _ in range(norm_terms):
            part = rem.astype(BF16)
            rem = rem - part.astype(F32)
            ms = _dot(part, seg_ref[...]) if ms is None else ms + _dot(part, seg_ref[...])
        return z * lax.rsqrt(ms + RMS_EPS) * gain

    u_ref[...] = zblk(0)
    q_refs = (q0_ref, q1_ref, q2_ref)
    kc_refs = (kc0_ref, kc1_ref, kc2_ref)
    kvo_refs = (kvo0_ref, kvo1_ref, kvo2_ref)
    slot = 0
    kept = {}
    for g in range(N_GROUPS):
        q = headnorm(zblk(1 + g), qg_ref[...])
        k = headnorm(zblk(1 + N_GROUPS + g), kg_ref[...])
        v = zblk(1 + 2 * N_GROUPS + g)
        for dst, col0, val in ((q_refs[g], 0, q), (kc_refs[g], 0, k), (kc_refs[g], GROUP_W, v)):
            _write_classes(dst, col0, val, dils[g], tmp_ref, slot % tmp_ref.shape[0])
            slot += dils[g] > 1
        kept.setdefault(n_tiles - max(keeps[g] // tm, 1), []).append((kvo_refs[g], min(keeps[g], tm), k, v))

    def write_kept(items):
        for ref, rows, k, v in items:
            if kv_positions_minor:
                ref[0] = k[tm - rows:, :].T
                ref[1] = v[tm - rows:, :].T
            else:
                ref[:, 0:GROUP_W] = k[tm - rows:, :]
                ref[:, GROUP_W:2 * GROUP_W] = v[tm - rows:, :]

    for first, items in kept.items():
        if first == 0:
            write_kept(items)
        else:
            pl.when(s >= first)(functools.partial(write_kept, items))


def _proj(x, layer, norm_attn, w_qkv, qgain, kgain, seg, batch, seq, tm, dils, keeps, out_layer, out_depth, prev_kvo,
          norm_terms, kv_positions_minor):
    n_tiles = seq // tm
    kvo_specs, kvo_shapes = [], []
    for g in range(N_GROUPS):
        keep = keeps[g]
        assert keep % tm == 0 or (keep < tm and keep % LANES == 0)
        first = n_tiles - max(keep // tm, 1)
        rows = min(keep, tm)
        if kv_positions_minor:
            kvo_specs.append(pl.BlockSpec((None, None, 2, GROUP_W, rows),
                                          lambda b, s, first=first: (out_layer, b, 0, 0, jnp.maximum(s - first, 0))))
            kvo_shapes.append(jax.ShapeDtypeStruct((out_depth, batch, 2, GROUP_W, keep), F32))
        else:
            kvo_specs.append(pl.BlockSpec((None, None, rows, 2 * GROUP_W),
                                          lambda b, s, first=first: (out_layer, b, jnp.maximum(s - first, 0), 0)))
            kvo_shapes.append(jax.ShapeDtypeStruct((out_depth, batch, keep, 2 * GROUP_W), F32))
    cls_spec = lambda g, width: pl.BlockSpec((None, dils[g], tm // dils[g], width), lambda b, s: (b, 0, s, 0))
    cls_shape = lambda g, width: jax.ShapeDtypeStruct((batch, dils[g], seq // dils[g], width), BF16)
    n_alias = len(prev_kvo)
    n_in = 6
    return pl.pallas_call(
        functools.partial(_proj_kernel, tm=tm, n_tiles=n_tiles, dils=dils, keeps=keeps, n_alias=n_alias,
                          norm_terms=norm_terms, kv_positions_minor=kv_positions_minor),
        grid=(batch, n_tiles),
        in_specs=[
            pl.BlockSpec((tm, D_MODEL), lambda b, s: (b * n_tiles + s, 0)),
            _resident((None, 1, D_MODEL), lambda b, s: (layer, 0, 0)),
            _resident((None, D_MODEL, PROJ_W), lambda b, s: (layer, 0, 0)),
            _resident((None, 1, GROUP_W), lambda b, s: (layer, 0, 0)),
            _resident((None, 1, GROUP_W), lambda b, s: (layer, 0, 0)),
            _resident((GROUP_W, GROUP_W), lambda b, s: (0, 0)),
        ] + [pl.BlockSpec(memory_space=pl.ANY)] * n_alias,
        out_specs=[pl.BlockSpec((tm, POOL_W), lambda b, s: (b * n_tiles + s, 0))]
                  + [cls_spec(g, GROUP_W) for g in range(N_GROUPS)]
                  + [cls_spec(g, 2 * GROUP_W) for g in range(N_GROUPS)] + kvo_specs,
        out_shape=[jax.ShapeDtypeStruct((batch * seq, POOL_W), F32)]
                  + [cls_shape(g, GROUP_W) for g in range(N_GROUPS)]
                  + [cls_shape(g, 2 * GROUP_W) for g in range(N_GROUPS)] + kvo_shapes,
        scratch_shapes=[pltpu.VMEM((3, LANE_CHUNKS, tm, LANES), F32)],
        input_output_aliases={n_in + g: 1 + 2 * N_GROUPS + g for g in range(n_alias)},
        compiler_params=_params(("parallel", "arbitrary")),
        name="proj",
    )(x, norm_attn, w_qkv, qgain, kgain, seg, *prev_kvo)


def _attn_kernel(q_ref, kvc_ref, kvh_ref, o_ref, lse_ref, kbuf, vbuf, *, slope_dil, tc, isolated_tiles):
    off = Q_BLOCK
    kbuf[0:off, :] = kvh_ref[:, 0:GROUP_W]
    vbuf[0:off, :] = kvh_ref[:, GROUP_W:2 * GROUP_W]
    kbuf[off:off + tc, :] = kvc_ref[:, 0:GROUP_W]
    vbuf[off:off + tc, :] = kvc_ref[:, GROUP_W:2 * GROUP_W]

    nk = Q_BLOCK + off
    kj = lax.broadcasted_iota(jnp.int32, (nk, Q_BLOCK), 0)
    qi = lax.broadcasted_iota(jnp.int32, (nk, Q_BLOCK), 1)
    dist = qi - kj + off
    distf = dist.astype(F32)
    maskneg = jnp.where((dist >= 0) & (dist <= Q_BLOCK), 0.0, NEG_INF).astype(F32)
    first = NEG_INF if isolated_tiles else jnp.where(pl.program_id(2) == 0, NEG_INF, 0.0).astype(F32)
    mask_first = maskneg + jnp.where(kj < Q_BLOCK, first, 0.0)

    for i in range(tc // Q_BLOCK):
        mask = mask_first if (isolated_tiles or i == 0) else maskneg
        rows = slice(i * Q_BLOCK, (i + 1) * Q_BLOCK)
        krows = slice(i * Q_BLOCK, i * Q_BLOCK + nk)
        for hp in range(HEADS // 2):
            outs, lses = [], []
            for h in (2 * hp, 2 * hp + 1):
                cols = slice(h * HEAD_DIM, (h + 1) * HEAD_DIM)
                s = lax.dot_general(kbuf[krows, cols], q_ref[rows, cols],
                                    (((1,), (1,)), ((), ())), preferred_element_type=F32)
                s = s - slope_dil[h] * distf + mask
                m = jnp.max(s, axis=0, keepdims=True)
                p = jnp.exp(s - m)
                l = jnp.sum(p, axis=0, keepdims=True)
                outs.append(lax.dot_general(vbuf[krows, cols], p.astype(BF16), (((0,), (0,)), ((), ())),
                                            preferred_element_type=F32) / l)
                lses.append(jnp.broadcast_to(m + jnp.log(l), (HEAD_DIM, Q_BLOCK)))
            pc = slice(hp * 2 * HEAD_DIM, (hp + 1) * 2 * HEAD_DIM)
            o_ref[rows, pc] = jnp.concatenate(outs, axis=0).T.astype(BF16)
            lse_ref[rows, pc] = jnp.concatenate(lses, axis=0).T


def _attention(q, kc, gi):
    win, dil = ATT_GROUPS[gi]
    assert win // dil == Q_BLOCK
    batch, _, l, _ = q.shape
    out_dims = (batch, dil, l, GROUP_W)
    isolated_tiles = l == Q_BLOCK
    slope_dil = tuple(float(s) * dil for s in _slopes()[gi])
    if isolated_tiles:
        q, kc = q.reshape(batch, 1, dil * l, GROUP_W), kc.reshape(batch, 1, dil * l, 2 * GROUP_W)
        dil, l = 1, dil * l
    tc = min(l, 512)
    nk = tc + Q_BLOCK
    o, lse = pl.pallas_call(
        functools.partial(_attn_kernel, slope_dil=slope_dil, tc=tc, isolated_tiles=isolated_tiles),
        grid=(batch, dil, l // tc),
        in_specs=[pl.BlockSpec((None, None, tc, GROUP_W), lambda b, r, c: (b, r, c, 0)),
                  pl.BlockSpec((None, None, tc, 2 * GROUP_W), lambda b, r, c: (b, r, c, 0)),
                  pl.BlockSpec((None, None, Q_BLOCK, 2 * GROUP_W),
                               lambda b, r, c: (b, r, jnp.maximum(c * (tc // Q_BLOCK) - 1, 0), 0))],
        out_specs=[pl.BlockSpec((None, None, tc, GROUP_W), lambda b, r, c: (b, r, c, 0))] * 2,
        out_shape=[jax.ShapeDtypeStruct((batch, dil, l, GROUP_W), BF16),
                   jax.ShapeDtypeStruct((batch, dil, l, GROUP_W), F32)],
        scratch_shapes=[pltpu.VMEM((nk, GROUP_W), BF16), pltpu.VMEM((nk, GROUP_W), BF16)],
        compiler_params=_params(("parallel", "parallel", "arbitrary")),
        name=f"attn_g{gi}",
    )(q, kc, kc)
    return o.reshape(out_dims), lse.reshape(out_dims)


def _mix_tail(x, pool_o, att_o, nattn_ref, wgate_ref, wbp_ref, wba_ref, wo_ref, nffn_ref, xo_ref, xn_ref):
    gates = jax.nn.sigmoid(_dot(_rms(x, nattn_ref[...]).astype(BF16), wgate_ref[...]))
    hp = _dot(pool_o.astype(BF16), wbp_ref[...])
    ha = _dot(att_o.astype(BF16), wba_ref[...])
    t = gates[:, 0:D_MODEL] * hp + gates[:, D_MODEL:2 * D_MODEL] * ha
    xo = x + _dot(t.astype(BF16), wo_ref[...])
    xo_ref[...] = xo
    xn_ref[...] = _rms(xo, nffn_ref[...]).astype(BF16)


def _group_linear(m, wpool_ref, pscale_ref):
    parts = [_dot(m[:, gi * POOL_GROUP:(gi + 1) * POOL_GROUP].astype(BF16), wpool_ref[gi])
             for gi in range(len(POOL_WINDOWS))]
    return jnp.concatenate(parts, axis=1) * pscale_ref[...]


def _read_classes(src_ref, dil, il_ref, slot):
    if dil == 1:
        return src_ref[0].astype(F32)
    rows = src_ref.shape[1]
    for r in range(dil):
        v = src_ref[r].astype(F32)
        for c in range(LANE_CHUNKS):
            il_ref[slot, c, pl.ds(r, rows, stride=dil), :] = v[:, c * LANES:(c + 1) * LANES]
    return jnp.concatenate([il_ref[slot, c] for c in range(LANE_CHUNKS)], axis=1)


def _merge_kernel(x_ref, u_ref, uh_ref, o0_ref, o1_ref, o2_ref, l0_ref, l1_ref, l2_ref,
                  wpool_ref, pscale_ref, nattn_ref, wgate_ref, wbp_ref, wba_ref, wo_ref, nffn_ref,
                  xo_ref, xn_ref, ext_ref, il_ref, *, tm, dils):
    si = pl.program_id(1)
    halo = POOL_STATE + 1
    ext_ref[0:halo, :] = jnp.where(si == 0, 0.0, uh_ref[...])
    ext_ref[halo:halo + tm, :] = u_ref[...]
    pos = si * tm + lax.broadcasted_iota(jnp.int32, (tm, 1), 0)
    parts = []
    for gi, win in enumerate(POOL_WINDOWS):
        cols = slice(gi * POOL_GROUP, (gi + 1) * POOL_GROUP)
        own = ext_ref[halo:halo + tm, cols]
        acc = own
        for back in range(1, win):
            acc = acc + ext_ref[halo - back:halo - back + tm, cols]
        inv = 1.0 / jnp.minimum(pos + 1, win).astype(F32)
        parts.append(acc * inv - own)
    pool_o = _group_linear(jnp.concatenate(parts, axis=1), wpool_ref, pscale_ref)

    slot = 0
    os_, ls_ = [], []
    for g, (o_ref, l_ref) in enumerate(((o0_ref, l0_ref), (o1_ref, l1_ref), (o2_ref, l2_ref))):
        os_.append(_read_classes(o_ref, dils[g], il_ref, slot))
        slot += dils[g] > 1
        ls_.append(_read_classes(l_ref, dils[g], il_ref, slot))
        slot += dils[g] > 1
    mx = jnp.maximum(ls_[0], jnp.maximum(ls_[1], ls_[2]))
    es = [jnp.exp(v - mx) for v in ls_]
    att_o = (es[0] * os_[0] + es[1] * os_[1] + es[2] * os_[2]) / (es[0] + es[1] + es[2])
    _mix_tail(x_ref[...], pool_o, att_o, nattn_ref, wgate_ref, wbp_ref, wba_ref, wo_ref, nffn_ref, xo_ref, xn_ref)


def _wspec(shape, layer, grid_rank):
    nd = len(shape)
    return _resident((None,) + tuple(shape), lambda *_: (layer,) + (0,) * nd)


def _tail_weight_specs(layer, grid_rank):
    return [_wspec((1, D_MODEL), layer, grid_rank), _wspec((D_MODEL, 2 * D_MODEL), layer, grid_rank),
            _wspec((POOL_W, D_MODEL), layer, grid_rank), _wspec((GROUP_W, D_MODEL), layer, grid_rank),
            _wspec((D_MODEL, D_MODEL), layer, grid_rank), _wspec((1, D_MODEL), layer, grid_rank)]


def _tail_weights(w):
    return (w["norm_attn"], w["w_gate"], w["w_bp"], w["w_ba"], w["w_o"], w["norm_ffn"])


def _merge(x, u, outs, lses, layer, w, batch, seq, tm, dils):
    n = x.shape[0]
    n_tiles = seq // tm
    halo = POOL_STATE + 1
    row = lambda width: pl.BlockSpec((tm, width), lambda b, s: (b * n_tiles + s, 0))
    cls = lambda g: pl.BlockSpec((None, dils[g], tm // dils[g], GROUP_W), lambda b, s: (b, 0, s, 0))
    n_il = 2 * sum(d > 1 for d in dils)
    return pl.pallas_call(
        functools.partial(_merge_kernel, tm=tm, dils=dils),
        grid=(batch, n_tiles),
        in_specs=[row(D_MODEL), row(POOL_W),
                  pl.BlockSpec((halo, POOL_W),
                               lambda b, s: (jnp.maximum((b * n_tiles + s) * (tm // halo) - 1, 0), 0))]
                 + [cls(g) for g in range(N_GROUPS)] * 2
                 + [_wspec((len(POOL_WINDOWS), POOL_GROUP, POOL_GROUP), layer, 2), _wspec((1, POOL_W), layer, 2)]
                 + _tail_weight_specs(layer, 2),
        out_specs=[row(D_MODEL), row(D_MODEL)],
        out_shape=[jax.ShapeDtypeStruct((n, D_MODEL), F32), jax.ShapeDtypeStruct((n, D_MODEL), BF16)],
        scratch_shapes=[pltpu.VMEM((halo + tm, POOL_W), F32), pltpu.VMEM((n_il, LANE_CHUNKS, tm, LANES), F32)],
        compiler_params=_params(("parallel", "parallel")),
        name="merge",
    )(x, u, u, *outs, *lses, w["w_pool"], w["pool_scale"], *_tail_weights(w))


def _head_mask():
    return jnp.asarray(np.arange(GROUP_W)[None, :] // HEAD_DIM == np.arange(HEADS)[:, None], F32)


def _sample_keys(dec_seq, cache_rows):
    slopes = _slopes()
    geo = []
    for gi, (win, dil) in enumerate(ATT_GROUPS):
        lc = cache_rows[gi]
        assert lc % dil == 0 and lc // dil == LANES
        cls = (lc + np.arange(dec_seq)) % dil
        n_cached = (cls.max() + 1) * LANES
        lane = np.arange(n_cached + LANES)
        sel = (np.arange(lc)[:, None] == (lane[:n_cached] % LANES) * dil + lane[:n_cached] // LANES)
        t = np.arange(dec_seq)[:, None]
        back = np.where(lane < n_cached, lc + t - ((lane % LANES) * dil + lane // LANES), t - (lane - n_cached))
        valid = (back >= 0) & (back % dil == 0) & (back // dil <= win // dil) & (lane < n_cached + dec_seq)
        bias = np.where(valid[:, None, :], -slopes[gi][None, :, None] * back[:, None, :], NEG_INF)
        geo.append((jnp.asarray(sel, BF16), jnp.asarray(bias.reshape(dec_seq * HEADS, -1), F32)))
    return geo


def _sample_mix_kernel(state_ref, u_ref, q0_ref, q1_ref, q2_ref, kn0_ref, kn1_ref, kn2_ref, c0_ref, c1_ref, c2_ref,
                       sel0_ref, sel1_ref, sel2_ref, bias0_ref, bias1_ref, bias2_ref, headmask_ref,
                       wpool_ref, pscale_ref, pool_ref, att_ref, newpool_ref, ext_ref, m_ref, *, dec_seq):
    ext_ref[...] = jnp.zeros_like(ext_ref)
    ext_ref[0:POOL_STATE, :] = state_ref[...]
    ext_ref[POOL_STATE:POOL_STATE + dec_seq, :] = u_ref[...]
    newpool_ref[...] = ext_ref[dec_seq:dec_seq + POOL_STATE, :]

    m_ref[...] = jnp.zeros_like(m_ref)
    for t in range(dec_seq):
        row = POOL_STATE + t
        for gi, win in enumerate(POOL_WINDOWS):
            cols = slice(gi * POOL_GROUP, (gi + 1) * POOL_GROUP)
            tot = jnp.sum(ext_ref[row - win + 1:row + 1, cols], axis=0, keepdims=True)
            cnt = float(min(PAST_LEN + t + 1, win))
            m_ref[t:t + 1, cols] = tot / cnt - ext_ref[row:row + 1, cols]
    pool_ref[...] = _group_linear(m_ref[...], wpool_ref, pscale_ref)[0:dec_seq]

    groups = ((q0_ref, kn0_ref, c0_ref, sel0_ref, bias0_ref), (q1_ref, kn1_ref, c1_ref, sel1_ref, bias1_ref),
              (q2_ref, kn2_ref, c2_ref, sel2_ref, bias2_ref))
    k_rows = HEADS * HEAD_DIM
    outs, lses = [], []
    for q_ref, kn_ref, c_ref, sel_ref, bias_ref in groups:
        flat = c_ref[...].reshape(2 * k_rows, c_ref.shape[-1]).astype(BF16)
        cached = _dot(flat, sel_ref[...]).astype(BF16)
        new = kn_ref[...].reshape(2 * k_rows, LANES).astype(BF16)
        keys = jnp.concatenate([cached[0:k_rows], new[0:k_rows]], axis=1)
        vals = jnp.concatenate([cached[k_rows:], new[k_rows:]], axis=1)
        s = _dot(q_ref[...], keys) + bias_ref[...]
        mx = jnp.max(s, axis=1, keepdims=True)
        p = jnp.exp(s - mx)
        l = jnp.sum(p, axis=1, keepdims=True)
        outs.append(lax.dot_general((p / l).astype(BF16), vals, (((1,), (1,)), ((), ())),
                                    preferred_element_type=F32))
        lses.append(mx + jnp.log(l))
    mx = jnp.maximum(lses[0], jnp.maximum(lses[1], lses[2]))
    es = [jnp.exp(v - mx) for v in lses]
    mixed = (es[0] * outs[0] + es[1] * outs[1] + es[2] * outs[2]) / (es[0] + es[1] + es[2])
    att_ref[...] = jnp.sum(mixed.reshape(dec_seq, HEADS, k_rows) * headmask_ref[...][None], axis=1)


def _sample_mix(state_pool, u, qs, kvn, caches, layer, w, dec_batch, dec_seq):
    cache_rows = [c.shape[2] for c in caches]
    geo = _sample_keys(dec_seq, cache_rows)
    consts = [g[i] for i in (0, 1) for g in geo] + [_head_mask()]
    kv_tail = (2, HEADS, HEAD_DIM)
    cviews = [jnp.transpose(c, (0, 1, 3, 4, 5, 2)).reshape((-1,) + kv_tail + (c.shape[2],)) for c in caches]
    cspecs = [pl.BlockSpec((None,) + kv_tail + (lc,), lambda b: (layer * dec_batch + b, 0, 0, 0, 0))
              for lc in cache_rows]
    per_batch = lambda *tail: pl.BlockSpec((None,) + tail, lambda b: (b,) + (0,) * len(tail))
    const = lambda shape: _resident(shape, lambda b: (0,) * len(shape))
    return pl.pallas_call(
        functools.partial(_sample_mix_kernel, dec_seq=dec_seq),
        grid=(dec_batch,),
        in_specs=[pl.BlockSpec((None, POOL_STATE, POOL_W), lambda b: (layer * dec_batch + b, 0, 0)),
                  per_batch(dec_seq, POOL_W)] + [per_batch(dec_seq * HEADS, GROUP_W)] * 3
                 + [per_batch(*kv_tail, LANES)] * 3 + cspecs + [const(c.shape) for c in consts]
                 + [_wspec((len(POOL_WINDOWS), POOL_GROUP, POOL_GROUP), layer, 1), _wspec((1, POOL_W), layer, 1)],
        out_specs=[per_batch(dec_seq, POOL_W), per_batch(dec_seq, GROUP_W),
                   pl.BlockSpec((None, POOL_STATE, POOL_W), lambda b: (b, 0, 0))],
        out_shape=[jax.ShapeDtypeStruct((dec_batch, dec_seq, POOL_W), F32),
                   jax.ShapeDtypeStruct((dec_batch, dec_seq, GROUP_W), F32),
                   jax.ShapeDtypeStruct((dec_batch, POOL_STATE, POOL_W), F32)],
        scratch_shapes=[pltpu.VMEM((POOL_STATE + dec_seq + 5, POOL_W), F32), pltpu.VMEM((8, POOL_W), F32)],
        compiler_params=_params(("parallel",)),
        name="sample_mix",
    )(state_pool.reshape(-1, POOL_STATE, POOL_W), u.reshape(dec_batch, dec_seq, POOL_W), *qs, *kvn, *cviews,
      *consts, w["w_pool"], w["pool_scale"])


def _tail_kernel(x_ref, pool_ref, att_ref, nattn_ref, wgate_ref, wbp_ref, wba_ref, wo_ref, nffn_ref, xo_ref, xn_ref):
    _mix_tail(x_ref[...], pool_ref[...], att_ref[...], nattn_ref, wgate_ref, wbp_ref, wba_ref, wo_ref, nffn_ref,
              xo_ref, xn_ref)


def _tail(x, pool_o, att_o, layer, w):
    n = x.shape[0]
    row = lambda width: pl.BlockSpec((n, width), lambda i: (0, 0))
    return pl.pallas_call(
        _tail_kernel,
        grid=(1,),
        in_specs=[row(D_MODEL), row(POOL_W), row(GROUP_W)] + _tail_weight_specs(layer, 1),
        out_specs=[row(D_MODEL), row(D_MODEL)],
        out_shape=[jax.ShapeDtypeStruct((n, D_MODEL), F32), jax.ShapeDtypeStruct((n, D_MODEL), BF16)],
        compiler_params=_params(("arbitrary",)),
        name="tail",
    )(x, pool_o, att_o, *_tail_weights(w))


def _swiglu_step(x, wg_ref, wu_ref, wd_ref, acc_ref):
    h = jax.nn.silu(_dot(x, wg_ref[...])) * _dot(x, wu_ref[...])
    acc_ref[...] += _dot(h.astype(BF16), wd_ref[...])


def _ffn_kernel(x_ref, xn_ref, w1_ref, w3_ref, w2_ref, y_ref, acc_ref):
    f = pl.program_id(1)

    @pl.when(f == 0)
    def _():
        acc_ref[...] = x_ref[...]

    _swiglu_step(xn_ref[...], w1_ref, w3_ref, w2_ref, acc_ref)

    @pl.when(f == pl.num_programs(1) - 1)
    def _():
        y_ref[...] = acc_ref[...]


def _ffn(x, xn, w1, w3, w2, idx, tm, tf):
    n = x.shape[0]
    d_ff = w1.shape[-1]
    return pl.pallas_call(
        _ffn_kernel,
        grid=(n // tm, d_ff // tf),
        in_specs=[pl.BlockSpec((tm, D_MODEL), lambda i, f: (i, 0)),
                  pl.BlockSpec((tm, D_MODEL), lambda i, f: (i, 0)),
                  pl.BlockSpec((None, D_MODEL, tf), lambda i, f: (idx, 0, f)),
                  pl.BlockSpec((None, D_MODEL, tf), lambda i, f: (idx, 0, f)),
                  pl.BlockSpec((None, tf, D_MODEL), lambda i, f: (idx, f, 0))],
        out_specs=pl.BlockSpec((tm, D_MODEL), lambda i, f: (i, 0)),
        out_shape=jax.ShapeDtypeStruct((n, D_MODEL), F32),
        scratch_shapes=[pltpu.VMEM((tm, D_MODEL), F32)],
        compiler_params=_params(("parallel", "arbitrary")),
        name="ffn",
    )(x, xn, w1, w3, w2)


SEG_ALIGN = 16
MOE_ROWS = 512


def _sorted_cap(ts):
    need = TOP_K * ts + N_EXPERTS * (SEG_ALIGN - 1)
    return -(-need // LANES) * LANES if ts >= LANES * 2 else -(-need // SEG_ALIGN) * SEG_ALIGN


def _segment_sizes(oh1, oh2):
    cnt1 = jnp.sum(oh1, axis=1, keepdims=True)
    cnt = cnt1 + jnp.sum(oh2, axis=1, keepdims=True)
    cpad = jnp.floor((cnt + (SEG_ALIGN - 1)) * (1.0 / SEG_ALIGN)) * SEG_ALIGN
    offs = [jnp.zeros((1, 1), F32)]
    for ei in range(1, N_EXPERTS):
        offs.append(offs[-1] + cpad[ei - 1:ei, :])
    return cnt1, cpad, jnp.concatenate(offs, axis=0)


def _gate_kernel(x_ref, g_ref, rwt_ref, rb_ref, route_ref, seg_ref, *, ts):
    xn = _rms(x_ref[...], g_ref[...]).astype(BF16)
    logits = lax.dot_general(rwt_ref[...], xn, (((1,), (1,)), ((), ())), preferred_element_type=F32) + rb_ref[...]
    row = lax.broadcasted_iota(jnp.int32, (N_EXPERTS, ts), 0)
    neg = jnp.float32(-jnp.inf)
    m1 = jnp.max(logits, axis=0, keepdims=True)
    i1 = jnp.min(jnp.where(logits == m1, row, N_EXPERTS), axis=0, keepdims=True)
    rest = jnp.where(row == i1, neg, logits)
    m2 = jnp.max(rest, axis=0, keepdims=True)
    i2 = jnp.min(jnp.where(rest == m2, row, N_EXPERTS), axis=0, keepdims=True)
    e = jnp.exp(m2 - m1)
    zeros = jnp.zeros((N_EXPERTS - 4, ts), F32)
    route_ref[...] = jnp.concatenate([i1.astype(F32), i2.astype(F32), 1.0 / (1.0 + e), e / (1.0 + e), zeros], axis=0)
    _, cpad, off = _segment_sizes(jnp.where(row == i1, 1.0, 0.0), jnp.where(row == i2, 1.0, 0.0))
    lane = lax.broadcasted_iota(jnp.int32, (N_EXPERTS, LANES), 1)
    seg_ref[...] = jnp.where(lane == 0, off, jnp.where(lane == 1, cpad, 0.0)).astype(jnp.int32)


def _gate(x, norm_ffn, rwt, rb, layer, idx, ts):
    n = x.shape[0]
    n_sub = n // ts
    return pl.pallas_call(
        functools.partial(_gate_kernel, ts=ts),
        grid=(n_sub,),
        in_specs=[pl.BlockSpec((ts, D_MODEL), lambda i: (i, 0)),
                  _resident((None, 1, D_MODEL), lambda i: (layer, 0, 0)),
                  _resident((None, N_EXPERTS, D_MODEL), lambda i: (idx, 0, 0)),
                  _resident((None, N_EXPERTS, 1), lambda i: (idx, 0, 0))],
        out_specs=[pl.BlockSpec((None, N_EXPERTS, ts), lambda i: (i, 0, 0)),
                   pl.BlockSpec((None, N_EXPERTS, LANES), lambda i: (i, 0, 0))],
        out_shape=[jax.ShapeDtypeStruct((n_sub, N_EXPERTS, ts), F32),
                   jax.ShapeDtypeStruct((n_sub, N_EXPERTS, LANES), jnp.int32)],
        compiler_params=_params(("parallel",)),
        name="gate",
    )(x, norm_ffn, rwt, rb)


def _segment_copies(i, n_bits, local_ref, off_ref, far_ref, far_rows_ref, nseg_ref, sem, to_far):
    copies = []
    for e in range(N_EXPERTS):
        j = i * N_EXPERTS + e
        n = nseg_ref[j]
        for b in reversed(range(n_bits)):
            size = SEG_ALIGN << b
            done = ((n >> (b + 1)) << (b + 1)) * SEG_ALIGN
            near = local_ref.at[pl.ds(pl.multiple_of(off_ref[j] + done, SEG_ALIGN), size)]
            far = far_ref.at[pl.ds(pl.multiple_of(far_rows_ref[j] + done, SEG_ALIGN), size)]
            cp = pltpu.make_async_copy(near, far, sem) if to_far else pltpu.make_async_copy(far, near, sem)
            copies.append((((n >> b) & 1) == 1, cp))
    return copies


def _scatter_kernel(off_ref, far_rows_ref, nseg_ref, x_ref, g_ref, route_ref, tri_ref, xb_in_ref,
                    meta_ref, xb_ref, xs_ref, sem, *, ts, cap, n_bits):
    del xb_in_ref
    i = pl.program_id(0)
    xn = _rms(x_ref[...], g_ref[...]).astype(BF16)
    row = lax.broadcasted_iota(jnp.int32, (N_EXPERTS, ts), 0).astype(F32)
    oh1 = jnp.where(row == route_ref[0:1, :], 1.0, 0.0)
    oh2 = jnp.where(row == route_ref[1:2, :], 1.0, 0.0)
    cnt1, _, off = _segment_sizes(oh1, oh2)
    pre1 = _dot(oh1.astype(BF16), tri_ref[...])
    pre2 = _dot(oh2.astype(BF16), tri_ref[...])
    d1 = jnp.sum(oh1 * (off + pre1), axis=0, keepdims=True)
    d2 = jnp.sum(oh2 * (off + cnt1 + pre2), axis=0, keepdims=True)
    slot = lax.broadcasted_iota(jnp.int32, (cap, ts), 0).astype(F32)
    p = jnp.where(slot == d1, 1.0, jnp.where(slot == d2, 1.0, 0.0)).astype(BF16)
    xs_ref[i % 2] = _dot(p, xn).astype(BF16)
    meta_t = jnp.concatenate([d1, d2, route_ref[2:4, :], jnp.zeros((LANES - 4, ts), F32)], axis=0)
    meta_ref[...] = meta_t.T

    def send(step, action):
        buf = step % 2
        for pred, cp in _segment_copies(step, n_bits, xs_ref.at[buf], off_ref, xb_ref, far_rows_ref, nseg_ref,
                                        sem.at[buf], True):
            pl.when(pred)(getattr(cp, action))

    send(i, "start")
    pl.when(i > 0)(lambda: send(i - 1, "wait"))
    pl.when(i == pl.num_programs(0) - 1)(lambda: send(i, "wait"))


def _scatter(off, far_rows, nseg, x, norm_ffn, route, xb, layer, ts):
    n = x.shape[0]
    n_sub = n // ts
    cap = _sorted_cap(ts)
    tri = jnp.asarray(np.triu(np.ones((ts, ts), np.float32), 1), BF16)
    n_bits = int(TOP_K * ts // SEG_ALIGN).bit_length()
    return pl.pallas_call(
        functools.partial(_scatter_kernel, ts=ts, cap=cap, n_bits=n_bits),
        grid_spec=pltpu.PrefetchScalarGridSpec(
            num_scalar_prefetch=3,
            grid=(n_sub,),
            in_specs=[pl.BlockSpec((ts, D_MODEL), lambda i, *_: (i, 0)),
                      _resident((None, 1, D_MODEL), lambda i, *_: (layer, 0, 0)),
                      pl.BlockSpec((None, N_EXPERTS, ts), lambda i, *_: (i, 0, 0)),
                      _resident((ts, ts), lambda i, *_: (0, 0)),
                      pl.BlockSpec(memory_space=pl.ANY)],
            out_specs=[pl.BlockSpec((ts, LANES), lambda i, *_: (i, 0)),
                       pl.BlockSpec(memory_space=pl.ANY)],
            scratch_shapes=[pltpu.VMEM((2, cap, D_MODEL), BF16), pltpu.SemaphoreType.DMA((2,))]),
        out_shape=[jax.ShapeDtypeStruct((n, LANES), F32), jax.ShapeDtypeStruct(xb.shape, xb.dtype)],
        input_output_aliases={7: 1},
        compiler_params=_params(("arbitrary",)),
        name="scatter",
    )(off, far_rows, nseg, x, norm_ffn, route, tri, xb)


def _experts_kernel(blk_ref, nreal_ref, xb_ref, wg_ref, wu_ref, wd_ref, yb_ref, acc_ref):
    del blk_ref
    i = pl.program_id(0)
    f = pl.program_id(1)

    @pl.when(f == 0)
    def _():
        acc_ref[...] = jnp.zeros_like(acc_ref)

    @pl.when(i < nreal_ref[0])
    def _():
        _swiglu_step(xb_ref[...], wg_ref, wu_ref, wd_ref, acc_ref)

    @pl.when(f == pl.num_programs(1) - 1)
    def _():
        yb_ref[...] = acc_ref[...].astype(yb_ref.dtype)


def _experts(blk_e, nreal, xb, wg, wu, wd, idx, tf):
    n_blocks = blk_e.shape[0]
    d_ff = wg.shape[-1]
    n_f = d_ff // tf

    def live_block(i, nreal):
        return jnp.minimum(i, nreal[0] - 1)

    def wcol(i, f, blk, nreal):
        return idx, blk[live_block(i, nreal)], 0, jnp.where(i < nreal[0], f, n_f - 1)

    def wrow(i, f, blk, nreal):
        return idx, blk[live_block(i, nreal)], jnp.where(i < nreal[0], f, n_f - 1), 0

    return pl.pallas_call(
        _experts_kernel,
        grid_spec=pltpu.PrefetchScalarGridSpec(
            num_scalar_prefetch=2,
            grid=(n_blocks, n_f),
            in_specs=[pl.BlockSpec((MOE_ROWS, D_MODEL), lambda i, f, blk, nreal: (live_block(i, nreal), 0)),
                      pl.BlockSpec((None, None, D_MODEL, tf), wcol),
                      pl.BlockSpec((None, None, D_MODEL, tf), wcol),
                      pl.BlockSpec((None, None, tf, D_MODEL), wrow)],
            out_specs=pl.BlockSpec((MOE_ROWS, D_MODEL), lambda i, f, blk, nreal: (i, 0)),
            scratch_shapes=[pltpu.VMEM((MOE_ROWS, D_MODEL), F32)]),
        out_shape=jax.ShapeDtypeStruct((n_blocks * MOE_ROWS, D_MODEL), BF16),
        compiler_params=_params(("parallel", "arbitrary")),
        name="experts",
    )(blk_e, nreal, xb, wg, wu, wd)


def _combine_kernel(off_ref, far_rows_ref, nseg_ref, x_ref, meta_ref, yb_ref, o_ref, ys_ref, sem, *, ts, cap, n_bits):
    i = pl.program_id(0)

    def fetch(step, action):
        buf = step % 2
        for pred, cp in _segment_copies(step, n_bits, ys_ref.at[buf], off_ref, yb_ref, far_rows_ref, nseg_ref,
                                        sem.at[buf], False):
            pl.when(pred)(getattr(cp, action))

    def start(step):
        ys_ref[step % 2] = jnp.zeros((cap, D_MODEL), BF16)
        fetch(step, "start")

    pl.when(i == 0)(lambda: start(i))
    pl.when(i + 1 < pl.num_programs(0))(lambda: start(i + 1))
    fetch(i, "wait")
    slot = lax.broadcasted_iota(jnp.int32, (ts, cap), 1).astype(F32)
    ys = ys_ref[i % 2]
    q1 = jnp.where(slot == meta_ref[:, 0:1], 1.0, 0.0).astype(BF16)
    q2 = jnp.where(slot == meta_ref[:, 1:2], 1.0, 0.0).astype(BF16)
    o_ref[...] = x_ref[...] + meta_ref[:, 2:3] * _dot(q1, ys) + meta_ref[:, 3:4] * _dot(q2, ys)


def _combine(off, far_rows, nseg, x, meta, yb, ts):
    n = x.shape[0]
    cap = _sorted_cap(ts)
    n_bits = int(TOP_K * ts // SEG_ALIGN).bit_length()
    return pl.pallas_call(
        functools.partial(_combine_kernel, ts=ts, cap=cap, n_bits=n_bits),
        grid_spec=pltpu.PrefetchScalarGridSpec(
            num_scalar_prefetch=3,
            grid=(n // ts,),
            in_specs=[pl.BlockSpec((ts, D_MODEL), lambda i, *_: (i, 0)),
                      pl.BlockSpec((ts, LANES), lambda i, *_: (i, 0)),
                      pl.BlockSpec(memory_space=pl.ANY)],
            out_specs=pl.BlockSpec((ts, D_MODEL), lambda i, *_: (i, 0)),
            scratch_shapes=[pltpu.VMEM((2, cap, D_MODEL), BF16), pltpu.SemaphoreType.DMA((2,))]),
        out_shape=jax.ShapeDtypeStruct((n, D_MODEL), F32),
        compiler_params=_params(("arbitrary",)),
        name="combine",
    )(off, far_rows, nseg, x, meta, yb)


def _moe(xs, tss, norm_ffn, rwt, rb, wg, wu, wd, layer, idx, tf):
    gated = [_gate(x, norm_ffn, rwt, rb, layer, idx, ts) for x, ts in zip(xs, tss)]
    off = jnp.concatenate([g[1][:, :, 0] for g in gated], axis=0)
    cpad = jnp.concatenate([g[1][:, :, 1] for g in gated], axis=0)
    n_subs = [g[1].shape[0] for g in gated]
    tot = jnp.sum(cpad, axis=0)
    padded = (tot + MOE_ROWS - 1) // MOE_ROWS * MOE_ROWS
    pend = jnp.cumsum(padded)
    within = jnp.cumsum(cpad, axis=0) - cpad
    off = off.astype(jnp.int32).reshape(-1)
    grouped_rows = ((pend - padded)[None, :] + within).astype(jnp.int32).reshape(-1)
    nseg = (cpad // SEG_ALIGN).astype(jnp.int32).reshape(-1)
    n_assign = TOP_K * sum(x.shape[0] for x in xs)
    n_blocks = (n_assign + (SEG_ALIGN - 1) * sum(n_subs) * N_EXPERTS + N_EXPERTS * (MOE_ROWS - 1)) // MOE_ROWS
    blk_e = jnp.minimum(jnp.sum(jnp.arange(n_blocks)[:, None] * MOE_ROWS >= pend[None, :], axis=1),
                        N_EXPERTS - 1).astype(jnp.int32)
    nreal = (pend[-1:] // MOE_ROWS).astype(jnp.int32)

    xb = jnp.zeros((n_blocks * MOE_ROWS, D_MODEL), BF16)
    metas, parts = [], []
    lo = 0
    for x, g, ns, ts in zip(xs, gated, n_subs, tss):
        sl = slice(lo * N_EXPERTS, (lo + ns) * N_EXPERTS)
        parts.append((off[sl], grouped_rows[sl], nseg[sl]))
        meta, xb = _scatter(*parts[-1], x, norm_ffn, g[0], xb, layer, ts)
        metas.append(meta)
        lo += ns
    yb = _experts(blk_e, nreal, xb, wg, wu, wd, idx, tf)
    return [_combine(*part, x, meta, yb, ts) for part, x, meta, ts in zip(parts, xs, metas, tss)]


def kernel(x_prompt, x_sample, state_pool, cache_kv_g0, cache_kv_g1, cache_kv_g2, norm_attn, w_in, q_norm, k_norm, w_pool, pool_scale, w_branch_pool, w_branch_attn, w_out, norm_ffn, w1_dense, w3_dense, w2_dense, router_w, router_b, we_gate, we_up, we_down):
    batch, seq, _ = x_prompt.shape
    dec_batch, dec_seq, _ = x_sample.shape
    depth = w_in.shape[0]
    caches = (cache_kv_g0, cache_kv_g1, cache_kv_g2)
    n_p, n_s = batch * seq, dec_batch * dec_seq
    tm_p = 512
    dils = tuple(d for _, d in ATT_GROUPS)
    keeps = tuple(min(win, seq) for win, _ in ATT_GROUPS)

    head_of = np.arange(GROUP_W) // HEAD_DIM
    same_head = (head_of[:, None] == head_of[None, :]).astype(np.float32)
    seg_mean = jnp.asarray(same_head / HEAD_DIM, BF16)
    w_qkv = w_in[:, :, :PROJ_W].astype(BF16)
    qgain = (jnp.tile(q_norm, (1, HEADS)) * (HEAD_DIM ** -0.5)).reshape(depth, 1, GROUP_W)
    kgain = jnp.tile(k_norm, (1, HEADS)).reshape(depth, 1, GROUP_W)
    norm_attn3 = norm_attn.reshape(depth, 1, D_MODEL)
    w = {"w_pool": w_pool.astype(BF16), "pool_scale": pool_scale.reshape(depth, 1, POOL_W),
         "norm_attn": norm_attn3, "w_gate": w_in[:, :, PROJ_W:].astype(BF16),
         "w_bp": w_branch_pool.astype(BF16), "w_ba": w_branch_attn.astype(BF16), "w_o": w_out.astype(BF16),
         "norm_ffn": norm_ffn.reshape(depth, 1, D_MODEL)}
    w1_b, w3_b, w2_b = w1_dense.astype(BF16), w3_dense.astype(BF16), w2_dense.astype(BF16)
    wg_b, wu_b, wd_b = we_gate.astype(BF16), we_up.astype(BF16), we_down.astype(BF16)
    rwt = jnp.swapaxes(router_w, 1, 2).astype(BF16)
    rb = router_b.reshape(-1, N_EXPERTS, 1)

    xp = x_prompt.reshape(n_p, D_MODEL)
    xs = x_sample.reshape(n_s, D_MODEL)
    pool_p, pool_s = [], []
    kv_s = [[] for _ in ATT_GROUPS]
    kvo_p = tuple(jnp.zeros((depth, batch, 2, GROUP_W, keep), F32) for keep in keeps)
    ones = (1,) * N_GROUPS
    for layer in range(depth):
        res = _proj(xp, layer, norm_attn3, w_qkv, qgain, kgain, seg_mean, batch, seq, tm_p, dils, keeps,
                    layer, depth, kvo_p, 1, True)
        u, qs, kcs, kvo_p = res[0], res[1:4], res[4:7], tuple(res[7:10])
        outs, lses = [], []
        for gi in range(N_GROUPS):
            o, lse = _attention(qs[gi], kcs[gi], gi)
            outs.append(o)
            lses.append(lse)
        pool_p.append(u.reshape(batch, seq, POOL_W)[:, seq - POOL_STATE:])
        xp, xpn = _merge(xp, u, outs, lses, layer, w, batch, seq, tm_p, dils)

        res = _proj(xs, layer, norm_attn3, w_qkv, qgain, kgain, seg_mean, 1, n_s, n_s, ones, (n_s,) * N_GROUPS,
                    0, 1, (), 3, False)
        u = res[0]
        qs = [(q.reshape(dec_batch, dec_seq, 1, GROUP_W) * _head_mask().astype(BF16))
              .reshape(dec_batch, dec_seq * HEADS, GROUP_W) for q in res[1:4]]
        kv_new = [kv.reshape(dec_batch, dec_seq, 2, HEADS, HEAD_DIM) for kv in res[7:10]]
        tokens_to_lanes = ((0, 0),) * 4 + ((0, LANES - dec_seq),)
        kvn = [jnp.pad(kv.transpose(0, 2, 3, 4, 1), tokens_to_lanes) for kv in kv_new]
        pool_o, att_o, new_pool = _sample_mix(state_pool, u, qs, kvn, caches, layer, w, dec_batch, dec_seq)
        att_o = att_o.reshape(n_s, GROUP_W)
        for gi in range(N_GROUPS):
            kv_s[gi].append(kv_new[gi])
        pool_s.append(new_pool)
        xs, xsn = _tail(xs, pool_o.reshape(n_s, POOL_W), att_o, layer, w)

        i = layer // 2
        if layer % 2 == 0:
            xp = _ffn(xp, xpn, w1_b, w3_b, w2_b, i, 512, w1_b.shape[-1] // 2)
            xs = _ffn(xs, xsn, w1_b, w3_b, w2_b, i, n_s, w1_b.shape[-1] // 2)
        else:
            xp, xs = _moe([xp, xs], [512, n_s], w["norm_ffn"], rwt, rb, wg_b, wu_b, wd_b, layer, i,
                           wg_b.shape[-1] // 2)
    kv_p = [kvo_p[g].reshape(depth, batch, 2, HEADS, HEAD_DIM, keeps[g]).transpose(0, 1, 5, 2, 3, 4)
            for g in range(N_GROUPS)]
    return (xp.reshape(batch, seq, D_MODEL), xs.reshape(dec_batch, dec_seq, D_MODEL),
            jnp.stack(pool_p), jnp.stack(pool_s),
            kv_p[0], jnp.stack(kv_s[0]),
            kv_p[1], jnp.stack(kv_s[1]),
            kv_p[2], jnp.stack(kv_s[2]))
```

```python
import functools

import numpy as np
import jax
import jax.numpy as jnp
from jax import lax
from jax.experimental import pallas as pl
from jax.experimental.pallas import tpu as pltpu

F32 = jnp.float32
BF16 = jnp.bfloat16

D_MODEL = 1024
PAST_LEN = 16384
POOL_WINDOWS = (2, 4, 8, 16)
POOL_GROUP = 128
POOL_W = 512
POOL_STATE = 15
ATT_GROUPS = ((128, 1), (512, 4), (2048, 16))
N_GROUPS = len(ATT_GROUPS)
HEAD_DIM = 64
HEADS = 8
GROUP_W = 512
QKV_W = 1536
Q_BLOCK = 128
ALIBI_MAX = 8.0
N_EXPERTS = 8
TOP_K = 2
RMS_EPS = 1e-6
NEG_INF = -1e30
LANES = 128
LANE_CHUNKS = GROUP_W // LANES
PROJ_W = POOL_W + 3 * QKV_W
VMEM_LIMIT = 56 * 1024 * 1024


def _slopes():
    i = np.arange(1, N_GROUPS * HEADS + 1, dtype=np.float32)
    return np.exp2(-ALIBI_MAX * i / (N_GROUPS * HEADS)).astype(np.float32).reshape(N_GROUPS, HEADS)


def _params(sem):
    return pltpu.CompilerParams(dimension_semantics=sem, vmem_limit_bytes=VMEM_LIMIT)


def _rms(x, gain):
    return x * lax.rsqrt(jnp.mean(x * x, axis=-1, keepdims=True) + RMS_EPS) * gain


def _dot(a, b):
    return jnp.dot(a, b, preferred_element_type=F32)


def _resident(shape, index_map):
    return pl.BlockSpec(shape, index_map, pipeline_mode=pl.Buffered(1))


def _write_classes(dst_ref, col0, val, dil, tmp_ref, slot):
    rows = val.shape[0] // dil
    cols = slice(col0, col0 + GROUP_W)
    if dil == 1:
        dst_ref[0, :, cols] = val.astype(dst_ref.dtype)
        return
    for c in range(LANE_CHUNKS):
        tmp_ref[slot, c] = val[:, c * LANES:(c + 1) * LANES]
    for r in range(dil):
        picked = [tmp_ref[slot, c, pl.ds(r, rows, stride=dil), :] for c in range(LANE_CHUNKS)]
        dst_ref[r, :, cols] = jnp.concatenate(picked, axis=1).astype(dst_ref.dtype)


def _proj_kernel(x_ref, g_ref, w_ref, qg_ref, kg_ref, seg_ref, *rest, tm, n_tiles, dils, keeps, n_alias,
                 norm_terms, kv_positions_minor):
    u_ref, q0_ref, q1_ref, q2_ref, kc0_ref, kc1_ref, kc2_ref, kvo0_ref, kvo1_ref, kvo2_ref, tmp_ref = rest[n_alias:]
    s = pl.program_id(1)
    xn = _rms(x_ref[...], g_ref[...]).astype(BF16)

    def zblk(j):
        return _dot(xn, w_ref[:, j * GROUP_W:(j + 1) * GROUP_W])

    def headnorm(z, gain):
        rem = z * z
        ms = None
        for _ in range(norm_terms):
            part = rem.astype(BF16)
            rem = rem - part.astype(F32)
            ms = _dot(part, seg_ref[...]) if ms is None else ms + _dot(part, seg_ref[...])
        return z * lax.rsqrt(ms + RMS_EPS) * gain

    u_ref[...] = zblk(0)
    q_refs = (q0_ref, q1_ref, q2_ref)
    kc_refs = (kc0_ref, kc1_ref, kc2_ref)
    kvo_refs = (kvo0_ref, kvo1_ref, kvo2_ref)
    slot = 0
    kept = {}
    for g in range(N_GROUPS):
        q = headnorm(zblk(1 + g), qg_ref[...])
        k = headnorm(zblk(1 + N_GROUPS + g), kg_ref[...])
        v = zblk(1 + 2 * N_GROUPS + g)
        for dst, col0, val in ((q_refs[g], 0, q), (kc_refs[g], 0, k), (kc_refs[g], GROUP_W, v)):
            _write_classes(dst, col0, val, dils[g], tmp_ref, slot % tmp_ref.shape[0])
            slot += dils[g] > 1
        kept.setdefault(n_tiles - max(keeps[g] // tm, 1), []).append((kvo_refs[g], min(keeps[g], tm), k, v))

    def write_kept(items):
        for ref, rows, k, v in items:
            if kv_positions_minor:
                ref[0] = k[tm - rows:, :].T
                ref[1] = v[tm - rows:, :].T
            else:
                ref[:, 0:GROUP_W] = k[tm - rows:, :]
                ref[:, GROUP_W:2 * GROUP_W] = v[tm - rows:, :]

    for first, items in kept.items():
        if first == 0:
            write_kept(items)
        else:
            pl.when(s >= first)(functools.partial(write_kept, items))


def _proj(x, layer, norm_attn, w_qkv, qgain, kgain, seg, batch, seq, tm, dils, keeps, out_layer, out_depth, prev_kvo,
          norm_terms, kv_positions_minor):
    n_tiles = seq // tm
    kvo_specs, kvo_shapes = [], []
    for g in range(N_GROUPS):
        keep = keeps[g]
        assert keep % tm == 0 or (keep < tm and keep % LANES == 0)
        first = n_tiles - max(keep // tm, 1)
        rows = min(keep, tm)
        if kv_positions_minor:
            kvo_specs.append(pl.BlockSpec((None, None, 2, GROUP_W, rows),
                                          lambda b, s, first=first: (out_layer, b, 0, 0, jnp.maximum(s - first, 0))))
            kvo_shapes.append(jax.ShapeDtypeStruct((out_depth, batch, 2, GROUP_W, keep), F32))
        else:
            kvo_specs.append(pl.BlockSpec((None, None, rows, 2 * GROUP_W),
                                          lambda b, s, first=first: (out_layer, b, jnp.maximum(s - first, 0), 0)))
            kvo_shapes.append(jax.ShapeDtypeStruct((out_depth, batch, keep, 2 * GROUP_W), F32))
    cls_spec = lambda g, width: pl.BlockSpec((None, dils[g], tm // dils[g], width), lambda b, s: (b, 0, s, 0))
    cls_shape = lambda g, width: jax.ShapeDtypeStruct((batch, dils[g], seq // dils[g], width), BF16)
    n_alias = len(prev_kvo)
    n_in = 6
    return pl.pallas_call(
        functools.partial(_proj_kernel, tm=tm, n_tiles=n_tiles, dils=dils, keeps=keeps, n_alias=n_alias,
                          norm_terms=norm_terms, kv_positions_minor=kv_positions_minor),
        grid=(batch, n_tiles),
        in_specs=[
            pl.BlockSpec((tm, D_MODEL), lambda b, s: (b * n_tiles + s, 0)),
            _resident((None, 1, D_MODEL), lambda b, s: (layer, 0, 0)),
            _resident((None, D_MODEL, PROJ_W), lambda b, s: (layer, 0, 0)),
            _resident((None, 1, GROUP_W), lambda b, s: (layer, 0, 0)),
            _resident((None, 1, GROUP_W), lambda b, s: (layer, 0, 0)),
            _resident((GROUP_W, GROUP_W), lambda b, s: (0, 0)),
        ] + [pl.BlockSpec(memory_space=pl.ANY)] * n_alias,
        out_specs=[pl.BlockSpec((tm, POOL_W), lambda b, s: (b * n_tiles + s, 0))]
                  + [cls_spec(g, GROUP_W) for g in range(N_GROUPS)]
                  + [cls_spec(g, 2 * GROUP_W) for g in range(N_GROUPS)] + kvo_specs,
        out_shape=[jax.ShapeDtypeStruct((batch * seq, POOL_W), F32)]
                  + [cls_shape(g, GROUP_W) for g in range(N_GROUPS)]
                  + [cls_shape(g, 2 * GROUP_W) for g in range(N_GROUPS)] + kvo_shapes,
        scratch_shapes=[pltpu.VMEM((3, LANE_CHUNKS, tm, LANES), F32)],
        input_output_aliases={n_in + g: 1 + 2 * N_GROUPS + g for g in range(n_alias)},
        compiler_params=_params(("parallel", "arbitrary")),
        name="proj",
    )(x, norm_attn, w_qkv, qgain, kgain, seg, *prev_kvo)


def _attn_kernel(q_ref, kvc_ref, kvh_ref, o_ref, lse_ref, kbuf, vbuf, *, slope_dil, tc, isolated_tiles):
    off = Q_BLOCK
    kbuf[0:off, :] = kvh_ref[:, 0:GROUP_W]
    vbuf[0:off, :] = kvh_ref[:, GROUP_W:2 * GROUP_W]
    kbuf[off:off + tc, :] = kvc_ref[:, 0:GROUP_W]
    vbuf[off:off + tc, :] = kvc_ref[:, GROUP_W:2 * GROUP_W]

    nk = Q_BLOCK + off
    kj = lax.broadcasted_iota(jnp.int32, (nk, Q_BLOCK), 0)
    qi = lax.broadcasted_iota(jnp.int32, (nk, Q_BLOCK), 1)
    dist = qi - kj + off
    distf = dist.astype(F32)
    maskneg = jnp.where((dist >= 0) & (dist <= Q_BLOCK), 0.0, NEG_INF).astype(F32)
    first = NEG_INF if isolated_tiles else jnp.where(pl.program_id(2) == 0, NEG_INF, 0.0).astype(F32)
    mask_first = maskneg + jnp.where(kj < Q_BLOCK, first, 0.0)

    for i in range(tc // Q_BLOCK):
        mask = mask_first if (isolated_tiles or i == 0) else maskneg
        rows = slice(i * Q_BLOCK, (i + 1) * Q_BLOCK)
        krows = slice(i * Q_BLOCK, i * Q_BLOCK + nk)
        for hp in range(HEADS // 2):
            outs, lses = [], []
            for h in (2 * hp, 2 * hp + 1):
                cols = slice(h * HEAD_DIM, (h + 1) * HEAD_DIM)
                s = lax.dot_general(kbuf[krows, cols], q_ref[rows, cols],
                                    (((1,), (1,)), ((), ())), preferred_element_type=F32)
                s = s - slope_dil[h] * distf + mask
                m = jnp.max(s, axis=0, keepdims=True)
                p = jnp.exp(s - m)
                l = jnp.sum(p, axis=0, keepdims=True)
                outs.append(lax.dot_general(vbuf[krows, cols], p.astype(BF16), (((0,), (0,)), ((), ())),
                                            preferred_element_type=F32) / l)
                lses.append(jnp.broadcast_to(m + jnp.log(l), (HEAD_DIM, Q_BLOCK)))
            pc = slice(hp * 2 * HEAD_DIM, (hp + 1) * 2 * HEAD_DIM)
            o_ref[rows, pc] = jnp.concatenate(outs, axis=0).T.astype(BF16)
            lse_ref[rows, pc] = jnp.concatenate(lses, axis=0).T


def _attention(q, kc, gi):
    win, dil = ATT_GROUPS[gi]
    assert win // dil == Q_BLOCK
    batch, _, l, _ = q.shape
    out_dims = (batch, dil, l, GROUP_W)
    isolated_tiles = l == Q_BLOCK
    slope_dil = tuple(float(s) * dil for s in _slopes()[gi])
    if isolated_tiles:
        q, kc = q.reshape(batch, 1, dil * l, GROUP_W), kc.reshape(batch, 1, dil * l, 2 * GROUP_W)
        dil, l = 1, dil * l
    tc = min(l, 512)
    nk = tc + Q_BLOCK
    o, lse = pl.pallas_call(
        functools.partial(_attn_kernel, slope_dil=slope_dil, tc=tc, isolated_tiles=isolated_tiles),
        grid=(batch, dil, l // tc),
        in_specs=[pl.BlockSpec((None, None, tc, GROUP_W), lambda b, r, c: (b, r, c, 0)),
                  pl.BlockSpec((None, None, tc, 2 * GROUP_W), lambda b, r, c: (b, r, c, 0)),
                  pl.BlockSpec((None, None, Q_BLOCK, 2 * GROUP_W),
                               lambda b, r, c: (b, r, jnp.maximum(c * (tc // Q_BLOCK) - 1, 0), 0))],
        out_specs=[pl.BlockSpec((None, None, tc, GROUP_W), lambda b, r, c: (b, r, c, 0))] * 2,
        out_shape=[jax.ShapeDtypeStruct((batch, dil, l, GROUP_W), BF16),
                   jax.ShapeDtypeStruct((batch, dil, l, GROUP_W), F32)],
        scratch_shapes=[pltpu.VMEM((nk, GROUP_W), BF16), pltpu.VMEM((nk, GROUP_W), BF16)],
        compiler_params=_params(("parallel", "parallel", "arbitrary")),
        name=f"attn_g{gi}",
    )(q, kc, kc)
    return o.reshape(out_dims), lse.reshape(out_dims)


def _mix_tail(x, pool_o, att_o, nattn_ref, wgate_ref, wbp_ref, wba_ref, wo_ref, nffn_ref, xo_ref, xn_ref):
    gates = jax.nn.sigmoid(_dot(_rms(x, nattn_ref[...]).astype(BF16), wgate_ref[...]))
    hp = _dot(pool_o.astype(BF16), wbp_ref[...])
    ha = _dot(att_o.astype(BF16), wba_ref[...])
    t = gates[:, 0:D_MODEL] * hp + gates[:, D_MODEL:2 * D_MODEL] * ha
    xo = x + _dot(t.astype(BF16), wo_ref[...])
    xo_ref[...] = xo
    xn_ref[...] = _rms(xo, nffn_ref[...]).astype(BF16)


def _group_linear(m, wpool_ref, pscale_ref):
    parts = [_dot(m[:, gi * POOL_GROUP:(gi + 1) * POOL_GROUP].astype(BF16), wpool_ref[gi])
             for gi in range(len(POOL_WINDOWS))]
    return jnp.concatenate(parts, axis=1) * pscale_ref[...]


def _read_classes(src_ref, dil, il_ref, slot):
    if dil == 1:
        return src_ref[0].astype(F32)
    rows = src_ref.shape[1]
    for r in range(dil):
        v = src_ref[r].astype(F32)
        for c in range(LANE_CHUNKS):
            il_ref[slot, c, pl.ds(r, rows, stride=dil), :] = v[:, c * LANES:(c + 1) * LANES]
    return jnp.concatenate([il_ref[slot, c] for c in range(LANE_CHUNKS)], axis=1)


def _merge_kernel(x_ref, u_ref, uh_ref, o0_ref, o1_ref, o2_ref, l0_ref, l1_ref, l2_ref,
                  wpool_ref, pscale_ref, nattn_ref, wgate_ref, wbp_ref, wba_ref, wo_ref, nffn_ref,
                  xo_ref, xn_ref, ext_ref, il_ref, *, tm, dils):
    si = pl.program_id(1)
    halo = POOL_STATE + 1
    ext_ref[0:halo, :] = jnp.where(si == 0, 0.0, uh_ref[...])
    ext_ref[halo:halo + tm, :] = u_ref[...]
    pos = si * tm + lax.broadcasted_iota(jnp.int32, (tm, 1), 0)
    parts = []
    for gi, win in enumerate(POOL_WINDOWS):
        cols = slice(gi * POOL_GROUP, (gi + 1) * POOL_GROUP)
        own = ext_ref[halo:halo + tm, cols]
        acc = own
        for back in range(1, win):
            acc = acc + ext_ref[halo - back:halo - back + tm, cols]
        inv = 1.0 / jnp.minimum(pos + 1, win).astype(F32)
        parts.append(acc * inv - own)
    pool_o = _group_linear(jnp.concatenate(parts, axis=1), wpool_ref, pscale_ref)

    slot = 0
    os_, ls_ = [], []
    for g, (o_ref, l_ref) in enumerate(((o0_ref, l0_ref), (o1_ref, l1_ref), (o2_ref, l2_ref))):
        os_.append(_read_classes(o_ref, dils[g], il_ref, slot))
        slot += dils[g] > 1
        ls_.append(_read_classes(l_ref, dils[g], il_ref, slot))
        slot += dils[g] > 1
    mx = jnp.maximum(ls_[0], jnp.maximum(ls_[1], ls_[2]))
    es = [jnp.exp(v - mx) for v in ls_]
    att_o = (es[0] * os_[0] + es[1] * os_[1] + es[2] * os_[2]) / (es[0] + es[1] + es[2])
    _mix_tail(x_ref[...], pool_o, att_o, nattn_ref, wgate_ref, wbp_ref, wba_ref, wo_ref, nffn_ref, xo_ref, xn_ref)


def _wspec(shape, layer, grid_rank):
    nd = len(shape)
    return _resident((None,) + tuple(shape), lambda *_: (layer,) + (0,) * nd)


def _tail_weight_specs(layer, grid_rank):
    return [_wspec((1, D_MODEL), layer, grid_rank), _wspec((D_MODEL, 2 * D_MODEL), layer, grid_rank),
            _wspec((POOL_W, D_MODEL), layer, grid_rank), _wspec((GROUP_W, D_MODEL), layer, grid_rank),
            _wspec((D_MODEL, D_MODEL), layer, grid_rank), _wspec((1, D_MODEL), layer, grid_rank)]


def _tail_weights(w):
    return (w["norm_attn"], w["w_gate"], w["w_bp"], w["w_ba"], w["w_o"], w["norm_ffn"])


def _merge(x, u, outs, lses, layer, w, batch, seq, tm, dils):
    n = x.shape[0]
    n_tiles = seq // tm
    halo = POOL_STATE + 1
    row = lambda width: pl.BlockSpec((tm, width), lambda b, s: (b * n_tiles + s, 0))
    cls = lambda g: pl.BlockSpec((None, dils[g], tm // dils[g], GROUP_W), lambda b, s: (b, 0, s, 0))
    n_il = 2 * sum(d > 1 for d in dils)
    return pl.pallas_call(
        functools.partial(_merge_kernel, tm=tm, dils=dils),
        grid=(batch, n_tiles),
        in_specs=[row(D_MODEL), row(POOL_W),
                  pl.BlockSpec((halo, POOL_W),
                               lambda b, s: (jnp.maximum((b * n_tiles + s) * (tm // halo) - 1, 0), 0))]
                 + [cls(g) for g in range(N_GROUPS)] * 2
                 + [_wspec((len(POOL_WINDOWS), POOL_GROUP, POOL_GROUP), layer, 2), _wspec((1, POOL_W), layer, 2)]
                 + _tail_weight_specs(layer, 2),
        out_specs=[row(D_MODEL), row(D_MODEL)],
        out_shape=[jax.ShapeDtypeStruct((n, D_MODEL), F32), jax.ShapeDtypeStruct((n, D_MODEL), BF16)],
        scratch_shapes=[pltpu.VMEM((halo + tm, POOL_W), F32), pltpu.VMEM((n_il, LANE_CHUNKS, tm, LANES), F32)],
        compiler_params=_params(("parallel", "parallel")),
        name="merge",
    )(x, u, u, *outs, *lses, w["w_pool"], w["pool_scale"], *_tail_weights(w))


def _head_mask():
    return jnp.asarray(np.arange(GROUP_W)[None, :] // HEAD_DIM == np.arange(HEADS)[:, None], F32)


def _sample_keys(dec_seq, cache_rows):
    slopes = _slopes()
    geo = []
    for gi, (win, dil) in enumerate(ATT_GROUPS):
        lc = cache_rows[gi]
        assert lc % dil == 0 and lc // dil == LANES
        cls = (lc + np.arange(dec_seq)) % dil
        n_cached = (cls.max() + 1) * LANES
        lane = np.arange(n_cached + LANES)
        sel = (np.arange(lc)[:, None] == (lane[:n_cached] % LANES) * dil + lane[:n_cached] // LANES)
        t = np.arange(dec_seq)[:, None]
        back = np.where(lane < n_cached, lc + t - ((lane % LANES) * dil + lane // LANES), t - (lane - n_cached))
        valid = (back >= 0) & (back % dil == 0) & (back // dil <= win // dil) & (lane < n_cached + dec_seq)
        bias = np.where(valid[:, None, :], -slopes[gi][None, :, None] * back[:, None, :], NEG_INF)
        geo.append((jnp.asarray(sel, BF16), jnp.asarray(bias.reshape(dec_seq * HEADS, -1), F32)))
    return geo


def _sample_mix_kernel(state_ref, u_ref, q0_ref, q1_ref, q2_ref, kn0_ref, kn1_ref, kn2_ref, c0_ref, c1_ref, c2_ref,
                       sel0_ref, sel1_ref, sel2_ref, bias0_ref, bias1_ref, bias2_ref, headmask_ref,
                       wpool_ref, pscale_ref, pool_ref, att_ref, newpool_ref, ext_ref, m_ref, *, dec_seq):
    ext_ref[...] = jnp.zeros_like(ext_ref)
    ext_ref[0:POOL_STATE, :] = state_ref[...]
    ext_ref[POOL_STATE:POOL_STATE + dec_seq, :] = u_ref[...]
    newpool_ref[...] = ext_ref[dec_seq:dec_seq + POOL_STATE, :]

    m_ref[...] = jnp.zeros_like(m_ref)
    for t in range(dec_seq):
        row = POOL_STATE + t
        for gi, win in enumerate(POOL_WINDOWS):
            cols = slice(gi * POOL_GROUP, (gi + 1) * POOL_GROUP)
            tot = jnp.sum(ext_ref[row - win + 1:row + 1, cols], axis=0, keepdims=True)
            cnt = float(min(PAST_LEN + t + 1, win))
            m_ref[t:t + 1, cols] = tot / cnt - ext_ref[row:row + 1, cols]
    pool_ref[...] = _group_linear(m_ref[...], wpool_ref, pscale_ref)[0:dec_seq]

    groups = ((q0_ref, kn0_ref, c0_ref, sel0_ref, bias0_ref), (q1_ref, kn1_ref, c1_ref, sel1_ref, bias1_ref),
              (q2_ref, kn2_ref, c2_ref, sel2_ref, bias2_ref))
    k_rows = HEADS * HEAD_DIM
    outs, lses = [], []
    for q_ref, kn_ref, c_ref, sel_ref, bias_ref in groups:
        flat = c_ref[...].reshape(2 * k_rows, c_ref.shape[-1]).astype(BF16)
        cached = _dot(flat, sel_ref[...]).astype(BF16)
        new = kn_ref[...].reshape(2 * k_rows, LANES).astype(BF16)
        keys = jnp.concatenate([cached[0:k_rows], new[0:k_rows]], axis=1)
        vals = jnp.concatenate([cached[k_rows:], new[k_rows:]], axis=1)
        s = _dot(q_ref[...], keys) + bias_ref[...]
        mx = jnp.max(s, axis=1, keepdims=True)
        p = jnp.exp(s - mx)
        l = jnp.sum(p, axis=1, keepdims=True)
        outs.append(lax.dot_general((p / l).astype(BF16), vals, (((1,), (1,)), ((), ())),
                                    preferred_element_type=F32))
        lses.append(mx + jnp.log(l))
    mx = jnp.maximum(lses[0], jnp.maximum(lses[1], lses[2]))
    es = [jnp.exp(v - mx) for v in lses]
    mixed = (es[0] * outs[0] + es[1] * outs[1] + es[2] * outs[2]) / (es[0] + es[1] + es[2])
    att_ref[...] = jnp.sum(mixed.reshape(dec_seq, HEADS, k_rows) * headmask_ref[...][None], axis=1)


def _sample_mix(state_pool, u, qs, kvn, caches, layer, w, dec_batch, dec_seq):
    cache_rows = [c.shape[2] for c in caches]
    geo = _sample_keys(dec_seq, cache_rows)
    consts = [g[i] for i in (0, 1) for g in geo] + [_head_mask()]
    kv_tail = (2, HEADS, HEAD_DIM)
    cviews = [jnp.transpose(c, (0, 1, 3, 4, 5, 2)).reshape((-1,) + kv_tail + (c.shape[2],)) for c in caches]
    cspecs = [pl.BlockSpec((None,) + kv_tail + (lc,), lambda b: (layer * dec_batch + b, 0, 0, 0, 0))
              for lc in cache_rows]
    per_batch = lambda *tail: pl.BlockSpec((None,) + tail, lambda b: (b,) + (0,) * len(tail))
    const = lambda shape: _resident(shape, lambda b: (0,) * len(shape))
    return pl.pallas_call(
        functools.partial(_sample_mix_kernel, dec_seq=dec_seq),
        grid=(dec_batch,),
        in_specs=[pl.BlockSpec((None, POOL_STATE, POOL_W), lambda b: (layer * dec_batch + b, 0, 0)),
                  per_batch(dec_seq, POOL_W)] + [per_batch(dec_seq * HEADS, GROUP_W)] * 3
                 + [per_batch(*kv_tail, LANES)] * 3 + cspecs + [const(c.shape) for c in consts]
                 + [_wspec((len(POOL_WINDOWS), POOL_GROUP, POOL_GROUP), layer, 1), _wspec((1, POOL_W), layer, 1)],
        out_specs=[per_batch(dec_seq, POOL_W), per_batch(dec_seq, GROUP_W),
                   pl.BlockSpec((None, POOL_STATE, POOL_W), lambda b: (b, 0, 0))],
        out_shape=[jax.ShapeDtypeStruct((dec_batch, dec_seq, POOL_W), F32),
                   jax.ShapeDtypeStruct((dec_batch, dec_seq, GROUP_W), F32),
                   jax.ShapeDtypeStruct((dec_batch, POOL_STATE, POOL_W), F32)],
        scratch_shapes=[pltpu.VMEM((POOL_STATE + dec_seq + 5, POOL_W), F32), pltpu.VMEM((8, POOL_W), F32)],
        compiler_params=_params(("parallel",)),
        name="sample_mix",
    )(state_pool.reshape(-1, POOL_STATE, POOL_W), u.reshape(dec_batch, dec_seq, POOL_W), *qs, *kvn, *cviews,
      *consts, w["w_pool"], w["pool_scale"])


def _tail_kernel(x_ref, pool_ref, att_ref, nattn_ref, wgate_ref, wbp_ref, wba_ref, wo_ref, nffn_ref, xo_ref, xn_ref):
    _mix_tail(x_ref[...], pool_ref[...], att_ref[...], nattn_ref, wgate_ref, wbp_ref, wba_ref, wo_ref, nffn_ref,
              xo_ref, xn_ref)


def _tail(x, pool_o, att_o, layer, w):
    n = x.shape[0]
    row = lambda width: pl.BlockSpec((n, width), lambda i: (0, 0))
    return pl.pallas_call(
        _tail_kernel,
        grid=(1,),
        in_specs=[row(D_MODEL), row(POOL_W), row(GROUP_W)] + _tail_weight_specs(layer, 1),
        out_specs=[row(D_MODEL), row(D_MODEL)],
        out_shape=[jax.ShapeDtypeStruct((n, D_MODEL), F32), jax.ShapeDtypeStruct((n, D_MODEL), BF16)],
        compiler_params=_params(("arbitrary",)),
        name="tail",
    )(x, pool_o, att_o, *_tail_weights(w))


def _swiglu_step(x, wg_ref, wu_ref, wd_ref, acc_ref):
    h = jax.nn.silu(_dot(x, wg_ref[...])) * _dot(x, wu_ref[...])
    acc_ref[...] += _dot(h.astype(BF16), wd_ref[...])


def _ffn_kernel(x_ref, xn_ref, w1_ref, w3_ref, w2_ref, y_ref):
    xn = xn_ref[...]
    h = jax.nn.silu(_dot(xn, w1_ref[...])) * _dot(xn, w3_ref[...])
    y_ref[...] = x_ref[...] + _dot(h.astype(BF16), w2_ref[...])


def _ffn(x, xn, w1, w3, w2, idx, tm):
    n = x.shape[0]
    d_ff = w1.shape[-1]
    return pl.pallas_call(
        _ffn_kernel,
        grid=(n // tm,),
        in_specs=[pl.BlockSpec((tm, D_MODEL), lambda i: (i, 0)),
                  pl.BlockSpec((tm, D_MODEL), lambda i: (i, 0)),
                  _resident((None, D_MODEL, d_ff), lambda i: (idx, 0, 0)),
                  _resident((None, D_MODEL, d_ff), lambda i: (idx, 0, 0)),
                  _resident((None, d_ff, D_MODEL), lambda i: (idx, 0, 0))],
        out_specs=pl.BlockSpec((tm, D_MODEL), lambda i: (i, 0)),
        out_shape=jax.ShapeDtypeStruct((n, D_MODEL), F32),
        compiler_params=_params(("parallel",)),
        name="ffn",
    )(x, xn, w1, w3, w2)


SEG_ALIGN = 16
MOE_ROWS = 512


def _sorted_cap(ts):
    need = TOP_K * ts + N_EXPERTS * (SEG_ALIGN - 1)
    return -(-need // LANES) * LANES if ts >= LANES * 2 else -(-need // SEG_ALIGN) * SEG_ALIGN


def _segment_sizes(oh1, oh2):
    cnt1 = jnp.sum(oh1, axis=1, keepdims=True)
    cnt = cnt1 + jnp.sum(oh2, axis=1, keepdims=True)
    cpad = jnp.floor((cnt + (SEG_ALIGN - 1)) * (1.0 / SEG_ALIGN)) * SEG_ALIGN
    offs = [jnp.zeros((1, 1), F32)]
    for ei in range(1, N_EXPERTS):
        offs.append(offs[-1] + cpad[ei - 1:ei, :])
    return cnt1, cpad, jnp.concatenate(offs, axis=0)


def _gate_kernel(x_ref, g_ref, rwt_ref, rb_ref, route_ref, seg_ref, *, ts):
    xn = _rms(x_ref[...], g_ref[...]).astype(BF16)
    logits = lax.dot_general(rwt_ref[...], xn, (((1,), (1,)), ((), ())), preferred_element_type=F32) + rb_ref[...]
    row = lax.broadcasted_iota(jnp.int32, (N_EXPERTS, ts), 0)
    neg = jnp.float32(-jnp.inf)
    m1 = jnp.max(logits, axis=0, keepdims=True)
    i1 = jnp.min(jnp.where(logits == m1, row, N_EXPERTS), axis=0, keepdims=True)
    rest = jnp.where(row == i1, neg, logits)
    m2 = jnp.max(rest, axis=0, keepdims=True)
    i2 = jnp.min(jnp.where(rest == m2, row, N_EXPERTS), axis=0, keepdims=True)
    e = jnp.exp(m2 - m1)
    zeros = jnp.zeros((N_EXPERTS - 4, ts), F32)
    route_ref[...] = jnp.concatenate([i1.astype(F32), i2.astype(F32), 1.0 / (1.0 + e), e / (1.0 + e), zeros], axis=0)
    _, cpad, off = _segment_sizes(jnp.where(row == i1, 1.0, 0.0), jnp.where(row == i2, 1.0, 0.0))
    lane = lax.broadcasted_iota(jnp.int32, (N_EXPERTS, LANES), 1)
    seg_ref[...] = jnp.where(lane == 0, off, jnp.where(lane == 1, cpad, 0.0)).astype(jnp.int32)


def _gate(x, norm_ffn, rwt, rb, layer, idx, ts):
    n = x.shape[0]
    n_sub = n // ts
    return pl.pallas_call(
        functools.partial(_gate_kernel, ts=ts),
        grid=(n_sub,),
        in_specs=[pl.BlockSpec((ts, D_MODEL), lambda i: (i, 0)),
                  _resident((None, 1, D_MODEL), lambda i: (layer, 0, 0)),
                  _resident((None, N_EXPERTS, D_MODEL), lambda i: (idx, 0, 0)),
                  _resident((None, N_EXPERTS, 1), lambda i: (idx, 0, 0))],
        out_specs=[pl.BlockSpec((None, N_EXPERTS, ts), lambda i: (i, 0, 0)),
                   pl.BlockSpec((None, N_EXPERTS, LANES), lambda i: (i, 0, 0))],
        out_shape=[jax.ShapeDtypeStruct((n_sub, N_EXPERTS, ts), F32),
                   jax.ShapeDtypeStruct((n_sub, N_EXPERTS, LANES), jnp.int32)],
        compiler_params=_params(("parallel",)),
        name="gate",
    )(x, norm_ffn, rwt, rb)


def _segment_copies(i, n_bits, local_ref, off_ref, far_ref, far_rows_ref, nseg_ref, sem, to_far):
    copies = []
    for e in range(N_EXPERTS):
        j = i * N_EXPERTS + e
        n = nseg_ref[j]
        for b in reversed(range(n_bits)):
            size = SEG_ALIGN << b
            done = ((n >> (b + 1)) << (b + 1)) * SEG_ALIGN
            near = local_ref.at[pl.ds(pl.multiple_of(off_ref[j] + done, SEG_ALIGN), size)]
            far = far_ref.at[pl.ds(pl.multiple_of(far_rows_ref[j] + done, SEG_ALIGN), size)]
            cp = pltpu.make_async_copy(near, far, sem) if to_far else pltpu.make_async_copy(far, near, sem)
            copies.append((((n >> b) & 1) == 1, cp))
    return copies


def _scatter_kernel(off_ref, far_rows_ref, nseg_ref, x_ref, g_ref, route_ref, tri_ref, xb_in_ref,
                    meta_ref, xb_ref, xs_ref, sem, *, ts, cap, n_bits):
    del xb_in_ref
    i = pl.program_id(0)
    xn = _rms(x_ref[...], g_ref[...]).astype(BF16)
    row = lax.broadcasted_iota(jnp.int32, (N_EXPERTS, ts), 0).astype(F32)
    oh1 = jnp.where(row == route_ref[0:1, :], 1.0, 0.0)
    oh2 = jnp.where(row == route_ref[1:2, :], 1.0, 0.0)
    cnt1, _, off = _segment_sizes(oh1, oh2)
    pre1 = _dot(oh1.astype(BF16), tri_ref[...])
    pre2 = _dot(oh2.astype(BF16), tri_ref[...])
    d1 = jnp.sum(oh1 * (off + pre1), axis=0, keepdims=True)
    d2 = jnp.sum(oh2 * (off + cnt1 + pre2), axis=0, keepdims=True)
    slot = lax.broadcasted_iota(jnp.int32, (cap, ts), 0).astype(F32)
    p = jnp.where(slot == d1, 1.0, jnp.where(slot == d2, 1.0, 0.0)).astype(BF16)
    xs_ref[i % 2] = _dot(p, xn).astype(BF16)
    meta_t = jnp.concatenate([d1, d2, route_ref[2:4, :], jnp.zeros((LANES - 4, ts), F32)], axis=0)
    meta_ref[...] = meta_t.T

    def send(step, action):
        buf = step % 2
        for pred, cp in _segment_copies(step, n_bits, xs_ref.at[buf], off_ref, xb_ref, far_rows_ref, nseg_ref,
                                        sem.at[buf], True):
            pl.when(pred)(getattr(cp, action))

    send(i, "start")
    pl.when(i > 0)(lambda: send(i - 1, "wait"))
    pl.when(i == pl.num_programs(0) - 1)(lambda: send(i, "wait"))


def _scatter(off, far_rows, nseg, x, norm_ffn, route, xb, layer, ts):
    n = x.shape[0]
    n_sub = n // ts
    cap = _sorted_cap(ts)
    tri = jnp.asarray(np.triu(np.ones((ts, ts), np.float32), 1), BF16)
    n_bits = int(TOP_K * ts // SEG_ALIGN).bit_length()
    return pl.pallas_call(
        functools.partial(_scatter_kernel, ts=ts, cap=cap, n_bits=n_bits),
        grid_spec=pltpu.PrefetchScalarGridSpec(
            num_scalar_prefetch=3,
            grid=(n_sub,),
            in_specs=[pl.BlockSpec((ts, D_MODEL), lambda i, *_: (i, 0)),
                      _resident((None, 1, D_MODEL), lambda i, *_: (layer, 0, 0)),
                      pl.BlockSpec((None, N_EXPERTS, ts), lambda i, *_: (i, 0, 0)),
                      _resident((ts, ts), lambda i, *_: (0, 0)),
                      pl.BlockSpec(memory_space=pl.ANY)],
            out_specs=[pl.BlockSpec((ts, LANES), lambda i, *_: (i, 0)),
                       pl.BlockSpec(memory_space=pl.ANY)],
            scratch_shapes=[pltpu.VMEM((2, cap, D_MODEL), BF16), pltpu.SemaphoreType.DMA((2,))]),
        out_shape=[jax.ShapeDtypeStruct((n, LANES), F32), jax.ShapeDtypeStruct(xb.shape, xb.dtype)],
        input_output_aliases={7: 1},
        compiler_params=_params(("arbitrary",)),
        name="scatter",
    )(off, far_rows, nseg, x, norm_ffn, route, tri, xb)


def _experts_kernel(blk_ref, nreal_ref, xb_ref, wg_ref, wu_ref, wd_ref, yb_ref, acc_ref):
    del blk_ref
    i = pl.program_id(0)
    f = pl.program_id(1)

    @pl.when(f == 0)
    def _():
        acc_ref[...] = jnp.zeros_like(acc_ref)

    @pl.when(i < nreal_ref[0])
    def _():
        _swiglu_step(xb_ref[...], wg_ref, wu_ref, wd_ref, acc_ref)

    @pl.when(f == pl.num_programs(1) - 1)
    def _():
        yb_ref[...] = acc_ref[...].astype(yb_ref.dtype)


def _experts(blk_e, nreal, xb, wg, wu, wd, idx, tf):
    n_blocks = blk_e.shape[0]
    d_ff = wg.shape[-1]
    n_f = d_ff // tf

    def live_block(i, nreal):
        return jnp.minimum(i, nreal[0] - 1)

    def wcol(i, f, blk, nreal):
        return idx, blk[live_block(i, nreal)], 0, jnp.where(i < nreal[0], f, n_f - 1)

    def wrow(i, f, blk, nreal):
        return idx, blk[live_block(i, nreal)], jnp.where(i < nreal[0], f, n_f - 1), 0

    return pl.pallas_call(
        _experts_kernel,
        grid_spec=pltpu.PrefetchScalarGridSpec(
            num_scalar_prefetch=2,
            grid=(n_blocks, n_f),
            in_specs=[pl.BlockSpec((MOE_ROWS, D_MODEL), lambda i, f, blk, nreal: (live_block(i, nreal), 0)),
                      pl.BlockSpec((None, None, D_MODEL, tf), wcol),
                      pl.BlockSpec((None, None, D_MODEL, tf), wcol),
                      pl.BlockSpec((None, None, tf, D_MODEL), wrow)],
            out_specs=pl.BlockSpec((MOE_ROWS, D_MODEL), lambda i, f, blk, nreal: (i, 0)),
            scratch_shapes=[pltpu.VMEM((MOE_ROWS, D_MODEL), F32)]),
        out_shape=jax.ShapeDtypeStruct((n_blocks * MOE_ROWS, D_MODEL), BF16),
        compiler_params=_params(("parallel", "arbitrary")),
        name="experts",
    )(blk_e, nreal, xb, wg, wu, wd)


def _combine_kernel(off_ref, far_rows_ref, nseg_ref, x_ref, meta_ref, yb_ref, o_ref, ys_ref, sem, *, ts, cap, n_bits):
    i = pl.program_id(0)

    def fetch(step, action):
        buf = step % 2
        for pred, cp in _segment_copies(step, n_bits, ys_ref.at[buf], off_ref, yb_ref, far_rows_ref, nseg_ref,
                                        sem.at[buf], False):
            pl.when(pred)(getattr(cp, action))

    def start(step):
        ys_ref[step % 2] = jnp.zeros((cap, D_MODEL), BF16)
        fetch(step, "start")

    pl.when(i == 0)(lambda: start(i))
    pl.when(i + 1 < pl.num_programs(0))(lambda: start(i + 1))
    fetch(i, "wait")
    slot = lax.broadcasted_iota(jnp.int32, (ts, cap), 1).astype(F32)
    ys = ys_ref[i % 2]
    q1 = jnp.where(slot == meta_ref[:, 0:1], 1.0, 0.0).astype(BF16)
    q2 = jnp.where(slot == meta_ref[:, 1:2], 1.0, 0.0).astype(BF16)
    o_ref[...] = x_ref[...] + meta_ref[:, 2:3] * _dot(q1, ys) + meta_ref[:, 3:4] * _dot(q2, ys)


def _combine(off, far_rows, nseg, x, meta, yb, ts):
    n = x.shape[0]
    cap = _sorted_cap(ts)
    n_bits = int(TOP_K * ts // SEG_ALIGN).bit_length()
    return pl.pallas_call(
        functools.partial(_combine_kernel, ts=ts, cap=cap, n_bits=n_bits),
        grid_spec=pltpu.PrefetchScalarGridSpec(
            num_scalar_prefetch=3,
            grid=(n // ts,),
            in_specs=[pl.BlockSpec((ts, D_MODEL), lambda i, *_: (i, 0)),
                      pl.BlockSpec((ts, LANES), lambda i, *_: (i, 0)),
                      pl.BlockSpec(memory_space=pl.ANY)],
            out_specs=pl.BlockSpec((ts, D_MODEL), lambda i, *_: (i, 0)),
            scratch_shapes=[pltpu.VMEM((2, cap, D_MODEL), BF16), pltpu.SemaphoreType.DMA((2,))]),
        out_shape=jax.ShapeDtypeStruct((n, D_MODEL), F32),
        compiler_params=_params(("arbitrary",)),
        name="combine",
    )(off, far_rows, nseg, x, meta, yb)


def _moe(xs, tss, norm_ffn, rwt, rb, wg, wu, wd, layer, idx, tf):
    gated = [_gate(x, norm_ffn, rwt, rb, layer, idx, ts) for x, ts in zip(xs, tss)]
    off = jnp.concatenate([g[1][:, :, 0] for g in gated], axis=0)
    cpad = jnp.concatenate([g[1][:, :, 1] for g in gated], axis=0)
    n_subs = [g[1].shape[0] for g in gated]
    tot = jnp.sum(cpad, axis=0)
    padded = (tot + MOE_ROWS - 1) // MOE_ROWS * MOE_ROWS
    pend = jnp.cumsum(padded)
    within = jnp.cumsum(cpad, axis=0) - cpad
    off = off.astype(jnp.int32).reshape(-1)
    grouped_rows = ((pend - padded)[None, :] + within).astype(jnp.int32).reshape(-1)
    nseg = (cpad // SEG_ALIGN).astype(jnp.int32).reshape(-1)
    n_assign = TOP_K * sum(x.shape[0] for x in xs)
    n_blocks = (n_assign + (SEG_ALIGN - 1) * sum(n_subs) * N_EXPERTS + N_EXPERTS * (MOE_ROWS - 1)) // MOE_ROWS
    blk_e = jnp.minimum(jnp.sum(jnp.arange(n_blocks)[:, None] * MOE_ROWS >= pend[None, :], axis=1),
                        N_EXPERTS - 1).astype(jnp.int32)
    nreal = (pend[-1:] // MOE_ROWS).astype(jnp.int32)

    xb = jnp.zeros((n_blocks * MOE_ROWS, D_MODEL), BF16)
    metas, parts = [], []
    lo = 0
    for x, g, ns, ts in zip(xs, gated, n_subs, tss):
        sl = slice(lo * N_EXPERTS, (lo + ns) * N_EXPERTS)
        parts.append((off[sl], grouped_rows[sl], nseg[sl]))
        meta, xb = _scatter(*parts[-1], x, norm_ffn, g[0], xb, layer, ts)
        metas.append(meta)
        lo += ns
    yb = _experts(blk_e, nreal, xb, wg, wu, wd, idx, tf)
    return [_combine(*part, x, meta, yb, ts) for part, x, meta, ts in zip(parts, xs, metas, tss)]


def kernel(x_prompt, x_sample, state_pool, cache_kv_g0, cache_kv_g1, cache_kv_g2, norm_attn, w_in, q_norm, k_norm, w_pool, pool_scale, w_branch_pool, w_branch_attn, w_out, norm_ffn, w1_dense, w3_dense, w2_dense, router_w, router_b, we_gate, we_up, we_down):
    batch, seq, _ = x_prompt.shape
    dec_batch, dec_seq, _ = x_sample.shape
    depth = w_in.shape[0]
    caches = (cache_kv_g0, cache_kv_g1, cache_kv_g2)
    n_p, n_s = batch * seq, dec_batch * dec_seq
    tm_p = 512
    dils = tuple(d for _, d in ATT_GROUPS)
    keeps = tuple(min(win, seq) for win, _ in ATT_GROUPS)

    head_of = np.arange(GROUP_W) // HEAD_DIM
    same_head = (head_of[:, None] == head_of[None, :]).astype(np.float32)
    seg_mean = jnp.asarray(same_head / HEAD_DIM, BF16)
    w_qkv = w_in[:, :, :PROJ_W].astype(BF16)
    qgain = (jnp.tile(q_norm, (1, HEADS)) * (HEAD_DIM ** -0.5)).reshape(depth, 1, GROUP_W)
    kgain = jnp.tile(k_norm, (1, HEADS)).reshape(depth, 1, GROUP_W)
    norm_attn3 = norm_attn.reshape(depth, 1, D_MODEL)
    w = {"w_pool": w_pool.astype(BF16), "pool_scale": pool_scale.reshape(depth, 1, POOL_W),
         "norm_attn": norm_attn3, "w_gate": w_in[:, :, PROJ_W:].astype(BF16),
         "w_bp": w_branch_pool.astype(BF16), "w_ba": w_branch_attn.astype(BF16), "w_o": w_out.astype(BF16),
         "norm_ffn": norm_ffn.reshape(depth, 1, D_MODEL)}
    w1_b, w3_b, w2_b = w1_dense.astype(BF16), w3_dense.astype(BF16), w2_dense.astype(BF16)
    wg_b, wu_b, wd_b = we_gate.astype(BF16), we_up.astype(BF16), we_down.astype(BF16)
    rwt = jnp.swapaxes(router_w, 1, 2).astype(BF16)
    rb = router_b.reshape(-1, N_EXPERTS, 1)

    xp = x_prompt.reshape(n_p, D_MODEL)
    xs = x_sample.reshape(n_s, D_MODEL)
    pool_p, pool_s = [], []
    kv_s = [[] for _ in ATT_GROUPS]
    kvo_p = tuple(jnp.zeros((depth, batch, 2, GROUP_W, keep), F32) for keep in keeps)
    ones = (1,) * N_GROUPS
    for layer in range(depth):
        res = _proj(xp, layer, norm_attn3, w_qkv, qgain, kgain, seg_mean, batch, seq, tm_p, dils, keeps,
                    layer, depth, kvo_p, 1, True)
        u, qs, kcs, kvo_p = res[0], res[1:4], res[4:7], tuple(res[7:10])
        outs, lses = [], []
        for gi in range(N_GROUPS):
            o, lse = _attention(qs[gi], kcs[gi], gi)
            outs.append(o)
            lses.append(lse)
        pool_p.append(u.reshape(batch, seq, POOL_W)[:, seq - POOL_STATE:])
        xp, xpn = _merge(xp, u, outs, lses, layer, w, batch, seq, tm_p, dils)

        res = _proj(xs, layer, norm_attn3, w_qkv, qgain, kgain, seg_mean, 1, n_s, n_s, ones, (n_s,) * N_GROUPS,
                    0, 1, (), 3, False)
        u = res[0]
        qs = [(q.reshape(dec_batch, dec_seq, 1, GROUP_W) * _head_mask().astype(BF16))
              .reshape(dec_batch, dec_seq * HEADS, GROUP_W) for q in res[1:4]]
        kv_new = [kv.reshape(dec_batch, dec_seq, 2, HEADS, HEAD_DIM) for kv in res[7:10]]
        tokens_to_lanes = ((0, 0),) * 4 + ((0, LANES - dec_seq),)
        kvn = [jnp.pad(kv.transpose(0, 2, 3, 4, 1), tokens_to_lanes) for kv in kv_new]
        pool_o, att_o, new_pool = _sample_mix(state_pool, u, qs, kvn, caches, layer, w, dec_batch, dec_seq)
        att_o = att_o.reshape(n_s, GROUP_W)
        for gi in range(N_GROUPS):
            kv_s[gi].append(kv_new[gi])
        pool_s.append(new_pool)
        xs, xsn = _tail(xs, pool_o.reshape(n_s, POOL_W), att_o, layer, w)

        i = layer // 2
        if layer % 2 == 0:
            xp = _ffn(xp, xpn, w1_b, w3_b, w2_b, i, 512)
            xs = _ffn(xs, xsn, w1_b, w3_b, w2_b, i, n_s)
        else:
            xp, xs = _moe([xp, xs], [512, n_s], w["norm_ffn"], rwt, rb, wg_b, wu_b, wd_b, layer, i,
                           wg_b.shape[-1] // 2)
    kv_p = [kvo_p[g].reshape(depth, batch, 2, HEADS, HEAD_DIM, keeps[g]).transpose(0, 1, 5, 2, 3, 4)
            for g in range(N_GROUPS)]
    return (xp.reshape(batch, seq, D_MODEL), xs.reshape(dec_batch, dec_seq, D_MODEL),
            jnp.stack(pool_p), jnp.stack(pool_s),
            kv_p[0], jnp.stack(kv_s[0]),
            kv_p[1], jnp.stack(kv_s[1]),
            kv_p[2], jnp.stack(kv_s[2]))
```

```python
import functools

import numpy as np
import jax
import jax.numpy as jnp
from jax import lax
from jax.experimental import pallas as pl
from jax.experimental.pallas import tpu as pltpu

F32 = jnp.float32
BF16 = jnp.bfloat16

D_MODEL = 1024
PAST_LEN = 16384
POOL_WINDOWS = (2, 4, 8, 16)
POOL_GROUP = 128
POOL_W = 512
POOL_STATE = 15
ATT_GROUPS = ((128, 1), (512, 4), (2048, 16))
N_GROUPS = len(ATT_GROUPS)
HEAD_DIM = 64
HEADS = 8
GROUP_W = 512
QKV_W = 1536
Q_BLOCK = 128
ALIBI_MAX = 8.0
N_EXPERTS = 8
TOP_K = 2
RMS_EPS = 1e-6
NEG_INF = -1e30
LANES = 128
LANE_CHUNKS = GROUP_W // LANES
PROJ_W = POOL_W + 3 * QKV_W
V7X_VMEM_BYTES = 64 * 1024 * 1024
VMEM_LIMIT = V7X_VMEM_BYTES * 7 // 8


def _slopes():
    i = np.arange(1, N_GROUPS * HEADS + 1, dtype=np.float32)
    return np.exp2(-ALIBI_MAX * i / (N_GROUPS * HEADS)).astype(np.float32).reshape(N_GROUPS, HEADS)


def _params(sem):
    return pltpu.CompilerParams(dimension_semantics=sem, vmem_limit_bytes=VMEM_LIMIT)


def _rms(x, gain):
    return x * lax.rsqrt(jnp.mean(x * x, axis=-1, keepdims=True) + RMS_EPS) * gain


def _dot(a, b):
    return jnp.dot(a, b, preferred_element_type=F32)


def _resident(shape, index_map):
    return pl.BlockSpec(shape, index_map, pipeline_mode=pl.Buffered(1))


def _write_classes(dst_ref, col0, val, dil, tmp_ref, slot):
    rows = val.shape[0] // dil
    cols = slice(col0, col0 + GROUP_W)
    if dil == 1:
        dst_ref[0, :, cols] = val.astype(dst_ref.dtype)
        return
    for c in range(LANE_CHUNKS):
        tmp_ref[slot, c] = val[:, c * LANES:(c + 1) * LANES]
    for r in range(dil):
        picked = [tmp_ref[slot, c, pl.ds(r, rows, stride=dil), :] for c in range(LANE_CHUNKS)]
        dst_ref[r, :, cols] = jnp.concatenate(picked, axis=1).astype(dst_ref.dtype)


def _proj_kernel(x_ref, g_ref, w_ref, qg_ref, kg_ref, seg_ref, *rest, tm, n_tiles, dils, keeps, n_alias,
                 norm_terms, kv_positions_minor):
    u_ref, q0_ref, q1_ref, q2_ref, kc0_ref, kc1_ref, kc2_ref, kvo0_ref, kvo1_ref, kvo2_ref, tmp_ref = rest[n_alias:]
    s = pl.program_id(1)
    xn = _rms(x_ref[...], g_ref[...]).astype(BF16)

    def zblk(j):
        return _dot(xn, w_ref[:, j * GROUP_W:(j + 1) * GROUP_W])

    def headnorm(z, gain):
        rem = z * z
        ms = None
        for _ in range(norm_terms):
            part = rem.astype(BF16)
            rem = rem - part.astype(F32)
            ms = _dot(part, seg_ref[...]) if ms is None else ms + _dot(part, seg_ref[...])
        return z * lax.rsqrt(ms + RMS_EPS) * gain

    u_ref[...] = zblk(0)
    q_refs = (q0_ref, q1_ref, q2_ref)
    kc_refs = (kc0_ref, kc1_ref, kc2_ref)
    kvo_refs = (kvo0_ref, kvo1_ref, kvo2_ref)
    slot = 0
    kept = {}
    for g in range(N_GROUPS):
        q = headnorm(zblk(1 + g), qg_ref[...])
        k = headnorm(zblk(1 + N_GROUPS + g), kg_ref[...])
        v = zblk(1 + 2 * N_GROUPS + g)
        for dst, col0, val in ((q_refs[g], 0, q), (kc_refs[g], 0, k), (kc_refs[g], GROUP_W, v)):
            _write_classes(dst, col0, val, dils[g], tmp_ref, slot % tmp_ref.shape[0])
            slot += dils[g] > 1
        kept.setdefault(n_tiles - max(keeps[g] // tm, 1), []).append((kvo_refs[g], min(keeps[g], tm), k, v))

    def write_kept(items):
        for ref, rows, k, v in items:
            if kv_positions_minor:
                ref[0] = k[tm - rows:, :].T
                ref[1] = v[tm - rows:, :].T
            else:
                ref[:, 0:GROUP_W] = k[tm - rows:, :]
                ref[:, GROUP_W:2 * GROUP_W] = v[tm - rows:, :]

    for first, items in kept.items():
        if first == 0:
            write_kept(items)
        else:
            pl.when(s >= first)(functools.partial(write_kept, items))


def _proj(x, layer, norm_attn, w_qkv, qgain, kgain, seg, batch, seq, tm, dils, keeps, out_layer, out_depth, prev_kvo,
          norm_terms, kv_positions_minor):
    n_tiles = seq // tm
    kvo_specs, kvo_shapes = [], []
    for g in range(N_GROUPS):
        keep = keeps[g]
        assert keep % tm == 0 or (keep < tm and keep % LANES == 0)
        first = n_tiles - max(keep // tm, 1)
        rows = min(keep, tm)
        if kv_positions_minor:
            kvo_specs.append(pl.BlockSpec((None, None, 2, GROUP_W, rows),
                                          lambda b, s, first=first: (out_layer, b, 0, 0, jnp.maximum(s - first, 0))))
            kvo_shapes.append(jax.ShapeDtypeStruct((out_depth, batch, 2, GROUP_W, keep), F32))
        else:
            kvo_specs.append(pl.BlockSpec((None, None, rows, 2 * GROUP_W),
                                          lambda b, s, first=first: (out_layer, b, jnp.maximum(s - first, 0), 0)))
            kvo_shapes.append(jax.ShapeDtypeStruct((out_depth, batch, keep, 2 * GROUP_W), F32))
    cls_spec = lambda g, width: pl.BlockSpec((None, dils[g], tm // dils[g], width), lambda b, s: (b, 0, s, 0))
    cls_shape = lambda g, width: jax.ShapeDtypeStruct((batch, dils[g], seq // dils[g], width), BF16)
    n_alias = len(prev_kvo)
    n_in = 6
    return pl.pallas_call(
        functools.partial(_proj_kernel, tm=tm, n_tiles=n_tiles, dils=dils, keeps=keeps, n_alias=n_alias,
                          norm_terms=norm_terms, kv_positions_minor=kv_positions_minor),
        grid=(batch, n_tiles),
        in_specs=[
            pl.BlockSpec((tm, D_MODEL), lambda b, s: (b * n_tiles + s, 0)),
            _resident((None, 1, D_MODEL), lambda b, s: (layer, 0, 0)),
            _resident((None, D_MODEL, PROJ_W), lambda b, s: (layer, 0, 0)),
            _resident((None, 1, GROUP_W), lambda b, s: (layer, 0, 0)),
            _resident((None, 1, GROUP_W), lambda b, s: (layer, 0, 0)),
            _resident((GROUP_W, GROUP_W), lambda b, s: (0, 0)),
        ] + [pl.BlockSpec(memory_space=pl.ANY)] * n_alias,
        out_specs=[pl.BlockSpec((tm, POOL_W), lambda b, s: (b * n_tiles + s, 0))]
                  + [cls_spec(g, GROUP_W) for g in range(N_GROUPS)]
                  + [cls_spec(g, 2 * GROUP_W) for g in range(N_GROUPS)] + kvo_specs,
        out_shape=[jax.ShapeDtypeStruct((batch * seq, POOL_W), F32)]
                  + [cls_shape(g, GROUP_W) for g in range(N_GROUPS)]
                  + [cls_shape(g, 2 * GROUP_W) for g in range(N_GROUPS)] + kvo_shapes,
        scratch_shapes=[pltpu.VMEM((3, LANE_CHUNKS, tm, LANES), F32)],
        input_output_aliases={n_in + g: 1 + 2 * N_GROUPS + g for g in range(n_alias)},
        compiler_params=_params(("parallel", "arbitrary")),
        name="proj",
    )(x, norm_attn, w_qkv, qgain, kgain, seg, *prev_kvo)


def _attn_kernel(q_ref, kvc_ref, kvh_ref, o_ref, lse_ref, kbuf, vbuf, *, slope_dil, tc, isolated_tiles):
    off = Q_BLOCK
    kbuf[0:off, :] = kvh_ref[:, 0:GROUP_W]
    vbuf[0:off, :] = kvh_ref[:, GROUP_W:2 * GROUP_W]
    kbuf[off:off + tc, :] = kvc_ref[:, 0:GROUP_W]
    vbuf[off:off + tc, :] = kvc_ref[:, GROUP_W:2 * GROUP_W]

    nk = Q_BLOCK + off
    kj = lax.broadcasted_iota(jnp.int32, (nk, Q_BLOCK), 0)
    qi = lax.broadcasted_iota(jnp.int32, (nk, Q_BLOCK), 1)
    dist = qi - kj + off
    distf = dist.astype(F32)
    maskneg = jnp.where((dist >= 0) & (dist <= Q_BLOCK), 0.0, NEG_INF).astype(F32)
    first = NEG_INF if isolated_tiles else jnp.where(pl.program_id(2) == 0, NEG_INF, 0.0).astype(F32)
    mask_first = maskneg + jnp.where(kj < Q_BLOCK, first, 0.0)

    for i in range(tc // Q_BLOCK):
        mask = mask_first if (isolated_tiles or i == 0) else maskneg
        rows = slice(i * Q_BLOCK, (i + 1) * Q_BLOCK)
        krows = slice(i * Q_BLOCK, i * Q_BLOCK + nk)
        for hp in range(HEADS // 2):
            outs, lses = [], []
            for h in (2 * hp, 2 * hp + 1):
                cols = slice(h * HEAD_DIM, (h + 1) * HEAD_DIM)
                s = lax.dot_general(kbuf[krows, cols], q_ref[rows, cols],
                                    (((1,), (1,)), ((), ())), preferred_element_type=F32)
                s = s - slope_dil[h] * distf + mask
                m = jnp.max(s, axis=0, keepdims=True)
                p = jnp.exp(s - m)
                l = jnp.sum(p, axis=0, keepdims=True)
                outs.append(lax.dot_general(vbuf[krows, cols], p.astype(BF16), (((0,), (0,)), ((), ())),
                                            preferred_element_type=F32) / l)
                lses.append(jnp.broadcast_to(m + jnp.log(l), (HEAD_DIM, Q_BLOCK)))
            pc = slice(hp * 2 * HEAD_DIM, (hp + 1) * 2 * HEAD_DIM)
            o_ref[rows, pc] = jnp.concatenate(outs, axis=0).T.astype(BF16)
            lse_ref[rows, pc] = jnp.concatenate(lses, axis=0).T


def _attention(q, kc, gi):
    win, dil = ATT_GROUPS[gi]
    assert win // dil == Q_BLOCK
    batch, _, l, _ = q.shape
    out_dims = (batch, dil, l, GROUP_W)
    isolated_tiles = l == Q_BLOCK
    slope_dil = tuple(float(s) * dil for s in _slopes()[gi])
    if isolated_tiles:
        q, kc = q.reshape(batch, 1, dil * l, GROUP_W), kc.reshape(batch, 1, dil * l, 2 * GROUP_W)
        dil, l = 1, dil * l
    tc = min(l, 512)
    nk = tc + Q_BLOCK
    o, lse = pl.pallas_call(
        functools.partial(_attn_kernel, slope_dil=slope_dil, tc=tc, isolated_tiles=isolated_tiles),
        grid=(batch, dil, l // tc),
        in_specs=[pl.BlockSpec((None, None, tc, GROUP_W), lambda b, r, c: (b, r, c, 0)),
                  pl.BlockSpec((None, None, tc, 2 * GROUP_W), lambda b, r, c: (b, r, c, 0)),
                  pl.BlockSpec((None, None, Q_BLOCK, 2 * GROUP_W),
                               lambda b, r, c: (b, r, jnp.maximum(c * (tc // Q_BLOCK) - 1, 0), 0))],
        out_specs=[pl.BlockSpec((None, None, tc, GROUP_W), lambda b, r, c: (b, r, c, 0))] * 2,
        out_shape=[jax.ShapeDtypeStruct((batch, dil, l, GROUP_W), BF16),
                   jax.ShapeDtypeStruct((batch, dil, l, GROUP_W), F32)],
        scratch_shapes=[pltpu.VMEM((nk, GROUP_W), BF16), pltpu.VMEM((nk, GROUP_W), BF16)],
        compiler_params=_params(("parallel", "parallel", "arbitrary")),
        name=f"attn_g{gi}",
    )(q, kc, kc)
    return o.reshape(out_dims), lse.reshape(out_dims)


def _mix_tail(x, pool_o, att_o, nattn_ref, wgate_ref, wbp_ref, wba_ref, wo_ref, nffn_ref, xo_ref, xn_ref):
    gates = jax.nn.sigmoid(_dot(_rms(x, nattn_ref[...]).astype(BF16), wgate_ref[...]))
    hp = _dot(pool_o.astype(BF16), wbp_ref[...])
    ha = _dot(att_o.astype(BF16), wba_ref[...])
    t = gates[:, 0:D_MODEL] * hp + gates[:, D_MODEL:2 * D_MODEL] * ha
    xo = x + _dot(t.astype(BF16), wo_ref[...])
    xo_ref[...] = xo
    xn_ref[...] = _rms(xo, nffn_ref[...]).astype(BF16)


def _group_linear(m, wpool_ref, pscale_ref):
    parts = [_dot(m[:, gi * POOL_GROUP:(gi + 1) * POOL_GROUP].astype(BF16), wpool_ref[gi])
             for gi in range(len(POOL_WINDOWS))]
    return jnp.concatenate(parts, axis=1) * pscale_ref[...]


def _read_classes(src_ref, dil, il_ref, slot):
    if dil == 1:
        return src_ref[0].astype(F32)
    rows = src_ref.shape[1]
    for r in range(dil):
        v = src_ref[r].astype(F32)
        for c in range(LANE_CHUNKS):
            il_ref[slot, c, pl.ds(r, rows, stride=dil), :] = v[:, c * LANES:(c + 1) * LANES]
    return jnp.concatenate([il_ref[slot, c] for c in range(LANE_CHUNKS)], axis=1)


def _merge_kernel(x_ref, u_ref, uh_ref, o0_ref, o1_ref, o2_ref, l0_ref, l1_ref, l2_ref,
                  wpool_ref, pscale_ref, nattn_ref, wgate_ref, wbp_ref, wba_ref, wo_ref, nffn_ref,
                  xo_ref, xn_ref, ext_ref, il_ref, *, tm, dils):
    si = pl.program_id(1)
    halo = POOL_STATE + 1
    ext_ref[0:halo, :] = jnp.where(si == 0, 0.0, uh_ref[...])
    ext_ref[halo:halo + tm, :] = u_ref[...]
    pos = si * tm + lax.broadcasted_iota(jnp.int32, (tm, 1), 0)
    parts = []
    for gi, win in enumerate(POOL_WINDOWS):
        cols = slice(gi * POOL_GROUP, (gi + 1) * POOL_GROUP)
        own = ext_ref[halo:halo + tm, cols]
        acc = own
        for back in range(1, win):
            acc = acc + ext_ref[halo - back:halo - back + tm, cols]
        inv = 1.0 / jnp.minimum(pos + 1, win).astype(F32)
        parts.append(acc * inv - own)
    pool_o = _group_linear(jnp.concatenate(parts, axis=1), wpool_ref, pscale_ref)

    slot = 0
    os_, ls_ = [], []
    for g, (o_ref, l_ref) in enumerate(((o0_ref, l0_ref), (o1_ref, l1_ref), (o2_ref, l2_ref))):
        os_.append(_read_classes(o_ref, dils[g], il_ref, slot))
        slot += dils[g] > 1
        ls_.append(_read_classes(l_ref, dils[g], il_ref, slot))
        slot += dils[g] > 1
    mx = jnp.maximum(ls_[0], jnp.maximum(ls_[1], ls_[2]))
    es = [jnp.exp(v - mx) for v in ls_]
    att_o = (es[0] * os_[0] + es[1] * os_[1] + es[2] * os_[2]) / (es[0] + es[1] + es[2])
    _mix_tail(x_ref[...], pool_o, att_o, nattn_ref, wgate_ref, wbp_ref, wba_ref, wo_ref, nffn_ref, xo_ref, xn_ref)


def _wspec(shape, layer):
    nd = len(shape)
    return _resident((None,) + tuple(shape), lambda *_: (layer,) + (0,) * nd)


def _tail_weight_specs(layer):
    return [_wspec((1, D_MODEL), layer), _wspec((D_MODEL, 2 * D_MODEL), layer),
            _wspec((POOL_W, D_MODEL), layer), _wspec((GROUP_W, D_MODEL), layer),
            _wspec((D_MODEL, D_MODEL), layer), _wspec((1, D_MODEL), layer)]


def _tail_weights(w):
    return (w["norm_attn"], w["w_gate"], w["w_bp"], w["w_ba"], w["w_o"], w["norm_ffn"])


def _merge(x, u, outs, lses, layer, w, batch, seq, tm, dils):
    n = x.shape[0]
    n_tiles = seq // tm
    halo = POOL_STATE + 1
    row = lambda width: pl.BlockSpec((tm, width), lambda b, s: (b * n_tiles + s, 0))
    cls = lambda g: pl.BlockSpec((None, dils[g], tm // dils[g], GROUP_W), lambda b, s: (b, 0, s, 0))
    n_il = 2 * sum(d > 1 for d in dils)
    return pl.pallas_call(
        functools.partial(_merge_kernel, tm=tm, dils=dils),
        grid=(batch, n_tiles),
        in_specs=[row(D_MODEL), row(POOL_W),
                  pl.BlockSpec((halo, POOL_W),
                               lambda b, s: (jnp.maximum((b * n_tiles + s) * (tm // halo) - 1, 0), 0))]
                 + [cls(g) for g in range(N_GROUPS)] * 2
                 + [_wspec((len(POOL_WINDOWS), POOL_GROUP, POOL_GROUP), layer), _wspec((1, POOL_W), layer)]
                 + _tail_weight_specs(layer),
        out_specs=[row(D_MODEL), row(D_MODEL)],
        out_shape=[jax.ShapeDtypeStruct((n, D_MODEL), F32), jax.ShapeDtypeStruct((n, D_MODEL), BF16)],
        scratch_shapes=[pltpu.VMEM((halo + tm, POOL_W), F32), pltpu.VMEM((n_il, LANE_CHUNKS, tm, LANES), F32)],
        compiler_params=_params(("parallel", "parallel")),
        name="merge",
    )(x, u, u, *outs, *lses, w["w_pool"], w["pool_scale"], *_tail_weights(w))


def _head_mask():
    return jnp.asarray(np.arange(GROUP_W)[None, :] // HEAD_DIM == np.arange(HEADS)[:, None], F32)


def _sample_keys(dec_seq, cache_rows):
    slopes = _slopes()
    geo = []
    for gi, (win, dil) in enumerate(ATT_GROUPS):
        lc = cache_rows[gi]
        assert lc % dil == 0 and lc // dil == LANES
        cls = (lc + np.arange(dec_seq)) % dil
        n_cached = (cls.max() + 1) * LANES
        lane = np.arange(n_cached + LANES)
        sel = (np.arange(lc)[:, None] == (lane[:n_cached] % LANES) * dil + lane[:n_cached] // LANES)
        t = np.arange(dec_seq)[:, None]
        back = np.where(lane < n_cached, lc + t - ((lane % LANES) * dil + lane // LANES), t - (lane - n_cached))
        valid = (back >= 0) & (back % dil == 0) & (back // dil <= win // dil) & (lane < n_cached + dec_seq)
        bias = np.where(valid[:, None, :], -slopes[gi][None, :, None] * back[:, None, :], NEG_INF)
        geo.append((jnp.asarray(sel, BF16), jnp.asarray(bias.reshape(dec_seq * HEADS, -1), F32)))
    return geo


def _sample_mix_kernel(state_ref, u_ref, q0_ref, q1_ref, q2_ref, kn0_ref, kn1_ref, kn2_ref, c0_ref, c1_ref, c2_ref,
                       sel0_ref, sel1_ref, sel2_ref, bias0_ref, bias1_ref, bias2_ref, headmask_ref,
                       wpool_ref, pscale_ref, pool_ref, att_ref, newpool_ref, ext_ref, m_ref, *, dec_seq):
    ext_ref[...] = jnp.zeros_like(ext_ref)
    ext_ref[0:POOL_STATE, :] = state_ref[...]
    ext_ref[POOL_STATE:POOL_STATE + dec_seq, :] = u_ref[...]
    newpool_ref[...] = ext_ref[dec_seq:dec_seq + POOL_STATE, :]

    m_ref[...] = jnp.zeros_like(m_ref)
    for t in range(dec_seq):
        row = POOL_STATE + t
        for gi, win in enumerate(POOL_WINDOWS):
            cols = slice(gi * POOL_GROUP, (gi + 1) * POOL_GROUP)
            tot = jnp.sum(ext_ref[row - win + 1:row + 1, cols], axis=0, keepdims=True)
            cnt = float(min(PAST_LEN + t + 1, win))
            m_ref[t:t + 1, cols] = tot / cnt - ext_ref[row:row + 1, cols]
    pool_ref[...] = _group_linear(m_ref[...], wpool_ref, pscale_ref)[0:dec_seq]

    groups = ((q0_ref, kn0_ref, c0_ref, sel0_ref, bias0_ref), (q1_ref, kn1_ref, c1_ref, sel1_ref, bias1_ref),
              (q2_ref, kn2_ref, c2_ref, sel2_ref, bias2_ref))
    k_rows = HEADS * HEAD_DIM
    outs, lses = [], []
    for q_ref, kn_ref, c_ref, sel_ref, bias_ref in groups:
        flat = c_ref[...].reshape(2 * k_rows, c_ref.shape[-1]).astype(BF16)
        cached = _dot(flat, sel_ref[...]).astype(BF16)
        new = kn_ref[...].reshape(2 * k_rows, LANES).astype(BF16)
        keys = jnp.concatenate([cached[0:k_rows], new[0:k_rows]], axis=1)
        vals = jnp.concatenate([cached[k_rows:], new[k_rows:]], axis=1)
        s = _dot(q_ref[...], keys) + bias_ref[...]
        mx = jnp.max(s, axis=1, keepdims=True)
        p = jnp.exp(s - mx)
        l = jnp.sum(p, axis=1, keepdims=True)
        outs.append(lax.dot_general((p / l).astype(BF16), vals, (((1,), (1,)), ((), ())),
                                    preferred_element_type=F32))
        lses.append(mx + jnp.log(l))
    mx = jnp.maximum(lses[0], jnp.maximum(lses[1], lses[2]))
    es = [jnp.exp(v - mx) for v in lses]
    mixed = (es[0] * outs[0] + es[1] * outs[1] + es[2] * outs[2]) / (es[0] + es[1] + es[2])
    att_ref[...] = jnp.sum(mixed.reshape(dec_seq, HEADS, k_rows) * headmask_ref[...][None], axis=1)


def _sample_mix(state_pool, u, qs, kvn, caches, layer, w, dec_batch, dec_seq):
    cache_rows = [c.shape[2] for c in caches]
    geo = _sample_keys(dec_seq, cache_rows)
    consts = [g[i] for i in (0, 1) for g in geo] + [_head_mask()]
    kv_tail = (2, HEADS, HEAD_DIM)
    cviews = [jnp.transpose(c, (0, 1, 3, 4, 5, 2)).reshape((-1,) + kv_tail + (c.shape[2],)) for c in caches]
    cspecs = [pl.BlockSpec((None,) + kv_tail + (lc,), lambda b: (layer * dec_batch + b, 0, 0, 0, 0))
              for lc in cache_rows]
    per_batch = lambda *tail: pl.BlockSpec((None,) + tail, lambda b: (b,) + (0,) * len(tail))
    const = lambda shape: _resident(shape, lambda b: (0,) * len(shape))
    return pl.pallas_call(
        functools.partial(_sample_mix_kernel, dec_seq=dec_seq),
        grid=(dec_batch,),
        in_specs=[pl.BlockSpec((None, POOL_STATE, POOL_W), lambda b: (layer * dec_batch + b, 0, 0)),
                  per_batch(dec_seq, POOL_W)] + [per_batch(dec_seq * HEADS, GROUP_W)] * 3
                 + [per_batch(*kv_tail, LANES)] * 3 + cspecs + [const(c.shape) for c in consts]
                 + [_wspec((len(POOL_WINDOWS), POOL_GROUP, POOL_GROUP), layer), _wspec((1, POOL_W), layer)],
        out_specs=[per_batch(dec_seq, POOL_W), per_batch(dec_seq, GROUP_W),
                   pl.BlockSpec((None, POOL_STATE, POOL_W), lambda b: (b, 0, 0))],
        out_shape=[jax.ShapeDtypeStruct((dec_batch, dec_seq, POOL_W), F32),
                   jax.ShapeDtypeStruct((dec_batch, dec_seq, GROUP_W), F32),
                   jax.ShapeDtypeStruct((dec_batch, POOL_STATE, POOL_W), F32)],
        scratch_shapes=[pltpu.VMEM((POOL_STATE + dec_seq + 5, POOL_W), F32), pltpu.VMEM((8, POOL_W), F32)],
        compiler_params=_params(("parallel",)),
        name="sample_mix",
    )(state_pool.reshape(-1, POOL_STATE, POOL_W), u.reshape(dec_batch, dec_seq, POOL_W), *qs, *kvn, *cviews,
      *consts, w["w_pool"], w["pool_scale"])


def _tail_kernel(x_ref, pool_ref, att_ref, nattn_ref, wgate_ref, wbp_ref, wba_ref, wo_ref, nffn_ref, xo_ref, xn_ref):
    _mix_tail(x_ref[...], pool_ref[...], att_ref[...], nattn_ref, wgate_ref, wbp_ref, wba_ref, wo_ref, nffn_ref,
              xo_ref, xn_ref)


def _tail(x, pool_o, att_o, layer, w):
    n = x.shape[0]
    row = lambda width: pl.BlockSpec((n, width), lambda i: (0, 0))
    return pl.pallas_call(
        _tail_kernel,
        grid=(1,),
        in_specs=[row(D_MODEL), row(POOL_W), row(GROUP_W)] + _tail_weight_specs(layer),
        out_specs=[row(D_MODEL), row(D_MODEL)],
        out_shape=[jax.ShapeDtypeStruct((n, D_MODEL), F32), jax.ShapeDtypeStruct((n, D_MODEL), BF16)],
        compiler_params=_params(("arbitrary",)),
        name="tail",
    )(x, pool_o, att_o, *_tail_weights(w))


def _swiglu_step(x, wg_ref, wu_ref, wd_ref, acc_ref):
    h = jax.nn.silu(_dot(x, wg_ref[...])) * _dot(x, wu_ref[...])
    acc_ref[...] += _dot(h.astype(BF16), wd_ref[...])


def _ffn_kernel(x_ref, xn_ref, w1_ref, w3_ref, w2_ref, y_ref):
    xn = xn_ref[...]
    h = jax.nn.silu(_dot(xn, w1_ref[...])) * _dot(xn, w3_ref[...])
    y_ref[...] = x_ref[...] + _dot(h.astype(BF16), w2_ref[...])


def _ffn(x, xn, w1, w3, w2, idx, tm):
    n = x.shape[0]
    d_ff = w1.shape[-1]
    return pl.pallas_call(
        _ffn_kernel,
        grid=(n // tm,),
        in_specs=[pl.BlockSpec((tm, D_MODEL), lambda i: (i, 0)),
                  pl.BlockSpec((tm, D_MODEL), lambda i: (i, 0)),
                  _resident((None, D_MODEL, d_ff), lambda i: (idx, 0, 0)),
                  _resident((None, D_MODEL, d_ff), lambda i: (idx, 0, 0)),
                  _resident((None, d_ff, D_MODEL), lambda i: (idx, 0, 0))],
        out_specs=pl.BlockSpec((tm, D_MODEL), lambda i: (i, 0)),
        out_shape=jax.ShapeDtypeStruct((n, D_MODEL), F32),
        compiler_params=_params(("parallel",)),
        name="ffn",
    )(x, xn, w1, w3, w2)


SEG_ALIGN = 16
MOE_ROWS = 512


def _sorted_cap(ts):
    need = TOP_K * ts + N_EXPERTS * (SEG_ALIGN - 1)
    return -(-need // LANES) * LANES if ts >= LANES * 2 else -(-need // SEG_ALIGN) * SEG_ALIGN


def _segment_sizes(oh1, oh2):
    cnt1 = jnp.sum(oh1, axis=1, keepdims=True)
    cnt = cnt1 + jnp.sum(oh2, axis=1, keepdims=True)
    cpad = jnp.floor((cnt + (SEG_ALIGN - 1)) * (1.0 / SEG_ALIGN)) * SEG_ALIGN
    offs = [jnp.zeros((1, 1), F32)]
    for ei in range(1, N_EXPERTS):
        offs.append(offs[-1] + cpad[ei - 1:ei, :])
    return cnt1, cpad, jnp.concatenate(offs, axis=0)


def _gate_kernel(x_ref, g_ref, rwt_ref, rb_ref, route_ref, seg_ref, *, ts):
    xn = _rms(x_ref[...], g_ref[...]).astype(BF16)
    logits = lax.dot_general(rwt_ref[...], xn, (((1,), (1,)), ((), ())), preferred_element_type=F32) + rb_ref[...]
    row = lax.broadcasted_iota(jnp.int32, (N_EXPERTS, ts), 0)
    neg = jnp.float32(-jnp.inf)
    m1 = jnp.max(logits, axis=0, keepdims=True)
    i1 = jnp.min(jnp.where(logits == m1, row, N_EXPERTS), axis=0, keepdims=True)
    rest = jnp.where(row == i1, neg, logits)
    m2 = jnp.max(rest, axis=0, keepdims=True)
    i2 = jnp.min(jnp.where(rest == m2, row, N_EXPERTS), axis=0, keepdims=True)
    e = jnp.exp(m2 - m1)
    zeros = jnp.zeros((N_EXPERTS - 4, ts), F32)
    route_ref[...] = jnp.concatenate([i1.astype(F32), i2.astype(F32), 1.0 / (1.0 + e), e / (1.0 + e), zeros], axis=0)
    _, cpad, off = _segment_sizes(jnp.where(row == i1, 1.0, 0.0), jnp.where(row == i2, 1.0, 0.0))
    lane = lax.broadcasted_iota(jnp.int32, (N_EXPERTS, LANES), 1)
    seg_ref[...] = jnp.where(lane == 0, off, jnp.where(lane == 1, cpad, 0.0)).astype(jnp.int32)


def _gate(x, norm_ffn, rwt, rb, layer, idx, ts):
    n = x.shape[0]
    n_sub = n // ts
    return pl.pallas_call(
        functools.partial(_gate_kernel, ts=ts),
        grid=(n_sub,),
        in_specs=[pl.BlockSpec((ts, D_MODEL), lambda i: (i, 0)),
                  _resident((None, 1, D_MODEL), lambda i: (layer, 0, 0)),
                  _resident((None, N_EXPERTS, D_MODEL), lambda i: (idx, 0, 0)),
                  _resident((None, N_EXPERTS, 1), lambda i: (idx, 0, 0))],
        out_specs=[pl.BlockSpec((None, N_EXPERTS, ts), lambda i: (i, 0, 0)),
                   pl.BlockSpec((None, N_EXPERTS, LANES), lambda i: (i, 0, 0))],
        out_shape=[jax.ShapeDtypeStruct((n_sub, N_EXPERTS, ts), F32),
                   jax.ShapeDtypeStruct((n_sub, N_EXPERTS, LANES), jnp.int32)],
        compiler_params=_params(("parallel",)),
        name="gate",
    )(x, norm_ffn, rwt, rb)


def _segment_copies(i, n_bits, local_ref, off_ref, far_ref, far_rows_ref, nseg_ref, sem, to_far):
    copies = []
    for e in range(N_EXPERTS):
        j = i * N_EXPERTS + e
        n = nseg_ref[j]
        for b in reversed(range(n_bits)):
            size = SEG_ALIGN << b
            done = ((n >> (b + 1)) << (b + 1)) * SEG_ALIGN
            near = local_ref.at[pl.ds(pl.multiple_of(off_ref[j] + done, SEG_ALIGN), size)]
            far = far_ref.at[pl.ds(pl.multiple_of(far_rows_ref[j] + done, SEG_ALIGN), size)]
            cp = pltpu.make_async_copy(near, far, sem) if to_far else pltpu.make_async_copy(far, near, sem)
            copies.append((((n >> b) & 1) == 1, cp))
    return copies


def _scatter_kernel(off_ref, far_rows_ref, nseg_ref, x_ref, g_ref, route_ref, tri_ref, xb_in_ref,
                    meta_ref, xb_ref, xs_ref, sem, *, ts, cap, n_bits):
    del xb_in_ref
    i = pl.program_id(0)
    xn = _rms(x_ref[...], g_ref[...]).astype(BF16)
    row = lax.broadcasted_iota(jnp.int32, (N_EXPERTS, ts), 0).astype(F32)
    oh1 = jnp.where(row == route_ref[0:1, :], 1.0, 0.0)
    oh2 = jnp.where(row == route_ref[1:2, :], 1.0, 0.0)
    cnt1, _, off = _segment_sizes(oh1, oh2)
    pre1 = _dot(oh1.astype(BF16), tri_ref[...])
    pre2 = _dot(oh2.astype(BF16), tri_ref[...])
    d1 = jnp.sum(oh1 * (off + pre1), axis=0, keepdims=True)
    d2 = jnp.sum(oh2 * (off + cnt1 + pre2), axis=0, keepdims=True)
    slot = lax.broadcasted_iota(jnp.int32, (cap, ts), 0).astype(F32)
    p = jnp.where(slot == d1, 1.0, jnp.where(slot == d2, 1.0, 0.0)).astype(BF16)
    xs_ref[i % 2] = _dot(p, xn).astype(BF16)
    meta_t = jnp.concatenate([d1, d2, route_ref[2:4, :], jnp.zeros((LANES - 4, ts), F32)], axis=0)
    meta_ref[...] = meta_t.T

    def send(step, action):
        buf = step % 2
        for pred, cp in _segment_copies(step, n_bits, xs_ref.at[buf], off_ref, xb_ref, far_rows_ref, nseg_ref,
                                        sem.at[buf], True):
            pl.when(pred)(getattr(cp, action))

    send(i, "start")
    pl.when(i > 0)(lambda: send(i - 1, "wait"))
    pl.when(i == pl.num_programs(0) - 1)(lambda: send(i, "wait"))


def _scatter(off, far_rows, nseg, x, norm_ffn, route, xb, layer, ts):
    n = x.shape[0]
    n_sub = n // ts
    cap = _sorted_cap(ts)
    tri = jnp.asarray(np.triu(np.ones((ts, ts), np.float32), 1), BF16)
    n_bits = int(TOP_K * ts // SEG_ALIGN).bit_length()
    return pl.pallas_call(
        functools.partial(_scatter_kernel, ts=ts, cap=cap, n_bits=n_bits),
        grid_spec=pltpu.PrefetchScalarGridSpec(
            num_scalar_prefetch=3,
            grid=(n_sub,),
            in_specs=[pl.BlockSpec((ts, D_MODEL), lambda i, *_: (i, 0)),
                      _resident((None, 1, D_MODEL), lambda i, *_: (layer, 0, 0)),
                      pl.BlockSpec((None, N_EXPERTS, ts), lambda i, *_: (i, 0, 0)),
                      _resident((ts, ts), lambda i, *_: (0, 0)),
                      pl.BlockSpec(memory_space=pl.ANY)],
            out_specs=[pl.BlockSpec((ts, LANES), lambda i, *_: (i, 0)),
                       pl.BlockSpec(memory_space=pl.ANY)],
            scratch_shapes=[pltpu.VMEM((2, cap, D_MODEL), BF16), pltpu.SemaphoreType.DMA((2,))]),
        out_shape=[jax.ShapeDtypeStruct((n, LANES), F32), jax.ShapeDtypeStruct(xb.shape, xb.dtype)],
        input_output_aliases={7: 1},
        compiler_params=_params(("arbitrary",)),
        name="scatter",
    )(off, far_rows, nseg, x, norm_ffn, route, tri, xb)


def _experts_kernel(blk_ref, nreal_ref, xb_ref, wg_ref, wu_ref, wd_ref, yb_ref, acc_ref):
    del blk_ref
    i = pl.program_id(0)
    f = pl.program_id(1)

    @pl.when(f == 0)
    def _():
        acc_ref[...] = jnp.zeros_like(acc_ref)

    @pl.when(i < nreal_ref[0])
    def _():
        _swiglu_step(xb_ref[...], wg_ref, wu_ref, wd_ref, acc_ref)

    @pl.when(f == pl.num_programs(1) - 1)
    def _():
        yb_ref[...] = acc_ref[...].astype(yb_ref.dtype)


def _experts(blk_e, nreal, xb, wg, wu, wd, idx, tf):
    n_blocks = blk_e.shape[0]
    d_ff = wg.shape[-1]
    n_f = d_ff // tf

    def live_block(i, nreal):
        return jnp.minimum(i, nreal[0] - 1)

    def wcol(i, f, blk, nreal):
        return idx, blk[live_block(i, nreal)], 0, jnp.where(i < nreal[0], f, n_f - 1)

    def wrow(i, f, blk, nreal):
        return idx, blk[live_block(i, nreal)], jnp.where(i < nreal[0], f, n_f - 1), 0

    return pl.pallas_call(
        _experts_kernel,
        grid_spec=pltpu.PrefetchScalarGridSpec(
            num_scalar_prefetch=2,
            grid=(n_blocks, n_f),
            in_specs=[pl.BlockSpec((MOE_ROWS, D_MODEL), lambda i, f, blk, nreal: (live_block(i, nreal), 0)),
                      pl.BlockSpec((None, None, D_MODEL, tf), wcol),
                      pl.BlockSpec((None, None, D_MODEL, tf), wcol),
                      pl.BlockSpec((None, None, tf, D_MODEL), wrow)],
            out_specs=pl.BlockSpec((MOE_ROWS, D_MODEL), lambda i, f, blk, nreal: (i, 0)),
            scratch_shapes=[pltpu.VMEM((MOE_ROWS, D_MODEL), F32)]),
        out_shape=jax.ShapeDtypeStruct((n_blocks * MOE_ROWS, D_MODEL), BF16),
        compiler_params=_params(("parallel", "arbitrary")),
        name="experts",
    )(blk_e, nreal, xb, wg, wu, wd)


def _combine_kernel(off_ref, far_rows_ref, nseg_ref, x_ref, meta_ref, yb_ref, o_ref, ys_ref, sem, *, ts, cap, n_bits):
    i = pl.program_id(0)

    def fetch(step, action):
        buf = step % 2
        for pred, cp in _segment_copies(step, n_bits, ys_ref.at[buf], off_ref, yb_ref, far_rows_ref, nseg_ref,
                                        sem.at[buf], False):
            pl.when(pred)(getattr(cp, action))

    def start(step):
        ys_ref[step % 2] = jnp.zeros((cap, D_MODEL), BF16)
        fetch(step, "start")

    pl.when(i == 0)(lambda: start(i))
    pl.when(i + 1 < pl.num_programs(0))(lambda: start(i + 1))
    fetch(i, "wait")
    slot = lax.broadcasted_iota(jnp.int32, (ts, cap), 1).astype(F32)
    ys = ys_ref[i % 2]
    q1 = jnp.where(slot == meta_ref[:, 0:1], 1.0, 0.0).astype(BF16)
    q2 = jnp.where(slot == meta_ref[:, 1:2], 1.0, 0.0).astype(BF16)
    o_ref[...] = x_ref[...] + meta_ref[:, 2:3] * _dot(q1, ys) + meta_ref[:, 3:4] * _dot(q2, ys)


def _combine(off, far_rows, nseg, x, meta, yb, ts):
    n = x.shape[0]
    cap = _sorted_cap(ts)
    n_bits = int(TOP_K * ts // SEG_ALIGN).bit_length()
    return pl.pallas_call(
        functools.partial(_combine_kernel, ts=ts, cap=cap, n_bits=n_bits),
        grid_spec=pltpu.PrefetchScalarGridSpec(
            num_scalar_prefetch=3,
            grid=(n // ts,),
            in_specs=[pl.BlockSpec((ts, D_MODEL), lambda i, *_: (i, 0)),
                      pl.BlockSpec((ts, LANES), lambda i, *_: (i, 0)),
                      pl.BlockSpec(memory_space=pl.ANY)],
            out_specs=pl.BlockSpec((ts, D_MODEL), lambda i, *_: (i, 0)),
            scratch_shapes=[pltpu.VMEM((2, cap, D_MODEL), BF16), pltpu.SemaphoreType.DMA((2,))]),
        out_shape=jax.ShapeDtypeStruct((n, D_MODEL), F32),
        compiler_params=_params(("arbitrary",)),
        name="combine",
    )(off, far_rows, nseg, x, meta, yb)


def _moe(xs, tss, norm_ffn, rwt, rb, wg, wu, wd, layer, idx, tf):
    gated = [_gate(x, norm_ffn, rwt, rb, layer, idx, ts) for x, ts in zip(xs, tss)]
    off = jnp.concatenate([g[1][:, :, 0] for g in gated], axis=0)
    cpad = jnp.concatenate([g[1][:, :, 1] for g in gated], axis=0)
    n_subs = [g[1].shape[0] for g in gated]
    tot = jnp.sum(cpad, axis=0)
    padded = (tot + MOE_ROWS - 1) // MOE_ROWS * MOE_ROWS
    pend = jnp.cumsum(padded)
    within = jnp.cumsum(cpad, axis=0) - cpad
    off = off.astype(jnp.int32).reshape(-1)
    grouped_rows = ((pend - padded)[None, :] + within).astype(jnp.int32).reshape(-1)
    nseg = (cpad // SEG_ALIGN).astype(jnp.int32).reshape(-1)
    n_assign = TOP_K * sum(x.shape[0] for x in xs)
    n_blocks = (n_assign + (SEG_ALIGN - 1) * sum(n_subs) * N_EXPERTS + N_EXPERTS * (MOE_ROWS - 1)) // MOE_ROWS
    blk_e = jnp.minimum(jnp.sum(jnp.arange(n_blocks)[:, None] * MOE_ROWS >= pend[None, :], axis=1),
                        N_EXPERTS - 1).astype(jnp.int32)
    nreal = (pend[-1:] // MOE_ROWS).astype(jnp.int32)

    xb = jnp.zeros((n_blocks * MOE_ROWS, D_MODEL), BF16)
    metas, parts = [], []
    lo = 0
    for x, g, ns, ts in zip(xs, gated, n_subs, tss):
        sl = slice(lo * N_EXPERTS, (lo + ns) * N_EXPERTS)
        parts.append((off[sl], grouped_rows[sl], nseg[sl]))
        meta, xb = _scatter(*parts[-1], x, norm_ffn, g[0], xb, layer, ts)
        metas.append(meta)
        lo += ns
    yb = _experts(blk_e, nreal, xb, wg, wu, wd, idx, tf)
    return [_combine(*part, x, meta, yb, ts) for part, x, meta, ts in zip(parts, xs, metas, tss)]


def kernel(x_prompt, x_sample, state_pool, cache_kv_g0, cache_kv_g1, cache_kv_g2, norm_attn, w_in, q_norm, k_norm, w_pool, pool_scale, w_branch_pool, w_branch_attn, w_out, norm_ffn, w1_dense, w3_dense, w2_dense, router_w, router_b, we_gate, we_up, we_down):
    batch, seq, _ = x_prompt.shape
    dec_batch, dec_seq, _ = x_sample.shape
    depth = w_in.shape[0]
    caches = (cache_kv_g0, cache_kv_g1, cache_kv_g2)
    n_p, n_s = batch * seq, dec_batch * dec_seq
    tm_p = 512
    dils = tuple(d for _, d in ATT_GROUPS)
    keeps = tuple(min(win, seq) for win, _ in ATT_GROUPS)

    head_of = np.arange(GROUP_W) // HEAD_DIM
    same_head = (head_of[:, None] == head_of[None, :]).astype(np.float32)
    seg_mean = jnp.asarray(same_head / HEAD_DIM, BF16)
    w_qkv = w_in[:, :, :PROJ_W].astype(BF16)
    qgain = (jnp.tile(q_norm, (1, HEADS)) * (HEAD_DIM ** -0.5)).reshape(depth, 1, GROUP_W)
    kgain = jnp.tile(k_norm, (1, HEADS)).reshape(depth, 1, GROUP_W)
    norm_attn3 = norm_attn.reshape(depth, 1, D_MODEL)
    w = {"w_pool": w_pool.astype(BF16), "pool_scale": pool_scale.reshape(depth, 1, POOL_W),
         "norm_attn": norm_attn3, "w_gate": w_in[:, :, PROJ_W:].astype(BF16),
         "w_bp": w_branch_pool.astype(BF16), "w_ba": w_branch_attn.astype(BF16), "w_o": w_out.astype(BF16),
         "norm_ffn": norm_ffn.reshape(depth, 1, D_MODEL)}
    w1_b, w3_b, w2_b = w1_dense.astype(BF16), w3_dense.astype(BF16), w2_dense.astype(BF16)
    wg_b, wu_b, wd_b = we_gate.astype(BF16), we_up.astype(BF16), we_down.astype(BF16)
    rwt = jnp.swapaxes(router_w, 1, 2).astype(BF16)
    rb = router_b.reshape(-1, N_EXPERTS, 1)

    xp = x_prompt.reshape(n_p, D_MODEL)
    xs = x_sample.reshape(n_s, D_MODEL)
    pool_p, pool_s = [], []
    kv_s = [[] for _ in ATT_GROUPS]
    kvo_p = tuple(jnp.zeros((depth, batch, 2, GROUP_W, keep), F32) for keep in keeps)
    ones = (1,) * N_GROUPS
    for layer in range(depth):
        res = _proj(xp, layer, norm_attn3, w_qkv, qgain, kgain, seg_mean, batch, seq, tm_p, dils, keeps,
                    layer, depth, kvo_p, 1, True)
        u, qs, kcs, kvo_p = res[0], res[1:4], res[4:7], tuple(res[7:10])
        outs, lses = [], []
        for gi in range(N_GROUPS):
            o, lse = _attention(qs[gi], kcs[gi], gi)
            outs.append(o)
            lses.append(lse)
        pool_p.append(u.reshape(batch, seq, POOL_W)[:, seq - POOL_STATE:])
        xp, xpn = _merge(xp, u, outs, lses, layer, w, batch, seq, tm_p, dils)

        res = _proj(xs, layer, norm_attn3, w_qkv, qgain, kgain, seg_mean, 1, n_s, n_s, ones, (n_s,) * N_GROUPS,
                    0, 1, (), 3, False)
        u = res[0]
        qs = [(q.reshape(dec_batch, dec_seq, 1, GROUP_W) * _head_mask().astype(BF16))
              .reshape(dec_batch, dec_seq * HEADS, GROUP_W) for q in res[1:4]]
        kv_new = [kv.reshape(dec_batch, dec_seq, 2, HEADS, HEAD_DIM) for kv in res[7:10]]
        tokens_to_lanes = ((0, 0),) * 4 + ((0, LANES - dec_seq),)
        kvn = [jnp.pad(kv.transpose(0, 2, 3, 4, 1), tokens_to_lanes) for kv in kv_new]
        pool_o, att_o, new_pool = _sample_mix(state_pool, u, qs, kvn, caches, layer, w, dec_batch, dec_seq)
        att_o = att_o.reshape(n_s, GROUP_W)
        for gi in range(N_GROUPS):
            kv_s[gi].append(kv_new[gi])
        pool_s.append(new_pool)
        xs, xsn = _tail(xs, pool_o.reshape(n_s, POOL_W), att_o, layer, w)

        i = layer // 2
        if layer % 2 == 0:
            xp = _ffn(xp, xpn, w1_b, w3_b, w2_b, i, 512)
            xs = _ffn(xs, xsn, w1_b, w3_b, w2_b, i, n_s)
        else:
            xp, xs = _moe([xp, xs], [512, n_s], w["norm_ffn"], rwt, rb, wg_b, wu_b, wd_b, layer, i,
                           wg_b.shape[-1] // 2)
    kv_p = [kvo_p[g].reshape(depth, batch, 2, HEADS, HEAD_DIM, keeps[g]).transpose(0, 1, 5, 2, 3, 4)
            for g in range(N_GROUPS)]
    return (xp.reshape(batch, seq, D_MODEL), xs.reshape(dec_batch, dec_seq, D_MODEL),
            jnp.stack(pool_p), jnp.stack(pool_s),
            kv_p[0], jnp.stack(kv_s[0]),
            kv_p[1], jnp.stack(kv_s[1]),
            kv_p[2], jnp.stack(kv_s[2]))
```

```python
import functools

import numpy as np
import jax
import jax.numpy as jnp
from jax import lax
from jax.experimental import pallas as pl
from jax.experimental.pallas import tpu as pltpu

F32 = jnp.float32
BF16 = jnp.bfloat16

D_MODEL = 1024
PAST_LEN = 16384
POOL_WINDOWS = (2, 4, 8, 16)
POOL_GROUP = 128
POOL_W = 512
POOL_STATE = 15
ATT_GROUPS = ((128, 1), (512, 4), (2048, 16))
N_GROUPS = len(ATT_GROUPS)
HEAD_DIM = 64
HEADS = 8
GROUP_W = 512
QKV_W = 1536
Q_BLOCK = 128
ALIBI_MAX = 8.0
N_EXPERTS = 8
TOP_K = 2
RMS_EPS = 1e-6
NEG_INF = -1e30
LANES = 128
LANE_CHUNKS = GROUP_W // LANES
PROJ_W = POOL_W + 3 * QKV_W
V7X_VMEM_BYTES = 64 * 1024 * 1024
VMEM_LIMIT = V7X_VMEM_BYTES * 7 // 8


def _slopes():
    i = np.arange(1, N_GROUPS * HEADS + 1, dtype=np.float32)
    return np.exp2(-ALIBI_MAX * i / (N_GROUPS * HEADS)).astype(np.float32).reshape(N_GROUPS, HEADS)


def _params(sem):
    return pltpu.CompilerParams(dimension_semantics=sem, vmem_limit_bytes=VMEM_LIMIT)


def _rms(x, gain):
    return x * lax.rsqrt(jnp.mean(x * x, axis=-1, keepdims=True) + RMS_EPS) * gain


def _dot(a, b):
    return jnp.dot(a, b, preferred_element_type=F32)


def _resident(shape, index_map):
    return pl.BlockSpec(shape, index_map, pipeline_mode=pl.Buffered(1))


def _write_classes(dst_ref, col0, val, dil, tmp_ref, slot):
    rows = val.shape[0] // dil
    cols = slice(col0, col0 + GROUP_W)
    if dil == 1:
        dst_ref[0, :, cols] = val.astype(dst_ref.dtype)
        return
    for c in range(LANE_CHUNKS):
        tmp_ref[slot, c] = val[:, c * LANES:(c + 1) * LANES]
    for r in range(dil):
        picked = [tmp_ref[slot, c, pl.ds(r, rows, stride=dil), :] for c in range(LANE_CHUNKS)]
        dst_ref[r, :, cols] = jnp.concatenate(picked, axis=1).astype(dst_ref.dtype)


def _proj_kernel(x_ref, g_ref, w_ref, qg_ref, kg_ref, seg_ref, *rest, tm, n_tiles, dils, keeps, n_alias,
                 norm_terms, kv_positions_minor, out_layer, fill_layers):
    u_ref, q0_ref, q1_ref, q2_ref, kc0_ref, kc1_ref, kc2_ref, kvo0_ref, kvo1_ref, kvo2_ref, tmp_ref = rest[n_alias:]
    s = pl.program_id(1)
    xn = _rms(x_ref[...], g_ref[...]).astype(BF16)

    def zblk(j):
        return _dot(xn, w_ref[:, j * GROUP_W:(j + 1) * GROUP_W])

    def headnorm(z, gain):
        rem = z * z
        ms = None
        for _ in range(norm_terms):
            part = rem.astype(BF16)
            rem = rem - part.astype(F32)
            ms = _dot(part, seg_ref[...]) if ms is None else ms + _dot(part, seg_ref[...])
        return z * lax.rsqrt(ms + RMS_EPS) * gain

    u_ref[...] = zblk(0)
    q_refs = (q0_ref, q1_ref, q2_ref)
    kc_refs = (kc0_ref, kc1_ref, kc2_ref)
    kvo_refs = (kvo0_ref, kvo1_ref, kvo2_ref)
    slot = 0
    kept = {}
    for g in range(N_GROUPS):
        q = headnorm(zblk(1 + g), qg_ref[...])
        k = headnorm(zblk(1 + N_GROUPS + g), kg_ref[...])
        v = zblk(1 + 2 * N_GROUPS + g)
        for dst, col0, val in ((q_refs[g], 0, q), (kc_refs[g], 0, k), (kc_refs[g], GROUP_W, v)):
            _write_classes(dst, col0, val, dils[g], tmp_ref, slot % tmp_ref.shape[0])
            slot += dils[g] > 1
        kept.setdefault(n_tiles - max(keeps[g] // tm, 1), []).append((kvo_refs[g], min(keeps[g], tm), k, v))

    def write_kept(items):
        for ref, rows, k, v in items:
            if fill_layers:
                for other in range(fill_layers):
                    if other != out_layer:
                        ref[other] = jnp.zeros(ref.shape[1:], ref.dtype)
                ref = ref.at[out_layer]
            if kv_positions_minor:
                ref[0] = k[tm - rows:, :].T
                ref[1] = v[tm - rows:, :].T
            else:
                ref[:, 0:GROUP_W] = k[tm - rows:, :]
                ref[:, GROUP_W:2 * GROUP_W] = v[tm - rows:, :]

    for first, items in kept.items():
        if first == 0:
            write_kept(items)
        else:
            pl.when(s >= first)(functools.partial(write_kept, items))


def _proj(x, layer, norm_attn, w_qkv, qgain, kgain, seg, batch, seq, tm, dils, keeps, out_layer, out_depth, prev_kvo,
          norm_terms, kv_positions_minor):
    n_tiles = seq // tm
    kvo_specs, kvo_shapes = [], []
    fill_layers = out_depth if (out_depth > 1 and not prev_kvo) else 0
    layer_block, layer_index = (out_depth, 0) if fill_layers else (None, out_layer)
    for g in range(N_GROUPS):
        keep = keeps[g]
        assert keep % tm == 0 or (keep < tm and keep % LANES == 0)
        first = n_tiles - max(keep // tm, 1)
        rows = min(keep, tm)
        if kv_positions_minor:
            kvo_specs.append(pl.BlockSpec((layer_block, None, 2, GROUP_W, rows),
                                          lambda b, s, first=first: (layer_index, b, 0, 0, jnp.maximum(s - first, 0))))
            kvo_shapes.append(jax.ShapeDtypeStruct((out_depth, batch, 2, GROUP_W, keep), F32))
        else:
            kvo_specs.append(pl.BlockSpec((layer_block, None, rows, 2 * GROUP_W),
                                          lambda b, s, first=first: (layer_index, b, jnp.maximum(s - first, 0), 0)))
            kvo_shapes.append(jax.ShapeDtypeStruct((out_depth, batch, keep, 2 * GROUP_W), F32))
    cls_spec = lambda g, width: pl.BlockSpec((None, dils[g], tm // dils[g], width), lambda b, s: (b, 0, s, 0))
    cls_shape = lambda g, width: jax.ShapeDtypeStruct((batch, dils[g], seq // dils[g], width), BF16)
    n_alias = len(prev_kvo)
    n_in = 6
    return pl.pallas_call(
        functools.partial(_proj_kernel, tm=tm, n_tiles=n_tiles, dils=dils, keeps=keeps, n_alias=n_alias,
                          norm_terms=norm_terms, kv_positions_minor=kv_positions_minor, out_layer=out_layer,
                          fill_layers=fill_layers),
        grid=(batch, n_tiles),
        in_specs=[
            pl.BlockSpec((tm, D_MODEL), lambda b, s: (b * n_tiles + s, 0)),
            _resident((None, 1, D_MODEL), lambda b, s: (layer, 0, 0)),
            _resident((None, D_MODEL, PROJ_W), lambda b, s: (layer, 0, 0)),
            _resident((None, 1, GROUP_W), lambda b, s: (layer, 0, 0)),
            _resident((None, 1, GROUP_W), lambda b, s: (layer, 0, 0)),
            _resident((GROUP_W, GROUP_W), lambda b, s: (0, 0)),
        ] + [pl.BlockSpec(memory_space=pl.ANY)] * n_alias,
        out_specs=[pl.BlockSpec((tm, POOL_W), lambda b, s: (b * n_tiles + s, 0))]
                  + [cls_spec(g, GROUP_W) for g in range(N_GROUPS)]
                  + [cls_spec(g, 2 * GROUP_W) for g in range(N_GROUPS)] + kvo_specs,
        out_shape=[jax.ShapeDtypeStruct((batch * seq, POOL_W), F32)]
                  + [cls_shape(g, GROUP_W) for g in range(N_GROUPS)]
                  + [cls_shape(g, 2 * GROUP_W) for g in range(N_GROUPS)] + kvo_shapes,
        scratch_shapes=[pltpu.VMEM((3, LANE_CHUNKS, tm, LANES), F32)],
        input_output_aliases={n_in + g: 1 + 2 * N_GROUPS + g for g in range(n_alias)},
        compiler_params=_params(("parallel", "arbitrary")),
        name="proj",
    )(x, norm_attn, w_qkv, qgain, kgain, seg, *prev_kvo)


def _attn_kernel(q_ref, kvc_ref, kvh_ref, o_ref, lse_ref, kbuf, vbuf, *, slope_dil, tc, isolated_tiles):
    off = Q_BLOCK
    kbuf[0:off, :] = kvh_ref[:, 0:GROUP_W]
    vbuf[0:off, :] = kvh_ref[:, GROUP_W:2 * GROUP_W]
    kbuf[off:off + tc, :] = kvc_ref[:, 0:GROUP_W]
    vbuf[off:off + tc, :] = kvc_ref[:, GROUP_W:2 * GROUP_W]

    nk = Q_BLOCK + off
    kj = lax.broadcasted_iota(jnp.int32, (nk, Q_BLOCK), 0)
    qi = lax.broadcasted_iota(jnp.int32, (nk, Q_BLOCK), 1)
    dist = qi - kj + off
    distf = dist.astype(F32)
    maskneg = jnp.where((dist >= 0) & (dist <= Q_BLOCK), 0.0, NEG_INF).astype(F32)
    first = NEG_INF if isolated_tiles else jnp.where(pl.program_id(2) == 0, NEG_INF, 0.0).astype(F32)
    mask_first = maskneg + jnp.where(kj < Q_BLOCK, first, 0.0)

    for i in range(tc // Q_BLOCK):
        mask = mask_first if (isolated_tiles or i == 0) else maskneg
        rows = slice(i * Q_BLOCK, (i + 1) * Q_BLOCK)
        krows = slice(i * Q_BLOCK, i * Q_BLOCK + nk)
        for hp in range(HEADS // 2):
            outs, lses = [], []
            for h in (2 * hp, 2 * hp + 1):
                cols = slice(h * HEAD_DIM, (h + 1) * HEAD_DIM)
                s = lax.dot_general(kbuf[krows, cols], q_ref[rows, cols],
                                    (((1,), (1,)), ((), ())), preferred_element_type=F32)
                s = s - slope_dil[h] * distf + mask
                m = jnp.max(s, axis=0, keepdims=True)
                p = jnp.exp(s - m)
                l = jnp.sum(p, axis=0, keepdims=True)
                outs.append(lax.dot_general(vbuf[krows, cols], p.astype(BF16), (((0,), (0,)), ((), ())),
                                            preferred_element_type=F32) / l)
                lses.append(jnp.broadcast_to(m + jnp.log(l), (HEAD_DIM, Q_BLOCK)))
            pc = slice(hp * 2 * HEAD_DIM, (hp + 1) * 2 * HEAD_DIM)
            o_ref[rows, pc] = jnp.concatenate(outs, axis=0).T.astype(BF16)
            lse_ref[rows, pc] = jnp.concatenate(lses, axis=0).T


def _attention(q, kc, gi):
    win, dil = ATT_GROUPS[gi]
    assert win // dil == Q_BLOCK
    batch, _, l, _ = q.shape
    out_dims = (batch, dil, l, GROUP_W)
    isolated_tiles = l == Q_BLOCK
    slope_dil = tuple(float(s) * dil for s in _slopes()[gi])
    if isolated_tiles:
        q, kc = q.reshape(batch, 1, dil * l, GROUP_W), kc.reshape(batch, 1, dil * l, 2 * GROUP_W)
        dil, l = 1, dil * l
    tc = min(l, 512)
    nk = tc + Q_BLOCK
    o, lse = pl.pallas_call(
        functools.partial(_attn_kernel, slope_dil=slope_dil, tc=tc, isolated_tiles=isolated_tiles),
        grid=(batch, dil, l // tc),
        in_specs=[pl.BlockSpec((None, None, tc, GROUP_W), lambda b, r, c: (b, r, c, 0)),
                  pl.BlockSpec((None, None, tc, 2 * GROUP_W), lambda b, r, c: (b, r, c, 0)),
                  pl.BlockSpec((None, None, Q_BLOCK, 2 * GROUP_W),
                               lambda b, r, c: (b, r, jnp.maximum(c * (tc // Q_BLOCK) - 1, 0), 0))],
        out_specs=[pl.BlockSpec((None, None, tc, GROUP_W), lambda b, r, c: (b, r, c, 0))] * 2,
        out_shape=[jax.ShapeDtypeStruct((batch, dil, l, GROUP_W), BF16),
                   jax.ShapeDtypeStruct((batch, dil, l, GROUP_W), F32)],
        scratch_shapes=[pltpu.VMEM((nk, GROUP_W), BF16), pltpu.VMEM((nk, GROUP_W), BF16)],
        compiler_params=_params(("parallel", "parallel", "arbitrary")),
        name=f"attn_g{gi}",
    )(q, kc, kc)
    return o.reshape(out_dims), lse.reshape(out_dims)


def _mix_tail(x, pool_o, att_o, nattn_ref, wgate_ref, wbp_ref, wba_ref, wo_ref, nffn_ref, xo_ref, xn_ref):
    gates = jax.nn.sigmoid(_dot(_rms(x, nattn_ref[...]).astype(BF16), wgate_ref[...]))
    hp = _dot(pool_o.astype(BF16), wbp_ref[...])
    ha = _dot(att_o.astype(BF16), wba_ref[...])
    t = gates[:, 0:D_MODEL] * hp + gates[:, D_MODEL:2 * D_MODEL] * ha
    xo = x + _dot(t.astype(BF16), wo_ref[...])
    xo_ref[...] = xo
    xn_ref[...] = _rms(xo, nffn_ref[...]).astype(BF16)


def _group_linear(m, wpool_ref, pscale_ref):
    parts = [_dot(m[:, gi * POOL_GROUP:(gi + 1) * POOL_GROUP].astype(BF16), wpool_ref[gi])
             for gi in range(len(POOL_WINDOWS))]
    return jnp.concatenate(parts, axis=1) * pscale_ref[...]


def _read_classes(src_ref, dil, il_ref, slot):
    if dil == 1:
        return src_ref[0].astype(F32)
    rows = src_ref.shape[1]
    for r in range(dil):
        v = src_ref[r].astype(F32)
        for c in range(LANE_CHUNKS):
            il_ref[slot, c, pl.ds(r, rows, stride=dil), :] = v[:, c * LANES:(c + 1) * LANES]
    return jnp.concatenate([il_ref[slot, c] for c in range(LANE_CHUNKS)], axis=1)


def _merge_kernel(x_ref, u_ref, uh_ref, o0_ref, o1_ref, o2_ref, l0_ref, l1_ref, l2_ref,
                  wpool_ref, pscale_ref, nattn_ref, wgate_ref, wbp_ref, wba_ref, wo_ref, nffn_ref,
                  xo_ref, xn_ref, ext_ref, il_ref, *, tm, dils):
    si = pl.program_id(1)
    halo = POOL_STATE + 1
    ext_ref[0:halo, :] = jnp.where(si == 0, 0.0, uh_ref[...])
    ext_ref[halo:halo + tm, :] = u_ref[...]
    pos = si * tm + lax.broadcasted_iota(jnp.int32, (tm, 1), 0)
    parts = []
    for gi, win in enumerate(POOL_WINDOWS):
        cols = slice(gi * POOL_GROUP, (gi + 1) * POOL_GROUP)
        own = ext_ref[halo:halo + tm, cols]
        acc = own
        for back in range(1, win):
            acc = acc + ext_ref[halo - back:halo - back + tm, cols]
        inv = 1.0 / jnp.minimum(pos + 1, win).astype(F32)
        parts.append(acc * inv - own)
    pool_o = _group_linear(jnp.concatenate(parts, axis=1), wpool_ref, pscale_ref)

    slot = 0
    os_, ls_ = [], []
    for g, (o_ref, l_ref) in enumerate(((o0_ref, l0_ref), (o1_ref, l1_ref), (o2_ref, l2_ref))):
        os_.append(_read_classes(o_ref, dils[g], il_ref, slot))
        slot += dils[g] > 1
        ls_.append(_read_classes(l_ref, dils[g], il_ref, slot))
        slot += dils[g] > 1
    mx = jnp.maximum(ls_[0], jnp.maximum(ls_[1], ls_[2]))
    es = [jnp.exp(v - mx) for v in ls_]
    att_o = (es[0] * os_[0] + es[1] * os_[1] + es[2] * os_[2]) / (es[0] + es[1] + es[2])
    _mix_tail(x_ref[...], pool_o, att_o, nattn_ref, wgate_ref, wbp_ref, wba_ref, wo_ref, nffn_ref, xo_ref, xn_ref)


def _wspec(shape, layer):
    nd = len(shape)
    return _resident((None,) + tuple(shape), lambda *_: (layer,) + (0,) * nd)


def _tail_weight_specs(layer):
    return [_wspec((1, D_MODEL), layer), _wspec((D_MODEL, 2 * D_MODEL), layer),
            _wspec((POOL_W, D_MODEL), layer), _wspec((GROUP_W, D_MODEL), layer),
            _wspec((D_MODEL, D_MODEL), layer), _wspec((1, D_MODEL), layer)]


def _tail_weights(w):
    return (w["norm_attn"], w["w_gate"], w["w_bp"], w["w_ba"], w["w_o"], w["norm_ffn"])


def _merge(x, u, outs, lses, layer, w, batch, seq, tm, dils):
    n = x.shape[0]
    n_tiles = seq // tm
    halo = POOL_STATE + 1
    row = lambda width: pl.BlockSpec((tm, width), lambda b, s: (b * n_tiles + s, 0))
    cls = lambda g: pl.BlockSpec((None, dils[g], tm // dils[g], GROUP_W), lambda b, s: (b, 0, s, 0))
    n_il = 2 * sum(d > 1 for d in dils)
    return pl.pallas_call(
        functools.partial(_merge_kernel, tm=tm, dils=dils),
        grid=(batch, n_tiles),
        in_specs=[row(D_MODEL), row(POOL_W),
                  pl.BlockSpec((halo, POOL_W),
                               lambda b, s: (jnp.maximum((b * n_tiles + s) * (tm // halo) - 1, 0), 0))]
                 + [cls(g) for g in range(N_GROUPS)] * 2
                 + [_wspec((len(POOL_WINDOWS), POOL_GROUP, POOL_GROUP), layer), _wspec((1, POOL_W), layer)]
                 + _tail_weight_specs(layer),
        out_specs=[row(D_MODEL), row(D_MODEL)],
        out_shape=[jax.ShapeDtypeStruct((n, D_MODEL), F32), jax.ShapeDtypeStruct((n, D_MODEL), BF16)],
        scratch_shapes=[pltpu.VMEM((halo + tm, POOL_W), F32), pltpu.VMEM((n_il, LANE_CHUNKS, tm, LANES), F32)],
        compiler_params=_params(("parallel", "parallel")),
        name="merge",
    )(x, u, u, *outs, *lses, w["w_pool"], w["pool_scale"], *_tail_weights(w))


def _head_mask():
    return jnp.asarray(np.arange(GROUP_W)[None, :] // HEAD_DIM == np.arange(HEADS)[:, None], F32)


def _sample_keys(dec_seq, cache_rows):
    slopes = _slopes()
    geo = []
    for gi, (win, dil) in enumerate(ATT_GROUPS):
        lc = cache_rows[gi]
        assert lc % dil == 0 and lc // dil == LANES
        cls = (lc + np.arange(dec_seq)) % dil
        n_cached = (cls.max() + 1) * LANES
        lane = np.arange(n_cached + LANES)
        sel = (np.arange(lc)[:, None] == (lane[:n_cached] % LANES) * dil + lane[:n_cached] // LANES)
        t = np.arange(dec_seq)[:, None]
        back = np.where(lane < n_cached, lc + t - ((lane % LANES) * dil + lane // LANES), t - (lane - n_cached))
        valid = (back >= 0) & (back % dil == 0) & (back // dil <= win // dil) & (lane < n_cached + dec_seq)
        bias = np.where(valid[:, None, :], -slopes[gi][None, :, None] * back[:, None, :], NEG_INF)
        geo.append((jnp.asarray(sel, BF16), jnp.asarray(bias.reshape(dec_seq * HEADS, -1), F32)))
    return geo


def _sample_mix_kernel(state_ref, u_ref, q0_ref, q1_ref, q2_ref, kn0_ref, kn1_ref, kn2_ref, c0_ref, c1_ref, c2_ref,
                       sel0_ref, sel1_ref, sel2_ref, bias0_ref, bias1_ref, bias2_ref, headmask_ref,
                       wpool_ref, pscale_ref, pool_ref, att_ref, newpool_ref, ext_ref, m_ref, *, dec_seq):
    ext_ref[...] = jnp.zeros_like(ext_ref)
    ext_ref[0:POOL_STATE, :] = state_ref[...]
    ext_ref[POOL_STATE:POOL_STATE + dec_seq, :] = u_ref[...]
    newpool_ref[...] = ext_ref[dec_seq:dec_seq + POOL_STATE, :]

    m_ref[...] = jnp.zeros_like(m_ref)
    for t in range(dec_seq):
        row = POOL_STATE + t
        for gi, win in enumerate(POOL_WINDOWS):
            cols = slice(gi * POOL_GROUP, (gi + 1) * POOL_GROUP)
            tot = jnp.sum(ext_ref[row - win + 1:row + 1, cols], axis=0, keepdims=True)
            cnt = float(min(PAST_LEN + t + 1, win))
            m_ref[t:t + 1, cols] = tot / cnt - ext_ref[row:row + 1, cols]
    pool_ref[...] = _group_linear(m_ref[...], wpool_ref, pscale_ref)[0:dec_seq]

    groups = ((q0_ref, kn0_ref, c0_ref, sel0_ref, bias0_ref), (q1_ref, kn1_ref, c1_ref, sel1_ref, bias1_ref),
              (q2_ref, kn2_ref, c2_ref, sel2_ref, bias2_ref))
    k_rows = HEADS * HEAD_DIM
    outs, lses = [], []
    for q_ref, kn_ref, c_ref, sel_ref, bias_ref in groups:
        flat = c_ref[...].reshape(2 * k_rows, c_ref.shape[-1]).astype(BF16)
        cached = _dot(flat, sel_ref[...]).astype(BF16)
        new = kn_ref[...].reshape(2 * k_rows, LANES).astype(BF16)
        keys = jnp.concatenate([cached[0:k_rows], new[0:k_rows]], axis=1)
        vals = jnp.concatenate([cached[k_rows:], new[k_rows:]], axis=1)
        s = _dot(q_ref[...], keys) + bias_ref[...]
        mx = jnp.max(s, axis=1, keepdims=True)
        p = jnp.exp(s - mx)
        l = jnp.sum(p, axis=1, keepdims=True)
        outs.append(lax.dot_general((p / l).astype(BF16), vals, (((1,), (1,)), ((), ())),
                                    preferred_element_type=F32))
        lses.append(mx + jnp.log(l))
    mx = jnp.maximum(lses[0], jnp.maximum(lses[1], lses[2]))
    es = [jnp.exp(v - mx) for v in lses]
    mixed = (es[0] * outs[0] + es[1] * outs[1] + es[2] * outs[2]) / (es[0] + es[1] + es[2])
    att_ref[...] = jnp.sum(mixed.reshape(dec_seq, HEADS, k_rows) * headmask_ref[...][None], axis=1)


def _sample_mix(state_pool, u, qs, kvn, caches, layer, w, dec_batch, dec_seq):
    cache_rows = [c.shape[2] for c in caches]
    geo = _sample_keys(dec_seq, cache_rows)
    consts = [g[i] for i in (0, 1) for g in geo] + [_head_mask()]
    kv_tail = (2, HEADS, HEAD_DIM)
    cviews = [jnp.transpose(c, (0, 1, 3, 4, 5, 2)).reshape((-1,) + kv_tail + (c.shape[2],)) for c in caches]
    cspecs = [pl.BlockSpec((None,) + kv_tail + (lc,), lambda b: (layer * dec_batch + b, 0, 0, 0, 0))
              for lc in cache_rows]
    per_batch = lambda *tail: pl.BlockSpec((None,) + tail, lambda b: (b,) + (0,) * len(tail))
    const = lambda shape: _resident(shape, lambda b: (0,) * len(shape))
    return pl.pallas_call(
        functools.partial(_sample_mix_kernel, dec_seq=dec_seq),
        grid=(dec_batch,),
        in_specs=[pl.BlockSpec((None, POOL_STATE, POOL_W), lambda b: (layer * dec_batch + b, 0, 0)),
                  per_batch(dec_seq, POOL_W)] + [per_batch(dec_seq * HEADS, GROUP_W)] * 3
                 + [per_batch(*kv_tail, LANES)] * 3 + cspecs + [const(c.shape) for c in consts]
                 + [_wspec((len(POOL_WINDOWS), POOL_GROUP, POOL_GROUP), layer), _wspec((1, POOL_W), layer)],
        out_specs=[per_batch(dec_seq, POOL_W), per_batch(dec_seq, GROUP_W),
                   pl.BlockSpec((None, POOL_STATE, POOL_W), lambda b: (b, 0, 0))],
        out_shape=[jax.ShapeDtypeStruct((dec_batch, dec_seq, POOL_W), F32),
                   jax.ShapeDtypeStruct((dec_batch, dec_seq, GROUP_W), F32),
                   jax.ShapeDtypeStruct((dec_batch, POOL_STATE, POOL_W), F32)],
        scratch_shapes=[pltpu.VMEM((POOL_STATE + dec_seq + 5, POOL_W), F32), pltpu.VMEM((8, POOL_W), F32)],
        compiler_params=_params(("parallel",)),
        name="sample_mix",
    )(state_pool.reshape(-1, POOL_STATE, POOL_W), u.reshape(dec_batch, dec_seq, POOL_W), *qs, *kvn, *cviews,
      *consts, w["w_pool"], w["pool_scale"])


def _tail_kernel(x_ref, pool_ref, att_ref, nattn_ref, wgate_ref, wbp_ref, wba_ref, wo_ref, nffn_ref, xo_ref, xn_ref):
    _mix_tail(x_ref[...], pool_ref[...], att_ref[...], nattn_ref, wgate_ref, wbp_ref, wba_ref, wo_ref, nffn_ref,
              xo_ref, xn_ref)


def _tail(x, pool_o, att_o, layer, w):
    n = x.shape[0]
    row = lambda width: pl.BlockSpec((n, width), lambda i: (0, 0))
    return pl.pallas_call(
        _tail_kernel,
        grid=(1,),
        in_specs=[row(D_MODEL), row(POOL_W), row(GROUP_W)] + _tail_weight_specs(layer),
        out_specs=[row(D_MODEL), row(D_MODEL)],
        out_shape=[jax.ShapeDtypeStruct((n, D_MODEL), F32), jax.ShapeDtypeStruct((n, D_MODEL), BF16)],
        compiler_params=_params(("arbitrary",)),
        name="tail",
    )(x, pool_o, att_o, *_tail_weights(w))


def _swiglu_step(x, wg_ref, wu_ref, wd_ref, acc_ref):
    h = jax.nn.silu(_dot(x, wg_ref[...])) * _dot(x, wu_ref[...])
    acc_ref[...] += _dot(h.astype(BF16), wd_ref[...])


def _ffn_kernel(x_ref, xn_ref, w1_ref, w3_ref, w2_ref, y_ref):
    xn = xn_ref[...]
    h = jax.nn.silu(_dot(xn, w1_ref[...])) * _dot(xn, w3_ref[...])
    y_ref[...] = x_ref[...] + _dot(h.astype(BF16), w2_ref[...])


def _ffn(x, xn, w1, w3, w2, idx, tm):
    n = x.shape[0]
    d_ff = w1.shape[-1]
    return pl.pallas_call(
        _ffn_kernel,
        grid=(n // tm,),
        in_specs=[pl.BlockSpec((tm, D_MODEL), lambda i: (i, 0)),
                  pl.BlockSpec((tm, D_MODEL), lambda i: (i, 0)),
                  _resident((None, D_MODEL, d_ff), lambda i: (idx, 0, 0)),
                  _resident((None, D_MODEL, d_ff), lambda i: (idx, 0, 0)),
                  _resident((None, d_ff, D_MODEL), lambda i: (idx, 0, 0))],
        out_specs=pl.BlockSpec((tm, D_MODEL), lambda i: (i, 0)),
        out_shape=jax.ShapeDtypeStruct((n, D_MODEL), F32),
        compiler_params=_params(("parallel",)),
        name="ffn",
    )(x, xn, w1, w3, w2)


SEG_ALIGN = 16
MOE_ROWS = 512


def _sorted_cap(ts):
    need = TOP_K * ts + N_EXPERTS * (SEG_ALIGN - 1)
    return -(-need // LANES) * LANES if ts >= LANES * 2 else -(-need // SEG_ALIGN) * SEG_ALIGN


def _segment_sizes(oh1, oh2):
    cnt1 = jnp.sum(oh1, axis=1, keepdims=True)
    cnt = cnt1 + jnp.sum(oh2, axis=1, keepdims=True)
    cpad = jnp.floor((cnt + (SEG_ALIGN - 1)) * (1.0 / SEG_ALIGN)) * SEG_ALIGN
    offs = [jnp.zeros((1, 1), F32)]
    for ei in range(1, N_EXPERTS):
        offs.append(offs[-1] + cpad[ei - 1:ei, :])
    return cnt1, cpad, jnp.concatenate(offs, axis=0)


def _gate_kernel(x_ref, g_ref, rwt_ref, rb_ref, route_ref, seg_ref, *, ts):
    xn = _rms(x_ref[...], g_ref[...]).astype(BF16)
    logits = lax.dot_general(rwt_ref[...], xn, (((1,), (1,)), ((), ())), preferred_element_type=F32) + rb_ref[...]
    row = lax.broadcasted_iota(jnp.int32, (N_EXPERTS, ts), 0)
    neg = jnp.float32(-jnp.inf)
    m1 = jnp.max(logits, axis=0, keepdims=True)
    i1 = jnp.min(jnp.where(logits == m1, row, N_EXPERTS), axis=0, keepdims=True)
    rest = jnp.where(row == i1, neg, logits)
    m2 = jnp.max(rest, axis=0, keepdims=True)
    i2 = jnp.min(jnp.where(rest == m2, row, N_EXPERTS), axis=0, keepdims=True)
    e = jnp.exp(m2 - m1)
    zeros = jnp.zeros((N_EXPERTS - 4, ts), F32)
    route_ref[...] = jnp.concatenate([i1.astype(F32), i2.astype(F32), 1.0 / (1.0 + e), e / (1.0 + e), zeros], axis=0)
    _, cpad, off = _segment_sizes(jnp.where(row == i1, 1.0, 0.0), jnp.where(row == i2, 1.0, 0.0))
    lane = lax.broadcasted_iota(jnp.int32, (N_EXPERTS, LANES), 1)
    seg_ref[...] = jnp.where(lane == 0, off, jnp.where(lane == 1, cpad, 0.0)).astype(jnp.int32)


def _gate(x, norm_ffn, rwt, rb, layer, idx, ts):
    n = x.shape[0]
    n_sub = n // ts
    return pl.pallas_call(
        functools.partial(_gate_kernel, ts=ts),
        grid=(n_sub,),
        in_specs=[pl.BlockSpec((ts, D_MODEL), lambda i: (i, 0)),
                  _resident((None, 1, D_MODEL), lambda i: (layer, 0, 0)),
                  _resident((None, N_EXPERTS, D_MODEL), lambda i: (idx, 0, 0)),
                  _resident((None, N_EXPERTS, 1), lambda i: (idx, 0, 0))],
        out_specs=[pl.BlockSpec((None, N_EXPERTS, ts), lambda i: (i, 0, 0)),
                   pl.BlockSpec((None, N_EXPERTS, LANES), lambda i: (i, 0, 0))],
        out_shape=[jax.ShapeDtypeStruct((n_sub, N_EXPERTS, ts), F32),
                   jax.ShapeDtypeStruct((n_sub, N_EXPERTS, LANES), jnp.int32)],
        compiler_params=_params(("parallel",)),
        name="gate",
    )(x, norm_ffn, rwt, rb)


def _segment_copies(i, n_bits, local_ref, off_ref, far_ref, far_rows_ref, nseg_ref, sem, to_far):
    copies = []
    for e in range(N_EXPERTS):
        j = i * N_EXPERTS + e
        n = nseg_ref[j]
        for b in reversed(range(n_bits)):
            size = SEG_ALIGN << b
            done = ((n >> (b + 1)) << (b + 1)) * SEG_ALIGN
            near = local_ref.at[pl.ds(pl.multiple_of(off_ref[j] + done, SEG_ALIGN), size)]
            far = far_ref.at[pl.ds(pl.multiple_of(far_rows_ref[j] + done, SEG_ALIGN), size)]
            cp = pltpu.make_async_copy(near, far, sem) if to_far else pltpu.make_async_copy(far, near, sem)
            copies.append((((n >> b) & 1) == 1, cp))
    return copies


def _scatter_kernel(off_ref, far_rows_ref, nseg_ref, x_ref, g_ref, route_ref, tri_ref, xb_in_ref,
                    meta_ref, xb_ref, xs_ref, sem, *, ts, cap, n_bits):
    del xb_in_ref
    i = pl.program_id(0)
    xn = _rms(x_ref[...], g_ref[...]).astype(BF16)
    row = lax.broadcasted_iota(jnp.int32, (N_EXPERTS, ts), 0).astype(F32)
    oh1 = jnp.where(row == route_ref[0:1, :], 1.0, 0.0)
    oh2 = jnp.where(row == route_ref[1:2, :], 1.0, 0.0)
    cnt1, _, off = _segment_sizes(oh1, oh2)
    pre1 = _dot(oh1.astype(BF16), tri_ref[...])
    pre2 = _dot(oh2.astype(BF16), tri_ref[...])
    d1 = jnp.sum(oh1 * (off + pre1), axis=0, keepdims=True)
    d2 = jnp.sum(oh2 * (off + cnt1 + pre2), axis=0, keepdims=True)
    slot = lax.broadcasted_iota(jnp.int32, (cap, ts), 0).astype(F32)
    p = jnp.where(slot == d1, 1.0, jnp.where(slot == d2, 1.0, 0.0)).astype(BF16)
    xs_ref[i % 2] = _dot(p, xn).astype(BF16)
    meta_t = jnp.concatenate([d1, d2, route_ref[2:4, :], jnp.zeros((LANES - 4, ts), F32)], axis=0)
    meta_ref[...] = meta_t.T

    def send(step, action):
        buf = step % 2
        for pred, cp in _segment_copies(step, n_bits, xs_ref.at[buf], off_ref, xb_ref, far_rows_ref, nseg_ref,
                                        sem.at[buf], True):
            pl.when(pred)(getattr(cp, action))

    send(i, "start")
    pl.when(i > 0)(lambda: send(i - 1, "wait"))
    pl.when(i == pl.num_programs(0) - 1)(lambda: send(i, "wait"))


def _scatter(off, far_rows, nseg, x, norm_ffn, route, xb, layer, ts):
    n = x.shape[0]
    n_sub = n // ts
    cap = _sorted_cap(ts)
    tri = jnp.asarray(np.triu(np.ones((ts, ts), np.float32), 1), BF16)
    n_bits = int(TOP_K * ts // SEG_ALIGN).bit_length()
    return pl.pallas_call(
        functools.partial(_scatter_kernel, ts=ts, cap=cap, n_bits=n_bits),
        grid_spec=pltpu.PrefetchScalarGridSpec(
            num_scalar_prefetch=3,
            grid=(n_sub,),
            in_specs=[pl.BlockSpec((ts, D_MODEL), lambda i, *_: (i, 0)),
                      _resident((None, 1, D_MODEL), lambda i, *_: (layer, 0, 0)),
                      pl.BlockSpec((None, N_EXPERTS, ts), lambda i, *_: (i, 0, 0)),
                      _resident((ts, ts), lambda i, *_: (0, 0)),
                      pl.BlockSpec(memory_space=pl.ANY)],
            out_specs=[pl.BlockSpec((ts, LANES), lambda i, *_: (i, 0)),
                       pl.BlockSpec(memory_space=pl.ANY)],
            scratch_shapes=[pltpu.VMEM((2, cap, D_MODEL), BF16), pltpu.SemaphoreType.DMA((2,))]),
        out_shape=[jax.ShapeDtypeStruct((n, LANES), F32), jax.ShapeDtypeStruct(xb.shape, xb.dtype)],
        input_output_aliases={7: 1},
        compiler_params=_params(("arbitrary",)),
        name="scatter",
    )(off, far_rows, nseg, x, norm_ffn, route, tri, xb)


def _experts_kernel(blk_ref, nreal_ref, xb_ref, wg_ref, wu_ref, wd_ref, yb_ref, acc_ref):
    del blk_ref
    i = pl.program_id(0)
    f = pl.program_id(1)

    @pl.when(f == 0)
    def _():
        acc_ref[...] = jnp.zeros_like(acc_ref)

    @pl.when(i < nreal_ref[0])
    def _():
        _swiglu_step(xb_ref[...], wg_ref, wu_ref, wd_ref, acc_ref)

    @pl.when(f == pl.num_programs(1) - 1)
    def _():
        yb_ref[...] = acc_ref[...].astype(yb_ref.dtype)


def _experts(blk_e, nreal, xb, wg, wu, wd, idx, tf):
    n_blocks = blk_e.shape[0]
    d_ff = wg.shape[-1]
    n_f = d_ff // tf

    def live_block(i, nreal):
        return jnp.minimum(i, nreal[0] - 1)

    def wcol(i, f, blk, nreal):
        return idx, blk[live_block(i, nreal)], 0, jnp.where(i < nreal[0], f, n_f - 1)

    def wrow(i, f, blk, nreal):
        return idx, blk[live_block(i, nreal)], jnp.where(i < nreal[0], f, n_f - 1), 0

    return pl.pallas_call(
        _experts_kernel,
        grid_spec=pltpu.PrefetchScalarGridSpec(
            num_scalar_prefetch=2,
            grid=(n_blocks, n_f),
            in_specs=[pl.BlockSpec((MOE_ROWS, D_MODEL), lambda i, f, blk, nreal: (live_block(i, nreal), 0)),
                      pl.BlockSpec((None, None, D_MODEL, tf), wcol),
                      pl.BlockSpec((None, None, D_MODEL, tf), wcol),
                      pl.BlockSpec((None, None, tf, D_MODEL), wrow)],
            out_specs=pl.BlockSpec((MOE_ROWS, D_MODEL), lambda i, f, blk, nreal: (i, 0)),
            scratch_shapes=[pltpu.VMEM((MOE_ROWS, D_MODEL), F32)]),
        out_shape=jax.ShapeDtypeStruct((n_blocks * MOE_ROWS, D_MODEL), BF16),
        compiler_params=_params(("parallel", "arbitrary")),
        name="experts",
    )(blk_e, nreal, xb, wg, wu, wd)


def _combine_kernel(off_ref, far_rows_ref, nseg_ref, x_ref, meta_ref, yb_ref, o_ref, ys_ref, sem, *, ts, cap, n_bits):
    i = pl.program_id(0)

    def fetch(step, action):
        buf = step % 2
        for pred, cp in _segment_copies(step, n_bits, ys_ref.at[buf], off_ref, yb_ref, far_rows_ref, nseg_ref,
                                        sem.at[buf], False):
            pl.when(pred)(getattr(cp, action))

    def start(step):
        ys_ref[step % 2] = jnp.zeros((cap, D_MODEL), BF16)
        fetch(step, "start")

    pl.when(i == 0)(lambda: start(i))
    pl.when(i + 1 < pl.num_programs(0))(lambda: start(i + 1))
    fetch(i, "wait")
    slot = lax.broadcasted_iota(jnp.int32, (ts, cap), 1).astype(F32)
    ys = ys_ref[i % 2]
    q1 = jnp.where(slot == meta_ref[:, 0:1], 1.0, 0.0).astype(BF16)
    q2 = jnp.where(slot == meta_ref[:, 1:2], 1.0, 0.0).astype(BF16)
    o_ref[...] = x_ref[...] + meta_ref[:, 2:3] * _dot(q1, ys) + meta_ref[:, 3:4] * _dot(q2, ys)


def _combine(off, far_rows, nseg, x, meta, yb, ts):
    n = x.shape[0]
    cap = _sorted_cap(ts)
    n_bits = int(TOP_K * ts // SEG_ALIGN).bit_length()
    return pl.pallas_call(
        functools.partial(_combine_kernel, ts=ts, cap=cap, n_bits=n_bits),
        grid_spec=pltpu.PrefetchScalarGridSpec(
            num_scalar_prefetch=3,
            grid=(n // ts,),
            in_specs=[pl.BlockSpec((ts, D_MODEL), lambda i, *_: (i, 0)),
                      pl.BlockSpec((ts, LANES), lambda i, *_: (i, 0)),
                      pl.BlockSpec(memory_space=pl.ANY)],
            out_specs=pl.BlockSpec((ts, D_MODEL), lambda i, *_: (i, 0)),
            scratch_shapes=[pltpu.VMEM((2, cap, D_MODEL), BF16), pltpu.SemaphoreType.DMA((2,))]),
        out_shape=jax.ShapeDtypeStruct((n, D_MODEL), F32),
        compiler_params=_params(("arbitrary",)),
        name="combine",
    )(off, far_rows, nseg, x, meta, yb)


def _moe(xs, tss, norm_ffn, rwt, rb, wg, wu, wd, layer, idx, tf):
    gated = [_gate(x, norm_ffn, rwt, rb, layer, idx, ts) for x, ts in zip(xs, tss)]
    off = jnp.concatenate([g[1][:, :, 0] for g in gated], axis=0)
    cpad = jnp.concatenate([g[1][:, :, 1] for g in gated], axis=0)
    n_subs = [g[1].shape[0] for g in gated]
    tot = jnp.sum(cpad, axis=0)
    padded = (tot + MOE_ROWS - 1) // MOE_ROWS * MOE_ROWS
    pend = jnp.cumsum(padded)
    within = jnp.cumsum(cpad, axis=0) - cpad
    off = off.astype(jnp.int32).reshape(-1)
    grouped_rows = ((pend - padded)[None, :] + within).astype(jnp.int32).reshape(-1)
    nseg = (cpad // SEG_ALIGN).astype(jnp.int32).reshape(-1)
    n_assign = TOP_K * sum(x.shape[0] for x in xs)
    n_blocks = (n_assign + (SEG_ALIGN - 1) * sum(n_subs) * N_EXPERTS + N_EXPERTS * (MOE_ROWS - 1)) // MOE_ROWS
    blk_e = jnp.minimum(jnp.sum(jnp.arange(n_blocks)[:, None] * MOE_ROWS >= pend[None, :], axis=1),
                        N_EXPERTS - 1).astype(jnp.int32)
    nreal = (pend[-1:] // MOE_ROWS).astype(jnp.int32)

    xb = jnp.zeros((n_blocks * MOE_ROWS, D_MODEL), BF16)
    metas, parts = [], []
    lo = 0
    for x, g, ns, ts in zip(xs, gated, n_subs, tss):
        sl = slice(lo * N_EXPERTS, (lo + ns) * N_EXPERTS)
        parts.append((off[sl], grouped_rows[sl], nseg[sl]))
        meta, xb = _scatter(*parts[-1], x, norm_ffn, g[0], xb, layer, ts)
        metas.append(meta)
        lo += ns
    yb = _experts(blk_e, nreal, xb, wg, wu, wd, idx, tf)
    return [_combine(*part, x, meta, yb, ts) for part, x, meta, ts in zip(parts, xs, metas, tss)]


def kernel(x_prompt, x_sample, state_pool, cache_kv_g0, cache_kv_g1, cache_kv_g2, norm_attn, w_in, q_norm, k_norm, w_pool, pool_scale, w_branch_pool, w_branch_attn, w_out, norm_ffn, w1_dense, w3_dense, w2_dense, router_w, router_b, we_gate, we_up, we_down):
    batch, seq, _ = x_prompt.shape
    dec_batch, dec_seq, _ = x_sample.shape
    depth = w_in.shape[0]
    caches = (cache_kv_g0, cache_kv_g1, cache_kv_g2)
    n_p, n_s = batch * seq, dec_batch * dec_seq
    tm_p = 512
    dils = tuple(d for _, d in ATT_GROUPS)
    keeps = tuple(min(win, seq) for win, _ in ATT_GROUPS)

    head_of = np.arange(GROUP_W) // HEAD_DIM
    same_head = (head_of[:, None] == head_of[None, :]).astype(np.float32)
    seg_mean = jnp.asarray(same_head / HEAD_DIM, BF16)
    w_qkv = w_in[:, :, :PROJ_W].astype(BF16)
    qgain = (jnp.tile(q_norm, (1, HEADS)) * (HEAD_DIM ** -0.5)).reshape(depth, 1, GROUP_W)
    kgain = jnp.tile(k_norm, (1, HEADS)).reshape(depth, 1, GROUP_W)
    norm_attn3 = norm_attn.reshape(depth, 1, D_MODEL)
    w = {"w_pool": w_pool.astype(BF16), "pool_scale": pool_scale.reshape(depth, 1, POOL_W),
         "norm_attn": norm_attn3, "w_gate": w_in[:, :, PROJ_W:].astype(BF16),
         "w_bp": w_branch_pool.astype(BF16), "w_ba": w_branch_attn.astype(BF16), "w_o": w_out.astype(BF16),
         "norm_ffn": norm_ffn.reshape(depth, 1, D_MODEL)}
    w1_b, w3_b, w2_b = w1_dense.astype(BF16), w3_dense.astype(BF16), w2_dense.astype(BF16)
    wg_b, wu_b, wd_b = we_gate.astype(BF16), we_up.astype(BF16), we_down.astype(BF16)
    rwt = jnp.swapaxes(router_w, 1, 2).astype(BF16)
    rb = router_b.reshape(-1, N_EXPERTS, 1)

    xp = x_prompt.reshape(n_p, D_MODEL)
    xs = x_sample.reshape(n_s, D_MODEL)
    pool_p, pool_s = [], []
    kv_s = [[] for _ in ATT_GROUPS]
    kvo_p = ()
    ones = (1,) * N_GROUPS
    for layer in range(depth):
        res = _proj(xp, layer, norm_attn3, w_qkv, qgain, kgain, seg_mean, batch, seq, tm_p, dils, keeps,
                    layer, depth, kvo_p, 1, True)
        u, qs, kcs, kvo_p = res[0], res[1:4], res[4:7], tuple(res[7:10])
        outs, lses = [], []
        for gi in range(N_GROUPS):
            o, lse = _attention(qs[gi], kcs[gi], gi)
            outs.append(o)
            lses.append(lse)
        pool_p.append(u.reshape(batch, seq, POOL_W)[:, seq - POOL_STATE:])
        xp, xpn = _merge(xp, u, outs, lses, layer, w, batch, seq, tm_p, dils)

        res = _proj(xs, layer, norm_attn3, w_qkv, qgain, kgain, seg_mean, 1, n_s, n_s, ones, (n_s,) * N_GROUPS,
                    0, 1, (), 3, False)
        u = res[0]
        qs = [(q.reshape(dec_batch, dec_seq, 1, GROUP_W) * _head_mask().astype(BF16))
              .reshape(dec_batch, dec_seq * HEADS, GROUP_W) for q in res[1:4]]
        kv_new = [kv.reshape(dec_batch, dec_seq, 2, HEADS, HEAD_DIM) for kv in res[7:10]]
        tokens_to_lanes = ((0, 0),) * 4 + ((0, LANES - dec_seq),)
        kvn = [jnp.pad(kv.transpose(0, 2, 3, 4, 1), tokens_to_lanes) for kv in kv_new]
        pool_o, att_o, new_pool = _sample_mix(state_pool, u, qs, kvn, caches, layer, w, dec_batch, dec_seq)
        att_o = att_o.reshape(n_s, GROUP_W)
        for gi in range(N_GROUPS):
            kv_s[gi].append(kv_new[gi])
        pool_s.append(new_pool)
        xs, xsn = _tail(xs, pool_o.reshape(n_s, POOL_W), att_o, layer, w)

        i = layer // 2
        if layer % 2 == 0:
            xp = _ffn(xp, xpn, w1_b, w3_b, w2_b, i, 512)
            xs = _ffn(xs, xsn, w1_b, w3_b, w2_b, i, n_s)
        else:
            xp, xs = _moe([xp, xs], [512, n_s], w["norm_ffn"], rwt, rb, wg_b, wu_b, wd_b, layer, i,
                           wg_b.shape[-1] // 2)
    kv_p = [kvo_p[g].reshape(depth, batch, 2, HEADS, HEAD_DIM, keeps[g]).transpose(0, 1, 5, 2, 3, 4)
            for g in range(N_GROUPS)]
    return (xp.reshape(batch, seq, D_MODEL), xs.reshape(dec_batch, dec_seq, D_MODEL),
            jnp.stack(pool_p), jnp.stack(pool_s),
            kv_p[0], jnp.stack(kv_s[0]),
            kv_p[1], jnp.stack(kv_s[1]),
            kv_p[2], jnp.stack(kv_s[2]))
```

```python
import functools

import numpy as np
import jax
import jax.numpy as jnp
from jax import lax
from jax.experimental import pallas as pl
from jax.experimental.pallas import tpu as pltpu

F32 = jnp.float32
BF16 = jnp.bfloat16

D_MODEL = 1024
PAST_LEN = 16384
POOL_WINDOWS = (2, 4, 8, 16)
POOL_GROUP = 128
POOL_W = 512
POOL_STATE = 15
ATT_GROUPS = ((128, 1), (512, 4), (2048, 16))
N_GROUPS = len(ATT_GROUPS)
HEAD_DIM = 64
HEADS = 8
GROUP_W = 512
QKV_W = 1536
Q_BLOCK = 128
ALIBI_MAX = 8.0
N_EXPERTS = 8
TOP_K = 2
RMS_EPS = 1e-6
NEG_INF = -1e30
LANES = 128
LANE_CHUNKS = GROUP_W // LANES
PROJ_W = POOL_W + 3 * QKV_W
V7X_VMEM_BYTES = 64 * 1024 * 1024
VMEM_LIMIT = V7X_VMEM_BYTES * 7 // 8


def _slopes():
    i = np.arange(1, N_GROUPS * HEADS + 1, dtype=np.float32)
    return np.exp2(-ALIBI_MAX * i / (N_GROUPS * HEADS)).astype(np.float32).reshape(N_GROUPS, HEADS)


def _params(sem):
    return pltpu.CompilerParams(dimension_semantics=sem, vmem_limit_bytes=VMEM_LIMIT)


def _rms(x, gain):
    return x * lax.rsqrt(jnp.mean(x * x, axis=-1, keepdims=True) + RMS_EPS) * gain


def _dot(a, b):
    return jnp.dot(a, b, preferred_element_type=F32)


def _resident(shape, index_map):
    return pl.BlockSpec(shape, index_map, pipeline_mode=pl.Buffered(1))


def _write_classes(dst_ref, col0, val, dil, tmp_ref, slot):
    rows = val.shape[0] // dil
    cols = slice(col0, col0 + GROUP_W)
    if dil == 1:
        dst_ref[0, :, cols] = val.astype(dst_ref.dtype)
        return
    for c in range(LANE_CHUNKS):
        tmp_ref[slot, c] = val[:, c * LANES:(c + 1) * LANES]
    for r in range(dil):
        picked = [tmp_ref[slot, c, pl.ds(r, rows, stride=dil), :] for c in range(LANE_CHUNKS)]
        dst_ref[r, :, cols] = jnp.concatenate(picked, axis=1).astype(dst_ref.dtype)


def _proj_kernel(x_ref, g_ref, w_ref, qg_ref, kg_ref, seg_ref, *rest, tm, n_tiles, dils, keeps, n_alias,
                 norm_terms, kv_positions_minor, out_layer, fill_layers):
    u_ref, q0_ref, q1_ref, q2_ref, kc0_ref, kc1_ref, kc2_ref, kvo0_ref, kvo1_ref, kvo2_ref, tmp_ref = rest[n_alias:]
    s = pl.program_id(1)
    xn = _rms(x_ref[...], g_ref[...]).astype(BF16)

    def zblk(j):
        return _dot(xn, w_ref[:, j * GROUP_W:(j + 1) * GROUP_W])

    def headnorm(z, gain):
        rem = z * z
        ms = None
        for _ in range(norm_terms):
            part = rem.astype(BF16)
            rem = rem - part.astype(F32)
            ms = _dot(part, seg_ref[...]) if ms is None else ms + _dot(part, seg_ref[...])
        return z * lax.rsqrt(ms + RMS_EPS) * gain

    u_ref[...] = zblk(0)
    q_refs = (q0_ref, q1_ref, q2_ref)
    kc_refs = (kc0_ref, kc1_ref, kc2_ref)
    kvo_refs = (kvo0_ref, kvo1_ref, kvo2_ref)
    slot = 0
    kept = {}
    for g in range(N_GROUPS):
        q = headnorm(zblk(1 + g), qg_ref[...])
        k = headnorm(zblk(1 + N_GROUPS + g), kg_ref[...])
        v = zblk(1 + 2 * N_GROUPS + g)
        for dst, col0, val in ((q_refs[g], 0, q), (kc_refs[g], 0, k), (kc_refs[g], GROUP_W, v)):
            _write_classes(dst, col0, val, dils[g], tmp_ref, slot % tmp_ref.shape[0])
            slot += dils[g] > 1
        kept.setdefault(n_tiles - max(keeps[g] // tm, 1), []).append((kvo_refs[g], min(keeps[g], tm), k, v))

    def write_kept(items):
        for ref, rows, k, v in items:
            if fill_layers:
                for other in range(fill_layers):
                    if other != out_layer:
                        ref[other] = jnp.zeros(ref.shape[1:], ref.dtype)
                ref = ref.at[out_layer]
            if kv_positions_minor:
                ref[0] = k[tm - rows:, :].T
                ref[1] = v[tm - rows:, :].T
            else:
                ref[:, 0:GROUP_W] = k[tm - rows:, :]
                ref[:, GROUP_W:2 * GROUP_W] = v[tm - rows:, :]

    for first, items in kept.items():
        if first == 0:
            write_kept(items)
        else:
            pl.when(s >= first)(functools.partial(write_kept, items))


def _proj(x, layer, norm_attn, w_qkv, qgain, kgain, seg, batch, seq, tm, dils, keeps, out_layer, out_depth, prev_kvo,
          norm_terms, kv_positions_minor):
    n_tiles = seq // tm
    kvo_specs, kvo_shapes = [], []
    fill_layers = out_depth if (out_depth > 1 and not prev_kvo) else 0
    layer_block, layer_index = (out_depth, 0) if fill_layers else (None, out_layer)
    for g in range(N_GROUPS):
        keep = keeps[g]
        assert keep % tm == 0 or (keep < tm and keep % LANES == 0)
        first = n_tiles - max(keep // tm, 1)
        rows = min(keep, tm)
        if kv_positions_minor:
            kvo_specs.append(pl.BlockSpec((layer_block, None, 2, GROUP_W, rows),
                                          lambda b, s, first=first: (layer_index, b, 0, 0, jnp.maximum(s - first, 0))))
            kvo_shapes.append(jax.ShapeDtypeStruct((out_depth, batch, 2, GROUP_W, keep), F32))
        else:
            kvo_specs.append(pl.BlockSpec((layer_block, None, rows, 2 * GROUP_W),
                                          lambda b, s, first=first: (layer_index, b, jnp.maximum(s - first, 0), 0)))
            kvo_shapes.append(jax.ShapeDtypeStruct((out_depth, batch, keep, 2 * GROUP_W), F32))
    cls_spec = lambda g, width: pl.BlockSpec((None, dils[g], tm // dils[g], width), lambda b, s: (b, 0, s, 0))
    cls_shape = lambda g, width: jax.ShapeDtypeStruct((batch, dils[g], seq // dils[g], width), BF16)
    n_alias = len(prev_kvo)
    n_in = 6
    return pl.pallas_call(
        functools.partial(_proj_kernel, tm=tm, n_tiles=n_tiles, dils=dils, keeps=keeps, n_alias=n_alias,
                          norm_terms=norm_terms, kv_positions_minor=kv_positions_minor, out_layer=out_layer,
                          fill_layers=fill_layers),
        grid=(batch, n_tiles),
        in_specs=[
            pl.BlockSpec((tm, D_MODEL), lambda b, s: (b * n_tiles + s, 0)),
            _resident((None, 1, D_MODEL), lambda b, s: (layer, 0, 0)),
            _resident((None, D_MODEL, PROJ_W), lambda b, s: (layer, 0, 0)),
            _resident((None, 1, GROUP_W), lambda b, s: (layer, 0, 0)),
            _resident((None, 1, GROUP_W), lambda b, s: (layer, 0, 0)),
            _resident((GROUP_W, GROUP_W), lambda b, s: (0, 0)),
        ] + [pl.BlockSpec(memory_space=pl.ANY)] * n_alias,
        out_specs=[pl.BlockSpec((tm, POOL_W), lambda b, s: (b * n_tiles + s, 0))]
                  + [cls_spec(g, GROUP_W) for g in range(N_GROUPS)]
                  + [cls_spec(g, 2 * GROUP_W) for g in range(N_GROUPS)] + kvo_specs,
        out_shape=[jax.ShapeDtypeStruct((batch * seq, POOL_W), F32)]
                  + [cls_shape(g, GROUP_W) for g in range(N_GROUPS)]
                  + [cls_shape(g, 2 * GROUP_W) for g in range(N_GROUPS)] + kvo_shapes,
        scratch_shapes=[pltpu.VMEM((3, LANE_CHUNKS, tm, LANES), F32)],
        input_output_aliases={n_in + g: 1 + 2 * N_GROUPS + g for g in range(n_alias)},
        compiler_params=_params(("parallel", "arbitrary")),
        name="proj",
    )(x, norm_attn, w_qkv, qgain, kgain, seg, *prev_kvo)


def _attn_kernel(q_ref, kvc_ref, kvh_ref, o_ref, lse_ref, kbuf, vbuf, *, slope_dil, tc, isolated_tiles):
    off = Q_BLOCK
    kbuf[0:off, :] = kvh_ref[:, 0:GROUP_W]
    vbuf[0:off, :] = kvh_ref[:, GROUP_W:2 * GROUP_W]
    kbuf[off:off + tc, :] = kvc_ref[:, 0:GROUP_W]
    vbuf[off:off + tc, :] = kvc_ref[:, GROUP_W:2 * GROUP_W]

    nk = Q_BLOCK + off
    kj = lax.broadcasted_iota(jnp.int32, (nk, Q_BLOCK), 0)
    qi = lax.broadcasted_iota(jnp.int32, (nk, Q_BLOCK), 1)
    dist = qi - kj + off
    distf = dist.astype(F32)
    maskneg = jnp.where((dist >= 0) & (dist <= Q_BLOCK), 0.0, NEG_INF).astype(F32)
    first = NEG_INF if isolated_tiles else jnp.where(pl.program_id(2) == 0, NEG_INF, 0.0).astype(F32)
    mask_first = maskneg + jnp.where(kj < Q_BLOCK, first, 0.0)

    for i in range(tc // Q_BLOCK):
        mask = mask_first if (isolated_tiles or i == 0) else maskneg
        rows = slice(i * Q_BLOCK, (i + 1) * Q_BLOCK)
        krows = slice(i * Q_BLOCK, i * Q_BLOCK + nk)
        for hp in range(HEADS // 2):
            outs, lses = [], []
            for h in (2 * hp, 2 * hp + 1):
                cols = slice(h * HEAD_DIM, (h + 1) * HEAD_DIM)
                s = lax.dot_general(kbuf[krows, cols], q_ref[rows, cols],
                                    (((1,), (1,)), ((), ())), preferred_element_type=F32)
                s = s - slope_dil[h] * distf + mask
                m = jnp.max(s, axis=0, keepdims=True)
                p = jnp.exp(s - m)
                l = jnp.sum(p, axis=0, keepdims=True)
                outs.append(lax.dot_general(vbuf[krows, cols], p.astype(BF16), (((0,), (0,)), ((), ())),
                                            preferred_element_type=F32) / l)
                lses.append(jnp.broadcast_to(m + jnp.log(l), (HEAD_DIM, Q_BLOCK)))
            pc = slice(hp * 2 * HEAD_DIM, (hp + 1) * 2 * HEAD_DIM)
            o_ref[rows, pc] = jnp.concatenate(outs, axis=0).T.astype(BF16)
            lse_ref[rows, pc] = jnp.concatenate(lses, axis=0).T


def _attention(q, kc, gi):
    win, dil = ATT_GROUPS[gi]
    assert win // dil == Q_BLOCK
    batch, _, l, _ = q.shape
    out_dims = (batch, dil, l, GROUP_W)
    isolated_tiles = l == Q_BLOCK
    slope_dil = tuple(float(s) * dil for s in _slopes()[gi])
    if isolated_tiles:
        q, kc = q.reshape(batch, 1, dil * l, GROUP_W), kc.reshape(batch, 1, dil * l, 2 * GROUP_W)
        dil, l = 1, dil * l
    tc = min(l, 512)
    nk = tc + Q_BLOCK
    o, lse = pl.pallas_call(
        functools.partial(_attn_kernel, slope_dil=slope_dil, tc=tc, isolated_tiles=isolated_tiles),
        grid=(batch, dil, l // tc),
        in_specs=[pl.BlockSpec((None, None, tc, GROUP_W), lambda b, r, c: (b, r, c, 0)),
                  pl.BlockSpec((None, None, tc, 2 * GROUP_W), lambda b, r, c: (b, r, c, 0)),
                  pl.BlockSpec((None, None, Q_BLOCK, 2 * GROUP_W),
                               lambda b, r, c: (b, r, jnp.maximum(c * (tc // Q_BLOCK) - 1, 0), 0))],
        out_specs=[pl.BlockSpec((None, None, tc, GROUP_W), lambda b, r, c: (b, r, c, 0))] * 2,
        out_shape=[jax.ShapeDtypeStruct((batch, dil, l, GROUP_W), BF16),
                   jax.ShapeDtypeStruct((batch, dil, l, GROUP_W), F32)],
        scratch_shapes=[pltpu.VMEM((nk, GROUP_W), BF16), pltpu.VMEM((nk, GROUP_W), BF16)],
        compiler_params=_params(("parallel", "parallel", "arbitrary")),
        name=f"attn_g{gi}",
    )(q, kc, kc)
    return o.reshape(out_dims), lse.reshape(out_dims)


def _mix_tail(x, pool_o, att_o, nattn_ref, wgate_ref, wbp_ref, wba_ref, wo_ref, nffn_ref, xo_ref, xn_ref):
    gates = jax.nn.sigmoid(_dot(_rms(x, nattn_ref[...]).astype(BF16), wgate_ref[...]))
    hp = _dot(pool_o.astype(BF16), wbp_ref[...])
    ha = _dot(att_o.astype(BF16), wba_ref[...])
    t = gates[:, 0:D_MODEL] * hp + gates[:, D_MODEL:2 * D_MODEL] * ha
    xo = x + _dot(t.astype(BF16), wo_ref[...])
    xo_ref[...] = xo
    xn_ref[...] = _rms(xo, nffn_ref[...]).astype(BF16)


def _group_linear(m, wpool_ref, pscale_ref):
    parts = [_dot(m[:, gi * POOL_GROUP:(gi + 1) * POOL_GROUP].astype(BF16), wpool_ref[gi])
             for gi in range(len(POOL_WINDOWS))]
    return jnp.concatenate(parts, axis=1) * pscale_ref[...]


def _read_classes(src_ref, dil, il_ref, slot):
    if dil == 1:
        return src_ref[0].astype(F32)
    rows = src_ref.shape[1]
    for r in range(dil):
        v = src_ref[r].astype(F32)
        for c in range(LANE_CHUNKS):
            il_ref[slot, c, pl.ds(r, rows, stride=dil), :] = v[:, c * LANES:(c + 1) * LANES]
    return jnp.concatenate([il_ref[slot, c] for c in range(LANE_CHUNKS)], axis=1)


def _merge_kernel(x_ref, u_ref, uh_ref, o0_ref, o1_ref, o2_ref, l0_ref, l1_ref, l2_ref,
                  wpool_ref, pscale_ref, nattn_ref, wgate_ref, wbp_ref, wba_ref, wo_ref, nffn_ref,
                  xo_ref, xn_ref, ext_ref, il_ref, *, tm, dils):
    si = pl.program_id(1)
    halo = POOL_STATE + 1
    ext_ref[0:halo, :] = jnp.where(si == 0, 0.0, uh_ref[...])
    ext_ref[halo:halo + tm, :] = u_ref[...]
    pos = si * tm + lax.broadcasted_iota(jnp.int32, (tm, 1), 0)
    parts = []
    for gi, win in enumerate(POOL_WINDOWS):
        cols = slice(gi * POOL_GROUP, (gi + 1) * POOL_GROUP)
        own = ext_ref[halo:halo + tm, cols]
        acc = own
        for back in range(1, win):
            acc = acc + ext_ref[halo - back:halo - back + tm, cols]
        inv = 1.0 / jnp.minimum(pos + 1, win).astype(F32)
        parts.append(acc * inv - own)
    pool_o = _group_linear(jnp.concatenate(parts, axis=1), wpool_ref, pscale_ref)

    slot = 0
    os_, ls_ = [], []
    for g, (o_ref, l_ref) in enumerate(((o0_ref, l0_ref), (o1_ref, l1_ref), (o2_ref, l2_ref))):
        os_.append(_read_classes(o_ref, dils[g], il_ref, slot))
        slot += dils[g] > 1
        ls_.append(_read_classes(l_ref, dils[g], il_ref, slot))
        slot += dils[g] > 1
    mx = jnp.maximum(ls_[0], jnp.maximum(ls_[1], ls_[2]))
    es = [jnp.exp(v - mx) for v in ls_]
    att_o = (es[0] * os_[0] + es[1] * os_[1] + es[2] * os_[2]) / (es[0] + es[1] + es[2])
    _mix_tail(x_ref[...], pool_o, att_o, nattn_ref, wgate_ref, wbp_ref, wba_ref, wo_ref, nffn_ref, xo_ref, xn_ref)


def _wspec(shape, layer):
    nd = len(shape)
    return _resident((None,) + tuple(shape), lambda *_: (layer,) + (0,) * nd)


def _tail_weight_specs(layer):
    return [_wspec((1, D_MODEL), layer), _wspec((D_MODEL, 2 * D_MODEL), layer),
            _wspec((POOL_W, D_MODEL), layer), _wspec((GROUP_W, D_MODEL), layer),
            _wspec((D_MODEL, D_MODEL), layer), _wspec((1, D_MODEL), layer)]


def _tail_weights(w):
    return (w["norm_attn"], w["w_gate"], w["w_bp"], w["w_ba"], w["w_o"], w["norm_ffn"])


def _merge(x, u, outs, lses, layer, w, batch, seq, tm, dils):
    n = x.shape[0]
    n_tiles = seq // tm
    halo = POOL_STATE + 1
    row = lambda width: pl.BlockSpec((tm, width), lambda b, s: (b * n_tiles + s, 0))
    cls = lambda g: pl.BlockSpec((None, dils[g], tm // dils[g], GROUP_W), lambda b, s: (b, 0, s, 0))
    n_il = 2 * sum(d > 1 for d in dils)
    return pl.pallas_call(
        functools.partial(_merge_kernel, tm=tm, dils=dils),
        grid=(batch, n_tiles),
        in_specs=[row(D_MODEL), row(POOL_W),
                  pl.BlockSpec((halo, POOL_W),
                               lambda b, s: (jnp.maximum((b * n_tiles + s) * (tm // halo) - 1, 0), 0))]
                 + [cls(g) for g in range(N_GROUPS)] * 2
                 + [_wspec((len(POOL_WINDOWS), POOL_GROUP, POOL_GROUP), layer), _wspec((1, POOL_W), layer)]
                 + _tail_weight_specs(layer),
        out_specs=[row(D_MODEL), row(D_MODEL)],
        out_shape=[jax.ShapeDtypeStruct((n, D_MODEL), F32), jax.ShapeDtypeStruct((n, D_MODEL), BF16)],
        scratch_shapes=[pltpu.VMEM((halo + tm, POOL_W), F32), pltpu.VMEM((n_il, LANE_CHUNKS, tm, LANES), F32)],
        compiler_params=_params(("parallel", "parallel")),
        name="merge",
    )(x, u, u, *outs, *lses, w["w_pool"], w["pool_scale"], *_tail_weights(w))


def _head_mask():
    return jnp.asarray(np.arange(GROUP_W)[None, :] // HEAD_DIM == np.arange(HEADS)[:, None], F32)


def _sample_keys(dec_seq, cache_rows):
    slopes = _slopes()
    geo = []
    for gi, (win, dil) in enumerate(ATT_GROUPS):
        lc = cache_rows[gi]
        assert lc % dil == 0 and lc // dil == LANES
        cls = (lc + np.arange(dec_seq)) % dil
        n_cached = (cls.max() + 1) * LANES
        lane = np.arange(n_cached + LANES)
        sel = (np.arange(lc)[:, None] == (lane[:n_cached] % LANES) * dil + lane[:n_cached] // LANES)
        t = np.arange(dec_seq)[:, None]
        back = np.where(lane < n_cached, lc + t - ((lane % LANES) * dil + lane // LANES), t - (lane - n_cached))
        valid = (back >= 0) & (back % dil == 0) & (back // dil <= win // dil) & (lane < n_cached + dec_seq)
        bias = np.where(valid[:, None, :], -slopes[gi][None, :, None] * back[:, None, :], NEG_INF)
        geo.append((jnp.asarray(sel, BF16), jnp.asarray(bias.reshape(dec_seq * HEADS, -1), F32)))
    return geo


def _sample_mix_kernel(state_ref, u_ref, q0_ref, q1_ref, q2_ref, kn0_ref, kn1_ref, kn2_ref, c0_ref, c1_ref, c2_ref,
                       sel0_ref, sel1_ref, sel2_ref, bias0_ref, bias1_ref, bias2_ref, headmask_ref,
                       wpool_ref, pscale_ref, pool_ref, att_ref, newpool_ref, ext_ref, m_ref, *, dec_seq):
    ext_ref[...] = jnp.zeros_like(ext_ref)
    ext_ref[0:POOL_STATE, :] = state_ref[...]
    ext_ref[POOL_STATE:POOL_STATE + dec_seq, :] = u_ref[...]
    newpool_ref[...] = ext_ref[dec_seq:dec_seq + POOL_STATE, :]

    m_ref[...] = jnp.zeros_like(m_ref)
    for t in range(dec_seq):
        row = POOL_STATE + t
        for gi, win in enumerate(POOL_WINDOWS):
            cols = slice(gi * POOL_GROUP, (gi + 1) * POOL_GROUP)
            tot = jnp.sum(ext_ref[row - win + 1:row + 1, cols], axis=0, keepdims=True)
            cnt = float(min(PAST_LEN + t + 1, win))
            m_ref[t:t + 1, cols] = tot / cnt - ext_ref[row:row + 1, cols]
    pool_ref[...] = _group_linear(m_ref[...], wpool_ref, pscale_ref)[0:dec_seq]

    groups = ((q0_ref, kn0_ref, c0_ref, sel0_ref, bias0_ref), (q1_ref, kn1_ref, c1_ref, sel1_ref, bias1_ref),
              (q2_ref, kn2_ref, c2_ref, sel2_ref, bias2_ref))
    k_rows = HEADS * HEAD_DIM
    outs, lses = [], []
    for q_ref, kn_ref, c_ref, sel_ref, bias_ref in groups:
        flat = c_ref[...].reshape(2 * k_rows, c_ref.shape[-1]).astype(BF16)
        cached = _dot(flat, sel_ref[...]).astype(BF16)
        new = kn_ref[...].reshape(2 * k_rows, LANES).astype(BF16)
        keys = jnp.concatenate([cached[0:k_rows], new[0:k_rows]], axis=1)
        vals = jnp.concatenate([cached[k_rows:], new[k_rows:]], axis=1)
        s = _dot(q_ref[...], keys) + bias_ref[...]
        mx = jnp.max(s, axis=1, keepdims=True)
        p = jnp.exp(s - mx)
        l = jnp.sum(p, axis=1, keepdims=True)
        outs.append(lax.dot_general((p / l).astype(BF16), vals, (((1,), (1,)), ((), ())),
                                    preferred_element_type=F32))
        lses.append(mx + jnp.log(l))
    mx = jnp.maximum(lses[0], jnp.maximum(lses[1], lses[2]))
    es = [jnp.exp(v - mx) for v in lses]
    mixed = (es[0] * outs[0] + es[1] * outs[1] + es[2] * outs[2]) / (es[0] + es[1] + es[2])
    att_ref[...] = jnp.sum(mixed.reshape(dec_seq, HEADS, k_rows) * headmask_ref[...][None], axis=1)


def _sample_mix(state_pool, u, qs, kvn, caches, layer, w, dec_batch, dec_seq):
    cache_rows = [c.shape[2] for c in caches]
    geo = _sample_keys(dec_seq, cache_rows)
    consts = [g[i] for i in (0, 1) for g in geo] + [_head_mask()]
    kv_tail = (2, HEADS, HEAD_DIM)
    cviews = [jnp.transpose(c, (0, 1, 3, 4, 5, 2)).reshape((-1,) + kv_tail + (c.shape[2],)) for c in caches]
    cspecs = [pl.BlockSpec((None,) + kv_tail + (lc,), lambda b: (layer * dec_batch + b, 0, 0, 0, 0))
              for lc in cache_rows]
    per_batch = lambda *tail: pl.BlockSpec((None,) + tail, lambda b: (b,) + (0,) * len(tail))
    const = lambda shape: _resident(shape, lambda b: (0,) * len(shape))
    return pl.pallas_call(
        functools.partial(_sample_mix_kernel, dec_seq=dec_seq),
        grid=(dec_batch,),
        in_specs=[pl.BlockSpec((None, POOL_STATE, POOL_W), lambda b: (layer * dec_batch + b, 0, 0)),
                  per_batch(dec_seq, POOL_W)] + [per_batch(dec_seq * HEADS, GROUP_W)] * 3
                 + [per_batch(*kv_tail, LANES)] * 3 + cspecs + [const(c.shape) for c in consts]
                 + [_wspec((len(POOL_WINDOWS), POOL_GROUP, POOL_GROUP), layer), _wspec((1, POOL_W), layer)],
        out_specs=[per_batch(dec_seq, POOL_W), per_batch(dec_seq, GROUP_W),
                   pl.BlockSpec((None, POOL_STATE, POOL_W), lambda b: (b, 0, 0))],
        out_shape=[jax.ShapeDtypeStruct((dec_batch, dec_seq, POOL_W), F32),
                   jax.ShapeDtypeStruct((dec_batch, dec_seq, GROUP_W), F32),
                   jax.ShapeDtypeStruct((dec_batch, POOL_STATE, POOL_W), F32)],
        scratch_shapes=[pltpu.VMEM((POOL_STATE + dec_seq + 5, POOL_W), F32), pltpu.VMEM((8, POOL_W), F32)],
        compiler_params=_params(("parallel",)),
        name="sample_mix",
    )(state_pool.reshape(-1, POOL_STATE, POOL_W), u.reshape(dec_batch, dec_seq, POOL_W), *qs, *kvn, *cviews,
      *consts, w["w_pool"], w["pool_scale"])


def _tail_kernel(x_ref, pool_ref, att_ref, nattn_ref, wgate_ref, wbp_ref, wba_ref, wo_ref, nffn_ref, xo_ref, xn_ref):
    _mix_tail(x_ref[...], pool_ref[...], att_ref[...], nattn_ref, wgate_ref, wbp_ref, wba_ref, wo_ref, nffn_ref,
              xo_ref, xn_ref)


def _tail(x, pool_o, att_o, layer, w):
    n = x.shape[0]
    row = lambda width: pl.BlockSpec((n, width), lambda i: (0, 0))
    return pl.pallas_call(
        _tail_kernel,
        grid=(1,),
        in_specs=[row(D_MODEL), row(POOL_W), row(GROUP_W)] + _tail_weight_specs(layer),
        out_specs=[row(D_MODEL), row(D_MODEL)],
        out_shape=[jax.ShapeDtypeStruct((n, D_MODEL), F32), jax.ShapeDtypeStruct((n, D_MODEL), BF16)],
        compiler_params=_params(("arbitrary",)),
        name="tail",
    )(x, pool_o, att_o, *_tail_weights(w))


def _swiglu_step(x, wg_ref, wu_ref, wd_ref, acc_ref):
    h = jax.nn.silu(_dot(x, wg_ref[...])) * _dot(x, wu_ref[...])
    acc_ref[...] += _dot(h.astype(BF16), wd_ref[...])


def _ffn_kernel(x_ref, xn_ref, w1_ref, w3_ref, w2_ref, y_ref):
    xn = xn_ref[...]
    h = jax.nn.silu(_dot(xn, w1_ref[...])) * _dot(xn, w3_ref[...])
    y_ref[...] = x_ref[...] + _dot(h.astype(BF16), w2_ref[...])


def _ffn(x, xn, w1, w3, w2, idx, tm):
    n = x.shape[0]
    d_ff = w1.shape[-1]
    return pl.pallas_call(
        _ffn_kernel,
        grid=(n // tm,),
        in_specs=[pl.BlockSpec((tm, D_MODEL), lambda i: (i, 0)),
                  pl.BlockSpec((tm, D_MODEL), lambda i: (i, 0)),
                  _resident((None, D_MODEL, d_ff), lambda i: (idx, 0, 0)),
                  _resident((None, D_MODEL, d_ff), lambda i: (idx, 0, 0)),
                  _resident((None, d_ff, D_MODEL), lambda i: (idx, 0, 0))],
        out_specs=pl.BlockSpec((tm, D_MODEL), lambda i: (i, 0)),
        out_shape=jax.ShapeDtypeStruct((n, D_MODEL), F32),
        compiler_params=_params(("parallel",)),
        name="ffn",
    )(x, xn, w1, w3, w2)


SEG_ALIGN = 16
MOE_ROWS = 1024


def _sorted_cap(ts):
    need = TOP_K * ts + N_EXPERTS * (SEG_ALIGN - 1)
    return -(-need // LANES) * LANES if ts >= LANES * 2 else -(-need // SEG_ALIGN) * SEG_ALIGN


def _segment_sizes(oh1, oh2):
    cnt1 = jnp.sum(oh1, axis=1, keepdims=True)
    cnt = cnt1 + jnp.sum(oh2, axis=1, keepdims=True)
    cpad = jnp.floor((cnt + (SEG_ALIGN - 1)) * (1.0 / SEG_ALIGN)) * SEG_ALIGN
    offs = [jnp.zeros((1, 1), F32)]
    for ei in range(1, N_EXPERTS):
        offs.append(offs[-1] + cpad[ei - 1:ei, :])
    return cnt1, cpad, jnp.concatenate(offs, axis=0)


def _gate_kernel(x_ref, g_ref, rwt_ref, rb_ref, route_ref, seg_ref, *, ts):
    xn = _rms(x_ref[...], g_ref[...]).astype(BF16)
    logits = lax.dot_general(rwt_ref[...], xn, (((1,), (1,)), ((), ())), preferred_element_type=F32) + rb_ref[...]
    row = lax.broadcasted_iota(jnp.int32, (N_EXPERTS, ts), 0)
    neg = jnp.float32(-jnp.inf)
    m1 = jnp.max(logits, axis=0, keepdims=True)
    i1 = jnp.min(jnp.where(logits == m1, row, N_EXPERTS), axis=0, keepdims=True)
    rest = jnp.where(row == i1, neg, logits)
    m2 = jnp.max(rest, axis=0, keepdims=True)
    i2 = jnp.min(jnp.where(rest == m2, row, N_EXPERTS), axis=0, keepdims=True)
    e = jnp.exp(m2 - m1)
    zeros = jnp.zeros((N_EXPERTS - 4, ts), F32)
    route_ref[...] = jnp.concatenate([i1.astype(F32), i2.astype(F32), 1.0 / (1.0 + e), e / (1.0 + e), zeros], axis=0)
    _, cpad, off = _segment_sizes(jnp.where(row == i1, 1.0, 0.0), jnp.where(row == i2, 1.0, 0.0))
    lane = lax.broadcasted_iota(jnp.int32, (N_EXPERTS, LANES), 1)
    seg_ref[...] = jnp.where(lane == 0, off, jnp.where(lane == 1, cpad, 0.0)).astype(jnp.int32)


def _gate(x, norm_ffn, rwt, rb, layer, idx, ts):
    n = x.shape[0]
    n_sub = n // ts
    return pl.pallas_call(
        functools.partial(_gate_kernel, ts=ts),
        grid=(n_sub,),
        in_specs=[pl.BlockSpec((ts, D_MODEL), lambda i: (i, 0)),
                  _resident((None, 1, D_MODEL), lambda i: (layer, 0, 0)),
                  _resident((None, N_EXPERTS, D_MODEL), lambda i: (idx, 0, 0)),
                  _resident((None, N_EXPERTS, 1), lambda i: (idx, 0, 0))],
        out_specs=[pl.BlockSpec((None, N_EXPERTS, ts), lambda i: (i, 0, 0)),
                   pl.BlockSpec((None, N_EXPERTS, LANES), lambda i: (i, 0, 0))],
        out_shape=[jax.ShapeDtypeStruct((n_sub, N_EXPERTS, ts), F32),
                   jax.ShapeDtypeStruct((n_sub, N_EXPERTS, LANES), jnp.int32)],
        compiler_params=_params(("parallel",)),
        name="gate",
    )(x, norm_ffn, rwt, rb)


def _segment_copies(i, n_bits, local_ref, off_ref, far_ref, far_rows_ref, nseg_ref, sem, to_far):
    copies = []
    for e in range(N_EXPERTS):
        j = i * N_EXPERTS + e
        n = nseg_ref[j]
        for b in reversed(range(n_bits)):
            size = SEG_ALIGN << b
            done = ((n >> (b + 1)) << (b + 1)) * SEG_ALIGN
            near = local_ref.at[pl.ds(pl.multiple_of(off_ref[j] + done, SEG_ALIGN), size)]
            far = far_ref.at[pl.ds(pl.multiple_of(far_rows_ref[j] + done, SEG_ALIGN), size)]
            cp = pltpu.make_async_copy(near, far, sem) if to_far else pltpu.make_async_copy(far, near, sem)
            copies.append((((n >> b) & 1) == 1, cp))
    return copies


def _scatter_kernel(off_ref, far_rows_ref, nseg_ref, x_ref, g_ref, route_ref, tri_ref, xb_in_ref,
                    meta_ref, xb_ref, xs_ref, sem, *, ts, cap, n_bits):
    del xb_in_ref
    i = pl.program_id(0)
    xn = _rms(x_ref[...], g_ref[...]).astype(BF16)
    row = lax.broadcasted_iota(jnp.int32, (N_EXPERTS, ts), 0).astype(F32)
    oh1 = jnp.where(row == route_ref[0:1, :], 1.0, 0.0)
    oh2 = jnp.where(row == route_ref[1:2, :], 1.0, 0.0)
    cnt1, _, off = _segment_sizes(oh1, oh2)
    pre1 = _dot(oh1.astype(BF16), tri_ref[...])
    pre2 = _dot(oh2.astype(BF16), tri_ref[...])
    d1 = jnp.sum(oh1 * (off + pre1), axis=0, keepdims=True)
    d2 = jnp.sum(oh2 * (off + cnt1 + pre2), axis=0, keepdims=True)
    slot = lax.broadcasted_iota(jnp.int32, (cap, ts), 0).astype(F32)
    p = jnp.where(slot == d1, 1.0, jnp.where(slot == d2, 1.0, 0.0)).astype(BF16)
    xs_ref[i % 2] = _dot(p, xn).astype(BF16)
    meta_t = jnp.concatenate([d1, d2, route_ref[2:4, :], jnp.zeros((LANES - 4, ts), F32)], axis=0)
    meta_ref[...] = meta_t.T

    def send(step, action):
        buf = step % 2
        for pred, cp in _segment_copies(step, n_bits, xs_ref.at[buf], off_ref, xb_ref, far_rows_ref, nseg_ref,
                                        sem.at[buf], True):
            pl.when(pred)(getattr(cp, action))

    send(i, "start")
    pl.when(i > 0)(lambda: send(i - 1, "wait"))
    pl.when(i == pl.num_programs(0) - 1)(lambda: send(i, "wait"))


def _scatter(off, far_rows, nseg, x, norm_ffn, route, xb, layer, ts):
    n = x.shape[0]
    n_sub = n // ts
    cap = _sorted_cap(ts)
    tri = jnp.asarray(np.triu(np.ones((ts, ts), np.float32), 1), BF16)
    n_bits = int(TOP_K * ts // SEG_ALIGN).bit_length()
    return pl.pallas_call(
        functools.partial(_scatter_kernel, ts=ts, cap=cap, n_bits=n_bits),
        grid_spec=pltpu.PrefetchScalarGridSpec(
            num_scalar_prefetch=3,
            grid=(n_sub,),
            in_specs=[pl.BlockSpec((ts, D_MODEL), lambda i, *_: (i, 0)),
                      _resident((None, 1, D_MODEL), lambda i, *_: (layer, 0, 0)),
                      pl.BlockSpec((None, N_EXPERTS, ts), lambda i, *_: (i, 0, 0)),
                      _resident((ts, ts), lambda i, *_: (0, 0)),
                      pl.BlockSpec(memory_space=pl.ANY)],
            out_specs=[pl.BlockSpec((ts, LANES), lambda i, *_: (i, 0)),
                       pl.BlockSpec(memory_space=pl.ANY)],
            scratch_shapes=[pltpu.VMEM((2, cap, D_MODEL), BF16), pltpu.SemaphoreType.DMA((2,))]),
        out_shape=[jax.ShapeDtypeStruct((n, LANES), F32), jax.ShapeDtypeStruct(xb.shape, xb.dtype)],
        input_output_aliases={7: 1},
        compiler_params=_params(("arbitrary",)),
        name="scatter",
    )(off, far_rows, nseg, x, norm_ffn, route, tri, xb)


def _experts_kernel(blk_ref, nreal_ref, xb_ref, wg_ref, wu_ref, wd_ref, yb_ref, acc_ref):
    del blk_ref
    i = pl.program_id(0)
    f = pl.program_id(1)

    @pl.when(f == 0)
    def _():
        acc_ref[...] = jnp.zeros_like(acc_ref)

    @pl.when(i < nreal_ref[0])
    def _():
        _swiglu_step(xb_ref[...], wg_ref, wu_ref, wd_ref, acc_ref)

    @pl.when(f == pl.num_programs(1) - 1)
    def _():
        yb_ref[...] = acc_ref[...].astype(yb_ref.dtype)


def _experts(blk_e, nreal, xb, wg, wu, wd, idx, tf):
    n_blocks = blk_e.shape[0]
    d_ff = wg.shape[-1]
    n_f = d_ff // tf

    def live_block(i, nreal):
        return jnp.minimum(i, nreal[0] - 1)

    def wcol(i, f, blk, nreal):
        return idx, blk[live_block(i, nreal)], 0, jnp.where(i < nreal[0], f, n_f - 1)

    def wrow(i, f, blk, nreal):
        return idx, blk[live_block(i, nreal)], jnp.where(i < nreal[0], f, n_f - 1), 0

    return pl.pallas_call(
        _experts_kernel,
        grid_spec=pltpu.PrefetchScalarGridSpec(
            num_scalar_prefetch=2,
            grid=(n_blocks, n_f),
            in_specs=[pl.BlockSpec((MOE_ROWS, D_MODEL), lambda i, f, blk, nreal: (live_block(i, nreal), 0)),
                      pl.BlockSpec((None, None, D_MODEL, tf), wcol),
                      pl.BlockSpec((None, None, D_MODEL, tf), wcol),
                      pl.BlockSpec((None, None, tf, D_MODEL), wrow)],
            out_specs=pl.BlockSpec((MOE_ROWS, D_MODEL), lambda i, f, blk, nreal: (i, 0)),
            scratch_shapes=[pltpu.VMEM((MOE_ROWS, D_MODEL), F32)]),
        out_shape=jax.ShapeDtypeStruct((n_blocks * MOE_ROWS, D_MODEL), BF16),
        compiler_params=_params(("parallel", "arbitrary")),
        name="experts",
    )(blk_e, nreal, xb, wg, wu, wd)


def _combine_kernel(off_ref, far_rows_ref, nseg_ref, x_ref, meta_ref, yb_ref, o_ref, ys_ref, sem, *, ts, cap, n_bits):
    i = pl.program_id(0)

    def fetch(step, action):
        buf = step % 2
        for pred, cp in _segment_copies(step, n_bits, ys_ref.at[buf], off_ref, yb_ref, far_rows_ref, nseg_ref,
                                        sem.at[buf], False):
            pl.when(pred)(getattr(cp, action))

    def start(step):
        ys_ref[step % 2] = jnp.zeros((cap, D_MODEL), BF16)
        fetch(step, "start")

    pl.when(i == 0)(lambda: start(i))
    pl.when(i + 1 < pl.num_programs(0))(lambda: start(i + 1))
    fetch(i, "wait")
    slot = lax.broadcasted_iota(jnp.int32, (ts, cap), 1).astype(F32)
    ys = ys_ref[i % 2]
    q1 = jnp.where(slot == meta_ref[:, 0:1], 1.0, 0.0).astype(BF16)
    q2 = jnp.where(slot == meta_ref[:, 1:2], 1.0, 0.0).astype(BF16)
    o_ref[...] = x_ref[...] + meta_ref[:, 2:3] * _dot(q1, ys) + meta_ref[:, 3:4] * _dot(q2, ys)


def _combine(off, far_rows, nseg, x, meta, yb, ts):
    n = x.shape[0]
    cap = _sorted_cap(ts)
    n_bits = int(TOP_K * ts // SEG_ALIGN).bit_length()
    return pl.pallas_call(
        functools.partial(_combine_kernel, ts=ts, cap=cap, n_bits=n_bits),
        grid_spec=pltpu.PrefetchScalarGridSpec(
            num_scalar_prefetch=3,
            grid=(n // ts,),
            in_specs=[pl.BlockSpec((ts, D_MODEL), lambda i, *_: (i, 0)),
                      pl.BlockSpec((ts, LANES), lambda i, *_: (i, 0)),
                      pl.BlockSpec(memory_space=pl.ANY)],
            out_specs=pl.BlockSpec((ts, D_MODEL), lambda i, *_: (i, 0)),
            scratch_shapes=[pltpu.VMEM((2, cap, D_MODEL), BF16), pltpu.SemaphoreType.DMA((2,))]),
        out_shape=jax.ShapeDtypeStruct((n, D_MODEL), F32),
        compiler_params=_params(("arbitrary",)),
        name="combine",
    )(off, far_rows, nseg, x, meta, yb)


def _moe(xs, tss, norm_ffn, rwt, rb, wg, wu, wd, layer, idx, tf):
    gated = [_gate(x, norm_ffn, rwt, rb, layer, idx, ts) for x, ts in zip(xs, tss)]
    off = jnp.concatenate([g[1][:, :, 0] for g in gated], axis=0)
    cpad = jnp.concatenate([g[1][:, :, 1] for g in gated], axis=0)
    n_subs = [g[1].shape[0] for g in gated]
    tot = jnp.sum(cpad, axis=0)
    padded = (tot + MOE_ROWS - 1) // MOE_ROWS * MOE_ROWS
    pend = jnp.cumsum(padded)
    within = jnp.cumsum(cpad, axis=0) - cpad
    off = off.astype(jnp.int32).reshape(-1)
    grouped_rows = ((pend - padded)[None, :] + within).astype(jnp.int32).reshape(-1)
    nseg = (cpad // SEG_ALIGN).astype(jnp.int32).reshape(-1)
    n_assign = TOP_K * sum(x.shape[0] for x in xs)
    n_blocks = (n_assign + (SEG_ALIGN - 1) * sum(n_subs) * N_EXPERTS + N_EXPERTS * (MOE_ROWS - 1)) // MOE_ROWS
    blk_e = jnp.minimum(jnp.sum(jnp.arange(n_blocks)[:, None] * MOE_ROWS >= pend[None, :], axis=1),
                        N_EXPERTS - 1).astype(jnp.int32)
    nreal = (pend[-1:] // MOE_ROWS).astype(jnp.int32)

    xb = jnp.zeros((n_blocks * MOE_ROWS, D_MODEL), BF16)
    metas, parts = [], []
    lo = 0
    for x, g, ns, ts in zip(xs, gated, n_subs, tss):
        sl = slice(lo * N_EXPERTS, (lo + ns) * N_EXPERTS)
        parts.append((off[sl], grouped_rows[sl], nseg[sl]))
        meta, xb = _scatter(*parts[-1], x, norm_ffn, g[0], xb, layer, ts)
        metas.append(meta)
        lo += ns
    yb = _experts(blk_e, nreal, xb, wg, wu, wd, idx, tf)
    return [_combine(*part, x, meta, yb, ts) for part, x, meta, ts in zip(parts, xs, metas, tss)]


def kernel(x_prompt, x_sample, state_pool, cache_kv_g0, cache_kv_g1, cache_kv_g2, norm_attn, w_in, q_norm, k_norm, w_pool, pool_scale, w_branch_pool, w_branch_attn, w_out, norm_ffn, w1_dense, w3_dense, w2_dense, router_w, router_b, we_gate, we_up, we_down):
    batch, seq, _ = x_prompt.shape
    dec_batch, dec_seq, _ = x_sample.shape
    depth = w_in.shape[0]
    caches = (cache_kv_g0, cache_kv_g1, cache_kv_g2)
    n_p, n_s = batch * seq, dec_batch * dec_seq
    tm_p = 512
    dils = tuple(d for _, d in ATT_GROUPS)
    keeps = tuple(min(win, seq) for win, _ in ATT_GROUPS)

    head_of = np.arange(GROUP_W) // HEAD_DIM
    same_head = (head_of[:, None] == head_of[None, :]).astype(np.float32)
    seg_mean = jnp.asarray(same_head / HEAD_DIM, BF16)
    w_qkv = w_in[:, :, :PROJ_W].astype(BF16)
    qgain = (jnp.tile(q_norm, (1, HEADS)) * (HEAD_DIM ** -0.5)).reshape(depth, 1, GROUP_W)
    kgain = jnp.tile(k_norm, (1, HEADS)).reshape(depth, 1, GROUP_W)
    norm_attn3 = norm_attn.reshape(depth, 1, D_MODEL)
    w = {"w_pool": w_pool.astype(BF16), "pool_scale": pool_scale.reshape(depth, 1, POOL_W),
         "norm_attn": norm_attn3, "w_gate": w_in[:, :, PROJ_W:].astype(BF16),
         "w_bp": w_branch_pool.astype(BF16), "w_ba": w_branch_attn.astype(BF16), "w_o": w_out.astype(BF16),
         "norm_ffn": norm_ffn.reshape(depth, 1, D_MODEL)}
    w1_b, w3_b, w2_b = w1_dense.astype(BF16), w3_dense.astype(BF16), w2_dense.astype(BF16)
    wg_b, wu_b, wd_b = we_gate.astype(BF16), we_up.astype(BF16), we_down.astype(BF16)
    rwt = jnp.swapaxes(router_w, 1, 2).astype(BF16)
    rb = router_b.reshape(-1, N_EXPERTS, 1)

    xp = x_prompt.reshape(n_p, D_MODEL)
    xs = x_sample.reshape(n_s, D_MODEL)
    pool_p, pool_s = [], []
    kv_s = [[] for _ in ATT_GROUPS]
    kvo_p = ()
    ones = (1,) * N_GROUPS
    for layer in range(depth):
        res = _proj(xp, layer, norm_attn3, w_qkv, qgain, kgain, seg_mean, batch, seq, tm_p, dils, keeps,
                    layer, depth, kvo_p, 1, True)
        u, qs, kcs, kvo_p = res[0], res[1:4], res[4:7], tuple(res[7:10])
        outs, lses = [], []
        for gi in range(N_GROUPS):
            o, lse = _attention(qs[gi], kcs[gi], gi)
            outs.append(o)
            lses.append(lse)
        pool_p.append(u.reshape(batch, seq, POOL_W)[:, seq - POOL_STATE:])
        xp, xpn = _merge(xp, u, outs, lses, layer, w, batch, seq, tm_p, dils)

        res = _proj(xs, layer, norm_attn3, w_qkv, qgain, kgain, seg_mean, 1, n_s, n_s, ones, (n_s,) * N_GROUPS,
                    0, 1, (), 3, False)
        u = res[0]
        qs = [(q.reshape(dec_batch, dec_seq, 1, GROUP_W) * _head_mask().astype(BF16))
              .reshape(dec_batch, dec_seq * HEADS, GROUP_W) for q in res[1:4]]
        kv_new = [kv.reshape(dec_batch, dec_seq, 2, HEADS, HEAD_DIM) for kv in res[7:10]]
        tokens_to_lanes = ((0, 0),) * 4 + ((0, LANES - dec_seq),)
        kvn = [jnp.pad(kv.transpose(0, 2, 3, 4, 1), tokens_to_lanes) for kv in kv_new]
        pool_o, att_o, new_pool = _sample_mix(state_pool, u, qs, kvn, caches, layer, w, dec_batch, dec_seq)
        att_o = att_o.reshape(n_s, GROUP_W)
        for gi in range(N_GROUPS):
            kv_s[gi].append(kv_new[gi])
        pool_s.append(new_pool)
        xs, xsn = _tail(xs, pool_o.reshape(n_s, POOL_W), att_o, layer, w)

        i = layer // 2
        if layer % 2 == 0:
            xp = _ffn(xp, xpn, w1_b, w3_b, w2_b, i, 512)
            xs = _ffn(xs, xsn, w1_b, w3_b, w2_b, i, n_s)
        else:
            xp, xs = _moe([xp, xs], [512, n_s], w["norm_ffn"], rwt, rb, wg_b, wu_b, wd_b, layer, i,
                           wg_b.shape[-1] // 2)
    kv_p = [kvo_p[g].reshape(depth, batch, 2, HEADS, HEAD_DIM, keeps[g]).transpose(0, 1, 5, 2, 3, 4)
            for g in range(N_GROUPS)]
    return (xp.reshape(batch, seq, D_MODEL), xs.reshape(dec_batch, dec_seq, D_MODEL),
            jnp.stack(pool_p), jnp.stack(pool_s),
            kv_p[0], jnp.stack(kv_s[0]),
            kv_p[1], jnp.stack(kv_s[1]),
            kv_p[2], jnp.stack(kv_s[2]))
```
